```python
import jax, jax.numpy as jnp
from jax import lax
import numpy as np

D_MODEL = 1024
BATCH = 2
SEQ = 8192
DEPTH = 1
DEC_BATCH = 32
DEC_SEQ = 4
PAST_LEN = 8192
PAGE_SIZE = 128

HEAD_DIM = 64
NSA_HEADS = 8
NSA_GROUPS = 2
NSA_HPG = NSA_HEADS // NSA_GROUPS
RET_HEADS = 4
RET_DK = 64
RET_DV = 64
MEM_HEADS = 4
N_MEM = 256
CMP_BLK = 32
CMP_STRIDE = 16
CMP_HID = 256
SLC_BLK = 64
N_SEL = 16
WINDOW = 512
Q_BLK = 128
RET_CHUNK = 128
N_EXPERTS = 32
TOP_K = 4
D_FF = 1024
SWIGLU_LIMIT = 7.0
SWIGLU_ALPHA = 1.702
MOE_BLK = 128
EPS = 1e-6
NEG = -1e30
BIG = 1e9
ROPE_BASE = 10000.0
SLC_RATIO = SLC_BLK // CMP_STRIDE
CMP_OVL = CMP_BLK // CMP_STRIDE - 1

NSA_Q_COLS = NSA_HEADS * HEAD_DIM
NSA_KV_COLS = 2 * NSA_GROUPS * HEAD_DIM
NSA_GATE_COLS = 3 * NSA_HEADS
RET_QK_COLS = RET_HEADS * RET_DK
RET_V_COLS = RET_HEADS * RET_DV
MEM_Q_COLS = MEM_HEADS * HEAD_DIM
SPLIT_SIZES = (NSA_Q_COLS, NSA_KV_COLS, NSA_KV_COLS, NSA_KV_COLS, NSA_GATE_COLS,
               RET_QK_COLS, RET_QK_COLS, RET_V_COLS, RET_V_COLS, MEM_Q_COLS)
IN_COLS = NSA_Q_COLS + 3 * NSA_KV_COLS + NSA_GATE_COLS + 2 * RET_QK_COLS + 2 * RET_V_COLS + MEM_Q_COLS
MIX_WIDTH = NSA_Q_COLS + RET_V_COLS + MEM_Q_COLS

kernel_name = "hymba_nsa_retention_moe_step"


def _rms(x, g):
    xf = x.astype(jnp.float32)
    y = xf * lax.rsqrt(jnp.mean(xf * xf, axis=-1, keepdims=True) + EPS)
    return (y * g.astype(jnp.float32)).astype(x.dtype)


def _norm_k(kv, g):
    return jnp.stack([_rms(kv[:, :, 0], g), kv[:, :, 1]], axis=2)


def _rotary(x, pos):
    half = x.shape[-1] // 2
    inv = ROPE_BASE ** (-jnp.arange(half, dtype=jnp.float32) / half)
    ang = pos.astype(jnp.float32)[:, None] * inv[None, :]
    cos = jnp.cos(ang)[:, None, :]
    sin = jnp.sin(ang)[:, None, :]
    xf = x.astype(jnp.float32)
    x1, x2 = xf[..., :half], xf[..., half:]
    return jnp.concatenate([x1 * cos - x2 * sin, x1 * sin + x2 * cos], axis=-1)


def _project(x, pos, norm1_g, w_in, q_norm_g, k_norm_slc_g, k_norm_win_g, mem_q_norm_g):
    B, S, _ = x.shape
    z = _rms(x, norm1_g) @ w_in
    parts = []
    off = 0
    for n in SPLIT_SIZES:
        parts.append(z[..., off:off + n])
        off += n
    q, kvc, kvs, kvw, gt, rq, rk, rv, rg, mq = parts
    q = _rms(q.reshape(B, S, NSA_GROUPS, NSA_HPG, HEAD_DIM), q_norm_g)
    kvc = kvc.reshape(B, S, 2, NSA_GROUPS, HEAD_DIM)
    kvs = _norm_k(kvs.reshape(B, S, 2, NSA_GROUPS, HEAD_DIM), k_norm_slc_g)
    kvw = _norm_k(kvw.reshape(B, S, 2, NSA_GROUPS, HEAD_DIM), k_norm_win_g)
    gates = jax.nn.sigmoid(gt.astype(jnp.float32)).reshape(B, S, NSA_GROUPS, NSA_HPG, 3)
    rq = _rotary(rq.reshape(B, S, RET_HEADS, RET_DK), pos)
    rk = _rotary(rk.reshape(B, S, RET_HEADS, RET_DK), pos) * (RET_DK ** -0.5)
    rv = rv.reshape(B, S, RET_HEADS, RET_DV)
    mq = _rms(mq.reshape(B, S, MEM_HEADS, HEAD_DIM), mem_q_norm_g)
    return q, kvc, kvs, kvw, gates, rq, rk, rv, rg, mq


def _compress(kv_rows, pe_k, w1_k, w2_k, pe_v, w1_v, w2_v, k_norm_g):
    B, T = kv_rows.shape[:2]
    n_cmp = (T - CMP_BLK) // CMP_STRIDE + 1
    idx = jnp.arange(n_cmp)[:, None] * CMP_STRIDE + jnp.arange(CMP_BLK)[None, :]
    blk = kv_rows[:, idx]

    def phi(x, pe, w1, w2):
        x = (x + pe[:, None, :]).transpose(0, 1, 3, 2, 4).reshape(B, n_cmp, NSA_GROUPS, CMP_BLK * HEAD_DIM)
        return jax.nn.gelu(x @ w1) @ w2

    kc = _rms(phi(blk[:, :, :, 0], pe_k, w1_k, w2_k), k_norm_g)
    vc = phi(blk[:, :, :, 1], pe_v, w1_v, w2_v)
    cend = jnp.arange(n_cmp, dtype=jnp.int32) * CMP_STRIDE + (CMP_BLK - 1)
    return kc, vc, cend


def _slc_blocks(kv):
    B, T = kv.shape[:2]
    n_slc = -(-T // SLC_BLK)
    kv = jnp.pad(kv, ((0, 0), (0, n_slc * SLC_BLK - T), (0, 0), (0, 0), (0, 0)))
    blk = kv.reshape(B, n_slc, SLC_BLK, 2, NSA_GROUPS, HEAD_DIM).transpose(0, 4, 1, 2, 3, 5)
    return blk[..., 0, :], blk[..., 1, :]


def _nsa_core(q, qpos, gates, kc, vc, cend, ks_blk, vs_blk, kw, vw, wpos):
    B, Q = q.shape[:2]
    scale = HEAD_DIM ** -0.5
    mc = cend[None, :] <= qpos[:, None]
    sc = jnp.einsum('bqghd,bngd->bghqn', q, kc).astype(jnp.float32) * scale
    pc = jax.nn.softmax(jnp.where(mc, sc, NEG), axis=-1)
    pc = jnp.where(jnp.any(mc, axis=-1)[:, None], pc, 0.0)
    oc = jnp.einsum('bghqn,bngd->bqghd', pc.astype(vc.dtype), vc)
    n_cmp = kc.shape[1]
    n_slc = ks_blk.shape[2]
    imp = jnp.pad(pc.sum(axis=2), ((0, 0), (0, 0), (0, 0), (CMP_OVL, SLC_RATIO * n_slc - n_cmp)))
    ps = sum(imp[..., r:r + SLC_RATIO * n_slc:SLC_RATIO] for r in range(SLC_RATIO + CMP_OVL))
    bidx = jnp.arange(n_slc, dtype=jnp.int32)
    cur = qpos // SLC_BLK
    forced = (bidx[None] == 0) | (bidx[None] == cur[:, None]) | (bidx[None] == cur[:, None] - 1)
    valid = bidx[None] <= cur[:, None]
    ps = jnp.where(forced, BIG, jnp.where(valid, ps, NEG))
    k_sel = min(N_SEL, n_slc)
    _, sel = lax.top_k(ps, k_sel)
    take = jax.vmap(jax.vmap(lambda blocks, i: blocks[i]))
    s_len = k_sel * SLC_BLK
    ksel = take(ks_blk, sel).reshape(B, NSA_GROUPS, Q, s_len, HEAD_DIM)
    vsel = take(vs_blk, sel).reshape(B, NSA_GROUPS, Q, s_len, HEAD_DIM)
    spos = (sel[..., None] * SLC_BLK + jnp.arange(SLC_BLK, dtype=jnp.int32)).reshape(B, NSA_GROUPS, Q, s_len)
    ms = spos[:, :, None] <= qpos[None, None, None, :, None]
    ss = jnp.einsum('bqghd,bgqsd->bghqs', q, ksel).astype(jnp.float32) * scale
    psel = jax.nn.softmax(jnp.where(ms, ss, NEG), axis=-1)
    osel = jnp.einsum('bghqs,bgqsd->bqghd', psel.astype(vsel.dtype), vsel)
    dpos = qpos[:, None] - wpos[None, :]
    mw = (dpos >= 0) & (dpos < WINDOW) & (wpos[None, :] >= 0)
    sw = jnp.einsum('bqghd,bsgd->bghqs', q, kw).astype(jnp.float32) * scale
    pw = jax.nn.softmax(jnp.where(mw, sw, NEG), axis=-1)
    ow = jnp.einsum('bghqs,bsgd->bqghd', pw.astype(vw.dtype), vw)
    o = (gates[..., 0:1] * oc.astype(jnp.float32) + gates[..., 1:2] * osel.astype(jnp.float32)
         + gates[..., 2:3] * ow.astype(jnp.float32))
    return o.astype(q.dtype)


def _nsa_prompt(q, gates, kvc, kvs, kvw, cmp_w):
    B, S = q.shape[:2]
    kc, vc, cend = _compress(kvc, *cmp_w)
    ks_blk, vs_blk = _slc_blocks(kvs)
    kw_pad = jnp.pad(kvw, ((0, 0), (WINDOW, 0), (0, 0), (0, 0), (0, 0)))
    n_qb = S // Q_BLK
    qb = q.reshape(B, n_qb, Q_BLK, NSA_GROUPS, NSA_HPG, HEAD_DIM).swapaxes(0, 1)
    gb = gates.reshape(B, n_qb, Q_BLK, NSA_GROUPS, NSA_HPG, 3).swapaxes(0, 1)
    starts = jnp.arange(n_qb, dtype=jnp.int32) * Q_BLK

    def body(args):
        qi, gi, s0 = args
        qpos = s0 + jnp.arange(Q_BLK, dtype=jnp.int32)
        kwi = lax.dynamic_slice_in_dim(kw_pad, s0, WINDOW + Q_BLK, axis=1)
        wpos = s0 - WINDOW + jnp.arange(WINDOW + Q_BLK, dtype=jnp.int32)
        return _nsa_core(qi, qpos, gi, kc, vc, cend, ks_blk, vs_blk, kwi[:, :, 0], kwi[:, :, 1], wpos)

    o = lax.map(body, (qb, gb, starts))
    return o.swapaxes(0, 1).reshape(B, S, NSA_Q_COLS)


def _nsa_sample(q, gates, kvc_new, kvs_new, kvw_new, pool_cmp, pool_slc, win_buf, page_table, cmp_w):
    DB, QS = q.shape[:2]

    def gather_pages(pool):
        pages = pool[page_table]
        return pages.reshape(DB, -1, 2, NSA_GROUPS, HEAD_DIM)

    full_c = jnp.concatenate([gather_pages(pool_cmp), kvc_new], axis=1)
    full_s = jnp.concatenate([gather_pages(pool_slc), kvs_new], axis=1)
    past = full_c.shape[1] - QS
    kc, vc, cend = _compress(full_c, *cmp_w)
    ks_blk, vs_blk = _slc_blocks(full_s)
    wb = win_buf.shape[1]
    win = jnp.concatenate([win_buf, kvw_new], axis=1)
    wpos = past - wb + jnp.arange(wb + QS, dtype=jnp.int32)
    qpos = past + jnp.arange(QS, dtype=jnp.int32)
    o = _nsa_core(q, qpos, gates, kc, vc, cend, ks_blk, vs_blk, win[:, :, 0], win[:, :, 1], wpos)
    return o.reshape(DB, QS, NSA_Q_COLS), win[:, QS:]


def _log_decay():
    return jnp.log(1.0 - 2.0 ** (-5.0 - jnp.arange(RET_HEADS, dtype=jnp.float32)))


def _ret_chunk(R, q, k, v):
    C = q.shape[1]
    lg = _log_decay()
    idx = jnp.arange(C, dtype=jnp.float32)
    diff = idx[:, None] - idx[None, :]
    dmask = jnp.where(diff >= 0, jnp.exp(jnp.maximum(diff, 0.0)[None] * lg[:, None, None]), 0.0)
    s = jnp.einsum('bnhd,bmhd->bhnm', q, k) * dmask
    o = jnp.einsum('bhnm,bmhe->bnhe', s, v)
    xi = jnp.exp((idx + 1.0)[:, None] * lg[None, :])
    o = o + jnp.einsum('bnhd,bhde->bnhe', q, R) * xi[None, :, :, None]
    zeta = jnp.exp((C - 1.0 - idx)[:, None] * lg[None, :])
    R = jnp.exp(C * lg)[None, :, None, None] * R + jnp.einsum('bmhd,bmhe->bhde', k * zeta[None, :, :, None], v)
    return o, R


def _retention_prompt(rq, rk, rv):
    B, S = rq.shape[:2]
    n = S // RET_CHUNK

    def to_chunks(a):
        return a.reshape(B, n, RET_CHUNK, *a.shape[2:]).swapaxes(0, 1)

    R0 = jnp.zeros((B, RET_HEADS, RET_DK, RET_DV), jnp.float32)

    def step(R, xs):
        o, R = _ret_chunk(R, *xs)
        return R, o

    R, o = lax.scan(step, R0, (to_chunks(rq), to_chunks(rk), to_chunks(rv.astype(jnp.float32))))
    return o.swapaxes(0, 1).reshape(B, S, RET_HEADS, RET_DV), R


def _ret_out(o, rg, g):
    B, S = o.shape[:2]
    gate = jax.nn.silu(rg.astype(jnp.float32)).reshape(B, S, RET_HEADS, RET_DV)
    return (_rms(o, g) * gate).reshape(B, S, RET_V_COLS)


def _mem_kv(mem, mem_norm_g, w_mem_kv, mem_k_norm_g):
    B = mem.shape[0]
    kv = (_rms(mem, mem_norm_g) @ w_mem_kv).reshape(B, N_MEM, 2, MEM_HEADS, HEAD_DIM)
    return _norm_k(kv, mem_k_norm_g)


def _mem_attend(mq, mkv):
    B, S = mq.shape[:2]
    s = jnp.einsum('bqhd,bmhd->bhqm', mq, mkv[:, :, 0]).astype(jnp.float32) * (HEAD_DIM ** -0.5)
    p = jax.nn.softmax(s, axis=-1)
    o = jnp.einsum('bhqm,bmhd->bqhd', p.astype(mkv.dtype), mkv[:, :, 1])
    return o.reshape(B, S, MEM_Q_COLS)


def _moe(h, w_router, b_router, w_up, b_up, w_down, b_down):
    T = h.shape[0]
    A = T * TOP_K
    logits = (h @ w_router).astype(jnp.float32) + b_router.astype(jnp.float32)
    top_val, top_idx = lax.top_k(logits, TOP_K)
    gate = jax.nn.softmax(top_val, axis=-1).reshape(-1)
    e_flat = top_idx.reshape(-1)
    order = jnp.argsort(e_flat)
    e_sorted = e_flat[order]
    counts = jnp.bincount(e_flat, length=N_EXPERTS)
    padded = (counts + MOE_BLK - 1) // MOE_BLK * MOE_BLK
    pad_end = jnp.cumsum(padded)
    pad_start = pad_end - padded
    start = jnp.cumsum(counts) - counts
    dest = pad_start[e_sorted] + jnp.arange(A) - start[e_sorted]
    n_blk = (A + N_EXPERTS * (MOE_BLK - 1) + MOE_BLK - 1) // MOE_BLK
    P = n_blk * MOE_BLK
    slot_tok = jnp.full((P,), T, jnp.int32).at[dest].set((order // TOP_K).astype(jnp.int32))
    slot_gate = jnp.zeros((P,), jnp.float32).at[dest].set(gate[order])
    blk_expert = jnp.minimum(jnp.searchsorted(pad_end, jnp.arange(n_blk) * MOE_BLK, side='right'), N_EXPERTS - 1)
    h_pad = jnp.concatenate([h, jnp.zeros((1, h.shape[1]), h.dtype)], axis=0)
    xb = h_pad[slot_tok].reshape(n_blk, MOE_BLK, h.shape[1])

    def expert_block(args):
        xblk, e = args
        up = xblk @ w_up[e] + b_up[e]
        x_glu = jnp.minimum(up[..., :D_FF], SWIGLU_LIMIT)
        x_lin = jnp.clip(up[..., D_FF:], -SWIGLU_LIMIT, SWIGLU_LIMIT)
        act = x_glu * jax.nn.sigmoid(SWIGLU_ALPHA * x_glu) * (x_lin + 1.0)
        return act @ w_down[e] + b_down[e]

    yb = lax.map(expert_block, (xb, blk_expert)).reshape(P, h.shape[1])
    yb = yb * slot_gate[:, None].astype(yb.dtype)
    return jax.ops.segment_sum(yb, slot_tok, num_segments=T + 1)[:T]


def setup_inputs(seed: int = 0) -> dict:
    key = jax.random.key(seed)
    keys = iter(jax.random.split(key, 48))
    f32 = jnp.float32
    n_pages = PAST_LEN // PAGE_SIZE
    n_phys = (DEC_BATCH * n_pages * 5) // 4
    win_buf = min(WINDOW, PAST_LEN)

    def normal(shape, scale):
        return jax.random.normal(next(keys), shape, f32) * scale

    def gain(shape):
        return 1.0 + normal(shape, 0.02)

    perm = jax.random.permutation(next(keys), n_phys)
    page_table = perm[:DEC_BATCH * n_pages].reshape(DEC_BATCH, n_pages).astype(jnp.int32)
    L = DEPTH
    return {
        'x_prompt': normal((BATCH, SEQ, D_MODEL), 1.0),
        'x_sample': normal((DEC_BATCH, DEC_SEQ, D_MODEL), 1.0),
        'mem_prompt': normal((BATCH, N_MEM, D_MODEL), 1.0),
        'cache_cmp_kv': normal((L, n_phys, PAGE_SIZE, 2, NSA_GROUPS, HEAD_DIM), 1.0),
        'cache_slc_kv': normal((L, n_phys, PAGE_SIZE, 2, NSA_GROUPS, HEAD_DIM), 1.0),
        'cache_win_kv': normal((L, DEC_BATCH, win_buf, 2, NSA_GROUPS, HEAD_DIM), 1.0),
        'state_ret': normal((L, DEC_BATCH, RET_HEADS, RET_DK, RET_DV), 1.0),
        'cache_mem_kv': normal((L, DEC_BATCH, N_MEM, 2, MEM_HEADS, HEAD_DIM), 1.0),
        'page_table': page_table,
        'norm1_g': gain((L, D_MODEL)),
        'w_in': normal((L, D_MODEL, IN_COLS), D_MODEL ** -0.5),
        'q_norm_g': gain((L, HEAD_DIM)),
        'k_norm_cmp_g': gain((L, HEAD_DIM)),
        'k_norm_slc_g': gain((L, HEAD_DIM)),
        'k_norm_win_g': gain((L, HEAD_DIM)),
        'cmp_pe_k': normal((L, CMP_BLK, HEAD_DIM), 0.1),
        'cmp_w1_k': normal((L, CMP_BLK * HEAD_DIM, CMP_HID), (CMP_BLK * HEAD_DIM) ** -0.5),
        'cmp_w2_k': normal((L, CMP_HID, HEAD_DIM), CMP_HID ** -0.5),
        'cmp_pe_v': normal((L, CMP_BLK, HEAD_DIM), 0.1),
        'cmp_w1_v': normal((L, CMP_BLK * HEAD_DIM, CMP_HID), (CMP_BLK * HEAD_DIM) ** -0.5),
        'cmp_w2_v': normal((L, CMP_HID, HEAD_DIM), CMP_HID ** -0.5),
        'ret_norm_g': gain((L, RET_DV)),
        'mem_norm_g': gain((L, D_MODEL)),
        'w_mem_kv': normal((L, D_MODEL, 2 * MEM_HEADS * HEAD_DIM), D_MODEL ** -0.5),
        'mem_q_norm_g': gain((L, HEAD_DIM)),
        'mem_k_norm_g': gain((L, HEAD_DIM)),
        'w_out': normal((L, MIX_WIDTH, D_MODEL), MIX_WIDTH ** -0.5),
        'norm2_g': gain((L, D_MODEL)),
        'w_router': normal((L, D_MODEL, N_EXPERTS), D_MODEL ** -0.5),
        'b_router': normal((L, N_EXPERTS), 0.01),
        'w_up': normal((L, N_EXPERTS, D_MODEL, 2 * D_FF), D_MODEL ** -0.5),
        'b_up': normal((L, N_EXPERTS, 2 * D_FF), 0.01),
        'w_down': normal((L, N_EXPERTS, D_FF, D_MODEL), D_FF ** -0.5),
        'b_down': normal((L, N_EXPERTS, D_MODEL), 0.01),
    }


def reference(x_prompt, x_sample, mem_prompt, cache_cmp_kv, cache_slc_kv, cache_win_kv, state_ret,
              cache_mem_kv, page_table, norm1_g, w_in, q_norm_g, k_norm_cmp_g, k_norm_slc_g, k_norm_win_g,
              cmp_pe_k, cmp_w1_k, cmp_w2_k, cmp_pe_v, cmp_w1_v, cmp_w2_v, ret_norm_g, mem_norm_g, w_mem_kv,
              mem_q_norm_g, mem_k_norm_g, w_out, norm2_g, w_router, b_router, w_up, b_up, w_down, b_down):
    B, S, D = x_prompt.shape
    DB, QS, _ = x_sample.shape
    past = page_table.shape[1] * cache_cmp_kv.shape[2]
    pos_p = jnp.arange(S, dtype=jnp.int32)
    pos_s = past + jnp.arange(QS, dtype=jnp.int32)
    xp, xs = x_prompt, x_sample
    cmp_p, slc_p, win_p, ret_p, mem_p = [], [], [], [], []
    cmp_s, slc_s, win_s, ret_s = [], [], [], []
    for l in range(DEPTH):
        cmp_w = (cmp_pe_k[l], cmp_w1_k[l], cmp_w2_k[l], cmp_pe_v[l], cmp_w1_v[l], cmp_w2_v[l], k_norm_cmp_g[l])
        q, kvc, kvs, kvw, gts, rq, rk, rv, rg, mq = _project(
            xp, pos_p, norm1_g[l], w_in[l], q_norm_g[l], k_norm_slc_g[l], k_norm_win_g[l], mem_q_norm_g[l])
        o_nsa = _nsa_prompt(q, gts, kvc, kvs, kvw, cmp_w)
        o_ret, r_new = _retention_prompt(rq, rk, rv)
        mkv = _mem_kv(mem_prompt, mem_norm_g[l], w_mem_kv[l], mem_k_norm_g[l])
        o_mem = _mem_attend(mq, mkv)
        mix = jnp.concatenate([o_nsa, _ret_out(o_ret, rg, ret_norm_g[l]).astype(xp.dtype),
                               o_mem.astype(xp.dtype)], axis=-1)
        xp = xp + mix @ w_out[l]
        cmp_p.append(kvc)
        slc_p.append(kvs)
        win_p.append(kvw[:, S - min(WINDOW, S):])
        ret_p.append(r_new)
        mem_p.append(mkv)
        q, kvc, kvs, kvw, gts, rq, rk, rv, rg, mq = _project(
            xs, pos_s, norm1_g[l], w_in[l], q_norm_g[l], k_norm_slc_g[l], k_norm_win_g[l], mem_q_norm_g[l])
        o_nsa, win_new = _nsa_sample(q, gts, kvc, kvs, kvw, cache_cmp_kv[l], cache_slc_kv[l],
                                     cache_win_kv[l], page_table, cmp_w)
        o_ret, r_new = _ret_chunk(state_ret[l].astype(jnp.float32), rq, rk, rv.astype(jnp.float32))
        o_mem = _mem_attend(mq, cache_mem_kv[l])
        mix = jnp.concatenate([o_nsa, _ret_out(o_ret, rg, ret_norm_g[l]).astype(xs.dtype),
                               o_mem.astype(xs.dtype)], axis=-1)
        xs = xs + mix @ w_out[l]
        cmp_s.append(kvc)
        slc_s.append(kvs)
        win_s.append(win_new)
        ret_s.append(r_new)
        h_all = jnp.concatenate([_rms(xp, norm2_g[l]).reshape(B * S, D),
                                 _rms(xs, norm2_g[l]).reshape(DB * QS, D)], axis=0)
        y_all = _moe(h_all, w_router[l], b_router[l], w_up[l], b_up[l], w_down[l], b_down[l])
        xp = xp + y_all[:B * S].reshape(B, S, D)
        xs = xs + y_all[B * S:].reshape(DB, QS, D)
    return (xp, xs, jnp.stack(cmp_p), jnp.stack(slc_p), jnp.stack(win_p), jnp.stack(ret_p), jnp.stack(mem_p),
            jnp.stack(cmp_s), jnp.stack(slc_s), jnp.stack(win_s), jnp.stack(ret_s))
```

```python
import functools

import jax
import jax.numpy as jnp
from jax import lax
from jax.experimental import pallas as pl
from jax.experimental.pallas import tpu as pltpu

F32 = jnp.float32
BF16 = jnp.bfloat16

HEAD_DIM = 64
NSA_HEADS = 8
NSA_GROUPS = 2
NSA_HPG = NSA_HEADS // NSA_GROUPS
RET_HEADS = 4
MEM_HEADS = 4
CMP_BLK = 32
CMP_STRIDE = 16
CMP_HID = 256
SLC_BLK = 64
N_SEL = 16
WINDOW = 512
Q_BLK = 128
RET_CHUNK = 128
N_EXPERTS = 32
TOP_K = 4
D_FF = 1024
SWIGLU_LIMIT = 7.0
SWIGLU_ALPHA = 1.702
EPS = 1e-6
NEG = -1e30
BIG = 1e9
ROPE_BASE = 10000.0
SLC_RATIO = SLC_BLK // CMP_STRIDE
CMP_OVL = CMP_BLK // CMP_STRIDE - 1

LANES = 128
KEY_TILE = 512
MOE_ROWS = 512
VMEM_LIMIT = 56 * 1024 * 1024

C_Q = 0
C_KVC = C_Q + NSA_HEADS * LANES
C_KVS = C_KVC + 256
C_KVW = C_KVS + 256
C_GATE = C_KVW + 256
C_RQ = C_GATE + LANES
C_RK = C_RQ + 256
C_RV = C_RK + 256
C_RG = C_RV + 256
C_MQ = C_RG + 256
C_END = C_MQ + 256


def _cparams(sem):
    return pltpu.CompilerParams(dimension_semantics=sem, vmem_limit_bytes=VMEM_LIMIT)


def _lane(shape):
    return lax.broadcasted_iota(jnp.int32, shape, len(shape) - 1)


def _seg_rms(x, gain):
    lo = _lane(x.shape) < HEAD_DIM
    x2 = x * x
    s_lo = jnp.sum(jnp.where(lo, x2, 0.0), axis=-1, keepdims=True)
    s_hi = jnp.sum(jnp.where(lo, 0.0, x2), axis=-1, keepdims=True)
    ms = jnp.where(lo, s_lo, s_hi) * (1.0 / HEAD_DIM)
    return x * lax.rsqrt(ms + EPS) * gain


def _swap_halves(x):
    first = (_lane(x.shape) & (HEAD_DIM // 2)) == 0
    return jnp.where(first, pltpu.roll(x, LANES - HEAD_DIM // 2, 1), pltpu.roll(x, HEAD_DIM // 2, 1))


def _proj_kernel(x_ref, g1_ref, w_ref, cos_ref, sin_ref, qg_ref, ksg_ref, kwg_ref, mqg_ref,
                 q_out, kvc_out, kvs_out, kvw_out, ks_bf, kw_bf, gate_out,
                 rq_out, rk_out, rv_out, rg_out, mq_out):
    x = x_ref[...]
    ms = jnp.mean(x * x, axis=-1, keepdims=True)
    xn = (x * lax.rsqrt(ms + EPS) * g1_ref[...]).astype(BF16)
    z = jnp.dot(xn, w_ref[...], preferred_element_type=F32)
    for i in range(NSA_HEADS):
        zq = z[:, C_Q + i * LANES:C_Q + (i + 1) * LANES]
        msq = jnp.sum(zq * zq, axis=-1, keepdims=True) * (1.0 / HEAD_DIM)
        qn = zq * lax.rsqrt(msq + EPS) * qg_ref[:, i * LANES:(i + 1) * LANES]
        q_out[:, i * LANES:(i + 1) * LANES] = (qn * (HEAD_DIM ** -0.5)).astype(BF16)
    kvc_out[...] = z[:, C_KVC:C_KVC + 256]
    ks = _seg_rms(z[:, C_KVS:C_KVS + LANES], ksg_ref[...])
    vs = z[:, C_KVS + LANES:C_KVS + 256]
    kvs_out[:, 0:LANES] = ks
    kvs_out[:, LANES:256] = vs
    ks_bf[:, 0:LANES] = ks.astype(BF16)
    ks_bf[:, LANES:256] = vs.astype(BF16)
    kw = _seg_rms(z[:, C_KVW:C_KVW + LANES], kwg_ref[...])
    vw = z[:, C_KVW + LANES:C_KVW + 256]
    kvw_out[:, 0:LANES] = kw
    kvw_out[:, LANES:256] = vw
    kw_bf[:, 0:LANES] = kw.astype(BF16)
    kw_bf[:, LANES:256] = vw.astype(BF16)
    gate_out[...] = jax.nn.sigmoid(z[:, C_GATE:C_GATE + LANES])
    for c in range(2):
        sl = slice(c * LANES, (c + 1) * LANES)
        cos = cos_ref[:, sl]
        sin = sin_ref[:, sl]
        rq = z[:, C_RQ + c * LANES:C_RQ + (c + 1) * LANES]
        rk = z[:, C_RK + c * LANES:C_RK + (c + 1) * LANES]
        rq_out[:, sl] = rq * cos + _swap_halves(rq) * sin
        rk_out[:, sl] = (rk * cos + _swap_halves(rk) * sin) * (HEAD_DIM ** -0.5)
        mq = z[:, C_MQ + c * LANES:C_MQ + (c + 1) * LANES]
        mq_out[:, sl] = _seg_rms(mq, mqg_ref[...]).astype(BF16)
    rv_out[...] = z[:, C_RV:C_RV + 256]
    rg_out[...] = z[:, C_RG:C_RG + 256]


def _project(x, pw, cos, sin, tm):
    T, D = x.shape
    nt = cos.shape[0] // tm
    row = lambda n: pl.BlockSpec((tm, n), lambda i: (i, 0))
    full = lambda a: pl.BlockSpec(a.shape, lambda i: (0, 0))
    tab = pl.BlockSpec((tm, 256), lambda i: (i % nt, 0))
    outs = [
        (1024, BF16),
        (256, F32), (256, F32), (256, F32),
        (256, BF16), (256, BF16),
        (LANES, F32),
        (256, F32), (256, F32), (256, F32), (256, F32),
        (256, BF16),
    ]
    return pl.pallas_call(
        _proj_kernel,
        grid=(T // tm,),
        in_specs=[row(D), full(pw['g1']), full(pw['w']), tab, tab,
                  full(pw['qg']), full(pw['ksg']), full(pw['kwg']), full(pw['mqg'])],
        out_specs=[row(n) for n, _ in outs],
        out_shape=[jax.ShapeDtypeStruct((T, n), dt) for n, dt in outs],
        compiler_params=_cparams(("arbitrary",)),
        name="proj",
    )(x, pw['g1'], pw['w'], cos, sin, pw['qg'], pw['ksg'], pw['kwg'], pw['mqg'])


def _prep_proj_weights(norm1_g, w_in, q_norm_g, k_norm_slc_g, k_norm_win_g, mem_q_norm_g):
    D = w_in.shape[0]
    sizes = (512, 256, 256, 256, 24, 256, 256, 256, 256, 256)
    parts, off = [], 0
    for n in sizes:
        parts.append(w_in[:, off:off + n])
        off += n
    wq, wkvc, wkvs, wkvw, wg, wrq, wrk, wrv, wrg, wmq = parts
    zero = jnp.zeros((D, HEAD_DIM), w_in.dtype)
    qcols, qg = [], []
    gz = jnp.zeros((HEAD_DIM,), F32)
    for i in range(NSA_HEADS):
        wh = wq[:, i * HEAD_DIM:(i + 1) * HEAD_DIM]
        if i < NSA_HPG:
            qcols += [wh, zero]
            qg += [q_norm_g, gz]
        else:
            qcols += [zero, wh]
            qg += [gz, q_norm_g]
    wgp = jnp.pad(wg, ((0, 0), (0, LANES - wg.shape[1])))
    w = jnp.concatenate(qcols + [wkvc, wkvs, wkvw, wgp, wrq, wrk, wrv, wrg, wmq], axis=1).astype(BF16)
    two = lambda g: jnp.tile(g, 2)[None, :]
    return dict(g1=norm1_g[None, :], w=w, qg=jnp.concatenate(qg)[None, :],
                ksg=two(k_norm_slc_g), kwg=two(k_norm_win_g), mqg=two(mem_q_norm_g))


def _rope_tables(pos):
    half = HEAD_DIM // 2
    inv = ROPE_BASE ** (-jnp.arange(half, dtype=F32) / half)
    ang = pos.astype(F32)[:, None] * inv[None, :]
    cos, sin = jnp.cos(ang), jnp.sin(ang)
    cos = jnp.tile(jnp.concatenate([cos, cos], axis=1), (1, RET_HEADS))
    sin = jnp.tile(jnp.concatenate([-sin, sin], axis=1), (1, RET_HEADS))
    return cos, sin


def _compress_half(src_ref, pe_ref, wa_ref, wb_ref, w2_ref, nchunks):
    a_parts, b_parts = [], []
    for r in range(CMP_STRIDE):
        xr = src_ref[pl.ds(r, nchunks, stride=CMP_STRIDE), :]
        a_parts.append((xr + pe_ref[r:r + 1, :]).astype(BF16))
        b_parts.append((xr + pe_ref[CMP_STRIDE + r:CMP_STRIDE + r + 1, :]).astype(BF16))
    ha = jnp.dot(jnp.concatenate(a_parts, axis=1), wa_ref[...], preferred_element_type=F32)
    hb = jnp.dot(jnp.concatenate(b_parts, axis=1), wb_ref[...], preferred_element_type=F32)
    h = ha + pltpu.roll(hb, nchunks - 1, 0)
    act = jax.nn.gelu(h).astype(BF16)
    return jnp.dot(act, w2_ref[...], preferred_element_type=F32)


def _compress_kernel(kv_ref, pe_ref, wa_ref, wb_ref, w2_ref, kg_ref, out_ref):
    nchunks = out_ref.shape[2]
    y = _compress_half(kv_ref.at[0], pe_ref.at[0], wa_ref.at[0], wb_ref.at[0], w2_ref.at[0], nchunks)
    is_k = pl.program_id(1) == 0
    out_ref[0, 0] = jnp.where(is_k, _seg_rms(y, kg_ref[...]), y).astype(BF16)


def _compress(kvc, cw):
    B, T, _ = kvc.shape
    nchunks = T // CMP_STRIDE
    per_kv = lambda a: pl.BlockSpec((1,) + a.shape[1:], lambda b, j: (j,) + (0,) * (a.ndim - 1))
    return pl.pallas_call(
        _compress_kernel,
        grid=(B, 2),
        in_specs=[pl.BlockSpec((1, T, LANES), lambda b, j: (b, 0, j)),
                  per_kv(cw['pe']), per_kv(cw['wa']), per_kv(cw['wb']), per_kv(cw['w2']),
                  pl.BlockSpec((1, LANES), lambda b, j: (0, 0))],
        out_specs=pl.BlockSpec((1, 1, nchunks, LANES), lambda b, j: (b, j, 0, 0)),
        out_shape=jax.ShapeDtypeStruct((B, 2, nchunks, LANES), BF16),
        compiler_params=_cparams(("arbitrary", "arbitrary")),
        name="compress",
    )(kvc, cw['pe'], cw['wa'], cw['wb'], cw['w2'], cw['kg'])


def _prep_compress_weights(pe_k, w1_k, w2_k, pe_v, w1_v, w2_v, k_norm_g):
    eye = jnp.eye(NSA_GROUPS, dtype=F32)

    def one(pe, w1, w2):
        w1r = w1.reshape(CMP_BLK, HEAD_DIM, CMP_HID)
        wf = jnp.einsum('gh,rdc->rgdhc', eye, w1r).reshape(CMP_BLK * LANES, NSA_GROUPS * CMP_HID)
        w2f = jnp.einsum('gh,cd->gchd', eye, w2).reshape(NSA_GROUPS * CMP_HID, LANES)
        half = CMP_STRIDE * LANES
        return jnp.tile(pe, (1, NSA_GROUPS)), wf[:half].astype(BF16), wf[half:].astype(BF16), w2f.astype(BF16)

    k, v = one(pe_k, w1_k, w2_k), one(pe_v, w1_v, w2_v)
    st = lambda i: jnp.stack([k[i], v[i]])
    return dict(pe=st(0), wa=st(1), wb=st(2), w2=st(3), kg=jnp.tile(k_norm_g, 2)[None, :])


def _dot_nt(a, b):
    return lax.dot_general(a, b, (((1,), (1,)), ((), ())), preferred_element_type=F32)


def _split3(x):
    hi = x.astype(BF16)
    r = x - hi.astype(F32)
    mid = r.astype(BF16)
    lo = (r - mid.astype(F32)).astype(BF16)
    return hi, mid, lo


def _select_bias(ps, cur, rounds, axis, cur_in_range=True):
    j = lax.broadcasted_iota(jnp.int32, ps.shape, axis)
    last = cur if cur_in_range else cur - 1
    key = jnp.where(j <= last, ps, NEG)
    for forced in (0, last, cur - 1):
        key = jnp.where(j == forced, BIG, key)
    bias = jnp.full(ps.shape, NEG, F32)
    for _ in range(rounds):
        m = jnp.max(key, axis=axis, keepdims=True)
        idx = jnp.min(jnp.where(key == m, j, LANES), axis=axis, keepdims=True)
        pick = j == idx
        bias = jnp.where(pick, 0.0, bias)
        key = jnp.where(pick, -jnp.inf, key)
    return bias


def _cmp_branch(qg, qpos, kc, vc):
    s = _dot_nt(qg, kc)
    cend = _lane((1, kc.shape[0])) * CMP_STRIDE + (CMP_BLK - 1)
    s = jnp.where(cend <= qpos, s, NEG)
    m = jnp.max(s, axis=-1, keepdims=True)
    p = jnp.exp(s - m)
    p = p * (1.0 / jnp.sum(p, axis=-1, keepdims=True))
    p = jnp.where(qpos >= CMP_BLK - 1, p, 0.0)
    return p, jnp.dot(p.astype(BF16), vc, preferred_element_type=F32)


def _sel_tile(qaug, k_tile, v_tile, blk0, mask, m_sc, l_sc, acc_sc):
    n = k_tile.shape[0]
    blk = blk0 + lax.broadcasted_iota(jnp.int32, (n, LANES), 0) // SLC_BLK
    onehot = jnp.where(_lane((n, LANES)) == blk, 1.0, 0.0).astype(BF16)
    s = _dot_nt(qaug, jnp.concatenate([k_tile, onehot], axis=1))
    if mask is not None:
        s = jnp.where(mask, s, NEG)
    _online_update(s, v_tile, m_sc, l_sc, acc_sc)


def _online_update(s, v_tile, m_sc, l_sc, acc_sc):
    m_old = m_sc[...]
    m_new = jnp.maximum(m_old, jnp.max(s, axis=-1, keepdims=True))
    alpha = jnp.exp(m_old - m_new)
    p = jnp.exp(s - m_new)
    l_sc[...] = alpha * l_sc[...] + jnp.sum(p, axis=-1, keepdims=True)
    acc_sc[...] = alpha * acc_sc[...] + jnp.dot(p.astype(BF16), v_tile, preferred_element_type=F32)
    m_sc[...] = m_new


def _gate_heads(gates, g, oc, osel, ow, nq):
    heads = []
    for h in range(NSA_HPG):
        rs = slice(h * nq, (h + 1) * nq)
        c0 = (NSA_HPG * g + h) * 3
        heads.append(gates[:, c0:c0 + 1] * oc[rs] + gates[:, c0 + 1:c0 + 2] * osel[rs]
                     + gates[:, c0 + 2:c0 + 3] * ow[rs])
    return heads


def _store_heads(heads, o_ref):
    low = _lane(heads[0].shape) < HEAD_DIM
    for pr in range(NSA_HEADS // 2):
        even, odd = heads[2 * pr], heads[2 * pr + 1]
        if pr < NSA_GROUPS:
            odd = pltpu.roll(odd, HEAD_DIM, 1)
        else:
            even = pltpu.roll(even, HEAD_DIM, 1)
        o_ref[0, :, pr * LANES:(pr + 1) * LANES] = jnp.where(low, even, odd).astype(o_ref.dtype)


def _nsa_sample_kernel(pt_ref, q_ref, gate_ref, ksn_ref, kwn_ref, win_ref, pool_c, pool_s,
                       pe_ref, wa_ref, wb_ref, w2_ref, kg_ref, mmap_ref, o_ref,
                       cbuf, sbuf, sems, m_sc, l_sc, acc_sc, *, page, n_new):
    b = pl.program_id(0)
    past = sbuf.shape[0]
    npages = past // page
    nq = q_ref.shape[1]

    def page_copies(p, pg):
        dst = pl.ds(pl.multiple_of(p * page, page), page)
        return (pltpu.make_async_copy(pool_c.at[pg, :, 0:LANES], cbuf.at[0, dst], sems.at[0]),
                pltpu.make_async_copy(pool_c.at[pg, :, LANES:256], cbuf.at[1, dst], sems.at[1]),
                pltpu.make_async_copy(pool_s.at[pg], sbuf.at[dst], sems.at[2]))

    def start(p, c):
        for cp in page_copies(p, pt_ref[b * npages + p]):
            cp.start()
        return c

    def wait(p, c):
        for cp in page_copies(p, 0):
            cp.wait()
        return c

    lax.fori_loop(0, npages, start, 0)
    lax.fori_loop(0, npages, wait, 0)

    nchunks = past // CMP_STRIDE
    kc = _seg_rms(_compress_half(cbuf.at[0], pe_ref.at[0], wa_ref.at[0], wb_ref.at[0], w2_ref.at[0], nchunks),
                  kg_ref[...]).astype(BF16)
    vc = _compress_half(cbuf.at[1], pe_ref.at[1], wa_ref.at[1], wb_ref.at[1], w2_ref.at[1], nchunks).astype(BF16)

    rows = NSA_HPG * nq
    srow = lax.broadcasted_iota(jnp.int32, (rows, 1), 0) % nq
    qpos = past + srow
    cur = (past + lax.broadcasted_iota(jnp.int32, (nq, 1), 0)) // SLC_BLK
    gates = gate_ref[0]
    pad_keys = lambda x: jnp.concatenate([x, jnp.zeros((LANES - nq, LANES), F32)], axis=0).astype(BF16)
    new_ok = _lane((1, LANES)) <= jnp.minimum(srow, n_new - 1)
    heads = []
    for g in range(NSA_GROUPS):
        qg = jnp.concatenate(
            [q_ref[0, :, (NSA_HPG * g + h) * LANES:(NSA_HPG * g + h + 1) * LANES] for h in range(NSA_HPG)],
            axis=0).astype(BF16)
        p, oc = _cmp_branch(qg, qpos, kc, vc)
        p4 = p[0:nq] + p[nq:2 * nq] + p[2 * nq:3 * nq] + p[3 * nq:4 * nq]
        ps = sum(_dot_nt(part, mmap_ref[...]) for part in _split3(p4))
        bias = _select_bias(ps, cur, N_SEL - 1, 1, cur_in_range=False)
        qaug = jnp.concatenate([qg, jnp.concatenate([bias] * NSA_HPG, axis=0).astype(BF16)], axis=1)
        m_sc[...] = jnp.full(m_sc.shape, NEG, F32)
        l_sc[...] = jnp.zeros(l_sc.shape, F32)
        acc_sc[...] = jnp.zeros(acc_sc.shape, F32)

        def body(kt, c, qaug=qaug):
            k0 = pl.multiple_of(kt * KEY_TILE, KEY_TILE)
            _sel_tile(qaug, sbuf[pl.ds(k0, KEY_TILE), 0:LANES].astype(BF16),
                      sbuf[pl.ds(k0, KEY_TILE), LANES:256].astype(BF16),
                      kt * (KEY_TILE // SLC_BLK), None, m_sc, l_sc, acc_sc)
            return c

        lax.fori_loop(0, past // KEY_TILE, body, 0)
        s_new = jnp.where(new_ok, _dot_nt(qg, pad_keys(ksn_ref[0, :, 0:LANES])), NEG)
        _online_update(s_new, pad_keys(ksn_ref[0, :, LANES:256]), m_sc, l_sc, acc_sc)
        osel = acc_sc[...] * (1.0 / l_sc[...])
        nwin = win_ref.shape[1]
        sw_old = jnp.where(_lane((1, nwin)) + (WINDOW - nwin) > srow,
                           _dot_nt(qg, win_ref[0, :, 0:LANES].astype(BF16)), NEG)
        sw_new = jnp.where(new_ok, _dot_nt(qg, pad_keys(kwn_ref[0, :, 0:LANES])), NEG)
        sw = jnp.concatenate([sw_old, sw_new], axis=1)
        mw = jnp.max(sw, axis=-1, keepdims=True)
        pw = jnp.exp(sw - mw)
        ow = (jnp.dot(pw[:, :nwin].astype(BF16), win_ref[0, :, LANES:256].astype(BF16), preferred_element_type=F32)
              + jnp.dot(pw[:, nwin:].astype(BF16), pad_keys(kwn_ref[0, :, LANES:256]), preferred_element_type=F32))
        ow = ow * (1.0 / jnp.sum(pw, axis=-1, keepdims=True))
        heads += _gate_heads(gates, g, oc, osel, ow, nq)
    _store_heads(heads, o_ref)


def _nsa_sample(q, gates, ksn, kwn, win, pool_c, pool_s, page_table, cw, n_new):
    DB, nq, _ = q.shape
    n_pages = page_table.shape[1]
    page = pool_c.shape[1]
    past = n_pages * page
    nchunks = past // CMP_STRIDE
    rows = NSA_HPG * nq
    mmap = _importance_map(nchunks)
    per_seq = lambda a: pl.BlockSpec((1,) + a.shape[1:], lambda b, pt: (b,) + (0,) * (a.ndim - 1))
    const = lambda a: pl.BlockSpec(a.shape, lambda b, pt: (0,) * a.ndim)
    hbm = pl.BlockSpec(memory_space=pl.ANY)
    grid_spec = pltpu.PrefetchScalarGridSpec(
        num_scalar_prefetch=1,
        grid=(DB,),
        in_specs=[per_seq(q), per_seq(gates), per_seq(ksn), per_seq(kwn), per_seq(win), hbm, hbm,
                  const(cw['pe']), const(cw['wa']), const(cw['wb']), const(cw['w2']), const(cw['kg']), const(mmap)],
        out_specs=pl.BlockSpec((1, nq, NSA_HEADS * HEAD_DIM), lambda b, pt: (b, 0, 0)),
        scratch_shapes=[pltpu.VMEM((2, past, LANES), F32), pltpu.VMEM((past, 256), F32),
                        pltpu.SemaphoreType.DMA((3,)),
                        pltpu.VMEM((rows, 1), F32), pltpu.VMEM((rows, 1), F32), pltpu.VMEM((rows, LANES), F32)],
    )
    return pl.pallas_call(
        functools.partial(_nsa_sample_kernel, page=page, n_new=n_new),
        grid_spec=grid_spec,
        out_shape=jax.ShapeDtypeStruct((DB, nq, NSA_HEADS * HEAD_DIM), F32),
        compiler_params=_cparams(("arbitrary",)),
        name="nsa_sample",
    )(page_table.reshape(-1), q, gates, ksn, kwn, win, pool_c, pool_s,
      cw['pe'], cw['wa'], cw['wb'], cw['w2'], cw['kg'], mmap)


def _nsa_prompt_kernel(q_ref, gate_ref, kc_ref, vc_ref, ks_ref, kw_ref, mmap_ref, o_ref,
                       m_sc, l_sc, acc_sc):
    i = pl.program_id(1)
    s0 = i * Q_BLK
    rows = NSA_HPG * Q_BLK
    qpos = s0 + lax.broadcasted_iota(jnp.int32, (rows, 1), 0) % Q_BLK
    cur = (s0 + _lane((1, Q_BLK))) // SLC_BLK
    kc, vc = kc_ref[0, 0], vc_ref[0, 0]
    gates = gate_ref[0]
    n_full = s0 // KEY_TILE
    w0 = pl.multiple_of(jnp.maximum(s0 - WINDOW, 0), Q_BLK)
    heads = []
    for g in range(NSA_GROUPS):
        qg = jnp.concatenate(
            [q_ref[0, :, (NSA_HPG * g + h) * LANES:(NSA_HPG * g + h + 1) * LANES] for h in range(NSA_HPG)], axis=0)
        p, oc = _cmp_branch(qg, qpos, kc, vc)
        p4 = p[0:Q_BLK] + p[Q_BLK:2 * Q_BLK] + p[2 * Q_BLK:3 * Q_BLK] + p[3 * Q_BLK:4 * Q_BLK]
        ps_t = sum(_dot_nt(mmap_ref[...], part) for part in _split3(p4))
        bias = _select_bias(ps_t, cur, N_SEL, 0).T.astype(BF16)
        qaug = jnp.concatenate([qg, jnp.concatenate([bias] * NSA_HPG, axis=0)], axis=1)
        m_sc[...] = jnp.full(m_sc.shape, NEG, F32)
        l_sc[...] = jnp.zeros(l_sc.shape, F32)
        acc_sc[...] = jnp.zeros(acc_sc.shape, F32)

        def tile(kt, mask, qaug=qaug):
            k0 = pl.multiple_of(kt * KEY_TILE, KEY_TILE)
            _sel_tile(qaug, ks_ref[0, pl.ds(k0, KEY_TILE), 0:LANES], ks_ref[0, pl.ds(k0, KEY_TILE), LANES:256],
                      kt * (KEY_TILE // SLC_BLK), mask, m_sc, l_sc, acc_sc)

        def body(kt, c):
            tile(kt, None)
            return c

        lax.fori_loop(0, n_full, body, 0)
        tile(n_full, n_full * KEY_TILE + _lane((1, KEY_TILE)) <= qpos)
        osel = acc_sc[...] * (1.0 / l_sc[...])
        nw = WINDOW + Q_BLK
        sw = _dot_nt(qg, kw_ref[0, pl.ds(w0, nw), 0:LANES])
        d = qpos - (w0 + _lane((1, nw)))
        sw = jnp.where(d >= 0, jnp.where(d < WINDOW, sw, NEG), NEG)
        mw = jnp.max(sw, axis=-1, keepdims=True)
        pw = jnp.exp(sw - mw)
        ow = jnp.dot(pw.astype(BF16), kw_ref[0, pl.ds(w0, nw), LANES:256], preferred_element_type=F32)
        ow = ow * (1.0 / jnp.sum(pw, axis=-1, keepdims=True))
        heads += _gate_heads(gates, g, oc, osel, ow, Q_BLK)
    _store_heads(heads, o_ref)


def _importance_map(ncmp):
    j = jnp.arange(LANES)[:, None]
    n = jnp.arange(ncmp)[None, :]
    return ((n >= SLC_RATIO * j - CMP_OVL) & (n < SLC_RATIO * j + SLC_RATIO)).astype(BF16)


def _nsa_prompt(q, gates, kcv, ks, kw):
    B, S, _ = q.shape
    ncmp = kcv.shape[2]
    rows = NSA_HPG * Q_BLK
    mmap = _importance_map(ncmp)
    return pl.pallas_call(
        _nsa_prompt_kernel,
        grid=(B, S // Q_BLK),
        in_specs=[pl.BlockSpec((1, Q_BLK, q.shape[2]), lambda b, i: (b, i, 0)),
                  pl.BlockSpec((1, Q_BLK, LANES), lambda b, i: (b, i, 0)),
                  pl.BlockSpec((1, 1, ncmp, LANES), lambda b, i: (b, 0, 0, 0)),
                  pl.BlockSpec((1, 1, ncmp, LANES), lambda b, i: (b, 1, 0, 0)),
                  pl.BlockSpec((1, S, 256), lambda b, i: (b, 0, 0)),
                  pl.BlockSpec((1, S, 256), lambda b, i: (b, 0, 0)),
                  pl.BlockSpec(mmap.shape, lambda b, i: (0, 0))],
        out_specs=pl.BlockSpec((1, Q_BLK, NSA_HEADS * HEAD_DIM), lambda b, i: (b, i, 0)),
        out_shape=jax.ShapeDtypeStruct((B, S, NSA_HEADS * HEAD_DIM), BF16),
        scratch_shapes=[pltpu.VMEM((rows, 1), F32), pltpu.VMEM((rows, 1), F32), pltpu.VMEM((rows, LANES), F32)],
        compiler_params=_cparams(("arbitrary", "arbitrary")),
        name="nsa_prompt",
    )(q, gates, kcv, kcv, ks, kw, mmap)


def _ret_kernel(rq_ref, rk_ref, rv_ref, r0_ref, dmask_ref, xi_ref, zeta_ref, dec_ref, o_ref, rout_ref, r_sc):
    @pl.when(pl.program_id(1) == 0)
    def _():
        r_sc[...] = r0_ref[0]

    C = rq_ref.shape[1]
    low = _lane((C, LANES)) < HEAD_DIM
    diag = lax.broadcasted_iota(jnp.int32, (LANES, LANES), 0) // HEAD_DIM == _lane((LANES, LANES)) // HEAD_DIM
    for pr in range(RET_HEADS // 2):
        sl = slice(pr * LANES, (pr + 1) * LANES)
        k = rk_ref[0, :, sl]
        qb, kb, vb = rq_ref[0, :, sl].astype(BF16), k.astype(BF16), rv_ref[0, :, sl].astype(BF16)
        zero = jnp.zeros_like(qb)
        s0 = _dot_nt(jnp.where(low, qb, zero), kb) * dmask_ref[2 * pr]
        s1 = _dot_nt(jnp.where(low, zero, qb), kb) * dmask_ref[2 * pr + 1]
        o = jnp.where(low, jnp.dot(s0.astype(BF16), vb, preferred_element_type=F32),
                      jnp.dot(s1.astype(BF16), vb, preferred_element_type=F32))
        r = r_sc[pr]
        o_ref[0, :, sl] = o + jnp.dot(qb, r.astype(BF16), preferred_element_type=F32) * xi_ref[:, sl]
        kz = (k * zeta_ref[:, sl]).astype(BF16)
        upd = lax.dot_general(kz, vb, (((0,), (0,)), ((), ())), preferred_element_type=F32)
        r_sc[pr] = dec_ref[:, sl] * r + jnp.where(diag, upd, 0.0)
    rout_ref[0] = r_sc[...]


def _ret_tables(c_true, c_pad):
    lg = jnp.log(1.0 - 2.0 ** (-5.0 - jnp.arange(RET_HEADS, dtype=F32)))
    idx = jnp.arange(c_pad, dtype=F32)
    diff = idx[:, None] - idx[None, :]
    dmask = jnp.where(diff >= 0, jnp.exp(jnp.maximum(diff, 0.0)[None] * lg[:, None, None]), 0.0)
    lanes = lambda a: jnp.repeat(a, HEAD_DIM, axis=-1)
    xi = lanes(jnp.exp((idx + 1.0)[:, None] * lg[None, :]))
    zeta = lanes(jnp.exp((c_true - 1.0 - idx)[:, None] * lg[None, :]))
    dec = lanes(jnp.exp(c_true * lg)[None, :])
    return dmask, xi, zeta, dec


def _retention(rq, rk, rv, r0, c_true, c_pad):
    B, T, _ = rq.shape
    dmask, xi, zeta, dec = _ret_tables(c_true, c_pad)
    row = pl.BlockSpec((1, c_pad, 256), lambda b, c: (b, c, 0))
    st = pl.BlockSpec((1, 2, LANES, LANES), lambda b, c: (b, 0, 0, 0))
    const = lambda a: pl.BlockSpec(a.shape, lambda b, c: (0,) * a.ndim)
    return pl.pallas_call(
        _ret_kernel,
        grid=(B, T // c_pad),
        in_specs=[row, row, row, st, const(dmask), const(xi), const(zeta), const(dec)],
        out_specs=[row, st],
        out_shape=[jax.ShapeDtypeStruct((B, T, 256), F32), jax.ShapeDtypeStruct((B, 2, LANES, LANES), F32)],
        scratch_shapes=[pltpu.VMEM((2, LANES, LANES), F32)],
        compiler_params=_cparams(("arbitrary", "arbitrary")),
        name="retention",
    )(rq, rk, rv, r0, dmask, xi, zeta, dec)


def _state_to_pairs(r):
    B = r.shape[0]
    r = r.reshape(B, 2, 2, HEAD_DIM, HEAD_DIM)
    eye = jnp.eye(2, dtype=r.dtype)
    return jnp.einsum('bphde,hk->bphdke', r, eye).reshape(B, 2, LANES, LANES)


def _pairs_to_state(rp):
    B = rp.shape[0]
    rp = rp.reshape(B, 2, 2, HEAD_DIM, 2, HEAD_DIM)
    return jnp.stack([rp[:, :, 0, :, 0, :], rp[:, :, 1, :, 1, :]], axis=2).reshape(B, RET_HEADS, HEAD_DIM, HEAD_DIM)


def _mem_kv_kernel(m_ref, g_ref, w_ref, kg_ref, kv_out):
    x = m_ref[...]
    ms = jnp.mean(x * x, axis=-1, keepdims=True)
    xn = (x * lax.rsqrt(ms + EPS) * g_ref[...]).astype(BF16)
    z = jnp.dot(xn, w_ref[...], preferred_element_type=F32)
    for c in range(2):
        kv_out[:, c * LANES:(c + 1) * LANES] = _seg_rms(z[:, c * LANES:(c + 1) * LANES], kg_ref[...])
    kv_out[:, 256:512] = z[:, 256:512]


def _mem_kv(mem, mem_norm_g, w_mem_kv, mem_k_norm_g):
    T, D = mem.shape
    w = w_mem_kv.astype(BF16)
    full = lambda a: pl.BlockSpec(a.shape, lambda i: (0, 0))
    g = mem_norm_g[None, :]
    kg = jnp.tile(mem_k_norm_g, 2)[None, :]
    return pl.pallas_call(
        _mem_kv_kernel,
        grid=(1,),
        in_specs=[full(mem), full(g), full(w), full(kg)],
        out_specs=pl.BlockSpec((T, 512), lambda i: (0, 0)),
        out_shape=jax.ShapeDtypeStruct((T, 512), F32),
        compiler_params=_cparams(("arbitrary",)),
        name="mem_kv",
    )(mem, g, w, kg)


def _mem_attn_kernel(mq_ref, mkv_ref, o_ref):
    q = mq_ref[0]
    rows = q.shape[0]
    low = _lane((rows, LANES)) < HEAD_DIM
    zero = jnp.zeros((rows, LANES), F32)
    for pr in range(MEM_HEADS // 2):
        sl = slice(pr * LANES, (pr + 1) * LANES)
        qp = q[:, sl].astype(F32)
        k = mkv_ref[0, :, sl].astype(BF16)
        v = mkv_ref[0, :, 256 + pr * LANES:256 + (pr + 1) * LANES].astype(BF16)
        outs = []
        for hh in range(2):
            qm = jnp.where(low, qp, zero) if hh == 0 else jnp.where(low, zero, qp)
            s = _dot_nt(qm.astype(BF16), k) * (HEAD_DIM ** -0.5)
            m = jnp.max(s, axis=-1, keepdims=True)
            p = jnp.exp(s - m)
            o = jnp.dot(p.astype(BF16), v, preferred_element_type=F32)
            outs.append(o * (1.0 / jnp.sum(p, axis=-1, keepdims=True)))
        o_ref[0, :, sl] = jnp.where(low, outs[0], outs[1]).astype(o_ref.dtype)


def _mem_attn(mq, mkv, tm):
    B, R, _ = mq.shape
    return pl.pallas_call(
        _mem_attn_kernel,
        grid=(B, R // tm),
        in_specs=[pl.BlockSpec((1, tm, 256), lambda b, i: (b, i, 0)),
                  pl.BlockSpec((1,) + mkv.shape[1:], lambda b, i: (b, 0, 0))],
        out_specs=pl.BlockSpec((1, tm, 256), lambda b, i: (b, i, 0)),
        out_shape=jax.ShapeDtypeStruct((B, R, 256), BF16),
        compiler_params=_cparams(("arbitrary", "arbitrary")),
        name="mem_attn",
    )(mq, mkv)


def _mix_kernel(x_ref, onsa_ref, oret_ref, rg_ref, omem_ref, wout_ref, rgain_ref, g2_ref,
                wr_hi_ref, wr_lo_ref, br_ref, x1_out, h_out, topi_out, topg_out):
    parts = [onsa_ref[...]]
    for c in range(2):
        sl = slice(c * LANES, (c + 1) * LANES)
        parts.append((_seg_rms(oret_ref[:, sl], rgain_ref[...]) * jax.nn.silu(rg_ref[:, sl])).astype(BF16))
    parts.append(omem_ref[...])
    mix = jnp.concatenate(parts, axis=1)
    x1 = x_ref[...] + jnp.dot(mix, wout_ref[...], preferred_element_type=F32)
    x1_out[...] = x1
    ms = jnp.mean(x1 * x1, axis=-1, keepdims=True)
    h = x1 * lax.rsqrt(ms + EPS) * g2_ref[...]
    h_out[...] = h.astype(BF16)
    h_hi, h_lo, _ = _split3(h)
    logits = (jnp.dot(h_hi, wr_hi_ref[...], preferred_element_type=F32)
              + jnp.dot(h_hi, wr_lo_ref[...], preferred_element_type=F32)
              + jnp.dot(h_lo, wr_hi_ref[...], preferred_element_type=F32)) + br_ref[...]
    lane = _lane(logits.shape)
    key = logits
    topi = jnp.zeros(logits.shape, jnp.int32)
    topv = jnp.zeros(logits.shape, F32)
    for r in range(TOP_K):
        m = jnp.max(key, axis=-1, keepdims=True)
        idx = jnp.min(jnp.where(key == m, lane, LANES), axis=-1, keepdims=True)
        if r == 0:
            m0 = m
        topi = jnp.where(lane == r, idx, topi)
        topv = jnp.where(lane == r, jnp.exp(m - m0), topv)
        key = jnp.where(lane == idx, -jnp.inf, key)
    topi_out[...] = topi
    topg_out[...] = topv * (1.0 / jnp.sum(topv, axis=-1, keepdims=True))


def _mix(x, onsa, oret, rg, omem, mw, tm):
    T, D = x.shape
    row = lambda n: pl.BlockSpec((tm, n), lambda i: (i, 0))
    full = lambda a: pl.BlockSpec(a.shape, lambda i: (0, 0))
    names = ('wout', 'rgain', 'g2', 'wr_hi', 'wr_lo', 'br')
    return pl.pallas_call(
        _mix_kernel,
        grid=(T // tm,),
        in_specs=[row(D), row(512), row(256), row(256), row(256)] + [full(mw[n]) for n in names],
        out_specs=[row(D), row(D), row(LANES), row(LANES)],
        out_shape=[jax.ShapeDtypeStruct((T, D), F32), jax.ShapeDtypeStruct((T, D), BF16),
                   jax.ShapeDtypeStruct((T, LANES), jnp.int32), jax.ShapeDtypeStruct((T, LANES), F32)],
        compiler_params=_cparams(("arbitrary",)),
        name="mix",
    )(x, onsa, oret, rg, omem, *[mw[n] for n in names])


def _prep_mix_weights(w_out, ret_norm_g, norm2_g, w_router, b_router):
    wr = jnp.pad(w_router, ((0, 0), (0, LANES - N_EXPERTS)))
    wr_hi = wr.astype(BF16)
    wr_lo = (wr - wr_hi.astype(F32)).astype(BF16)
    br = jnp.concatenate([b_router.astype(F32), jnp.full((LANES - N_EXPERTS,), NEG, F32)])[None, :]
    return dict(wout=w_out.astype(BF16), rgain=jnp.tile(ret_norm_g, 2)[None, :], g2=norm2_g[None, :],
                wr_hi=wr_hi, wr_lo=wr_lo, br=br)


def _moe_kernel(be_ref, x_ref, wup_ref, bup_ref, wdn_ref, bdn_ref, y_ref):
    del be_ref
    up = jnp.dot(x_ref[...], wup_ref[0], preferred_element_type=F32) + bup_ref[0]
    x_glu = jnp.minimum(up[:, :D_FF], SWIGLU_LIMIT)
    x_lin = jnp.clip(up[:, D_FF:], -SWIGLU_LIMIT, SWIGLU_LIMIT)
    act = x_glu * jax.nn.sigmoid(SWIGLU_ALPHA * x_glu) * (x_lin + 1.0)
    y_ref[...] = jnp.dot(act.astype(BF16), wdn_ref[0], preferred_element_type=F32) + bdn_ref[0]


def _moe_experts(xb, blk_expert, w_up, b_up, w_down, b_down):
    P, D = xb.shape
    n_blk = P // MOE_ROWS
    grid_spec = pltpu.PrefetchScalarGridSpec(
        num_scalar_prefetch=1,
        grid=(n_blk,),
        in_specs=[pl.BlockSpec((MOE_ROWS, D), lambda j, be: (j, 0)),
                  pl.BlockSpec((1, D, 2 * D_FF), lambda j, be: (be[j], 0, 0)),
                  pl.BlockSpec((1, 1, 2 * D_FF), lambda j, be: (be[j], 0, 0)),
                  pl.BlockSpec((1, D_FF, D), lambda j, be: (be[j], 0, 0)),
                  pl.BlockSpec((1, 1, D), lambda j, be: (be[j], 0, 0))],
        out_specs=pl.BlockSpec((MOE_ROWS, D), lambda j, be: (j, 0)),
    )
    return pl.pallas_call(
        _moe_kernel,
        grid_spec=grid_spec,
        out_shape=jax.ShapeDtypeStruct((P, D), F32),
        compiler_params=_cparams(("arbitrary",)),
        name="moe_experts",
    )(blk_expert, xb, w_up, b_up[:, None, :], w_down, b_down[:, None, :])


def _moe(h, topi, topg, w_up, b_up, w_down, b_down):
    T, D = h.shape
    A = T * TOP_K
    onehot = (topi[:, :, None] == jnp.arange(N_EXPERTS, dtype=jnp.int32)).astype(jnp.int32).sum(axis=1)
    rank = jnp.cumsum(onehot, axis=0) - onehot
    counts = onehot.sum(axis=0)
    padded = (counts + MOE_ROWS - 1) // MOE_ROWS * MOE_ROWS
    pad_end = jnp.cumsum(padded)
    pad_start = pad_end - padded
    dest = pad_start[topi] + jnp.take_along_axis(rank, topi, axis=1)
    n_blk = (A + N_EXPERTS * (MOE_ROWS - 1) + MOE_ROWS - 1) // MOE_ROWS
    P = n_blk * MOE_ROWS
    tok = jnp.broadcast_to(jnp.arange(T, dtype=jnp.int32)[:, None], (T, TOP_K))
    slot_tok = jnp.full((P,), T, jnp.int32).at[dest.reshape(-1)].set(tok.reshape(-1))
    blk_expert = jnp.minimum(jnp.searchsorted(pad_end, jnp.arange(n_blk) * MOE_ROWS, side='right'),
                             N_EXPERTS - 1).astype(jnp.int32)
    h_pad = jnp.concatenate([h, jnp.zeros((1, D), h.dtype)], axis=0)
    yb = _moe_experts(h_pad[slot_tok], blk_expert, w_up, b_up, w_down, b_down)
    return (yb[dest] * topg[:, :, None]).sum(axis=1)


SAMPLE_ROWS = 8


def _token_mixers(x, pw, cos, sin, tm):
    names = ('q', 'kvc', 'kvs', 'kvw', 'ks_bf', 'kw_bf', 'gates', 'rq', 'rk', 'rv', 'rg', 'mq')
    return dict(zip(names, _project(x, pw, cos, sin, tm)))


def kernel(x_prompt, x_sample, mem_prompt, cache_cmp_kv, cache_slc_kv, cache_win_kv, state_ret, cache_mem_kv, page_table, norm1_g, w_in, q_norm_g, k_norm_cmp_g, k_norm_slc_g, k_norm_win_g, cmp_pe_k, cmp_w1_k, cmp_w2_k, cmp_pe_v, cmp_w1_v, cmp_w2_v, ret_norm_g, mem_norm_g, w_mem_kv, mem_q_norm_g, mem_k_norm_g, w_out, norm2_g, w_router, b_router, w_up, b_up, w_down, b_down):
    B, S, D = x_prompt.shape
    DB, QS, _ = x_sample.shape
    n_mem = mem_prompt.shape[1]
    n_pages, page = page_table.shape[1], cache_cmp_kv.shape[2]
    past = n_pages * page
    NQ = SAMPLE_ROWS
    G, HD = NSA_GROUPS, HEAD_DIM
    win_rows = min(WINDOW, S)
    TP, TS = B * S, DB * NQ

    cos_p, sin_p = _rope_tables(jnp.arange(S, dtype=jnp.int32))
    cos_s, sin_s = _rope_tables(past + jnp.arange(NQ, dtype=jnp.int32))
    cos_s, sin_s = jnp.tile(cos_s, (DB, 1)), jnp.tile(sin_s, (DB, 1))

    xp = x_prompt.reshape(TP, D)
    xs = jnp.pad(x_sample, ((0, 0), (0, NQ - QS), (0, 0))).reshape(TS, D)
    unpad = lambda a: a.reshape(DB, NQ, -1)[:, :QS]
    outs = [[] for _ in range(9)]
    for l in range(w_in.shape[0]):
        pw = _prep_proj_weights(norm1_g[l], w_in[l], q_norm_g[l], k_norm_slc_g[l], k_norm_win_g[l], mem_q_norm_g[l])
        cw = _prep_compress_weights(cmp_pe_k[l], cmp_w1_k[l], cmp_w2_k[l], cmp_pe_v[l], cmp_w1_v[l], cmp_w2_v[l],
                                    k_norm_cmp_g[l])
        mw = _prep_mix_weights(w_out[l], ret_norm_g[l], norm2_g[l], w_router[l], b_router[l])

        t = _token_mixers(xp, pw, cos_p, sin_p, 512)
        kcv = _compress(t['kvc'].reshape(B, S, 256), cw)
        o_nsa = _nsa_prompt(t['q'].reshape(B, S, -1), t['gates'].reshape(B, S, LANES), kcv,
                            t['ks_bf'].reshape(B, S, 256), t['kw_bf'].reshape(B, S, 256))
        o_ret, r_p = _retention(t['rq'].reshape(B, S, 256), t['rk'].reshape(B, S, 256), t['rv'].reshape(B, S, 256),
                                jnp.zeros((B, 2, LANES, LANES), F32), RET_CHUNK, RET_CHUNK)
        mkv = _mem_kv(mem_prompt.reshape(B * n_mem, D), mem_norm_g[l], w_mem_kv[l], mem_k_norm_g[l])
        o_mem = _mem_attn(t['mq'].reshape(B, S, 256), mkv.reshape(B, n_mem, 512), 512)
        x1_p, h_p, topi_p, topg_p = _mix(xp, o_nsa.reshape(TP, -1), o_ret.reshape(TP, 256), t['rg'],
                                         o_mem.reshape(TP, 256), mw, 512)
        outs[0].append(t['kvc'].reshape(B, S, 2, G, HD))
        outs[1].append(t['kvs'].reshape(B, S, 2, G, HD))
        outs[2].append(t['kvw'].reshape(B, S, 2, G, HD)[:, S - win_rows:])
        outs[3].append(_pairs_to_state(r_p))
        outs[4].append(mkv.reshape(B, n_mem, 2, MEM_HEADS, HD))

        t = _token_mixers(xs, pw, cos_s, sin_s, TS)
        win = cache_win_kv[l]
        o_nsa = _nsa_sample(t['q'].astype(F32).reshape(DB, NQ, -1), t['gates'].reshape(DB, NQ, LANES),
                            t['kvs'].reshape(DB, NQ, 256), t['kvw'].reshape(DB, NQ, 256),
                            win.reshape(DB, win.shape[1], 256),
                            cache_cmp_kv[l].reshape(-1, page, 256), cache_slc_kv[l].reshape(-1, page, 256),
                            page_table, cw, QS)
        o_ret, r_s = _retention(t['rq'].reshape(DB, NQ, 256), t['rk'].reshape(DB, NQ, 256),
                                t['rv'].reshape(DB, NQ, 256), _state_to_pairs(state_ret[l].astype(F32)), QS, NQ)
        o_mem = _mem_attn(t['mq'].reshape(DB, NQ, 256), cache_mem_kv[l].reshape(DB, n_mem, 512), NQ)
        x1_s, h_s, topi_s, topg_s = _mix(xs, o_nsa.reshape(TS, -1).astype(BF16), o_ret.reshape(TS, 256), t['rg'],
                                         o_mem.reshape(TS, 256), mw, TS)
        kv5 = lambda a: unpad(a).reshape(DB, QS, 2, G, HD)
        outs[5].append(kv5(t['kvc']))
        outs[6].append(kv5(t['kvs']))
        outs[7].append(jnp.concatenate([win, kv5(t['kvw'])], axis=1)[:, QS:])
        outs[8].append(_pairs_to_state(r_s))

        cat = lambda a, b: jnp.concatenate([a, unpad(b).reshape(DB * QS, -1)], axis=0)
        y = _moe(cat(h_p, h_s), cat(topi_p, topi_s)[:, :TOP_K], cat(topg_p, topg_s)[:, :TOP_K],
                 w_up[l].astype(BF16), b_up[l], w_down[l].astype(BF16), b_down[l])
        xp = x1_p + y[:TP]
        xs_new = unpad(x1_s).reshape(DB * QS, D) + y[TP:]
        xs = jnp.pad(xs_new.reshape(DB, QS, D), ((0, 0), (0, NQ - QS), (0, 0))).reshape(TS, D)
    y_sample = xs.reshape(DB, NQ, D)[:, :QS]
    return (xp.reshape(B, S, D), y_sample) + tuple(jnp.stack(o) for o in outs)
```

```python
import functools

import jax
import jax.numpy as jnp
from jax import lax
from jax.experimental import pallas as pl
from jax.experimental.pallas import tpu as pltpu

F32 = jnp.float32
BF16 = jnp.bfloat16

HEAD_DIM = 64
NSA_HEADS = 8
NSA_GROUPS = 2
NSA_HPG = NSA_HEADS // NSA_GROUPS
RET_HEADS = 4
MEM_HEADS = 4
CMP_BLK = 32
CMP_STRIDE = 16
CMP_HID = 256
SLC_BLK = 64
N_SEL = 16
WINDOW = 512
Q_BLK = 128
RET_CHUNK = 128
N_EXPERTS = 32
TOP_K = 4
D_FF = 1024
SWIGLU_LIMIT = 7.0
SWIGLU_ALPHA = 1.702
EPS = 1e-6
NEG = -1e30
BIG = 1e9
ROPE_BASE = 10000.0
SLC_RATIO = SLC_BLK // CMP_STRIDE
CMP_OVL = CMP_BLK // CMP_STRIDE - 1

LANES = 128
KEY_TILE = 512
MOE_ROWS = 512
VMEM_LIMIT = 56 * 1024 * 1024

C_Q = 0
C_KVC = C_Q + NSA_HEADS * LANES
C_KVS = C_KVC + 256
C_KVW = C_KVS + 256
C_GATE = C_KVW + 256
C_RQ = C_GATE + LANES
C_RK = C_RQ + 256
C_RV = C_RK + 256
C_RG = C_RV + 256
C_MQ = C_RG + 256
C_END = C_MQ + 256


def _cparams(sem):
    return pltpu.CompilerParams(dimension_semantics=sem, vmem_limit_bytes=VMEM_LIMIT)


def _lane(shape):
    return lax.broadcasted_iota(jnp.int32, shape, len(shape) - 1)


def _seg_rms(x, gain):
    lo = _lane(x.shape) < HEAD_DIM
    x2 = x * x
    s_lo = jnp.sum(jnp.where(lo, x2, 0.0), axis=-1, keepdims=True)
    s_hi = jnp.sum(jnp.where(lo, 0.0, x2), axis=-1, keepdims=True)
    ms = jnp.where(lo, s_lo, s_hi) * (1.0 / HEAD_DIM)
    return x * lax.rsqrt(ms + EPS) * gain


def _swap_halves(x):
    first = (_lane(x.shape) & (HEAD_DIM // 2)) == 0
    return jnp.where(first, pltpu.roll(x, LANES - HEAD_DIM // 2, 1), pltpu.roll(x, HEAD_DIM // 2, 1))


def _proj_kernel(x_ref, g1_ref, w_ref, cos_ref, sin_ref, qg_ref, ksg_ref, kwg_ref, mqg_ref,
                 q_out, kvc_out, kvs_out, kvw_out, ks_bf, kw_bf, gate_out,
                 rq_out, rk_out, rv_out, rg_out, mq_out, *extra, seq_tiles):
    x = x_ref[...]
    tm = x.shape[0]
    ms = jnp.mean(x * x, axis=-1, keepdims=True)
    xn = (x * lax.rsqrt(ms + EPS) * g1_ref[...]).astype(BF16)
    z = jnp.dot(xn, w_ref[...], preferred_element_type=F32)
    for i in range(NSA_HEADS):
        zq = z[:, C_Q + i * LANES:C_Q + (i + 1) * LANES]
        msq = jnp.sum(zq * zq, axis=-1, keepdims=True) * (1.0 / HEAD_DIM)
        qn = zq * lax.rsqrt(msq + EPS) * qg_ref[:, i * LANES:(i + 1) * LANES] * (HEAD_DIM ** -0.5)
        if seq_tiles:
            q_out[i * LANES:(i + 1) * LANES, :] = qn.T.astype(BF16)
        else:
            q_out[:, i * LANES:(i + 1) * LANES] = qn.astype(BF16)
    kvc_out[...] = z[:, C_KVC:C_KVC + 256]
    ks = _seg_rms(z[:, C_KVS:C_KVS + LANES], ksg_ref[...])
    vs = z[:, C_KVS + LANES:C_KVS + 256]
    kvs_out[:, 0:LANES] = ks
    kvs_out[:, LANES:256] = vs
    kw = _seg_rms(z[:, C_KVW:C_KVW + LANES], kwg_ref[...])
    vw = z[:, C_KVW + LANES:C_KVW + 256]
    kvw_out[:, 0:LANES] = kw
    kvw_out[:, LANES:256] = vw
    gates = jax.nn.sigmoid(z[:, C_GATE:C_GATE + LANES])
    ks_bf[:, 0:LANES] = ks.astype(BF16)
    if seq_tiles:
        vs_t, vw_t = extra
        pos = (pl.program_id(0) % seq_tiles) * tm + lax.broadcasted_iota(jnp.int32, (tm, LANES), 0)
        ks_bf[:, LANES:256] = jnp.where(_lane((tm, LANES)) == pos // SLC_BLK, 1.0, 0.0).astype(BF16)
        kw_bf[...] = kw.astype(BF16)
        vs_t[...] = vs.T.astype(BF16)
        vw_t[...] = vw.T.astype(BF16)
        gate_out[...] = gates.T
    else:
        ks_bf[:, LANES:256] = vs.astype(BF16)
        kw_bf[:, 0:LANES] = kw.astype(BF16)
        kw_bf[:, LANES:256] = vw.astype(BF16)
        gate_out[...] = gates
    for c in range(2):
        sl = slice(c * LANES, (c + 1) * LANES)
        cos = cos_ref[:, sl]
        sin = sin_ref[:, sl]
        rq = z[:, C_RQ + c * LANES:C_RQ + (c + 1) * LANES]
        rk = z[:, C_RK + c * LANES:C_RK + (c + 1) * LANES]
        rq_out[:, sl] = rq * cos + _swap_halves(rq) * sin
        rk_out[:, sl] = (rk * cos + _swap_halves(rk) * sin) * (HEAD_DIM ** -0.5)
        mq = z[:, C_MQ + c * LANES:C_MQ + (c + 1) * LANES]
        mq_out[:, sl] = _seg_rms(mq, mqg_ref[...]).astype(BF16)
    rv_out[...] = z[:, C_RV:C_RV + 256]
    rg_out[...] = z[:, C_RG:C_RG + 256]


def _project(x, pw, cos, sin, tm, key_major):
    T, D = x.shape
    nt = cos.shape[0] // tm
    row = lambda n: (pl.BlockSpec((tm, n), lambda i: (i, 0)), (T, n))
    col = lambda n: (pl.BlockSpec((n, tm), lambda i: (0, i)), (n, T))
    full = lambda a: pl.BlockSpec(a.shape, lambda i: (0, 0))
    tab = pl.BlockSpec((tm, 256), lambda i: (i % nt, 0))
    tok = col if key_major else row
    outs = [
        (tok(NSA_HEADS * LANES), BF16),
        (row(256), F32), (row(256), F32), (row(256), F32),
        (row(256), BF16), (row(LANES if key_major else 256), BF16),
        (tok(LANES), F32),
        (row(256), F32), (row(256), F32), (row(256), F32), (row(256), F32),
        (row(256), BF16),
    ]
    if key_major:
        outs += [(col(LANES), BF16), (col(LANES), BF16)]
    return pl.pallas_call(
        functools.partial(_proj_kernel, seq_tiles=nt if key_major else 0),
        grid=(T // tm,),
        in_specs=[row(D)[0], full(pw['g1']), full(pw['w']), tab, tab,
                  full(pw['qg']), full(pw['ksg']), full(pw['kwg']), full(pw['mqg'])],
        out_specs=[spec for (spec, _), _ in outs],
        out_shape=[jax.ShapeDtypeStruct(shape, dt) for (_, shape), dt in outs],
        compiler_params=_cparams(("arbitrary",)),
        name="proj",
    )(x, pw['g1'], pw['w'], cos, sin, pw['qg'], pw['ksg'], pw['kwg'], pw['mqg'])


def _prep_proj_weights(norm1_g, w_in, q_norm_g, k_norm_slc_g, k_norm_win_g, mem_q_norm_g):
    D = w_in.shape[0]
    sizes = (512, 256, 256, 256, 24, 256, 256, 256, 256, 256)
    parts, off = [], 0
    for n in sizes:
        parts.append(w_in[:, off:off + n])
        off += n
    wq, wkvc, wkvs, wkvw, wg, wrq, wrk, wrv, wrg, wmq = parts
    zero = jnp.zeros((D, HEAD_DIM), w_in.dtype)
    qcols, qg = [], []
    gz = jnp.zeros((HEAD_DIM,), F32)
    for i in range(NSA_HEADS):
        wh = wq[:, i * HEAD_DIM:(i + 1) * HEAD_DIM]
        if i < NSA_HPG:
            qcols += [wh, zero]
            qg += [q_norm_g, gz]
        else:
            qcols += [zero, wh]
            qg += [gz, q_norm_g]
    wgp = jnp.pad(wg, ((0, 0), (0, LANES - wg.shape[1])))
    w = jnp.concatenate(qcols + [wkvc, wkvs, wkvw, wgp, wrq, wrk, wrv, wrg, wmq], axis=1).astype(BF16)
    two = lambda g: jnp.tile(g, 2)[None, :]
    return dict(g1=norm1_g[None, :], w=w, qg=jnp.concatenate(qg)[None, :],
                ksg=two(k_norm_slc_g), kwg=two(k_norm_win_g), mqg=two(mem_q_norm_g))


def _rope_tables(pos):
    half = HEAD_DIM // 2
    inv = ROPE_BASE ** (-jnp.arange(half, dtype=F32) / half)
    ang = pos.astype(F32)[:, None] * inv[None, :]
    cos, sin = jnp.cos(ang), jnp.sin(ang)
    cos = jnp.tile(jnp.concatenate([cos, cos], axis=1), (1, RET_HEADS))
    sin = jnp.tile(jnp.concatenate([-sin, sin], axis=1), (1, RET_HEADS))
    return cos, sin


def _compress_half(src_ref, pe_ref, wa_ref, wb_ref, w2_ref, nchunks):
    a_parts, b_parts = [], []
    for r in range(CMP_STRIDE):
        xr = src_ref[pl.ds(r, nchunks, stride=CMP_STRIDE), :]
        a_parts.append((xr + pe_ref[r:r + 1, :]).astype(BF16))
        b_parts.append((xr + pe_ref[CMP_STRIDE + r:CMP_STRIDE + r + 1, :]).astype(BF16))
    ha = jnp.dot(jnp.concatenate(a_parts, axis=1), wa_ref[...], preferred_element_type=F32)
    hb = jnp.dot(jnp.concatenate(b_parts, axis=1), wb_ref[...], preferred_element_type=F32)
    h = ha + pltpu.roll(hb, nchunks - 1, 0)
    act = jax.nn.gelu(h).astype(BF16)
    return jnp.dot(act, w2_ref[...], preferred_element_type=F32)


def _compress_kernel(kv_ref, pe_ref, wa_ref, wb_ref, w2_ref, kg_ref, kc_out, vct_out):
    nchunks = kc_out.shape[1]
    y = _compress_half(kv_ref.at[0], pe_ref.at[0], wa_ref.at[0], wb_ref.at[0], w2_ref.at[0], nchunks)

    @pl.when(pl.program_id(1) == 0)
    def _():
        kc_out[0] = _seg_rms(y, kg_ref[...]).astype(BF16)

    @pl.when(pl.program_id(1) == 1)
    def _():
        vct_out[0] = y.T.astype(BF16)


def _compress(kvc, cw):
    B, T, _ = kvc.shape
    nchunks = T // CMP_STRIDE
    per_kv = lambda a: pl.BlockSpec((1,) + a.shape[1:], lambda b, j: (j,) + (0,) * (a.ndim - 1))
    return pl.pallas_call(
        _compress_kernel,
        grid=(B, 2),
        in_specs=[pl.BlockSpec((1, T, LANES), lambda b, j: (b, 0, j)),
                  per_kv(cw['pe']), per_kv(cw['wa']), per_kv(cw['wb']), per_kv(cw['w2']),
                  pl.BlockSpec((1, LANES), lambda b, j: (0, 0))],
        out_specs=[pl.BlockSpec((1, nchunks, LANES), lambda b, j: (b, 0, 0)),
                   pl.BlockSpec((1, LANES, nchunks), lambda b, j: (b, 0, 0))],
        out_shape=[jax.ShapeDtypeStruct((B, nchunks, LANES), BF16),
                   jax.ShapeDtypeStruct((B, LANES, nchunks), BF16)],
        compiler_params=_cparams(("arbitrary", "arbitrary")),
        name="compress",
    )(kvc, cw['pe'], cw['wa'], cw['wb'], cw['w2'], cw['kg'])


def _prep_compress_weights(pe_k, w1_k, w2_k, pe_v, w1_v, w2_v, k_norm_g):
    eye = jnp.eye(NSA_GROUPS, dtype=F32)

    def one(pe, w1, w2):
        w1r = w1.reshape(CMP_BLK, HEAD_DIM, CMP_HID)
        wf = jnp.einsum('gh,rdc->rgdhc', eye, w1r).reshape(CMP_BLK * LANES, NSA_GROUPS * CMP_HID)
        w2f = jnp.einsum('gh,cd->gchd', eye, w2).reshape(NSA_GROUPS * CMP_HID, LANES)
        half = CMP_STRIDE * LANES
        return jnp.tile(pe, (1, NSA_GROUPS)), wf[:half].astype(BF16), wf[half:].astype(BF16), w2f.astype(BF16)

    k, v = one(pe_k, w1_k, w2_k), one(pe_v, w1_v, w2_v)
    st = lambda i: jnp.stack([k[i], v[i]])
    return dict(pe=st(0), wa=st(1), wb=st(2), w2=st(3), kg=jnp.tile(k_norm_g, 2)[None, :])


def _dot_nt(a, b):
    return lax.dot_general(a, b, (((1,), (1,)), ((), ())), preferred_element_type=F32)


def _split3(x):
    hi = x.astype(BF16)
    r = x - hi.astype(F32)
    mid = r.astype(BF16)
    lo = (r - mid.astype(F32)).astype(BF16)
    return hi, mid, lo


def _select_bias(ps, cur, rounds, axis, cur_in_range=True):
    j = lax.broadcasted_iota(jnp.int32, ps.shape, axis)
    last = cur if cur_in_range else cur - 1
    key = jnp.where(j <= last, ps, NEG)
    for forced in (0, last, cur - 1):
        key = jnp.where(j == forced, BIG, key)
    bias = jnp.full(ps.shape, NEG, F32)
    for _ in range(rounds):
        m = jnp.max(key, axis=axis, keepdims=True)
        idx = jnp.min(jnp.where(key == m, j, LANES), axis=axis, keepdims=True)
        pick = j == idx
        bias = jnp.where(pick, 0.0, bias)
        key = jnp.where(pick, -jnp.inf, key)
    return bias


def _cmp_branch(qg, qpos, kc, vc):
    s = _dot_nt(qg, kc)
    cend = _lane((1, kc.shape[0])) * CMP_STRIDE + (CMP_BLK - 1)
    s = jnp.where(cend <= qpos, s, NEG)
    m = jnp.max(s, axis=-1, keepdims=True)
    p = jnp.exp(s - m)
    p = p * (1.0 / jnp.sum(p, axis=-1, keepdims=True))
    p = jnp.where(qpos >= CMP_BLK - 1, p, 0.0)
    return p, jnp.dot(p.astype(BF16), vc, preferred_element_type=F32)


def _sel_tile(qaug, k_tile, v_tile, blk0, mask, m_sc, l_sc, acc_sc):
    n = k_tile.shape[0]
    blk = blk0 + lax.broadcasted_iota(jnp.int32, (n, LANES), 0) // SLC_BLK
    onehot = jnp.where(_lane((n, LANES)) == blk, 1.0, 0.0).astype(BF16)
    s = _dot_nt(qaug, jnp.concatenate([k_tile, onehot], axis=1))
    if mask is not None:
        s = jnp.where(mask, s, NEG)
    _online_update(s, v_tile, m_sc, l_sc, acc_sc)


def _online_update(s, v_tile, m_sc, l_sc, acc_sc):
    m_old = m_sc[...]
    m_new = jnp.maximum(m_old, jnp.max(s, axis=-1, keepdims=True))
    alpha = jnp.exp(m_old - m_new)
    p = jnp.exp(s - m_new)
    l_sc[...] = alpha * l_sc[...] + jnp.sum(p, axis=-1, keepdims=True)
    acc_sc[...] = alpha * acc_sc[...] + jnp.dot(p.astype(BF16), v_tile, preferred_element_type=F32)
    m_sc[...] = m_new


def _gate_heads(gates, g, oc, osel, ow, nq):
    heads = []
    for h in range(NSA_HPG):
        rs = slice(h * nq, (h + 1) * nq)
        c0 = (NSA_HPG * g + h) * 3
        heads.append(gates[:, c0:c0 + 1] * oc[rs] + gates[:, c0 + 1:c0 + 2] * osel[rs]
                     + gates[:, c0 + 2:c0 + 3] * ow[rs])
    return heads


def _store_heads(heads, o_ref):
    low = _lane(heads[0].shape) < HEAD_DIM
    for pr in range(NSA_HEADS // 2):
        even, odd = heads[2 * pr], heads[2 * pr + 1]
        if pr < NSA_GROUPS:
            odd = pltpu.roll(odd, HEAD_DIM, 1)
        else:
            even = pltpu.roll(even, HEAD_DIM, 1)
        o_ref[0, :, pr * LANES:(pr + 1) * LANES] = jnp.where(low, even, odd).astype(o_ref.dtype)


def _nsa_sample_kernel(pt_ref, q_ref, gate_ref, ksn_ref, kwn_ref, win_ref, pool_c, pool_s,
                       pe_ref, wa_ref, wb_ref, w2_ref, kg_ref, mmap_ref, o_ref,
                       cbuf, sbuf, sems, m_sc, l_sc, acc_sc, *, page, n_new):
    b = pl.program_id(0)
    past = sbuf.shape[0]
    npages = past // page
    nq = q_ref.shape[1]

    def page_copies(p, pg):
        dst = pl.ds(pl.multiple_of(p * page, page), page)
        return (pltpu.make_async_copy(pool_c.at[pg, :, 0:LANES], cbuf.at[0, dst], sems.at[0]),
                pltpu.make_async_copy(pool_c.at[pg, :, LANES:256], cbuf.at[1, dst], sems.at[1]),
                pltpu.make_async_copy(pool_s.at[pg], sbuf.at[dst], sems.at[2]))

    def start(p, c):
        for cp in page_copies(p, pt_ref[b * npages + p]):
            cp.start()
        return c

    def wait(p, c):
        for cp in page_copies(p, 0):
            cp.wait()
        return c

    lax.fori_loop(0, npages, start, 0)
    lax.fori_loop(0, npages, wait, 0)

    nchunks = past // CMP_STRIDE
    kc = _seg_rms(_compress_half(cbuf.at[0], pe_ref.at[0], wa_ref.at[0], wb_ref.at[0], w2_ref.at[0], nchunks),
                  kg_ref[...]).astype(BF16)
    vc = _compress_half(cbuf.at[1], pe_ref.at[1], wa_ref.at[1], wb_ref.at[1], w2_ref.at[1], nchunks).astype(BF16)

    rows = NSA_HPG * nq
    srow = lax.broadcasted_iota(jnp.int32, (rows, 1), 0) % nq
    qpos = past + srow
    cur = (past + lax.broadcasted_iota(jnp.int32, (nq, 1), 0)) // SLC_BLK
    gates = gate_ref[0]
    pad_keys = lambda x: jnp.concatenate([x, jnp.zeros((LANES - nq, LANES), F32)], axis=0).astype(BF16)
    new_ok = _lane((1, LANES)) <= jnp.minimum(srow, n_new - 1)
    heads = []
    for g in range(NSA_GROUPS):
        qg = jnp.concatenate(
            [q_ref[0, :, (NSA_HPG * g + h) * LANES:(NSA_HPG * g + h + 1) * LANES] for h in range(NSA_HPG)],
            axis=0).astype(BF16)
        p, oc = _cmp_branch(qg, qpos, kc, vc)
        p4 = p[0:nq] + p[nq:2 * nq] + p[2 * nq:3 * nq] + p[3 * nq:4 * nq]
        ps = sum(_dot_nt(part, mmap_ref[...]) for part in _split3(p4))
        bias = _select_bias(ps, cur, N_SEL - 1, 1, cur_in_range=False)
        qaug = jnp.concatenate([qg, jnp.concatenate([bias] * NSA_HPG, axis=0).astype(BF16)], axis=1)
        m_sc[...] = jnp.full(m_sc.shape, NEG, F32)
        l_sc[...] = jnp.zeros(l_sc.shape, F32)
        acc_sc[...] = jnp.zeros(acc_sc.shape, F32)

        def body(kt, c, qaug=qaug):
            k0 = pl.multiple_of(kt * KEY_TILE, KEY_TILE)
            _sel_tile(qaug, sbuf[pl.ds(k0, KEY_TILE), 0:LANES].astype(BF16),
                      sbuf[pl.ds(k0, KEY_TILE), LANES:256].astype(BF16),
                      kt * (KEY_TILE // SLC_BLK), None, m_sc, l_sc, acc_sc)
            return c

        lax.fori_loop(0, past // KEY_TILE, body, 0)
        s_new = jnp.where(new_ok, _dot_nt(qg, pad_keys(ksn_ref[0, :, 0:LANES])), NEG)
        _online_update(s_new, pad_keys(ksn_ref[0, :, LANES:256]), m_sc, l_sc, acc_sc)
        osel = acc_sc[...] * (1.0 / l_sc[...])
        nwin = win_ref.shape[1]
        sw_old = jnp.where(_lane((1, nwin)) + (WINDOW - nwin) > srow,
                           _dot_nt(qg, win_ref[0, :, 0:LANES].astype(BF16)), NEG)
        sw_new = jnp.where(new_ok, _dot_nt(qg, pad_keys(kwn_ref[0, :, 0:LANES])), NEG)
        sw = jnp.concatenate([sw_old, sw_new], axis=1)
        mw = jnp.max(sw, axis=-1, keepdims=True)
        pw = jnp.exp(sw - mw)
        ow = (jnp.dot(pw[:, :nwin].astype(BF16), win_ref[0, :, LANES:256].astype(BF16), preferred_element_type=F32)
              + jnp.dot(pw[:, nwin:].astype(BF16), pad_keys(kwn_ref[0, :, LANES:256]), preferred_element_type=F32))
        ow = ow * (1.0 / jnp.sum(pw, axis=-1, keepdims=True))
        heads += _gate_heads(gates, g, oc, osel, ow, nq)
    _store_heads(heads, o_ref)


def _nsa_sample(q, gates, ksn, kwn, win, pool_c, pool_s, page_table, cw, n_new):
    DB, nq, _ = q.shape
    n_pages = page_table.shape[1]
    page = pool_c.shape[1]
    past = n_pages * page
    nchunks = past // CMP_STRIDE
    rows = NSA_HPG * nq
    mmap = _importance_map(nchunks)
    per_seq = lambda a: pl.BlockSpec((1,) + a.shape[1:], lambda b, pt: (b,) + (0,) * (a.ndim - 1))
    const = lambda a: pl.BlockSpec(a.shape, lambda b, pt: (0,) * a.ndim)
    hbm = pl.BlockSpec(memory_space=pl.ANY)
    grid_spec = pltpu.PrefetchScalarGridSpec(
        num_scalar_prefetch=1,
        grid=(DB,),
        in_specs=[per_seq(q), per_seq(gates), per_seq(ksn), per_seq(kwn), per_seq(win), hbm, hbm,
                  const(cw['pe']), const(cw['wa']), const(cw['wb']), const(cw['w2']), const(cw['kg']), const(mmap)],
        out_specs=pl.BlockSpec((1, nq, NSA_HEADS * HEAD_DIM), lambda b, pt: (b, 0, 0)),
        scratch_shapes=[pltpu.VMEM((2, past, LANES), F32), pltpu.VMEM((past, 256), F32),
                        pltpu.SemaphoreType.DMA((3,)),
                        pltpu.VMEM((rows, 1), F32), pltpu.VMEM((rows, 1), F32), pltpu.VMEM((rows, LANES), F32)],
    )
    return pl.pallas_call(
        functools.partial(_nsa_sample_kernel, page=page, n_new=n_new),
        grid_spec=grid_spec,
        out_shape=jax.ShapeDtypeStruct((DB, nq, NSA_HEADS * HEAD_DIM), F32),
        compiler_params=_cparams(("arbitrary",)),
        name="nsa_sample",
    )(page_table.reshape(-1), q, gates, ksn, kwn, win, pool_c, pool_s,
      cw['pe'], cw['wa'], cw['wb'], cw['w2'], cw['kg'], mmap)


def _nsa_prompt_kernel(q_ref, gate_ref, kc_ref, vct_ref, ks_ref, vst_ref, kw_ref, vwt_ref, mmap_ref, o_ref,
                       m_sc, l_sc, acc_sc):
    i = pl.program_id(1)
    s0 = i * Q_BLK
    cols = NSA_HPG * Q_BLK
    qpos = s0 + _lane((1, cols)) % Q_BLK
    cur = (s0 + _lane((1, Q_BLK))) // SLC_BLK
    kc, vct = kc_ref[0], vct_ref[0]
    ncmp = kc.shape[0]
    n_full = s0 // KEY_TILE
    w0 = pl.multiple_of(jnp.maximum(s0 - WINDOW, 0), Q_BLK)
    nw = WINDOW + Q_BLK
    dot = functools.partial(jnp.dot, preferred_element_type=F32)

    def softmax_keys(s):
        p = jnp.exp(s - jnp.max(s, axis=0, keepdims=True))
        return p, 1.0 / jnp.sum(p, axis=0, keepdims=True)

    qaugs = []
    branch = []
    for g in range(NSA_GROUPS):
        qt = jnp.concatenate([q_ref[(NSA_HPG * g + h) * LANES:(NSA_HPG * g + h + 1) * LANES, :]
                              for h in range(NSA_HPG)], axis=1)
        s = dot(kc, qt)
        n_idx = lax.broadcasted_iota(jnp.int32, (ncmp, cols), 0)
        s = jnp.where(n_idx * CMP_STRIDE + (CMP_BLK - 1) <= qpos, s, NEG)
        p, inv = softmax_keys(s)
        p = jnp.where(qpos >= CMP_BLK - 1, p * inv, 0.0)
        oc = dot(vct, p.astype(BF16))
        p4 = p[:, 0:Q_BLK] + p[:, Q_BLK:2 * Q_BLK] + p[:, 2 * Q_BLK:3 * Q_BLK] + p[:, 3 * Q_BLK:4 * Q_BLK]
        ps_t = sum(dot(mmap_ref[...], part) for part in _split3(p4))
        bias = _select_bias(ps_t, cur, N_SEL, 0).astype(BF16)
        qaugs.append(jnp.concatenate([qt, jnp.concatenate([bias] * NSA_HPG, axis=1)], axis=0))
        sw = dot(kw_ref[0, pl.ds(w0, nw), :], qt)
        d = qpos - (w0 + lax.broadcasted_iota(jnp.int32, (nw, cols), 0))
        sw = jnp.where(d >= 0, jnp.where(d < WINDOW, sw, NEG), NEG)
        pw, invw = softmax_keys(sw)
        ow = dot(vwt_ref[:, pl.ds(w0, nw)], pw.astype(BF16)) * invw
        branch.append((oc, ow))
        m_sc[g] = jnp.full(m_sc.shape[1:], NEG, F32)
        l_sc[g] = jnp.zeros(l_sc.shape[1:], F32)
        acc_sc[g] = jnp.zeros(acc_sc.shape[1:], F32)

    def tile(kt, causal):
        k0 = pl.multiple_of(kt * KEY_TILE, KEY_TILE)
        k_aug = ks_ref[0, pl.ds(k0, KEY_TILE), :]
        v_t = vst_ref[:, pl.ds(k0, KEY_TILE)]
        for g in range(NSA_GROUPS):
            s = dot(k_aug, qaugs[g])
            if causal:
                s = jnp.where(k0 + lax.broadcasted_iota(jnp.int32, s.shape, 0) <= qpos, s, NEG)
            m_old = m_sc[g]
            m_new = jnp.maximum(m_old, jnp.max(s, axis=0, keepdims=True))
            alpha = jnp.exp(m_old - m_new)
            p = jnp.exp(s - m_new)
            l_sc[g] = alpha * l_sc[g] + jnp.sum(p, axis=0, keepdims=True)
            acc_sc[g] = alpha * acc_sc[g] + dot(v_t, p.astype(BF16))
            m_sc[g] = m_new

    def body(kt, c):
        tile(kt, False)
        return c

    lax.fori_loop(0, n_full, body, 0)
    tile(n_full, True)

    pieces = []
    for g in range(NSA_GROUPS):
        oc, ow = branch[g]
        osel = acc_sc[g] * (1.0 / l_sc[g])
        for h in range(NSA_HPG):
            cs = slice(h * Q_BLK, (h + 1) * Q_BLK)
            c0 = (NSA_HPG * g + h) * 3
            o_h = (gate_ref[c0:c0 + 1, :] * oc[:, cs] + gate_ref[c0 + 1:c0 + 2, :] * osel[:, cs]
                   + gate_ref[c0 + 2:c0 + 3, :] * ow[:, cs])
            pieces.append(o_h[g * HEAD_DIM:(g + 1) * HEAD_DIM, :])
    o_ref[0] = jnp.concatenate(pieces, axis=0).T.astype(o_ref.dtype)


def _importance_map(ncmp):
    j = jnp.arange(LANES)[:, None]
    n = jnp.arange(ncmp)[None, :]
    return ((n >= SLC_RATIO * j - CMP_OVL) & (n < SLC_RATIO * j + SLC_RATIO)).astype(BF16)


def _nsa_prompt(B, qt, gates_t, kc, vct, ks_aug, vs_t, kw, vw_t):
    S = qt.shape[1] // B
    nq = S // Q_BLK
    ncmp = kc.shape[1]
    cols = NSA_HPG * Q_BLK
    mmap = _importance_map(ncmp)
    per_block = lambda n: pl.BlockSpec((n, Q_BLK), lambda b, i: (0, b * nq + i))
    seq_rows = lambda n: pl.BlockSpec((1, S, n), lambda b, i: (b, 0, 0))
    seq_cols = pl.BlockSpec((LANES, S), lambda b, i: (0, b))
    return pl.pallas_call(
        _nsa_prompt_kernel,
        grid=(B, nq),
        in_specs=[per_block(qt.shape[0]), per_block(LANES),
                  pl.BlockSpec((1, ncmp, LANES), lambda b, i: (b, 0, 0)),
                  pl.BlockSpec((1, LANES, ncmp), lambda b, i: (b, 0, 0)),
                  seq_rows(256), seq_cols, seq_rows(LANES), seq_cols,
                  pl.BlockSpec(mmap.shape, lambda b, i: (0, 0))],
        out_specs=pl.BlockSpec((1, Q_BLK, NSA_HEADS * HEAD_DIM), lambda b, i: (b, i, 0)),
        out_shape=jax.ShapeDtypeStruct((B, S, NSA_HEADS * HEAD_DIM), BF16),
        scratch_shapes=[pltpu.VMEM((NSA_GROUPS, 1, cols), F32), pltpu.VMEM((NSA_GROUPS, 1, cols), F32),
                        pltpu.VMEM((NSA_GROUPS, LANES, cols), F32)],
        compiler_params=_cparams(("arbitrary", "arbitrary")),
        name="nsa_prompt",
    )(qt, gates_t, kc, vct, ks_aug.reshape(B, S, 256), vs_t, kw.reshape(B, S, LANES), vw_t, mmap)


def _ret_kernel(rq_ref, rk_ref, rv_ref, r0_ref, dmask_ref, xi_ref, zeta_ref, dec_ref, o_ref, rout_ref, r_sc):
    @pl.when(pl.program_id(1) == 0)
    def _():
        r_sc[...] = r0_ref[0]

    C = rq_ref.shape[1]
    low = _lane((C, LANES)) < HEAD_DIM
    diag = lax.broadcasted_iota(jnp.int32, (LANES, LANES), 0) // HEAD_DIM == _lane((LANES, LANES)) // HEAD_DIM
    for pr in range(RET_HEADS // 2):
        sl = slice(pr * LANES, (pr + 1) * LANES)
        k = rk_ref[0, :, sl]
        qb, kb, vb = rq_ref[0, :, sl].astype(BF16), k.astype(BF16), rv_ref[0, :, sl].astype(BF16)
        zero = jnp.zeros_like(qb)
        s0 = _dot_nt(jnp.where(low, qb, zero), kb) * dmask_ref[2 * pr]
        s1 = _dot_nt(jnp.where(low, zero, qb), kb) * dmask_ref[2 * pr + 1]
        o = jnp.where(low, jnp.dot(s0.astype(BF16), vb, preferred_element_type=F32),
                      jnp.dot(s1.astype(BF16), vb, preferred_element_type=F32))
        r = r_sc[pr]
        o_ref[0, :, sl] = o + jnp.dot(qb, r.astype(BF16), preferred_element_type=F32) * xi_ref[:, sl]
        kz = (k * zeta_ref[:, sl]).astype(BF16)
        upd = lax.dot_general(kz, vb, (((0,), (0,)), ((), ())), preferred_element_type=F32)
        r_sc[pr] = dec_ref[:, sl] * r + jnp.where(diag, upd, 0.0)
    rout_ref[0] = r_sc[...]


def _ret_tables(c_true, c_pad):
    lg = jnp.log(1.0 - 2.0 ** (-5.0 - jnp.arange(RET_HEADS, dtype=F32)))
    idx = jnp.arange(c_pad, dtype=F32)
    diff = idx[:, None] - idx[None, :]
    dmask = jnp.where(diff >= 0, jnp.exp(jnp.maximum(diff, 0.0)[None] * lg[:, None, None]), 0.0)
    lanes = lambda a: jnp.repeat(a, HEAD_DIM, axis=-1)
    xi = lanes(jnp.exp((idx + 1.0)[:, None] * lg[None, :]))
    zeta = lanes(jnp.exp((c_true - 1.0 - idx)[:, None] * lg[None, :]))
    dec = lanes(jnp.exp(c_true * lg)[None, :])
    return dmask, xi, zeta, dec


def _retention(rq, rk, rv, r0, c_true, c_pad):
    B, T, _ = rq.shape
    dmask, xi, zeta, dec = _ret_tables(c_true, c_pad)
    row = pl.BlockSpec((1, c_pad, 256), lambda b, c: (b, c, 0))
    st = pl.BlockSpec((1, 2, LANES, LANES), lambda b, c: (b, 0, 0, 0))
    const = lambda a: pl.BlockSpec(a.shape, lambda b, c: (0,) * a.ndim)
    return pl.pallas_call(
        _ret_kernel,
        grid=(B, T // c_pad),
        in_specs=[row, row, row, st, const(dmask), const(xi), const(zeta), const(dec)],
        out_specs=[row, st],
        out_shape=[jax.ShapeDtypeStruct((B, T, 256), F32), jax.ShapeDtypeStruct((B, 2, LANES, LANES), F32)],
        scratch_shapes=[pltpu.VMEM((2, LANES, LANES), F32)],
        compiler_params=_cparams(("arbitrary", "arbitrary")),
        name="retention",
    )(rq, rk, rv, r0, dmask, xi, zeta, dec)


def _state_to_pairs(r):
    B = r.shape[0]
    r = r.reshape(B, 2, 2, HEAD_DIM, HEAD_DIM)
    eye = jnp.eye(2, dtype=r.dtype)
    return jnp.einsum('bphde,hk->bphdke', r, eye).reshape(B, 2, LANES, LANES)


def _pairs_to_state(rp):
    B = rp.shape[0]
    rp = rp.reshape(B, 2, 2, HEAD_DIM, 2, HEAD_DIM)
    return jnp.stack([rp[:, :, 0, :, 0, :], rp[:, :, 1, :, 1, :]], axis=2).reshape(B, RET_HEADS, HEAD_DIM, HEAD_DIM)


def _mem_kv_kernel(m_ref, g_ref, w_ref, kg_ref, kv_out):
    x = m_ref[...]
    ms = jnp.mean(x * x, axis=-1, keepdims=True)
    xn = (x * lax.rsqrt(ms + EPS) * g_ref[...]).astype(BF16)
    z = jnp.dot(xn, w_ref[...], preferred_element_type=F32)
    for c in range(2):
        kv_out[:, c * LANES:(c + 1) * LANES] = _seg_rms(z[:, c * LANES:(c + 1) * LANES], kg_ref[...])
    kv_out[:, 256:512] = z[:, 256:512]


def _mem_kv(mem, mem_norm_g, w_mem_kv, mem_k_norm_g):
    T, D = mem.shape
    w = w_mem_kv.astype(BF16)
    full = lambda a: pl.BlockSpec(a.shape, lambda i: (0, 0))
    g = mem_norm_g[None, :]
    kg = jnp.tile(mem_k_norm_g, 2)[None, :]
    return pl.pallas_call(
        _mem_kv_kernel,
        grid=(1,),
        in_specs=[full(mem), full(g), full(w), full(kg)],
        out_specs=pl.BlockSpec((T, 512), lambda i: (0, 0)),
        out_shape=jax.ShapeDtypeStruct((T, 512), F32),
        compiler_params=_cparams(("arbitrary",)),
        name="mem_kv",
    )(mem, g, w, kg)


def _mem_attn_kernel(mq_ref, mkv_ref, o_ref):
    q = mq_ref[0]
    rows = q.shape[0]
    low = _lane((rows, LANES)) < HEAD_DIM
    zero = jnp.zeros((rows, LANES), F32)
    for pr in range(MEM_HEADS // 2):
        sl = slice(pr * LANES, (pr + 1) * LANES)
        qp = q[:, sl].astype(F32)
        k = mkv_ref[0, :, sl].astype(BF16)
        v = mkv_ref[0, :, 256 + pr * LANES:256 + (pr + 1) * LANES].astype(BF16)
        outs = []
        for hh in range(2):
            qm = jnp.where(low, qp, zero) if hh == 0 else jnp.where(low, zero, qp)
            s = _dot_nt(qm.astype(BF16), k) * (HEAD_DIM ** -0.5)
            m = jnp.max(s, axis=-1, keepdims=True)
            p = jnp.exp(s - m)
            o = jnp.dot(p.astype(BF16), v, preferred_element_type=F32)
            outs.append(o * (1.0 / jnp.sum(p, axis=-1, keepdims=True)))
        o_ref[0, :, sl] = jnp.where(low, outs[0], outs[1]).astype(o_ref.dtype)


def _mem_attn(mq, mkv, tm):
    B, R, _ = mq.shape
    return pl.pallas_call(
        _mem_attn_kernel,
        grid=(B, R // tm),
        in_specs=[pl.BlockSpec((1, tm, 256), lambda b, i: (b, i, 0)),
                  pl.BlockSpec((1,) + mkv.shape[1:], lambda b, i: (b, 0, 0))],
        out_specs=pl.BlockSpec((1, tm, 256), lambda b, i: (b, i, 0)),
        out_shape=jax.ShapeDtypeStruct((B, R, 256), BF16),
        compiler_params=_cparams(("arbitrary", "arbitrary")),
        name="mem_attn",
    )(mq, mkv)


def _mix_kernel(x_ref, onsa_ref, oret_ref, rg_ref, omem_ref, wout_ref, rgain_ref, g2_ref,
                wr_hi_ref, wr_lo_ref, br_ref, x1_out, h_out, topi_out, topg_out):
    parts = [onsa_ref[...]]
    for c in range(2):
        sl = slice(c * LANES, (c + 1) * LANES)
        parts.append((_seg_rms(oret_ref[:, sl], rgain_ref[...]) * jax.nn.silu(rg_ref[:, sl])).astype(BF16))
    parts.append(omem_ref[...])
    mix = jnp.concatenate(parts, axis=1)
    x1 = x_ref[...] + jnp.dot(mix, wout_ref[...], preferred_element_type=F32)
    x1_out[...] = x1
    ms = jnp.mean(x1 * x1, axis=-1, keepdims=True)
    h = x1 * lax.rsqrt(ms + EPS) * g2_ref[...]
    h_out[...] = h.astype(BF16)
    h_hi, h_lo, _ = _split3(h)
    logits = (jnp.dot(h_hi, wr_hi_ref[...], preferred_element_type=F32)
              + jnp.dot(h_hi, wr_lo_ref[...], preferred_element_type=F32)
              + jnp.dot(h_lo, wr_hi_ref[...], preferred_element_type=F32)) + br_ref[...]
    lane = _lane(logits.shape)
    key = logits
    topi = jnp.zeros(logits.shape, jnp.int32)
    topv = jnp.zeros(logits.shape, F32)
    for r in range(TOP_K):
        m = jnp.max(key, axis=-1, keepdims=True)
        idx = jnp.min(jnp.where(key == m, lane, LANES), axis=-1, keepdims=True)
        if r == 0:
            m0 = m
        topi = jnp.where(lane == r, idx, topi)
        topv = jnp.where(lane == r, jnp.exp(m - m0), topv)
        key = jnp.where(lane == idx, -jnp.inf, key)
    topi_out[...] = topi
    topg_out[...] = topv * (1.0 / jnp.sum(topv, axis=-1, keepdims=True))


def _mix(x, onsa, oret, rg, omem, mw, tm):
    T, D = x.shape
    row = lambda n: pl.BlockSpec((tm, n), lambda i: (i, 0))
    full = lambda a: pl.BlockSpec(a.shape, lambda i: (0, 0))
    names = ('wout', 'rgain', 'g2', 'wr_hi', 'wr_lo', 'br')
    return pl.pallas_call(
        _mix_kernel,
        grid=(T // tm,),
        in_specs=[row(D), row(512), row(256), row(256), row(256)] + [full(mw[n]) for n in names],
        out_specs=[row(D), row(D), row(LANES), row(LANES)],
        out_shape=[jax.ShapeDtypeStruct((T, D), F32), jax.ShapeDtypeStruct((T, D), BF16),
                   jax.ShapeDtypeStruct((T, LANES), jnp.int32), jax.ShapeDtypeStruct((T, LANES), F32)],
        compiler_params=_cparams(("arbitrary",)),
        name="mix",
    )(x, onsa, oret, rg, omem, *[mw[n] for n in names])


def _prep_mix_weights(w_out, ret_norm_g, norm2_g, w_router, b_router):
    wr = jnp.pad(w_router, ((0, 0), (0, LANES - N_EXPERTS)))
    wr_hi = wr.astype(BF16)
    wr_lo = (wr - wr_hi.astype(F32)).astype(BF16)
    br = jnp.concatenate([b_router.astype(F32), jnp.full((LANES - N_EXPERTS,), NEG, F32)])[None, :]
    return dict(wout=w_out.astype(BF16), rgain=jnp.tile(ret_norm_g, 2)[None, :], g2=norm2_g[None, :],
                wr_hi=wr_hi, wr_lo=wr_lo, br=br)


def _moe_kernel(be_ref, x_ref, wup_ref, bup_ref, wdn_ref, bdn_ref, y_ref):
    del be_ref
    up = jnp.dot(x_ref[...], wup_ref[0], preferred_element_type=F32) + bup_ref[0]
    x_glu = jnp.minimum(up[:, :D_FF], SWIGLU_LIMIT)
    x_lin = jnp.clip(up[:, D_FF:], -SWIGLU_LIMIT, SWIGLU_LIMIT)
    act = x_glu * jax.nn.sigmoid(SWIGLU_ALPHA * x_glu) * (x_lin + 1.0)
    y_ref[...] = jnp.dot(act.astype(BF16), wdn_ref[0], preferred_element_type=F32) + bdn_ref[0]


def _moe_experts(xb, blk_expert, w_up, b_up, w_down, b_down):
    P, D = xb.shape
    n_blk = P // MOE_ROWS
    grid_spec = pltpu.PrefetchScalarGridSpec(
        num_scalar_prefetch=1,
        grid=(n_blk,),
        in_specs=[pl.BlockSpec((MOE_ROWS, D), lambda j, be: (j, 0)),
                  pl.BlockSpec((1, D, 2 * D_FF), lambda j, be: (be[j], 0, 0)),
                  pl.BlockSpec((1, 1, 2 * D_FF), lambda j, be: (be[j], 0, 0)),
                  pl.BlockSpec((1, D_FF, D), lambda j, be: (be[j], 0, 0)),
                  pl.BlockSpec((1, 1, D), lambda j, be: (be[j], 0, 0))],
        out_specs=pl.BlockSpec((MOE_ROWS, D), lambda j, be: (j, 0)),
    )
    return pl.pallas_call(
        _moe_kernel,
        grid_spec=grid_spec,
        out_shape=jax.ShapeDtypeStruct((P, D), F32),
        compiler_params=_cparams(("arbitrary",)),
        name="moe_experts",
    )(blk_expert, xb, w_up, b_up[:, None, :], w_down, b_down[:, None, :])


def _moe(h, topi, topg, w_up, b_up, w_down, b_down):
    T, D = h.shape
    A = T * TOP_K
    onehot = (topi[:, :, None] == jnp.arange(N_EXPERTS, dtype=jnp.int32)).astype(jnp.int32).sum(axis=1)
    rank = jnp.cumsum(onehot, axis=0) - onehot
    counts = onehot.sum(axis=0)
    padded = (counts + MOE_ROWS - 1) // MOE_ROWS * MOE_ROWS
    pad_end = jnp.cumsum(padded)
    pad_start = pad_end - padded
    dest = pad_start[topi] + jnp.take_along_axis(rank, topi, axis=1)
    n_blk = (A + N_EXPERTS * (MOE_ROWS - 1) + MOE_ROWS - 1) // MOE_ROWS
    P = n_blk * MOE_ROWS
    tok = jnp.broadcast_to(jnp.arange(T, dtype=jnp.int32)[:, None], (T, TOP_K))
    slot_tok = jnp.full((P,), T, jnp.int32).at[dest.reshape(-1)].set(tok.reshape(-1))
    blk_expert = jnp.minimum(jnp.searchsorted(pad_end, jnp.arange(n_blk) * MOE_ROWS, side='right'),
                             N_EXPERTS - 1).astype(jnp.int32)
    h_pad = jnp.concatenate([h, jnp.zeros((1, D), h.dtype)], axis=0)
    yb = _moe_experts(h_pad[slot_tok], blk_expert, w_up, b_up, w_down, b_down)
    return (yb[dest] * topg[:, :, None]).sum(axis=1)


SAMPLE_ROWS = 8


def _token_mixers(x, pw, cos, sin, tm, key_major):
    names = ('q', 'kvc', 'kvs', 'kvw', 'ks_bf', 'kw_bf', 'gates', 'rq', 'rk', 'rv', 'rg', 'mq', 'vs_t', 'vw_t')
    return dict(zip(names, _project(x, pw, cos, sin, tm, key_major)))


def kernel(x_prompt, x_sample, mem_prompt, cache_cmp_kv, cache_slc_kv, cache_win_kv, state_ret, cache_mem_kv, page_table, norm1_g, w_in, q_norm_g, k_norm_cmp_g, k_norm_slc_g, k_norm_win_g, cmp_pe_k, cmp_w1_k, cmp_w2_k, cmp_pe_v, cmp_w1_v, cmp_w2_v, ret_norm_g, mem_norm_g, w_mem_kv, mem_q_norm_g, mem_k_norm_g, w_out, norm2_g, w_router, b_router, w_up, b_up, w_down, b_down):
    B, S, D = x_prompt.shape
    DB, QS, _ = x_sample.shape
    n_mem = mem_prompt.shape[1]
    n_pages, page = page_table.shape[1], cache_cmp_kv.shape[2]
    past = n_pages * page
    NQ = SAMPLE_ROWS
    G, HD = NSA_GROUPS, HEAD_DIM
    win_rows = min(WINDOW, S)
    TP, TS = B * S, DB * NQ

    cos_p, sin_p = _rope_tables(jnp.arange(S, dtype=jnp.int32))
    cos_s, sin_s = _rope_tables(past + jnp.arange(NQ, dtype=jnp.int32))
    cos_s, sin_s = jnp.tile(cos_s, (DB, 1)), jnp.tile(sin_s, (DB, 1))

    xp = x_prompt.reshape(TP, D)
    xs = jnp.pad(x_sample, ((0, 0), (0, NQ - QS), (0, 0))).reshape(TS, D)
    unpad = lambda a: a.reshape(DB, NQ, -1)[:, :QS]
    outs = [[] for _ in range(9)]
    for l in range(w_in.shape[0]):
        pw = _prep_proj_weights(norm1_g[l], w_in[l], q_norm_g[l], k_norm_slc_g[l], k_norm_win_g[l], mem_q_norm_g[l])
        cw = _prep_compress_weights(cmp_pe_k[l], cmp_w1_k[l], cmp_w2_k[l], cmp_pe_v[l], cmp_w1_v[l], cmp_w2_v[l],
                                    k_norm_cmp_g[l])
        mw = _prep_mix_weights(w_out[l], ret_norm_g[l], norm2_g[l], w_router[l], b_router[l])

        t = _token_mixers(xp, pw, cos_p, sin_p, 512, True)
        kc, vct = _compress(t['kvc'].reshape(B, S, 256), cw)
        o_nsa = _nsa_prompt(B, t['q'], t['gates'], kc, vct, t['ks_bf'], t['vs_t'], t['kw_bf'], t['vw_t'])
        o_ret, r_p = _retention(t['rq'].reshape(B, S, 256), t['rk'].reshape(B, S, 256), t['rv'].reshape(B, S, 256),
                                jnp.zeros((B, 2, LANES, LANES), F32), RET_CHUNK, RET_CHUNK)
        mkv = _mem_kv(mem_prompt.reshape(B * n_mem, D), mem_norm_g[l], w_mem_kv[l], mem_k_norm_g[l])
        o_mem = _mem_attn(t['mq'].reshape(B, S, 256), mkv.reshape(B, n_mem, 512), 512)
        x1_p, h_p, topi_p, topg_p = _mix(xp, o_nsa.reshape(TP, -1), o_ret.reshape(TP, 256), t['rg'],
                                         o_mem.reshape(TP, 256), mw, 512)
        outs[0].append(t['kvc'].reshape(B, S, 2, G, HD))
        outs[1].append(t['kvs'].reshape(B, S, 2, G, HD))
        outs[2].append(t['kvw'].reshape(B, S, 2, G, HD)[:, S - win_rows:])
        outs[3].append(_pairs_to_state(r_p))
        outs[4].append(mkv.reshape(B, n_mem, 2, MEM_HEADS, HD))

        t = _token_mixers(xs, pw, cos_s, sin_s, TS, False)
        win = cache_win_kv[l]
        o_nsa = _nsa_sample(t['q'].astype(F32).reshape(DB, NQ, -1), t['gates'].reshape(DB, NQ, LANES),
                            t['kvs'].reshape(DB, NQ, 256), t['kvw'].reshape(DB, NQ, 256),
                            win.reshape(DB, win.shape[1], 256),
                            cache_cmp_kv[l].reshape(-1, page, 256), cache_slc_kv[l].reshape(-1, page, 256),
                            page_table, cw, QS)
        o_ret, r_s = _retention(t['rq'].reshape(DB, NQ, 256), t['rk'].reshape(DB, NQ, 256),
                                t['rv'].reshape(DB, NQ, 256), _state_to_pairs(state_ret[l].astype(F32)), QS, NQ)
        o_mem = _mem_attn(t['mq'].reshape(DB, NQ, 256), cache_mem_kv[l].reshape(DB, n_mem, 512), NQ)
        x1_s, h_s, topi_s, topg_s = _mix(xs, o_nsa.reshape(TS, -1).astype(BF16), o_ret.reshape(TS, 256), t['rg'],
                                         o_mem.reshape(TS, 256), mw, TS)
        kv5 = lambda a: unpad(a).reshape(DB, QS, 2, G, HD)
        outs[5].append(kv5(t['kvc']))
        outs[6].append(kv5(t['kvs']))
        outs[7].append(jnp.concatenate([win, kv5(t['kvw'])], axis=1)[:, QS:])
        outs[8].append(_pairs_to_state(r_s))

        cat = lambda a, b: jnp.concatenate([a, unpad(b).reshape(DB * QS, -1)], axis=0)
        y = _moe(cat(h_p, h_s), cat(topi_p, topi_s)[:, :TOP_K], cat(topg_p, topg_s)[:, :TOP_K],
                 w_up[l].astype(BF16), b_up[l], w_down[l].astype(BF16), b_down[l])
        xp = x1_p + y[:TP]
        xs_new = unpad(x1_s).reshape(DB * QS, D) + y[TP:]
        xs = jnp.pad(xs_new.reshape(DB, QS, D), ((0, 0), (0, NQ - QS), (0, 0))).reshape(TS, D)
    y_sample = xs.reshape(DB, NQ, D)[:, :QS]
    return (xp.reshape(B, S, D), y_sample) + tuple(jnp.stack(o) for o in outs)
```

```python
import functools

import jax
import jax.numpy as jnp
from jax import lax
from jax.experimental import pallas as pl
from jax.experimental.pallas import tpu as pltpu

F32 = jnp.float32
BF16 = jnp.bfloat16

HEAD_DIM = 64
NSA_HEADS = 8
NSA_GROUPS = 2
NSA_HPG = NSA_HEADS // NSA_GROUPS
RET_HEADS = 4
MEM_HEADS = 4
CMP_BLK = 32
CMP_STRIDE = 16
CMP_HID = 256
SLC_BLK = 64
N_SEL = 16
WINDOW = 512
Q_BLK = 128
RET_CHUNK = 128
N_EXPERTS = 32
TOP_K = 4
D_FF = 1024
SWIGLU_LIMIT = 7.0
SWIGLU_ALPHA = 1.702
EPS = 1e-6
NEG = -1e30
BIG = 1e9
ROPE_BASE = 10000.0
SLC_RATIO = SLC_BLK // CMP_STRIDE
CMP_OVL = CMP_BLK // CMP_STRIDE - 1

LANES = 128
KEY_TILE = 512
MOE_ROWS = 512
VMEM_LIMIT = 56 * 1024 * 1024

C_Q = 0
C_KVC = C_Q + NSA_HEADS * LANES
C_KVS = C_KVC + 256
C_KVW = C_KVS + 256
C_GATE = C_KVW + 256
C_RQ = C_GATE + LANES
C_RK = C_RQ + 256
C_RV = C_RK + 256
C_RG = C_RV + 256
C_MQ = C_RG + 256
C_END = C_MQ + 256


def _cparams(sem):
    return pltpu.CompilerParams(dimension_semantics=sem, vmem_limit_bytes=VMEM_LIMIT)


def _lane(shape):
    return lax.broadcasted_iota(jnp.int32, shape, len(shape) - 1)


def _seg_rms(x, gain):
    lo = _lane(x.shape) < HEAD_DIM
    x2 = x * x
    s_lo = jnp.sum(jnp.where(lo, x2, 0.0), axis=-1, keepdims=True)
    s_hi = jnp.sum(jnp.where(lo, 0.0, x2), axis=-1, keepdims=True)
    ms = jnp.where(lo, s_lo, s_hi) * (1.0 / HEAD_DIM)
    return x * lax.rsqrt(ms + EPS) * gain


def _swap_halves(x):
    first = (_lane(x.shape) & (HEAD_DIM // 2)) == 0
    return jnp.where(first, pltpu.roll(x, LANES - HEAD_DIM // 2, 1), pltpu.roll(x, HEAD_DIM // 2, 1))


def _proj_kernel(x_ref, g1_ref, w_ref, cos_ref, sin_ref, qg_ref, ksg_ref, kwg_ref, mqg_ref,
                 q_out, kvc_out, kvs_out, kvw_out, ks_bf, kw_bf, gate_out,
                 rq_out, rk_out, rv_out, rg_out, mq_out, *extra, seq_tiles):
    x = x_ref[...]
    tm = x.shape[0]
    ms = jnp.mean(x * x, axis=-1, keepdims=True)
    xn = (x * lax.rsqrt(ms + EPS) * g1_ref[...]).astype(BF16)
    z = jnp.dot(xn, w_ref[...], preferred_element_type=F32)
    for i in range(NSA_HEADS):
        zq = z[:, C_Q + i * LANES:C_Q + (i + 1) * LANES]
        msq = jnp.sum(zq * zq, axis=-1, keepdims=True) * (1.0 / HEAD_DIM)
        qn = zq * lax.rsqrt(msq + EPS) * qg_ref[:, i * LANES:(i + 1) * LANES] * (HEAD_DIM ** -0.5)
        if seq_tiles:
            q_out[i * LANES:(i + 1) * LANES, :] = qn.T.astype(BF16)
        else:
            q_out[:, i * LANES:(i + 1) * LANES] = qn.astype(BF16)
    kvc_out[...] = z[:, C_KVC:C_KVC + 256]
    ks = _seg_rms(z[:, C_KVS:C_KVS + LANES], ksg_ref[...])
    vs = z[:, C_KVS + LANES:C_KVS + 256]
    kvs_out[:, 0:LANES] = ks
    kvs_out[:, LANES:256] = vs
    kw = _seg_rms(z[:, C_KVW:C_KVW + LANES], kwg_ref[...])
    vw = z[:, C_KVW + LANES:C_KVW + 256]
    kvw_out[:, 0:LANES] = kw
    kvw_out[:, LANES:256] = vw
    gates = jax.nn.sigmoid(z[:, C_GATE:C_GATE + LANES])
    ks_bf[:, 0:LANES] = ks.astype(BF16)
    if seq_tiles:
        vs_t, vw_t = extra
        pos = (pl.program_id(0) % seq_tiles) * tm + lax.broadcasted_iota(jnp.int32, (tm, LANES), 0)
        ks_bf[:, LANES:256] = jnp.where(_lane((tm, LANES)) == pos // SLC_BLK, 1.0, 0.0).astype(BF16)
        kw_bf[...] = kw.astype(BF16)
        vs_t[...] = vs.T.astype(BF16)
        vw_t[...] = vw.T.astype(BF16)
        gate_out[...] = gates.T
    else:
        ks_bf[:, LANES:256] = vs.astype(BF16)
        kw_bf[:, 0:LANES] = kw.astype(BF16)
        kw_bf[:, LANES:256] = vw.astype(BF16)
        gate_out[...] = gates
    for c in range(2):
        sl = slice(c * LANES, (c + 1) * LANES)
        cos = cos_ref[:, sl]
        sin = sin_ref[:, sl]
        rq = z[:, C_RQ + c * LANES:C_RQ + (c + 1) * LANES]
        rk = z[:, C_RK + c * LANES:C_RK + (c + 1) * LANES]
        rq_out[:, sl] = rq * cos + _swap_halves(rq) * sin
        rk_out[:, sl] = (rk * cos + _swap_halves(rk) * sin) * (HEAD_DIM ** -0.5)
        mq = z[:, C_MQ + c * LANES:C_MQ + (c + 1) * LANES]
        mq_out[:, sl] = _seg_rms(mq, mqg_ref[...]).astype(BF16)
    rv_out[...] = z[:, C_RV:C_RV + 256]
    rg_out[...] = z[:, C_RG:C_RG + 256]


def _project(x, pw, cos, sin, tm, key_major):
    T, D = x.shape
    nt = cos.shape[0] // tm
    row = lambda n: (pl.BlockSpec((tm, n), lambda i: (i, 0)), (T, n))
    col = lambda n: (pl.BlockSpec((n, tm), lambda i: (0, i)), (n, T))
    full = lambda a: pl.BlockSpec(a.shape, lambda i: (0, 0))
    tab = pl.BlockSpec((tm, 256), lambda i: (i % nt, 0))
    tok = col if key_major else row
    outs = [
        (tok(NSA_HEADS * LANES), BF16),
        (row(256), F32), (row(256), F32), (row(256), F32),
        (row(256), BF16), (row(LANES if key_major else 256), BF16),
        (tok(LANES), F32),
        (row(256), F32), (row(256), F32), (row(256), F32), (row(256), F32),
        (row(256), BF16),
    ]
    if key_major:
        outs += [(col(LANES), BF16), (col(LANES), BF16)]
    return pl.pallas_call(
        functools.partial(_proj_kernel, seq_tiles=nt if key_major else 0),
        grid=(T // tm,),
        in_specs=[row(D)[0], full(pw['g1']), full(pw['w']), tab, tab,
                  full(pw['qg']), full(pw['ksg']), full(pw['kwg']), full(pw['mqg'])],
        out_specs=[spec for (spec, _), _ in outs],
        out_shape=[jax.ShapeDtypeStruct(shape, dt) for (_, shape), dt in outs],
        compiler_params=_cparams(("arbitrary",)),
        name="proj",
    )(x, pw['g1'], pw['w'], cos, sin, pw['qg'], pw['ksg'], pw['kwg'], pw['mqg'])


def _prep_proj_weights(norm1_g, w_in, q_norm_g, k_norm_slc_g, k_norm_win_g, mem_q_norm_g):
    D = w_in.shape[0]
    sizes = (512, 256, 256, 256, 24, 256, 256, 256, 256, 256)
    parts, off = [], 0
    for n in sizes:
        parts.append(w_in[:, off:off + n])
        off += n
    wq, wkvc, wkvs, wkvw, wg, wrq, wrk, wrv, wrg, wmq = parts
    zero = jnp.zeros((D, HEAD_DIM), w_in.dtype)
    qcols, qg = [], []
    gz = jnp.zeros((HEAD_DIM,), F32)
    for i in range(NSA_HEADS):
        wh = wq[:, i * HEAD_DIM:(i + 1) * HEAD_DIM]
        if i < NSA_HPG:
            qcols += [wh, zero]
            qg += [q_norm_g, gz]
        else:
            qcols += [zero, wh]
            qg += [gz, q_norm_g]
    wgp = jnp.pad(wg, ((0, 0), (0, LANES - wg.shape[1])))
    w = jnp.concatenate(qcols + [wkvc, wkvs, wkvw, wgp, wrq, wrk, wrv, wrg, wmq], axis=1).astype(BF16)
    two = lambda g: jnp.tile(g, 2)[None, :]
    return dict(g1=norm1_g[None, :], w=w, qg=jnp.concatenate(qg)[None, :],
                ksg=two(k_norm_slc_g), kwg=two(k_norm_win_g), mqg=two(mem_q_norm_g))


def _rope_tables(pos):
    half = HEAD_DIM // 2
    inv = ROPE_BASE ** (-jnp.arange(half, dtype=F32) / half)
    ang = pos.astype(F32)[:, None] * inv[None, :]
    cos, sin = jnp.cos(ang), jnp.sin(ang)
    cos = jnp.tile(jnp.concatenate([cos, cos], axis=1), (1, RET_HEADS))
    sin = jnp.tile(jnp.concatenate([-sin, sin], axis=1), (1, RET_HEADS))
    return cos, sin


def _compress_half(src_ref, pe_ref, wa_ref, wb_ref, w2_ref, nchunks):
    a_parts, b_parts = [], []
    for r in range(CMP_STRIDE):
        xr = src_ref[pl.ds(r, nchunks, stride=CMP_STRIDE), :]
        a_parts.append((xr + pe_ref[r:r + 1, :]).astype(BF16))
        b_parts.append((xr + pe_ref[CMP_STRIDE + r:CMP_STRIDE + r + 1, :]).astype(BF16))
    ha = jnp.dot(jnp.concatenate(a_parts, axis=1), wa_ref[...], preferred_element_type=F32)
    hb = jnp.dot(jnp.concatenate(b_parts, axis=1), wb_ref[...], preferred_element_type=F32)
    h = ha + pltpu.roll(hb, nchunks - 1, 0)
    act = jax.nn.gelu(h).astype(BF16)
    return jnp.dot(act, w2_ref[...], preferred_element_type=F32)


def _compress_kernel(kv_ref, pe_ref, wa_ref, wb_ref, w2_ref, kg_ref, kc_out, vct_out):
    nchunks = kc_out.shape[1]
    y = _compress_half(kv_ref.at[0], pe_ref.at[0], wa_ref.at[0], wb_ref.at[0], w2_ref.at[0], nchunks)

    @pl.when(pl.program_id(1) == 0)
    def _():
        kc_out[0] = _seg_rms(y, kg_ref[...]).astype(BF16)

    @pl.when(pl.program_id(1) == 1)
    def _():
        vct_out[0] = y.T.astype(BF16)


def _compress(kvc, cw):
    B, T, _ = kvc.shape
    nchunks = T // CMP_STRIDE
    per_kv = lambda a: pl.BlockSpec((1,) + a.shape[1:], lambda b, j: (j,) + (0,) * (a.ndim - 1))
    return pl.pallas_call(
        _compress_kernel,
        grid=(B, 2),
        in_specs=[pl.BlockSpec((1, T, LANES), lambda b, j: (b, 0, j)),
                  per_kv(cw['pe']), per_kv(cw['wa']), per_kv(cw['wb']), per_kv(cw['w2']),
                  pl.BlockSpec((1, LANES), lambda b, j: (0, 0))],
        out_specs=[pl.BlockSpec((1, nchunks, LANES), lambda b, j: (b, 0, 0)),
                   pl.BlockSpec((1, LANES, nchunks), lambda b, j: (b, 0, 0))],
        out_shape=[jax.ShapeDtypeStruct((B, nchunks, LANES), BF16),
                   jax.ShapeDtypeStruct((B, LANES, nchunks), BF16)],
        compiler_params=_cparams(("arbitrary", "arbitrary")),
        name="compress",
    )(kvc, cw['pe'], cw['wa'], cw['wb'], cw['w2'], cw['kg'])


def _prep_compress_weights(pe_k, w1_k, w2_k, pe_v, w1_v, w2_v, k_norm_g):
    eye = jnp.eye(NSA_GROUPS, dtype=F32)

    def one(pe, w1, w2):
        w1r = w1.reshape(CMP_BLK, HEAD_DIM, CMP_HID)
        wf = jnp.einsum('gh,rdc->rgdhc', eye, w1r).reshape(CMP_BLK * LANES, NSA_GROUPS * CMP_HID)
        w2f = jnp.einsum('gh,cd->gchd', eye, w2).reshape(NSA_GROUPS * CMP_HID, LANES)
        half = CMP_STRIDE * LANES
        return jnp.tile(pe, (1, NSA_GROUPS)), wf[:half].astype(BF16), wf[half:].astype(BF16), w2f.astype(BF16)

    k, v = one(pe_k, w1_k, w2_k), one(pe_v, w1_v, w2_v)
    st = lambda i: jnp.stack([k[i], v[i]])
    return dict(pe=st(0), wa=st(1), wb=st(2), w2=st(3), kg=jnp.tile(k_norm_g, 2)[None, :])


def _dot_nt(a, b):
    return lax.dot_general(a, b, (((1,), (1,)), ((), ())), preferred_element_type=F32)


def _split3(x):
    hi = x.astype(BF16)
    r = x - hi.astype(F32)
    mid = r.astype(BF16)
    lo = (r - mid.astype(F32)).astype(BF16)
    return hi, mid, lo


def _select_bias(ps, cur, rounds, axis, cur_in_range=True):
    j = lax.broadcasted_iota(jnp.int32, ps.shape, axis)
    last = cur if cur_in_range else cur - 1
    key = jnp.where(j <= last, ps, NEG)
    for forced in (0, last, cur - 1):
        key = jnp.where(j == forced, BIG, key)
    bias = jnp.full(ps.shape, NEG, F32)
    for _ in range(rounds):
        m = jnp.max(key, axis=axis, keepdims=True)
        idx = jnp.min(jnp.where(key == m, j, LANES), axis=axis, keepdims=True)
        pick = j == idx
        bias = jnp.where(pick, 0.0, bias)
        key = jnp.where(pick, -jnp.inf, key)
    return bias


def _cmp_branch(qg, qpos, kc, vc):
    s = _dot_nt(qg, kc)
    cend = _lane((1, kc.shape[0])) * CMP_STRIDE + (CMP_BLK - 1)
    s = jnp.where(cend <= qpos, s, NEG)
    m = jnp.max(s, axis=-1, keepdims=True)
    p = jnp.exp(s - m)
    p = p * (1.0 / jnp.sum(p, axis=-1, keepdims=True))
    p = jnp.where(qpos >= CMP_BLK - 1, p, 0.0)
    return p, jnp.dot(p.astype(BF16), vc, preferred_element_type=F32)


def _sel_tile(qaug, k_tile, v_tile, blk0, mask, m_sc, l_sc, acc_sc):
    n = k_tile.shape[0]
    blk = blk0 + lax.broadcasted_iota(jnp.int32, (n, LANES), 0) // SLC_BLK
    onehot = jnp.where(_lane((n, LANES)) == blk, 1.0, 0.0).astype(BF16)
    s = _dot_nt(qaug, jnp.concatenate([k_tile, onehot], axis=1))
    if mask is not None:
        s = jnp.where(mask, s, NEG)
    _online_update(s, v_tile, m_sc, l_sc, acc_sc)


def _online_update(s, v_tile, m_sc, l_sc, acc_sc):
    m_old = m_sc[...]
    m_new = jnp.maximum(m_old, jnp.max(s, axis=-1, keepdims=True))
    alpha = jnp.exp(m_old - m_new)
    p = jnp.exp(s - m_new)
    l_sc[...] = alpha * l_sc[...] + jnp.sum(p, axis=-1, keepdims=True)
    acc_sc[...] = alpha * acc_sc[...] + jnp.dot(p.astype(BF16), v_tile, preferred_element_type=F32)
    m_sc[...] = m_new


def _gate_heads(gates, g, oc, osel, ow, nq):
    heads = []
    for h in range(NSA_HPG):
        rs = slice(h * nq, (h + 1) * nq)
        c0 = (NSA_HPG * g + h) * 3
        heads.append(gates[:, c0:c0 + 1] * oc[rs] + gates[:, c0 + 1:c0 + 2] * osel[rs]
                     + gates[:, c0 + 2:c0 + 3] * ow[rs])
    return heads


def _store_heads(heads, o_ref):
    low = _lane(heads[0].shape) < HEAD_DIM
    for pr in range(NSA_HEADS // 2):
        even, odd = heads[2 * pr], heads[2 * pr + 1]
        if pr < NSA_GROUPS:
            odd = pltpu.roll(odd, HEAD_DIM, 1)
        else:
            even = pltpu.roll(even, HEAD_DIM, 1)
        o_ref[0, :, pr * LANES:(pr + 1) * LANES] = jnp.where(low, even, odd).astype(o_ref.dtype)


def _nsa_sample_kernel(pt_ref, q_ref, gate_ref, ksn_ref, kwn_ref, win_ref, pool_c, pool_s,
                       pe_ref, wa_ref, wb_ref, w2_ref, kg_ref, mmap_ref, o_ref,
                       cbuf, sbuf, sems, m_sc, l_sc, acc_sc, *, page, n_new):
    b = pl.program_id(0)
    past = sbuf.shape[0]
    npages = past // page
    nq = q_ref.shape[1]

    def page_copies(p, pg):
        dst = pl.ds(pl.multiple_of(p * page, page), page)
        return (pltpu.make_async_copy(pool_c.at[pg, :, 0:LANES], cbuf.at[0, dst], sems.at[0]),
                pltpu.make_async_copy(pool_c.at[pg, :, LANES:256], cbuf.at[1, dst], sems.at[1]),
                pltpu.make_async_copy(pool_s.at[pg], sbuf.at[dst], sems.at[2]))

    def start(p, c):
        for cp in page_copies(p, pt_ref[b * npages + p]):
            cp.start()
        return c

    def wait(p, c):
        for cp in page_copies(p, 0):
            cp.wait()
        return c

    lax.fori_loop(0, npages, start, 0)
    lax.fori_loop(0, npages, wait, 0)

    nchunks = past // CMP_STRIDE
    kc = _seg_rms(_compress_half(cbuf.at[0], pe_ref.at[0], wa_ref.at[0], wb_ref.at[0], w2_ref.at[0], nchunks),
                  kg_ref[...]).astype(BF16)
    vc = _compress_half(cbuf.at[1], pe_ref.at[1], wa_ref.at[1], wb_ref.at[1], w2_ref.at[1], nchunks).astype(BF16)

    rows = NSA_HPG * nq
    srow = lax.broadcasted_iota(jnp.int32, (rows, 1), 0) % nq
    qpos = past + srow
    cur = (past + lax.broadcasted_iota(jnp.int32, (nq, 1), 0)) // SLC_BLK
    gates = gate_ref[0]
    pad_keys = lambda x: jnp.concatenate([x, jnp.zeros((LANES - nq, LANES), F32)], axis=0).astype(BF16)
    new_ok = _lane((1, LANES)) <= jnp.minimum(srow, n_new - 1)
    heads = []
    for g in range(NSA_GROUPS):
        qg = jnp.concatenate(
            [q_ref[0, :, (NSA_HPG * g + h) * LANES:(NSA_HPG * g + h + 1) * LANES] for h in range(NSA_HPG)],
            axis=0).astype(BF16)
        p, oc = _cmp_branch(qg, qpos, kc, vc)
        p4 = p[0:nq] + p[nq:2 * nq] + p[2 * nq:3 * nq] + p[3 * nq:4 * nq]
        ps = sum(_dot_nt(part, mmap_ref[...]) for part in _split3(p4))
        bias = _select_bias(ps, cur, N_SEL - 1, 1, cur_in_range=False)
        qaug = jnp.concatenate([qg, jnp.concatenate([bias] * NSA_HPG, axis=0).astype(BF16)], axis=1)
        m_sc[...] = jnp.full(m_sc.shape, NEG, F32)
        l_sc[...] = jnp.zeros(l_sc.shape, F32)
        acc_sc[...] = jnp.zeros(acc_sc.shape, F32)

        def body(kt, c, qaug=qaug):
            k0 = pl.multiple_of(kt * KEY_TILE, KEY_TILE)
            _sel_tile(qaug, sbuf[pl.ds(k0, KEY_TILE), 0:LANES].astype(BF16),
                      sbuf[pl.ds(k0, KEY_TILE), LANES:256].astype(BF16),
                      kt * (KEY_TILE // SLC_BLK), None, m_sc, l_sc, acc_sc)
            return c

        lax.fori_loop(0, past // KEY_TILE, body, 0)
        s_new = jnp.where(new_ok, _dot_nt(qg, pad_keys(ksn_ref[0, :, 0:LANES])), NEG)
        _online_update(s_new, pad_keys(ksn_ref[0, :, LANES:256]), m_sc, l_sc, acc_sc)
        osel = acc_sc[...] * (1.0 / l_sc[...])
        nwin = win_ref.shape[1]
        sw_old = jnp.where(_lane((1, nwin)) + (WINDOW - nwin) > srow,
                           _dot_nt(qg, win_ref[0, :, 0:LANES].astype(BF16)), NEG)
        sw_new = jnp.where(new_ok, _dot_nt(qg, pad_keys(kwn_ref[0, :, 0:LANES])), NEG)
        sw = jnp.concatenate([sw_old, sw_new], axis=1)
        mw = jnp.max(sw, axis=-1, keepdims=True)
        pw = jnp.exp(sw - mw)
        ow = (jnp.dot(pw[:, :nwin].astype(BF16), win_ref[0, :, LANES:256].astype(BF16), preferred_element_type=F32)
              + jnp.dot(pw[:, nwin:].astype(BF16), pad_keys(kwn_ref[0, :, LANES:256]), preferred_element_type=F32))
        ow = ow * (1.0 / jnp.sum(pw, axis=-1, keepdims=True))
        heads += _gate_heads(gates, g, oc, osel, ow, nq)
    _store_heads(heads, o_ref)


def _nsa_sample(q, gates, ksn, kwn, win, pool_c, pool_s, page_table, cw, n_new):
    DB, nq, _ = q.shape
    n_pages = page_table.shape[1]
    page = pool_c.shape[1]
    past = n_pages * page
    nchunks = past // CMP_STRIDE
    rows = NSA_HPG * nq
    mmap = _importance_map(nchunks)
    per_seq = lambda a: pl.BlockSpec((1,) + a.shape[1:], lambda b, pt: (b,) + (0,) * (a.ndim - 1))
    const = lambda a: pl.BlockSpec(a.shape, lambda b, pt: (0,) * a.ndim)
    hbm = pl.BlockSpec(memory_space=pl.ANY)
    grid_spec = pltpu.PrefetchScalarGridSpec(
        num_scalar_prefetch=1,
        grid=(DB,),
        in_specs=[per_seq(q), per_seq(gates), per_seq(ksn), per_seq(kwn), per_seq(win), hbm, hbm,
                  const(cw['pe']), const(cw['wa']), const(cw['wb']), const(cw['w2']), const(cw['kg']), const(mmap)],
        out_specs=pl.BlockSpec((1, nq, NSA_HEADS * HEAD_DIM), lambda b, pt: (b, 0, 0)),
        scratch_shapes=[pltpu.VMEM((2, past, LANES), F32), pltpu.VMEM((past, 256), F32),
                        pltpu.SemaphoreType.DMA((3,)),
                        pltpu.VMEM((rows, 1), F32), pltpu.VMEM((rows, 1), F32), pltpu.VMEM((rows, LANES), F32)],
    )
    return pl.pallas_call(
        functools.partial(_nsa_sample_kernel, page=page, n_new=n_new),
        grid_spec=grid_spec,
        out_shape=jax.ShapeDtypeStruct((DB, nq, NSA_HEADS * HEAD_DIM), F32),
        compiler_params=_cparams(("arbitrary",)),
        name="nsa_sample",
    )(page_table.reshape(-1), q, gates, ksn, kwn, win, pool_c, pool_s,
      cw['pe'], cw['wa'], cw['wb'], cw['w2'], cw['kg'], mmap)


def _nsa_prompt_kernel(q_ref, gate_ref, kc_ref, vct_ref, ks_ref, vst_ref, kw_ref, vwt_ref, mmap_ref, o_ref,
                       m_sc, l_sc, acc_sc):
    i = pl.program_id(1)
    s0 = i * Q_BLK
    cols = NSA_HPG * Q_BLK
    qpos = s0 + _lane((1, cols)) % Q_BLK
    cur = (s0 + _lane((1, Q_BLK))) // SLC_BLK
    kc, vct = kc_ref[0], vct_ref[0]
    ncmp = kc.shape[0]
    n_full = s0 // KEY_TILE
    w0 = pl.multiple_of(jnp.maximum(s0 - WINDOW, 0), Q_BLK)
    nw = WINDOW + Q_BLK
    dot = functools.partial(jnp.dot, preferred_element_type=F32)

    def softmax_keys(s):
        p = jnp.exp(s - jnp.max(s, axis=0, keepdims=True))
        return p, 1.0 / jnp.sum(p, axis=0, keepdims=True)

    qaugs = []
    branch = []
    for g in range(NSA_GROUPS):
        qt = jnp.concatenate([q_ref[(NSA_HPG * g + h) * LANES:(NSA_HPG * g + h + 1) * LANES, :]
                              for h in range(NSA_HPG)], axis=1)
        s = dot(kc, qt)
        n_idx = lax.broadcasted_iota(jnp.int32, (ncmp, cols), 0)
        s = jnp.where(n_idx * CMP_STRIDE + (CMP_BLK - 1) <= qpos, s, NEG)
        p, inv = softmax_keys(s)
        p = jnp.where(qpos >= CMP_BLK - 1, p * inv, 0.0)
        oc = dot(vct, p.astype(BF16))
        p4 = p[:, 0:Q_BLK] + p[:, Q_BLK:2 * Q_BLK] + p[:, 2 * Q_BLK:3 * Q_BLK] + p[:, 3 * Q_BLK:4 * Q_BLK]
        ps_t = sum(dot(mmap_ref[...], part) for part in _split3(p4))
        bias = _select_bias(ps_t, cur, N_SEL, 0).astype(BF16)
        qaugs.append(jnp.concatenate([qt, jnp.concatenate([bias] * NSA_HPG, axis=1)], axis=0))
        sw = dot(kw_ref[0, pl.ds(w0, nw), :], qt)
        d = qpos - (w0 + lax.broadcasted_iota(jnp.int32, (nw, cols), 0))
        sw = jnp.where(d >= 0, jnp.where(d < WINDOW, sw, NEG), NEG)
        pw, invw = softmax_keys(sw)
        ow = dot(vwt_ref[:, pl.ds(w0, nw)], pw.astype(BF16)) * invw
        branch.append((oc, ow))
        m_sc[g] = jnp.full(m_sc.shape[1:], NEG, F32)
        l_sc[g] = jnp.zeros(l_sc.shape[1:], F32)
        acc_sc[g] = jnp.zeros(acc_sc.shape[1:], F32)

    def tile(kt, causal):
        k0 = pl.multiple_of(kt * KEY_TILE, KEY_TILE)
        k_aug = ks_ref[0, pl.ds(k0, KEY_TILE), :]
        v_t = vst_ref[:, pl.ds(k0, KEY_TILE)]
        for g in range(NSA_GROUPS):
            s = dot(k_aug, qaugs[g])
            if causal:
                s = jnp.where(k0 + lax.broadcasted_iota(jnp.int32, s.shape, 0) <= qpos, s, NEG)
            m_old = m_sc[g]
            m_new = jnp.maximum(m_old, jnp.max(s, axis=0, keepdims=True))
            alpha = jnp.exp(m_old - m_new)
            p = jnp.exp(s - m_new)
            l_sc[g] = alpha * l_sc[g] + jnp.sum(p, axis=0, keepdims=True)
            acc_sc[g] = alpha * acc_sc[g] + dot(v_t, p.astype(BF16))
            m_sc[g] = m_new

    def body(kt, c):
        tile(kt, False)
        return c

    lax.fori_loop(0, n_full, body, 0)
    tile(n_full, True)

    pieces = []
    for g in range(NSA_GROUPS):
        oc, ow = branch[g]
        osel = acc_sc[g] * (1.0 / l_sc[g])
        for h in range(NSA_HPG):
            cs = slice(h * Q_BLK, (h + 1) * Q_BLK)
            c0 = (NSA_HPG * g + h) * 3
            o_h = (gate_ref[c0:c0 + 1, :] * oc[:, cs] + gate_ref[c0 + 1:c0 + 2, :] * osel[:, cs]
                   + gate_ref[c0 + 2:c0 + 3, :] * ow[:, cs])
            pieces.append(o_h[g * HEAD_DIM:(g + 1) * HEAD_DIM, :])
    o_ref[0] = jnp.concatenate(pieces, axis=0).T.astype(o_ref.dtype)


def _importance_map(ncmp):
    j = jnp.arange(LANES)[:, None]
    n = jnp.arange(ncmp)[None, :]
    return ((n >= SLC_RATIO * j - CMP_OVL) & (n < SLC_RATIO * j + SLC_RATIO)).astype(BF16)


def _nsa_prompt(B, qt, gates_t, kc, vct, ks_aug, vs_t, kw, vw_t):
    S = qt.shape[1] // B
    nq = S // Q_BLK
    ncmp = kc.shape[1]
    cols = NSA_HPG * Q_BLK
    mmap = _importance_map(ncmp)
    per_block = lambda n: pl.BlockSpec((n, Q_BLK), lambda b, i: (0, b * nq + i))
    seq_rows = lambda n: pl.BlockSpec((1, S, n), lambda b, i: (b, 0, 0))
    seq_cols = pl.BlockSpec((LANES, S), lambda b, i: (0, b))
    return pl.pallas_call(
        _nsa_prompt_kernel,
        grid=(B, nq),
        in_specs=[per_block(qt.shape[0]), per_block(LANES),
                  pl.BlockSpec((1, ncmp, LANES), lambda b, i: (b, 0, 0)),
                  pl.BlockSpec((1, LANES, ncmp), lambda b, i: (b, 0, 0)),
                  seq_rows(256), seq_cols, seq_rows(LANES), seq_cols,
                  pl.BlockSpec(mmap.shape, lambda b, i: (0, 0))],
        out_specs=pl.BlockSpec((1, Q_BLK, NSA_HEADS * HEAD_DIM), lambda b, i: (b, i, 0)),
        out_shape=jax.ShapeDtypeStruct((B, S, NSA_HEADS * HEAD_DIM), BF16),
        scratch_shapes=[pltpu.VMEM((NSA_GROUPS, 1, cols), F32), pltpu.VMEM((NSA_GROUPS, 1, cols), F32),
                        pltpu.VMEM((NSA_GROUPS, LANES, cols), F32)],
        compiler_params=_cparams(("arbitrary", "arbitrary")),
        name="nsa_prompt",
    )(qt, gates_t, kc, vct, ks_aug.reshape(B, S, 256), vs_t, kw.reshape(B, S, LANES), vw_t, mmap)


def _ret_kernel(rq_ref, rk_ref, rv_ref, r0_ref, dmask_ref, xi_ref, zeta_ref, dec_ref, o_ref, rout_ref, r_sc):
    @pl.when(pl.program_id(1) == 0)
    def _():
        r_sc[...] = r0_ref[0]

    C = rq_ref.shape[1]
    low = _lane((C, LANES)) < HEAD_DIM
    diag = lax.broadcasted_iota(jnp.int32, (LANES, LANES), 0) // HEAD_DIM == _lane((LANES, LANES)) // HEAD_DIM
    for pr in range(RET_HEADS // 2):
        sl = slice(pr * LANES, (pr + 1) * LANES)
        k = rk_ref[0, :, sl]
        qb, kb, vb = rq_ref[0, :, sl].astype(BF16), k.astype(BF16), rv_ref[0, :, sl].astype(BF16)
        zero = jnp.zeros_like(qb)
        s0 = _dot_nt(jnp.where(low, qb, zero), kb) * dmask_ref[2 * pr]
        s1 = _dot_nt(jnp.where(low, zero, qb), kb) * dmask_ref[2 * pr + 1]
        o = jnp.where(low, jnp.dot(s0.astype(BF16), vb, preferred_element_type=F32),
                      jnp.dot(s1.astype(BF16), vb, preferred_element_type=F32))
        r = r_sc[pr]
        o_ref[0, :, sl] = o + jnp.dot(qb, r.astype(BF16), preferred_element_type=F32) * xi_ref[:, sl]
        kz = (k * zeta_ref[:, sl]).astype(BF16)
        upd = lax.dot_general(kz, vb, (((0,), (0,)), ((), ())), preferred_element_type=F32)
        r_sc[pr] = dec_ref[:, sl] * r + jnp.where(diag, upd, 0.0)
    rout_ref[0] = r_sc[...]


def _ret_tables(c_true, c_pad):
    lg = jnp.log(1.0 - 2.0 ** (-5.0 - jnp.arange(RET_HEADS, dtype=F32)))
    idx = jnp.arange(c_pad, dtype=F32)
    diff = idx[:, None] - idx[None, :]
    dmask = jnp.where(diff >= 0, jnp.exp(jnp.maximum(diff, 0.0)[None] * lg[:, None, None]), 0.0)
    lanes = lambda a: jnp.repeat(a, HEAD_DIM, axis=-1)
    xi = lanes(jnp.exp((idx + 1.0)[:, None] * lg[None, :]))
    zeta = lanes(jnp.exp((c_true - 1.0 - idx)[:, None] * lg[None, :]))
    dec = lanes(jnp.exp(c_true * lg)[None, :])
    return dmask, xi, zeta, dec


def _retention(rq, rk, rv, r0, c_true, c_pad):
    B, T, _ = rq.shape
    dmask, xi, zeta, dec = _ret_tables(c_true, c_pad)
    row = pl.BlockSpec((1, c_pad, 256), lambda b, c: (b, c, 0))
    st = pl.BlockSpec((1, 2, LANES, LANES), lambda b, c: (b, 0, 0, 0))
    const = lambda a: pl.BlockSpec(a.shape, lambda b, c: (0,) * a.ndim)
    return pl.pallas_call(
        _ret_kernel,
        grid=(B, T // c_pad),
        in_specs=[row, row, row, st, const(dmask), const(xi), const(zeta), const(dec)],
        out_specs=[row, st],
        out_shape=[jax.ShapeDtypeStruct((B, T, 256), F32), jax.ShapeDtypeStruct((B, 2, LANES, LANES), F32)],
        scratch_shapes=[pltpu.VMEM((2, LANES, LANES), F32)],
        compiler_params=_cparams(("arbitrary", "arbitrary")),
        name="retention",
    )(rq, rk, rv, r0, dmask, xi, zeta, dec)


def _state_to_pairs(r):
    B = r.shape[0]
    r = r.reshape(B, 2, 2, HEAD_DIM, HEAD_DIM)
    eye = jnp.eye(2, dtype=r.dtype)
    return jnp.einsum('bphde,hk->bphdke', r, eye).reshape(B, 2, LANES, LANES)


def _pairs_to_state(rp):
    B = rp.shape[0]
    rp = rp.reshape(B, 2, 2, HEAD_DIM, 2, HEAD_DIM)
    return jnp.stack([rp[:, :, 0, :, 0, :], rp[:, :, 1, :, 1, :]], axis=2).reshape(B, RET_HEADS, HEAD_DIM, HEAD_DIM)


def _mem_kv_kernel(m_ref, g_ref, w_ref, kg_ref, kv_out):
    x = m_ref[...]
    ms = jnp.mean(x * x, axis=-1, keepdims=True)
    xn = (x * lax.rsqrt(ms + EPS) * g_ref[...]).astype(BF16)
    z = jnp.dot(xn, w_ref[...], preferred_element_type=F32)
    for c in range(2):
        kv_out[:, c * LANES:(c + 1) * LANES] = _seg_rms(z[:, c * LANES:(c + 1) * LANES], kg_ref[...])
    kv_out[:, 256:512] = z[:, 256:512]


def _mem_kv(mem, mem_norm_g, w_mem_kv, mem_k_norm_g):
    T, D = mem.shape
    w = w_mem_kv.astype(BF16)
    full = lambda a: pl.BlockSpec(a.shape, lambda i: (0, 0))
    g = mem_norm_g[None, :]
    kg = jnp.tile(mem_k_norm_g, 2)[None, :]
    return pl.pallas_call(
        _mem_kv_kernel,
        grid=(1,),
        in_specs=[full(mem), full(g), full(w), full(kg)],
        out_specs=pl.BlockSpec((T, 512), lambda i: (0, 0)),
        out_shape=jax.ShapeDtypeStruct((T, 512), F32),
        compiler_params=_cparams(("arbitrary",)),
        name="mem_kv",
    )(mem, g, w, kg)


def _mem_attn_kernel(mq_ref, mkv_ref, o_ref):
    q = mq_ref[0]
    rows = q.shape[0]
    low = _lane((rows, LANES)) < HEAD_DIM
    zero = jnp.zeros((rows, LANES), F32)
    for pr in range(MEM_HEADS // 2):
        sl = slice(pr * LANES, (pr + 1) * LANES)
        qp = q[:, sl].astype(F32)
        k = mkv_ref[0, :, sl].astype(BF16)
        v = mkv_ref[0, :, 256 + pr * LANES:256 + (pr + 1) * LANES].astype(BF16)
        outs = []
        for hh in range(2):
            qm = jnp.where(low, qp, zero) if hh == 0 else jnp.where(low, zero, qp)
            s = _dot_nt(qm.astype(BF16), k) * (HEAD_DIM ** -0.5)
            m = jnp.max(s, axis=-1, keepdims=True)
            p = jnp.exp(s - m)
            o = jnp.dot(p.astype(BF16), v, preferred_element_type=F32)
            outs.append(o * (1.0 / jnp.sum(p, axis=-1, keepdims=True)))
        o_ref[0, :, sl] = jnp.where(low, outs[0], outs[1]).astype(o_ref.dtype)


def _mem_attn(mq, mkv, tm):
    B, R, _ = mq.shape
    return pl.pallas_call(
        _mem_attn_kernel,
        grid=(B, R // tm),
        in_specs=[pl.BlockSpec((1, tm, 256), lambda b, i: (b, i, 0)),
                  pl.BlockSpec((1,) + mkv.shape[1:], lambda b, i: (b, 0, 0))],
        out_specs=pl.BlockSpec((1, tm, 256), lambda b, i: (b, i, 0)),
        out_shape=jax.ShapeDtypeStruct((B, R, 256), BF16),
        compiler_params=_cparams(("arbitrary", "arbitrary")),
        name="mem_attn",
    )(mq, mkv)


def _mix_kernel(x_ref, onsa_ref, oret_ref, rg_ref, omem_ref, wout_ref, rgain_ref, g2_ref,
                wr_hi_ref, wr_lo_ref, br_ref, x1_out, h_out, topi_out, topg_out):
    parts = [onsa_ref[...]]
    for c in range(2):
        sl = slice(c * LANES, (c + 1) * LANES)
        parts.append((_seg_rms(oret_ref[:, sl], rgain_ref[...]) * jax.nn.silu(rg_ref[:, sl])).astype(BF16))
    parts.append(omem_ref[...])
    mix = jnp.concatenate(parts, axis=1)
    x1 = x_ref[...] + jnp.dot(mix, wout_ref[...], preferred_element_type=F32)
    x1_out[...] = x1
    ms = jnp.mean(x1 * x1, axis=-1, keepdims=True)
    h = x1 * lax.rsqrt(ms + EPS) * g2_ref[...]
    h_out[...] = h
    h_hi, h_lo, _ = _split3(h)
    logits = (jnp.dot(h_hi, wr_hi_ref[...], preferred_element_type=F32)
              + jnp.dot(h_hi, wr_lo_ref[...], preferred_element_type=F32)
              + jnp.dot(h_lo, wr_hi_ref[...], preferred_element_type=F32)) + br_ref[...]
    lane = _lane(logits.shape)
    key = logits
    topi = jnp.zeros(logits.shape, jnp.int32)
    topv = jnp.zeros(logits.shape, F32)
    for r in range(TOP_K):
        m = jnp.max(key, axis=-1, keepdims=True)
        idx = jnp.min(jnp.where(key == m, lane, LANES), axis=-1, keepdims=True)
        if r == 0:
            m0 = m
        topi = jnp.where(lane == r, idx, topi)
        topv = jnp.where(lane == r, jnp.exp(m - m0), topv)
        key = jnp.where(lane == idx, -jnp.inf, key)
    topi_out[...] = topi
    topg_out[...] = topv * (1.0 / jnp.sum(topv, axis=-1, keepdims=True))


def _mix(x, onsa, oret, rg, omem, mw, tm):
    T, D = x.shape
    row = lambda n: pl.BlockSpec((tm, n), lambda i: (i, 0))
    full = lambda a: pl.BlockSpec(a.shape, lambda i: (0, 0))
    names = ('wout', 'rgain', 'g2', 'wr_hi', 'wr_lo', 'br')
    return pl.pallas_call(
        _mix_kernel,
        grid=(T // tm,),
        in_specs=[row(D), row(512), row(256), row(256), row(256)] + [full(mw[n]) for n in names],
        out_specs=[row(D), row(D), row(LANES), row(LANES)],
        out_shape=[jax.ShapeDtypeStruct((T, D), F32), jax.ShapeDtypeStruct((T, D), F32),
                   jax.ShapeDtypeStruct((T, LANES), jnp.int32), jax.ShapeDtypeStruct((T, LANES), F32)],
        compiler_params=_cparams(("arbitrary",)),
        name="mix",
    )(x, onsa, oret, rg, omem, *[mw[n] for n in names])


def _prep_mix_weights(w_out, ret_norm_g, norm2_g, w_router, b_router):
    wr = jnp.pad(w_router, ((0, 0), (0, LANES - N_EXPERTS)))
    wr_hi = wr.astype(BF16)
    wr_lo = (wr - wr_hi.astype(F32)).astype(BF16)
    br = jnp.concatenate([b_router.astype(F32), jnp.full((LANES - N_EXPERTS,), NEG, F32)])[None, :]
    return dict(wout=w_out.astype(BF16), rgain=jnp.tile(ret_norm_g, 2)[None, :], g2=norm2_g[None, :],
                wr_hi=wr_hi, wr_lo=wr_lo, br=br)


MOE_TOK_TILE = 128


def _row_copy(src, i, dst, j, sem):
    return pltpu.make_async_copy(src.at[pl.ds(i, 1)], dst.at[pl.ds(j, 1)], sem)


def _dispatch_kernel(dest_ref, h_hbm, xb_in, xb_out, sem):
    del xb_in
    tile = dest_ref.shape[0] // TOP_K
    base = pl.program_id(0) * tile

    def start(t, c):
        for k in range(TOP_K):
            _row_copy(h_hbm, base + t, xb_out, dest_ref[t * TOP_K + k], sem).start()
        return c

    def wait(t, c):
        for k in range(TOP_K):
            _row_copy(h_hbm, 0, xb_out, 0, sem).wait()
        return c

    lax.fori_loop(0, tile, start, 0)
    lax.fori_loop(0, tile, wait, 0)


def _dispatch(h, dest, xb):
    T, D = h.shape
    hbm = pl.BlockSpec(memory_space=pl.ANY)
    n = MOE_TOK_TILE * TOP_K
    return pl.pallas_call(
        _dispatch_kernel,
        grid=(T // MOE_TOK_TILE,),
        in_specs=[pl.BlockSpec((n,), lambda i: (i,), memory_space=pltpu.SMEM), hbm, hbm],
        out_specs=hbm,
        out_shape=jax.ShapeDtypeStruct(xb.shape, xb.dtype),
        scratch_shapes=[pltpu.SemaphoreType.DMA(())],
        input_output_aliases={2: 0},
        compiler_params=_cparams(("arbitrary",)),
        name="moe_dispatch",
    )(dest.reshape(-1), h, xb)


def _moe_kernel(be_ref, nb_ref, x_ref, wup_ref, bup_ref, wdn_ref, bdn_ref, y_ref, wup_bf, wdn_bf):
    j = pl.program_id(0)

    @pl.when(j < nb_ref[0])
    def _():
        @pl.when(jnp.logical_or(j == 0, be_ref[j] != be_ref[jnp.maximum(j - 1, 0)]))
        def _():
            wup_bf[...] = wup_ref[0].astype(BF16)
            wdn_bf[...] = wdn_ref[0].astype(BF16)

        up = jnp.dot(x_ref[...].astype(BF16), wup_bf[...], preferred_element_type=F32) + bup_ref[0]
        x_glu = jnp.minimum(up[:, :D_FF], SWIGLU_LIMIT)
        x_lin = jnp.clip(up[:, D_FF:], -SWIGLU_LIMIT, SWIGLU_LIMIT)
        act = x_glu * jax.nn.sigmoid(SWIGLU_ALPHA * x_glu) * (x_lin + 1.0)
        y_ref[...] = jnp.dot(act.astype(BF16), wdn_bf[...], preferred_element_type=F32) + bdn_ref[0]

    @pl.when(j >= nb_ref[0])
    def _():
        y_ref[...] = jnp.zeros(y_ref.shape, y_ref.dtype)


def _moe_experts(xb, blk_expert, n_used, w_up, b_up, w_down, b_down):
    P, D = xb.shape
    n_blk = P // MOE_ROWS
    grid_spec = pltpu.PrefetchScalarGridSpec(
        num_scalar_prefetch=2,
        grid=(n_blk,),
        in_specs=[pl.BlockSpec((MOE_ROWS, D), lambda j, be, nb: (j, 0)),
                  pl.BlockSpec((1, D, 2 * D_FF), lambda j, be, nb: (be[j], 0, 0)),
                  pl.BlockSpec((1, 1, 2 * D_FF), lambda j, be, nb: (be[j], 0, 0)),
                  pl.BlockSpec((1, D_FF, D), lambda j, be, nb: (be[j], 0, 0)),
                  pl.BlockSpec((1, 1, D), lambda j, be, nb: (be[j], 0, 0))],
        out_specs=pl.BlockSpec((MOE_ROWS, D), lambda j, be, nb: (j, 0)),
        scratch_shapes=[pltpu.VMEM((D, 2 * D_FF), BF16), pltpu.VMEM((D_FF, D), BF16)],
    )
    return pl.pallas_call(
        _moe_kernel,
        grid_spec=grid_spec,
        out_shape=jax.ShapeDtypeStruct((P, D), F32),
        compiler_params=_cparams(("arbitrary",)),
        name="moe_experts",
    )(blk_expert, n_used, xb, w_up, b_up[:, None, :], w_down, b_down[:, None, :])


def _combine_kernel(dest_ref, g_ref, x1_ref, yb_hbm, out_ref, buf, sem):
    tile = x1_ref.shape[0]

    def start(t, c):
        for k in range(TOP_K):
            _row_copy(yb_hbm, dest_ref[t * TOP_K + k], buf.at[k], t, sem).start()
        return c

    def wait(t, c):
        for k in range(TOP_K):
            _row_copy(yb_hbm, 0, buf.at[k], 0, sem).wait()
        return c

    lax.fori_loop(0, tile, start, 0)
    lax.fori_loop(0, tile, wait, 0)
    acc = x1_ref[...]
    for k in range(TOP_K):
        acc = acc + g_ref[:, k:k + 1] * buf[k]
    out_ref[...] = acc


def _combine(x1, topg, dest, yb):
    T, D = x1.shape
    tile = MOE_TOK_TILE
    row = lambda n: pl.BlockSpec((tile, n), lambda i: (i, 0))
    return pl.pallas_call(
        _combine_kernel,
        grid=(T // tile,),
        in_specs=[pl.BlockSpec((tile * TOP_K,), lambda i: (i,), memory_space=pltpu.SMEM),
                  row(LANES), row(D), pl.BlockSpec(memory_space=pl.ANY)],
        out_specs=row(D),
        out_shape=jax.ShapeDtypeStruct((T, D), F32),
        scratch_shapes=[pltpu.VMEM((TOP_K, tile, D), F32), pltpu.SemaphoreType.DMA(())],
        compiler_params=_cparams(("arbitrary",)),
        name="moe_combine",
    )(dest.reshape(-1), topg, x1, yb)


def _route(topi):
    T = topi.shape[0]
    onehot = (topi[:, :, None] == jnp.arange(N_EXPERTS, dtype=jnp.int32)).astype(jnp.int32).sum(axis=1)
    rank = jnp.cumsum(onehot, axis=0) - onehot
    counts = onehot.sum(axis=0)
    padded = (counts + MOE_ROWS - 1) // MOE_ROWS * MOE_ROWS
    pad_end = jnp.cumsum(padded)
    pad_start = pad_end - padded
    dest = pad_start[topi] + jnp.take_along_axis(rank, topi, axis=1)
    n_blk = (T * TOP_K + N_EXPERTS * (MOE_ROWS - 1) + MOE_ROWS - 1) // MOE_ROWS
    blk_start = jnp.arange(n_blk, dtype=jnp.int32) * MOE_ROWS
    blk_expert = jnp.minimum((pad_end[None, :] <= blk_start[:, None]).sum(axis=1), N_EXPERTS - 1)
    return dest.astype(jnp.int32), blk_expert.astype(jnp.int32), (pad_end[-1:] // MOE_ROWS).astype(jnp.int32), n_blk


SAMPLE_ROWS = 8


def _token_mixers(x, pw, cos, sin, tm, key_major):
    names = ('q', 'kvc', 'kvs', 'kvw', 'ks_bf', 'kw_bf', 'gates', 'rq', 'rk', 'rv', 'rg', 'mq', 'vs_t', 'vw_t')
    return dict(zip(names, _project(x, pw, cos, sin, tm, key_major)))


def kernel(x_prompt, x_sample, mem_prompt, cache_cmp_kv, cache_slc_kv, cache_win_kv, state_ret, cache_mem_kv, page_table, norm1_g, w_in, q_norm_g, k_norm_cmp_g, k_norm_slc_g, k_norm_win_g, cmp_pe_k, cmp_w1_k, cmp_w2_k, cmp_pe_v, cmp_w1_v, cmp_w2_v, ret_norm_g, mem_norm_g, w_mem_kv, mem_q_norm_g, mem_k_norm_g, w_out, norm2_g, w_router, b_router, w_up, b_up, w_down, b_down):
    B, S, D = x_prompt.shape
    DB, QS, _ = x_sample.shape
    n_mem = mem_prompt.shape[1]
    n_pages, page = page_table.shape[1], cache_cmp_kv.shape[2]
    past = n_pages * page
    NQ = SAMPLE_ROWS
    G, HD = NSA_GROUPS, HEAD_DIM
    win_rows = min(WINDOW, S)
    TP, TS = B * S, DB * NQ

    cos_p, sin_p = _rope_tables(jnp.arange(S, dtype=jnp.int32))
    cos_s, sin_s = _rope_tables(past + jnp.arange(NQ, dtype=jnp.int32))
    cos_s, sin_s = jnp.tile(cos_s, (DB, 1)), jnp.tile(sin_s, (DB, 1))

    xp = x_prompt.reshape(TP, D)
    xs = jnp.pad(x_sample, ((0, 0), (0, NQ - QS), (0, 0))).reshape(TS, D)
    unpad = lambda a: a.reshape(DB, NQ, -1)[:, :QS]
    outs = [[] for _ in range(9)]
    for l in range(w_in.shape[0]):
        pw = _prep_proj_weights(norm1_g[l], w_in[l], q_norm_g[l], k_norm_slc_g[l], k_norm_win_g[l], mem_q_norm_g[l])
        cw = _prep_compress_weights(cmp_pe_k[l], cmp_w1_k[l], cmp_w2_k[l], cmp_pe_v[l], cmp_w1_v[l], cmp_w2_v[l],
                                    k_norm_cmp_g[l])
        mw = _prep_mix_weights(w_out[l], ret_norm_g[l], norm2_g[l], w_router[l], b_router[l])

        t = _token_mixers(xp, pw, cos_p, sin_p, 512, True)
        kc, vct = _compress(t['kvc'].reshape(B, S, 256), cw)
        o_nsa = _nsa_prompt(B, t['q'], t['gates'], kc, vct, t['ks_bf'], t['vs_t'], t['kw_bf'], t['vw_t'])
        o_ret, r_p = _retention(t['rq'].reshape(B, S, 256), t['rk'].reshape(B, S, 256), t['rv'].reshape(B, S, 256),
                                jnp.zeros((B, 2, LANES, LANES), F32), RET_CHUNK, RET_CHUNK)
        mkv = _mem_kv(mem_prompt.reshape(B * n_mem, D), mem_norm_g[l], w_mem_kv[l], mem_k_norm_g[l])
        o_mem = _mem_attn(t['mq'].reshape(B, S, 256), mkv.reshape(B, n_mem, 512), 512)
        x1_p, h_p, topi_p, topg_p = _mix(xp, o_nsa.reshape(TP, -1), o_ret.reshape(TP, 256), t['rg'],
                                         o_mem.reshape(TP, 256), mw, 512)
        outs[0].append(t['kvc'].reshape(B, S, 2, G, HD))
        outs[1].append(t['kvs'].reshape(B, S, 2, G, HD))
        outs[2].append(t['kvw'].reshape(B, S, 2, G, HD)[:, S - win_rows:])
        outs[3].append(_pairs_to_state(r_p))
        outs[4].append(mkv.reshape(B, n_mem, 2, MEM_HEADS, HD))

        t = _token_mixers(xs, pw, cos_s, sin_s, TS, False)
        win = cache_win_kv[l]
        o_nsa = _nsa_sample(t['q'].astype(F32).reshape(DB, NQ, -1), t['gates'].reshape(DB, NQ, LANES),
                            t['kvs'].reshape(DB, NQ, 256), t['kvw'].reshape(DB, NQ, 256),
                            win.reshape(DB, win.shape[1], 256),
                            cache_cmp_kv[l].reshape(-1, page, 256), cache_slc_kv[l].reshape(-1, page, 256),
                            page_table, cw, QS)
        o_ret, r_s = _retention(t['rq'].reshape(DB, NQ, 256), t['rk'].reshape(DB, NQ, 256),
                                t['rv'].reshape(DB, NQ, 256), _state_to_pairs(state_ret[l].astype(F32)), QS, NQ)
        o_mem = _mem_attn(t['mq'].reshape(DB, NQ, 256), cache_mem_kv[l].reshape(DB, n_mem, 512), NQ)
        x1_s, h_s, topi_s, topg_s = _mix(xs, o_nsa.reshape(TS, -1).astype(BF16), o_ret.reshape(TS, 256), t['rg'],
                                         o_mem.reshape(TS, 256), mw, TS)
        kv5 = lambda a: unpad(a).reshape(DB, QS, 2, G, HD)
        outs[5].append(kv5(t['kvc']))
        outs[6].append(kv5(t['kvs']))
        outs[7].append(jnp.concatenate([win, kv5(t['kvw'])], axis=1)[:, QS:])
        outs[8].append(_pairs_to_state(r_s))

        valid = lambda a: unpad(a).reshape(DB * QS, -1)
        dest, blk_expert, n_used, n_blk = _route(jnp.concatenate([topi_p, valid(topi_s)], axis=0)[:, :TOP_K])
        xb = jnp.zeros((n_blk * MOE_ROWS, D), F32)
        xb = _dispatch(h_p, dest[:TP], xb)
        xb = _dispatch(valid(h_s), dest[TP:], xb)
        yb = _moe_experts(xb, blk_expert, n_used, w_up[l], b_up[l], w_down[l], b_down[l])
        xp = _combine(x1_p, topg_p, dest[:TP], yb)
        xs_new = _combine(valid(x1_s), valid(topg_s), dest[TP:], yb)
        xs = jnp.pad(xs_new.reshape(DB, QS, D), ((0, 0), (0, NQ - QS), (0, 0))).reshape(TS, D)
    y_sample = xs.reshape(DB, NQ, D)[:, :QS]
    return (xp.reshape(B, S, D), y_sample) + tuple(jnp.stack(o) for o in outs)
```

```python
import functools

import jax
import jax.numpy as jnp
from jax import lax
from jax.experimental import pallas as pl
from jax.experimental.pallas import tpu as pltpu

F32 = jnp.float32
BF16 = jnp.bfloat16

HEAD_DIM = 64
NSA_HEADS = 8
NSA_GROUPS = 2
NSA_HPG = NSA_HEADS // NSA_GROUPS
RET_HEADS = 4
MEM_HEADS = 4
CMP_BLK = 32
CMP_STRIDE = 16
CMP_HID = 256
SLC_BLK = 64
N_SEL = 16
WINDOW = 512
Q_BLK = 128
RET_CHUNK = 128
N_EXPERTS = 32
TOP_K = 4
D_FF = 1024
SWIGLU_LIMIT = 7.0
SWIGLU_ALPHA = 1.702
EPS = 1e-6
NEG = -1e30
BIG = 1e9
ROPE_BASE = 10000.0
SLC_RATIO = SLC_BLK // CMP_STRIDE
CMP_OVL = CMP_BLK // CMP_STRIDE - 1

LANES = 128
KEY_TILE = 512
MOE_ROWS = 512
VMEM_LIMIT = 56 * 1024 * 1024

C_Q = 0
C_KVC = C_Q + NSA_HEADS * LANES
C_KVS = C_KVC + 256
C_KVW = C_KVS + 256
C_GATE = C_KVW + 256
C_RQ = C_GATE + LANES
C_RK = C_RQ + 256
C_RV = C_RK + 256
C_RG = C_RV + 256
C_MQ = C_RG + 256
C_END = C_MQ + 256


def _cparams(sem):
    return pltpu.CompilerParams(dimension_semantics=sem, vmem_limit_bytes=VMEM_LIMIT)


def _lane(shape):
    return lax.broadcasted_iota(jnp.int32, shape, len(shape) - 1)


def _seg_rms(x, gain):
    lo = _lane(x.shape) < HEAD_DIM
    x2 = x * x
    s_lo = jnp.sum(jnp.where(lo, x2, 0.0), axis=-1, keepdims=True)
    s_hi = jnp.sum(jnp.where(lo, 0.0, x2), axis=-1, keepdims=True)
    ms = jnp.where(lo, s_lo, s_hi) * (1.0 / HEAD_DIM)
    return x * lax.rsqrt(ms + EPS) * gain


def _swap_halves(x):
    first = (_lane(x.shape) & (HEAD_DIM // 2)) == 0
    return jnp.where(first, pltpu.roll(x, LANES - HEAD_DIM // 2, 1), pltpu.roll(x, HEAD_DIM // 2, 1))


def _proj_kernel(x_ref, g1_ref, w_ref, cos_ref, sin_ref, qg_ref, ksg_ref, kwg_ref, mqg_ref,
                 q_out, kvc_out, kvs_out, kvw_out, ks_bf, kw_bf, gate_out,
                 rq_out, rk_out, rv_out, rg_out, mq_out, *extra, seq_tiles):
    x = x_ref[...]
    tm = x.shape[0]
    ms = jnp.mean(x * x, axis=-1, keepdims=True)
    xn = (x * lax.rsqrt(ms + EPS) * g1_ref[...]).astype(BF16)
    z = jnp.dot(xn, w_ref[...], preferred_element_type=F32)
    for i in range(NSA_HEADS):
        zq = z[:, C_Q + i * LANES:C_Q + (i + 1) * LANES]
        msq = jnp.sum(zq * zq, axis=-1, keepdims=True) * (1.0 / HEAD_DIM)
        qn = zq * lax.rsqrt(msq + EPS) * qg_ref[:, i * LANES:(i + 1) * LANES] * (HEAD_DIM ** -0.5)
        if seq_tiles:
            q_out[i * LANES:(i + 1) * LANES, :] = qn.T.astype(BF16)
        else:
            q_out[:, i * LANES:(i + 1) * LANES] = qn.astype(BF16)
    kvc_out[...] = z[:, C_KVC:C_KVC + 256]
    ks = _seg_rms(z[:, C_KVS:C_KVS + LANES], ksg_ref[...])
    vs = z[:, C_KVS + LANES:C_KVS + 256]
    kvs_out[:, 0:LANES] = ks
    kvs_out[:, LANES:256] = vs
    kw = _seg_rms(z[:, C_KVW:C_KVW + LANES], kwg_ref[...])
    vw = z[:, C_KVW + LANES:C_KVW + 256]
    kvw_out[:, 0:LANES] = kw
    kvw_out[:, LANES:256] = vw
    gates = jax.nn.sigmoid(z[:, C_GATE:C_GATE + LANES])
    ks_bf[:, 0:LANES] = ks.astype(BF16)
    if seq_tiles:
        vs_t, vw_t = extra
        pos = (pl.program_id(0) % seq_tiles) * tm + lax.broadcasted_iota(jnp.int32, (tm, LANES), 0)
        ks_bf[:, LANES:256] = jnp.where(_lane((tm, LANES)) == pos // SLC_BLK, 1.0, 0.0).astype(BF16)
        kw_bf[...] = kw.astype(BF16)
        vs_t[...] = vs.T.astype(BF16)
        vw_t[...] = vw.T.astype(BF16)
        gate_out[...] = gates.T
    else:
        ks_bf[:, LANES:256] = vs.astype(BF16)
        kw_bf[:, 0:LANES] = kw.astype(BF16)
        kw_bf[:, LANES:256] = vw.astype(BF16)
        gate_out[...] = gates
    for c in range(2):
        sl = slice(c * LANES, (c + 1) * LANES)
        cos = cos_ref[:, sl]
        sin = sin_ref[:, sl]
        rq = z[:, C_RQ + c * LANES:C_RQ + (c + 1) * LANES]
        rk = z[:, C_RK + c * LANES:C_RK + (c + 1) * LANES]
        rq_out[:, sl] = rq * cos + _swap_halves(rq) * sin
        rk_out[:, sl] = (rk * cos + _swap_halves(rk) * sin) * (HEAD_DIM ** -0.5)
        mq = z[:, C_MQ + c * LANES:C_MQ + (c + 1) * LANES]
        mq_out[:, sl] = _seg_rms(mq, mqg_ref[...]).astype(BF16)
    rv_out[...] = z[:, C_RV:C_RV + 256]
    rg_out[...] = z[:, C_RG:C_RG + 256]


def _project(x, pw, cos, sin, tm, key_major):
    T, D = x.shape
    nt = cos.shape[0] // tm
    row = lambda n: (pl.BlockSpec((tm, n), lambda i: (i, 0)), (T, n))
    col = lambda n: (pl.BlockSpec((n, tm), lambda i: (0, i)), (n, T))
    full = lambda a: pl.BlockSpec(a.shape, lambda i: (0, 0))
    tab = pl.BlockSpec((tm, 256), lambda i: (i % nt, 0))
    tok = col if key_major else row
    outs = [
        (tok(NSA_HEADS * LANES), BF16),
        (row(256), F32), (row(256), F32), (row(256), F32),
        (row(256), BF16), (row(LANES if key_major else 256), BF16),
        (tok(LANES), F32),
        (row(256), F32), (row(256), F32), (row(256), F32), (row(256), F32),
        (row(256), BF16),
    ]
    if key_major:
        outs += [(col(LANES), BF16), (col(LANES), BF16)]
    return pl.pallas_call(
        functools.partial(_proj_kernel, seq_tiles=nt if key_major else 0),
        grid=(T // tm,),
        in_specs=[row(D)[0], full(pw['g1']), full(pw['w']), tab, tab,
                  full(pw['qg']), full(pw['ksg']), full(pw['kwg']), full(pw['mqg'])],
        out_specs=[spec for (spec, _), _ in outs],
        out_shape=[jax.ShapeDtypeStruct(shape, dt) for (_, shape), dt in outs],
        compiler_params=_cparams(("arbitrary",)),
        name="proj",
    )(x, pw['g1'], pw['w'], cos, sin, pw['qg'], pw['ksg'], pw['kwg'], pw['mqg'])


def _prep_proj_weights(norm1_g, w_in, q_norm_g, k_norm_slc_g, k_norm_win_g, mem_q_norm_g):
    D = w_in.shape[0]
    sizes = (512, 256, 256, 256, 24, 256, 256, 256, 256, 256)
    parts, off = [], 0
    for n in sizes:
        parts.append(w_in[:, off:off + n])
        off += n
    wq, wkvc, wkvs, wkvw, wg, wrq, wrk, wrv, wrg, wmq = parts
    zero = jnp.zeros((D, HEAD_DIM), w_in.dtype)
    qcols, qg = [], []
    gz = jnp.zeros((HEAD_DIM,), F32)
    for i in range(NSA_HEADS):
        wh = wq[:, i * HEAD_DIM:(i + 1) * HEAD_DIM]
        if i < NSA_HPG:
            qcols += [wh, zero]
            qg += [q_norm_g, gz]
        else:
            qcols += [zero, wh]
            qg += [gz, q_norm_g]
    wgp = jnp.pad(wg, ((0, 0), (0, LANES - wg.shape[1])))
    w = jnp.concatenate(qcols + [wkvc, wkvs, wkvw, wgp, wrq, wrk, wrv, wrg, wmq], axis=1).astype(BF16)
    two = lambda g: jnp.tile(g, 2)[None, :]
    return dict(g1=norm1_g[None, :], w=w, qg=jnp.concatenate(qg)[None, :],
                ksg=two(k_norm_slc_g), kwg=two(k_norm_win_g), mqg=two(mem_q_norm_g))


def _rope_tables(pos):
    half = HEAD_DIM // 2
    inv = ROPE_BASE ** (-jnp.arange(half, dtype=F32) / half)
    ang = pos.astype(F32)[:, None] * inv[None, :]
    cos, sin = jnp.cos(ang), jnp.sin(ang)
    cos = jnp.tile(jnp.concatenate([cos, cos], axis=1), (1, RET_HEADS))
    sin = jnp.tile(jnp.concatenate([-sin, sin], axis=1), (1, RET_HEADS))
    return cos, sin


def _compress_half(src_ref, pe_ref, wa_ref, wb_ref, w2_ref, nchunks):
    a_parts, b_parts = [], []
    for r in range(CMP_STRIDE):
        xr = src_ref[pl.ds(r, nchunks, stride=CMP_STRIDE), :]
        a_parts.append((xr + pe_ref[r:r + 1, :]).astype(BF16))
        b_parts.append((xr + pe_ref[CMP_STRIDE + r:CMP_STRIDE + r + 1, :]).astype(BF16))
    ha = jnp.dot(jnp.concatenate(a_parts, axis=1), wa_ref[...], preferred_element_type=F32)
    hb = jnp.dot(jnp.concatenate(b_parts, axis=1), wb_ref[...], preferred_element_type=F32)
    h = ha + pltpu.roll(hb, nchunks - 1, 0)
    act = jax.nn.gelu(h).astype(BF16)
    return jnp.dot(act, w2_ref[...], preferred_element_type=F32)


def _compress_kernel(kv_ref, pe_ref, wa_ref, wb_ref, w2_ref, kg_ref, kc_out, vct_out):
    nchunks = kc_out.shape[1]
    y = _compress_half(kv_ref.at[0], pe_ref.at[0], wa_ref.at[0], wb_ref.at[0], w2_ref.at[0], nchunks)

    @pl.when(pl.program_id(1) == 0)
    def _():
        kc_out[0] = _seg_rms(y, kg_ref[...]).astype(BF16)

    @pl.when(pl.program_id(1) == 1)
    def _():
        vct_out[0] = y.T.astype(BF16)


def _compress(kvc, cw):
    B, T, _ = kvc.shape
    nchunks = T // CMP_STRIDE
    per_kv = lambda a: pl.BlockSpec((1,) + a.shape[1:], lambda b, j: (j,) + (0,) * (a.ndim - 1))
    return pl.pallas_call(
        _compress_kernel,
        grid=(B, 2),
        in_specs=[pl.BlockSpec((1, T, LANES), lambda b, j: (b, 0, j)),
                  per_kv(cw['pe']), per_kv(cw['wa']), per_kv(cw['wb']), per_kv(cw['w2']),
                  pl.BlockSpec((1, LANES), lambda b, j: (0, 0))],
        out_specs=[pl.BlockSpec((1, nchunks, LANES), lambda b, j: (b, 0, 0)),
                   pl.BlockSpec((1, LANES, nchunks), lambda b, j: (b, 0, 0))],
        out_shape=[jax.ShapeDtypeStruct((B, nchunks, LANES), BF16),
                   jax.ShapeDtypeStruct((B, LANES, nchunks), BF16)],
        compiler_params=_cparams(("arbitrary", "arbitrary")),
        name="compress",
    )(kvc, cw['pe'], cw['wa'], cw['wb'], cw['w2'], cw['kg'])


def _prep_compress_weights(pe_k, w1_k, w2_k, pe_v, w1_v, w2_v, k_norm_g):
    eye = jnp.eye(NSA_GROUPS, dtype=F32)

    def one(pe, w1, w2):
        w1r = w1.reshape(CMP_BLK, HEAD_DIM, CMP_HID)
        wf = jnp.einsum('gh,rdc->rgdhc', eye, w1r).reshape(CMP_BLK * LANES, NSA_GROUPS * CMP_HID)
        w2f = jnp.einsum('gh,cd->gchd', eye, w2).reshape(NSA_GROUPS * CMP_HID, LANES)
        half = CMP_STRIDE * LANES
        return jnp.tile(pe, (1, NSA_GROUPS)), wf[:half].astype(BF16), wf[half:].astype(BF16), w2f.astype(BF16)

    k, v = one(pe_k, w1_k, w2_k), one(pe_v, w1_v, w2_v)
    st = lambda i: jnp.stack([k[i], v[i]])
    return dict(pe=st(0), wa=st(1), wb=st(2), w2=st(3), kg=jnp.tile(k_norm_g, 2)[None, :])


def _dot_nt(a, b):
    return lax.dot_general(a, b, (((1,), (1,)), ((), ())), preferred_element_type=F32)


def _split3(x):
    hi = x.astype(BF16)
    r = x - hi.astype(F32)
    mid = r.astype(BF16)
    lo = (r - mid.astype(F32)).astype(BF16)
    return hi, mid, lo


def _select_bias(ps, cur, rounds, axis, cur_in_range=True):
    j = lax.broadcasted_iota(jnp.int32, ps.shape, axis)
    last = cur if cur_in_range else cur - 1
    key = jnp.where(j <= last, ps, NEG)
    for forced in (0, last, cur - 1):
        key = jnp.where(j == forced, BIG, key)
    bias = jnp.full(ps.shape, NEG, F32)
    for _ in range(rounds):
        m = jnp.max(key, axis=axis, keepdims=True)
        idx = jnp.min(jnp.where(key == m, j, LANES), axis=axis, keepdims=True)
        pick = j == idx
        bias = jnp.where(pick, 0.0, bias)
        key = jnp.where(pick, -jnp.inf, key)
    return bias


def _cmp_branch(qg, qpos, kc, vc):
    s = _dot_nt(qg, kc)
    cend = _lane((1, kc.shape[0])) * CMP_STRIDE + (CMP_BLK - 1)
    s = jnp.where(cend <= qpos, s, NEG)
    m = jnp.max(s, axis=-1, keepdims=True)
    p = jnp.exp(s - m)
    p = p * (1.0 / jnp.sum(p, axis=-1, keepdims=True))
    p = jnp.where(qpos >= CMP_BLK - 1, p, 0.0)
    return p, jnp.dot(p.astype(BF16), vc, preferred_element_type=F32)


def _sel_tile(qaug, k_tile, v_tile, blk0, mask, m_sc, l_sc, acc_sc):
    n = k_tile.shape[0]
    blk = blk0 + lax.broadcasted_iota(jnp.int32, (n, LANES), 0) // SLC_BLK
    onehot = jnp.where(_lane((n, LANES)) == blk, 1.0, 0.0).astype(BF16)
    s = _dot_nt(qaug, jnp.concatenate([k_tile, onehot], axis=1))
    if mask is not None:
        s = jnp.where(mask, s, NEG)
    _online_update(s, v_tile, m_sc, l_sc, acc_sc)


def _online_update(s, v_tile, m_sc, l_sc, acc_sc):
    m_old = m_sc[...]
    m_new = jnp.maximum(m_old, jnp.max(s, axis=-1, keepdims=True))
    alpha = jnp.exp(m_old - m_new)
    p = jnp.exp(s - m_new)
    l_sc[...] = alpha * l_sc[...] + jnp.sum(p, axis=-1, keepdims=True)
    acc_sc[...] = alpha * acc_sc[...] + jnp.dot(p.astype(BF16), v_tile, preferred_element_type=F32)
    m_sc[...] = m_new


def _gate_heads(gates, g, oc, osel, ow, nq):
    heads = []
    for h in range(NSA_HPG):
        rs = slice(h * nq, (h + 1) * nq)
        c0 = (NSA_HPG * g + h) * 3
        heads.append(gates[:, c0:c0 + 1] * oc[rs] + gates[:, c0 + 1:c0 + 2] * osel[rs]
                     + gates[:, c0 + 2:c0 + 3] * ow[rs])
    return heads


def _store_heads(heads, o_ref):
    low = _lane(heads[0].shape) < HEAD_DIM
    for pr in range(NSA_HEADS // 2):
        even, odd = heads[2 * pr], heads[2 * pr + 1]
        if pr < NSA_GROUPS:
            odd = pltpu.roll(odd, HEAD_DIM, 1)
        else:
            even = pltpu.roll(even, HEAD_DIM, 1)
        o_ref[0, :, pr * LANES:(pr + 1) * LANES] = jnp.where(low, even, odd).astype(o_ref.dtype)


def _nsa_sample_kernel(pt_ref, q_ref, gate_ref, ksn_ref, kwn_ref, win_ref, pool_c, pool_s,
                       pe_ref, wa_ref, wb_ref, w2_ref, kg_ref, mmap_ref, o_ref,
                       cbuf, sbuf, sems, m_sc, l_sc, acc_sc, *, page, n_new):
    b = pl.program_id(0)
    past = sbuf.shape[0]
    npages = past // page
    nq = q_ref.shape[1]

    def page_copies(p, pg):
        dst = pl.ds(pl.multiple_of(p * page, page), page)
        return (pltpu.make_async_copy(pool_c.at[pg, :, 0:LANES], cbuf.at[0, dst], sems.at[0]),
                pltpu.make_async_copy(pool_c.at[pg, :, LANES:256], cbuf.at[1, dst], sems.at[1]),
                pltpu.make_async_copy(pool_s.at[pg], sbuf.at[dst], sems.at[2]))

    def start(p, c):
        for cp in page_copies(p, pt_ref[b * npages + p]):
            cp.start()
        return c

    def wait(p, c):
        for cp in page_copies(p, 0):
            cp.wait()
        return c

    lax.fori_loop(0, npages, start, 0)
    lax.fori_loop(0, npages, wait, 0)

    nchunks = past // CMP_STRIDE
    kc = _seg_rms(_compress_half(cbuf.at[0], pe_ref.at[0], wa_ref.at[0], wb_ref.at[0], w2_ref.at[0], nchunks),
                  kg_ref[...]).astype(BF16)
    vc = _compress_half(cbuf.at[1], pe_ref.at[1], wa_ref.at[1], wb_ref.at[1], w2_ref.at[1], nchunks).astype(BF16)

    rows = NSA_HPG * nq
    srow = lax.broadcasted_iota(jnp.int32, (rows, 1), 0) % nq
    qpos = past + srow
    cur = (past + lax.broadcasted_iota(jnp.int32, (nq, 1), 0)) // SLC_BLK
    gates = gate_ref[0]
    pad_keys = lambda x: jnp.concatenate([x, jnp.zeros((LANES - nq, LANES), F32)], axis=0).astype(BF16)
    new_ok = _lane((1, LANES)) <= jnp.minimum(srow, n_new - 1)
    heads = []
    for g in range(NSA_GROUPS):
        qg = jnp.concatenate(
            [q_ref[0, :, (NSA_HPG * g + h) * LANES:(NSA_HPG * g + h + 1) * LANES] for h in range(NSA_HPG)],
            axis=0).astype(BF16)
        p, oc = _cmp_branch(qg, qpos, kc, vc)
        p4 = p[0:nq] + p[nq:2 * nq] + p[2 * nq:3 * nq] + p[3 * nq:4 * nq]
        ps = sum(_dot_nt(part, mmap_ref[...]) for part in _split3(p4))
        bias = _select_bias(ps, cur, N_SEL - 1, 1, cur_in_range=False)
        qaug = jnp.concatenate([qg, jnp.concatenate([bias] * NSA_HPG, axis=0).astype(BF16)], axis=1)
        m_sc[...] = jnp.full(m_sc.shape, NEG, F32)
        l_sc[...] = jnp.zeros(l_sc.shape, F32)
        acc_sc[...] = jnp.zeros(acc_sc.shape, F32)

        def body(kt, c, qaug=qaug):
            k0 = pl.multiple_of(kt * KEY_TILE, KEY_TILE)
            _sel_tile(qaug, sbuf[pl.ds(k0, KEY_TILE), 0:LANES].astype(BF16),
                      sbuf[pl.ds(k0, KEY_TILE), LANES:256].astype(BF16),
                      kt * (KEY_TILE // SLC_BLK), None, m_sc, l_sc, acc_sc)
            return c

        lax.fori_loop(0, past // KEY_TILE, body, 0)
        s_new = jnp.where(new_ok, _dot_nt(qg, pad_keys(ksn_ref[0, :, 0:LANES])), NEG)
        _online_update(s_new, pad_keys(ksn_ref[0, :, LANES:256]), m_sc, l_sc, acc_sc)
        osel = acc_sc[...] * (1.0 / l_sc[...])
        nwin = win_ref.shape[1]
        sw_old = jnp.where(_lane((1, nwin)) + (WINDOW - nwin) > srow,
                           _dot_nt(qg, win_ref[0, :, 0:LANES].astype(BF16)), NEG)
        sw_new = jnp.where(new_ok, _dot_nt(qg, pad_keys(kwn_ref[0, :, 0:LANES])), NEG)
        sw = jnp.concatenate([sw_old, sw_new], axis=1)
        mw = jnp.max(sw, axis=-1, keepdims=True)
        pw = jnp.exp(sw - mw)
        ow = (jnp.dot(pw[:, :nwin].astype(BF16), win_ref[0, :, LANES:256].astype(BF16), preferred_element_type=F32)
              + jnp.dot(pw[:, nwin:].astype(BF16), pad_keys(kwn_ref[0, :, LANES:256]), preferred_element_type=F32))
        ow = ow * (1.0 / jnp.sum(pw, axis=-1, keepdims=True))
        heads += _gate_heads(gates, g, oc, osel, ow, nq)
    _store_heads(heads, o_ref)


def _nsa_sample(q, gates, ksn, kwn, win, pool_c, pool_s, page_table, cw, n_new):
    DB, nq, _ = q.shape
    n_pages = page_table.shape[1]
    page = pool_c.shape[1]
    past = n_pages * page
    nchunks = past // CMP_STRIDE
    rows = NSA_HPG * nq
    mmap = _importance_map(nchunks)
    per_seq = lambda a: pl.BlockSpec((1,) + a.shape[1:], lambda b, pt: (b,) + (0,) * (a.ndim - 1))
    const = lambda a: pl.BlockSpec(a.shape, lambda b, pt: (0,) * a.ndim)
    hbm = pl.BlockSpec(memory_space=pl.ANY)
    grid_spec = pltpu.PrefetchScalarGridSpec(
        num_scalar_prefetch=1,
        grid=(DB,),
        in_specs=[per_seq(q), per_seq(gates), per_seq(ksn), per_seq(kwn), per_seq(win), hbm, hbm,
                  const(cw['pe']), const(cw['wa']), const(cw['wb']), const(cw['w2']), const(cw['kg']), const(mmap)],
        out_specs=pl.BlockSpec((1, nq, NSA_HEADS * HEAD_DIM), lambda b, pt: (b, 0, 0)),
        scratch_shapes=[pltpu.VMEM((2, past, LANES), F32), pltpu.VMEM((past, 256), F32),
                        pltpu.SemaphoreType.DMA((3,)),
                        pltpu.VMEM((rows, 1), F32), pltpu.VMEM((rows, 1), F32), pltpu.VMEM((rows, LANES), F32)],
    )
    return pl.pallas_call(
        functools.partial(_nsa_sample_kernel, page=page, n_new=n_new),
        grid_spec=grid_spec,
        out_shape=jax.ShapeDtypeStruct((DB, nq, NSA_HEADS * HEAD_DIM), F32),
        compiler_params=_cparams(("arbitrary",)),
        name="nsa_sample",
    )(page_table.reshape(-1), q, gates, ksn, kwn, win, pool_c, pool_s,
      cw['pe'], cw['wa'], cw['wb'], cw['w2'], cw['kg'], mmap)


def _nsa_prompt_kernel(q_ref, gate_ref, kc_ref, vct_ref, ks_ref, vst_ref, kw_ref, vwt_ref, mmap_ref, o_ref,
                       m_sc, l_sc, acc_sc):
    i = pl.program_id(1)
    s0 = i * Q_BLK
    cols = NSA_HPG * Q_BLK
    qpos = s0 + _lane((1, cols)) % Q_BLK
    cur = (s0 + _lane((1, Q_BLK))) // SLC_BLK
    kc, vct = kc_ref[0], vct_ref[0]
    ncmp = kc.shape[0]
    n_full = s0 // KEY_TILE
    w0 = pl.multiple_of(jnp.maximum(s0 - WINDOW, 0), Q_BLK)
    nw = WINDOW + Q_BLK
    dot = functools.partial(jnp.dot, preferred_element_type=F32)

    def softmax_keys(s):
        p = jnp.exp(s - jnp.max(s, axis=0, keepdims=True))
        return p, 1.0 / jnp.sum(p, axis=0, keepdims=True)

    qaugs = []
    branch = []
    for g in range(NSA_GROUPS):
        qt = jnp.concatenate([q_ref[(NSA_HPG * g + h) * LANES:(NSA_HPG * g + h + 1) * LANES, :]
                              for h in range(NSA_HPG)], axis=1)
        s = dot(kc, qt)
        n_idx = lax.broadcasted_iota(jnp.int32, (ncmp, cols), 0)
        s = jnp.where(n_idx * CMP_STRIDE + (CMP_BLK - 1) <= qpos, s, NEG)
        p, inv = softmax_keys(s)
        p = jnp.where(qpos >= CMP_BLK - 1, p * inv, 0.0)
        oc = dot(vct, p.astype(BF16))
        p4 = p[:, 0:Q_BLK] + p[:, Q_BLK:2 * Q_BLK] + p[:, 2 * Q_BLK:3 * Q_BLK] + p[:, 3 * Q_BLK:4 * Q_BLK]
        ps_t = sum(dot(mmap_ref[...], part) for part in _split3(p4))
        bias = _select_bias(ps_t, cur, N_SEL, 0).astype(BF16)
        qaugs.append(jnp.concatenate([qt, jnp.concatenate([bias] * NSA_HPG, axis=1)], axis=0))
        sw = dot(kw_ref[0, pl.ds(w0, nw), :], qt)
        d = qpos - (w0 + lax.broadcasted_iota(jnp.int32, (nw, cols), 0))
        sw = jnp.where(d >= 0, jnp.where(d < WINDOW, sw, NEG), NEG)
        pw, invw = softmax_keys(sw)
        ow = dot(vwt_ref[:, pl.ds(w0, nw)], pw.astype(BF16)) * invw
        branch.append((oc, ow))
        m_sc[g] = jnp.full(m_sc.shape[1:], NEG, F32)
        l_sc[g] = jnp.zeros(l_sc.shape[1:], F32)
        acc_sc[g] = jnp.zeros(acc_sc.shape[1:], F32)

    def tile(kt, causal):
        k0 = pl.multiple_of(kt * KEY_TILE, KEY_TILE)
        k_aug = ks_ref[0, pl.ds(k0, KEY_TILE), :]
        v_t = vst_ref[:, pl.ds(k0, KEY_TILE)]
        for g in range(NSA_GROUPS):
            s = dot(k_aug, qaugs[g])
            if causal:
                s = jnp.where(k0 + lax.broadcasted_iota(jnp.int32, s.shape, 0) <= qpos, s, NEG)
            m_old = m_sc[g]
            m_new = jnp.maximum(m_old, jnp.max(s, axis=0, keepdims=True))
            alpha = jnp.exp(m_old - m_new)
            p = jnp.exp(s - m_new)
            l_sc[g] = alpha * l_sc[g] + jnp.sum(p, axis=0, keepdims=True)
            acc_sc[g] = alpha * acc_sc[g] + dot(v_t, p.astype(BF16))
            m_sc[g] = m_new

    def body(kt, c):
        tile(kt, False)
        return c

    lax.fori_loop(0, n_full, body, 0)
    tile(n_full, True)

    pieces = []
    for g in range(NSA_GROUPS):
        oc, ow = branch[g]
        osel = acc_sc[g] * (1.0 / l_sc[g])
        for h in range(NSA_HPG):
            cs = slice(h * Q_BLK, (h + 1) * Q_BLK)
            c0 = (NSA_HPG * g + h) * 3
            o_h = (gate_ref[c0:c0 + 1, :] * oc[:, cs] + gate_ref[c0 + 1:c0 + 2, :] * osel[:, cs]
                   + gate_ref[c0 + 2:c0 + 3, :] * ow[:, cs])
            pieces.append(o_h[g * HEAD_DIM:(g + 1) * HEAD_DIM, :])
    o_ref[0] = jnp.concatenate(pieces, axis=0).T.astype(o_ref.dtype)


def _importance_map(ncmp):
    j = jnp.arange(LANES)[:, None]
    n = jnp.arange(ncmp)[None, :]
    return ((n >= SLC_RATIO * j - CMP_OVL) & (n < SLC_RATIO * j + SLC_RATIO)).astype(BF16)


def _nsa_prompt(B, qt, gates_t, kc, vct, ks_aug, vs_t, kw, vw_t):
    S = qt.shape[1] // B
    nq = S // Q_BLK
    ncmp = kc.shape[1]
    cols = NSA_HPG * Q_BLK
    mmap = _importance_map(ncmp)
    per_block = lambda n: pl.BlockSpec((n, Q_BLK), lambda b, i: (0, b * nq + i))
    seq_rows = lambda n: pl.BlockSpec((1, S, n), lambda b, i: (b, 0, 0))
    seq_cols = pl.BlockSpec((LANES, S), lambda b, i: (0, b))
    return pl.pallas_call(
        _nsa_prompt_kernel,
        grid=(B, nq),
        in_specs=[per_block(qt.shape[0]), per_block(LANES),
                  pl.BlockSpec((1, ncmp, LANES), lambda b, i: (b, 0, 0)),
                  pl.BlockSpec((1, LANES, ncmp), lambda b, i: (b, 0, 0)),
                  seq_rows(256), seq_cols, seq_rows(LANES), seq_cols,
                  pl.BlockSpec(mmap.shape, lambda b, i: (0, 0))],
        out_specs=pl.BlockSpec((1, Q_BLK, NSA_HEADS * HEAD_DIM), lambda b, i: (b, i, 0)),
        out_shape=jax.ShapeDtypeStruct((B, S, NSA_HEADS * HEAD_DIM), BF16),
        scratch_shapes=[pltpu.VMEM((NSA_GROUPS, 1, cols), F32), pltpu.VMEM((NSA_GROUPS, 1, cols), F32),
                        pltpu.VMEM((NSA_GROUPS, LANES, cols), F32)],
        compiler_params=_cparams(("arbitrary", "arbitrary")),
        name="nsa_prompt",
    )(qt, gates_t, kc, vct, ks_aug.reshape(B, S, 256), vs_t, kw.reshape(B, S, LANES), vw_t, mmap)


def _ret_kernel(rq_ref, rk_ref, rv_ref, r0_ref, dmask_ref, xi_ref, zeta_ref, dec_ref, o_ref, rout_ref, r_sc):
    @pl.when(pl.program_id(1) == 0)
    def _():
        r_sc[...] = r0_ref[0]

    C = rq_ref.shape[1]
    low = _lane((C, LANES)) < HEAD_DIM
    diag = lax.broadcasted_iota(jnp.int32, (LANES, LANES), 0) // HEAD_DIM == _lane((LANES, LANES)) // HEAD_DIM
    for pr in range(RET_HEADS // 2):
        sl = slice(pr * LANES, (pr + 1) * LANES)
        k = rk_ref[0, :, sl]
        qb, kb, vb = rq_ref[0, :, sl].astype(BF16), k.astype(BF16), rv_ref[0, :, sl].astype(BF16)
        zero = jnp.zeros_like(qb)
        s0 = _dot_nt(jnp.where(low, qb, zero), kb) * dmask_ref[2 * pr]
        s1 = _dot_nt(jnp.where(low, zero, qb), kb) * dmask_ref[2 * pr + 1]
        o = jnp.where(low, jnp.dot(s0.astype(BF16), vb, preferred_element_type=F32),
                      jnp.dot(s1.astype(BF16), vb, preferred_element_type=F32))
        r = r_sc[pr]
        o_ref[0, :, sl] = o + jnp.dot(qb, r.astype(BF16), preferred_element_type=F32) * xi_ref[:, sl]
        kz = (k * zeta_ref[:, sl]).astype(BF16)
        upd = lax.dot_general(kz, vb, (((0,), (0,)), ((), ())), preferred_element_type=F32)
        r_sc[pr] = dec_ref[:, sl] * r + jnp.where(diag, upd, 0.0)
    rout_ref[0] = r_sc[...]


def _ret_tables(c_true, c_pad):
    lg = jnp.log(1.0 - 2.0 ** (-5.0 - jnp.arange(RET_HEADS, dtype=F32)))
    idx = jnp.arange(c_pad, dtype=F32)
    diff = idx[:, None] - idx[None, :]
    dmask = jnp.where(diff >= 0, jnp.exp(jnp.maximum(diff, 0.0)[None] * lg[:, None, None]), 0.0)
    lanes = lambda a: jnp.repeat(a, HEAD_DIM, axis=-1)
    xi = lanes(jnp.exp((idx + 1.0)[:, None] * lg[None, :]))
    zeta = lanes(jnp.exp((c_true - 1.0 - idx)[:, None] * lg[None, :]))
    dec = lanes(jnp.exp(c_true * lg)[None, :])
    return dmask, xi, zeta, dec


def _retention(rq, rk, rv, r0, c_true, c_pad):
    B, T, _ = rq.shape
    dmask, xi, zeta, dec = _ret_tables(c_true, c_pad)
    row = pl.BlockSpec((1, c_pad, 256), lambda b, c: (b, c, 0))
    st = pl.BlockSpec((1, 2, LANES, LANES), lambda b, c: (b, 0, 0, 0))
    const = lambda a: pl.BlockSpec(a.shape, lambda b, c: (0,) * a.ndim)
    return pl.pallas_call(
        _ret_kernel,
        grid=(B, T // c_pad),
        in_specs=[row, row, row, st, const(dmask), const(xi), const(zeta), const(dec)],
        out_specs=[row, st],
        out_shape=[jax.ShapeDtypeStruct((B, T, 256), F32), jax.ShapeDtypeStruct((B, 2, LANES, LANES), F32)],
        scratch_shapes=[pltpu.VMEM((2, LANES, LANES), F32)],
        compiler_params=_cparams(("arbitrary", "arbitrary")),
        name="retention",
    )(rq, rk, rv, r0, dmask, xi, zeta, dec)


def _state_to_pairs(r):
    B = r.shape[0]
    r = r.reshape(B, 2, 2, HEAD_DIM, HEAD_DIM)
    eye = jnp.eye(2, dtype=r.dtype)
    return jnp.einsum('bphde,hk->bphdke', r, eye).reshape(B, 2, LANES, LANES)


def _pairs_to_state(rp):
    B = rp.shape[0]
    rp = rp.reshape(B, 2, 2, HEAD_DIM, 2, HEAD_DIM)
    return jnp.stack([rp[:, :, 0, :, 0, :], rp[:, :, 1, :, 1, :]], axis=2).reshape(B, RET_HEADS, HEAD_DIM, HEAD_DIM)


def _mem_kv_kernel(m_ref, g_ref, w_ref, kg_ref, kv_out):
    x = m_ref[...]
    ms = jnp.mean(x * x, axis=-1, keepdims=True)
    xn = (x * lax.rsqrt(ms + EPS) * g_ref[...]).astype(BF16)
    z = jnp.dot(xn, w_ref[...], preferred_element_type=F32)
    for c in range(2):
        kv_out[:, c * LANES:(c + 1) * LANES] = _seg_rms(z[:, c * LANES:(c + 1) * LANES], kg_ref[...])
    kv_out[:, 256:512] = z[:, 256:512]


def _mem_kv(mem, mem_norm_g, w_mem_kv, mem_k_norm_g):
    T, D = mem.shape
    w = w_mem_kv.astype(BF16)
    full = lambda a: pl.BlockSpec(a.shape, lambda i: (0, 0))
    g = mem_norm_g[None, :]
    kg = jnp.tile(mem_k_norm_g, 2)[None, :]
    return pl.pallas_call(
        _mem_kv_kernel,
        grid=(1,),
        in_specs=[full(mem), full(g), full(w), full(kg)],
        out_specs=pl.BlockSpec((T, 512), lambda i: (0, 0)),
        out_shape=jax.ShapeDtypeStruct((T, 512), F32),
        compiler_params=_cparams(("arbitrary",)),
        name="mem_kv",
    )(mem, g, w, kg)


def _mem_attn_kernel(mq_ref, mkv_ref, o_ref):
    q = mq_ref[0]
    rows = q.shape[0]
    low = _lane((rows, LANES)) < HEAD_DIM
    zero = jnp.zeros((rows, LANES), F32)
    for pr in range(MEM_HEADS // 2):
        sl = slice(pr * LANES, (pr + 1) * LANES)
        qp = q[:, sl].astype(F32)
        k = mkv_ref[0, :, sl].astype(BF16)
        v = mkv_ref[0, :, 256 + pr * LANES:256 + (pr + 1) * LANES].astype(BF16)
        outs = []
        for hh in range(2):
            qm = jnp.where(low, qp, zero) if hh == 0 else jnp.where(low, zero, qp)
            s = _dot_nt(qm.astype(BF16), k) * (HEAD_DIM ** -0.5)
            m = jnp.max(s, axis=-1, keepdims=True)
            p = jnp.exp(s - m)
            o = jnp.dot(p.astype(BF16), v, preferred_element_type=F32)
            outs.append(o * (1.0 / jnp.sum(p, axis=-1, keepdims=True)))
        o_ref[0, :, sl] = jnp.where(low, outs[0], outs[1]).astype(o_ref.dtype)


def _mem_attn(mq, mkv, tm):
    B, R, _ = mq.shape
    return pl.pallas_call(
        _mem_attn_kernel,
        grid=(B, R // tm),
        in_specs=[pl.BlockSpec((1, tm, 256), lambda b, i: (b, i, 0)),
                  pl.BlockSpec((1,) + mkv.shape[1:], lambda b, i: (b, 0, 0))],
        out_specs=pl.BlockSpec((1, tm, 256), lambda b, i: (b, i, 0)),
        out_shape=jax.ShapeDtypeStruct((B, R, 256), BF16),
        compiler_params=_cparams(("arbitrary", "arbitrary")),
        name="mem_attn",
    )(mq, mkv)


TOK_ROWS = 8


def _load_token_tiles(ref, n):
    return jnp.concatenate([ref[pl.ds(c, n, stride=TOK_ROWS), :] for c in range(TOK_ROWS)], axis=1)


def _store_token_tiles(ref, x):
    n = x.shape[0]
    for c in range(TOK_ROWS):
        ref[pl.ds(c, n, stride=TOK_ROWS), :] = x[:, c * LANES:(c + 1) * LANES]


def _mix_kernel(x_ref, onsa_ref, oret_ref, rg_ref, omem_ref, wout_ref, rgain_ref, g2_ref,
                wr_hi_ref, wr_lo_ref, br_ref, x1_out, h_out, topi_out, topg_out):
    parts = [onsa_ref[...]]
    for c in range(2):
        sl = slice(c * LANES, (c + 1) * LANES)
        parts.append((_seg_rms(oret_ref[:, sl], rgain_ref[...]) * jax.nn.silu(rg_ref[:, sl])).astype(BF16))
    parts.append(omem_ref[...])
    mix = jnp.concatenate(parts, axis=1)
    x1 = x_ref[...] + jnp.dot(mix, wout_ref[...], preferred_element_type=F32)
    x1_out[...] = x1
    ms = jnp.mean(x1 * x1, axis=-1, keepdims=True)
    h = x1 * lax.rsqrt(ms + EPS) * g2_ref[...]
    _store_token_tiles(h_out, h)
    h_hi, h_lo, _ = _split3(h)
    logits = (jnp.dot(h_hi, wr_hi_ref[...], preferred_element_type=F32)
              + jnp.dot(h_hi, wr_lo_ref[...], preferred_element_type=F32)
              + jnp.dot(h_lo, wr_hi_ref[...], preferred_element_type=F32)) + br_ref[...]
    lane = _lane(logits.shape)
    key = logits
    topi = jnp.zeros(logits.shape, jnp.int32)
    topv = jnp.zeros(logits.shape, F32)
    for r in range(TOP_K):
        m = jnp.max(key, axis=-1, keepdims=True)
        idx = jnp.min(jnp.where(key == m, lane, LANES), axis=-1, keepdims=True)
        if r == 0:
            m0 = m
        topi = jnp.where(lane == r, idx, topi)
        topv = jnp.where(lane == r, jnp.exp(m - m0), topv)
        key = jnp.where(lane == idx, -jnp.inf, key)
    topi_out[...] = topi
    topg_out[...] = topv * (1.0 / jnp.sum(topv, axis=-1, keepdims=True))


def _mix(x, onsa, oret, rg, omem, mw, tm):
    T, D = x.shape
    row = lambda n: pl.BlockSpec((tm, n), lambda i: (i, 0))
    full = lambda a: pl.BlockSpec(a.shape, lambda i: (0, 0))
    names = ('wout', 'rgain', 'g2', 'wr_hi', 'wr_lo', 'br')
    return pl.pallas_call(
        _mix_kernel,
        grid=(T // tm,),
        in_specs=[row(D), row(512), row(256), row(256), row(256)] + [full(mw[n]) for n in names],
        out_specs=[row(D), pl.BlockSpec((tm * TOK_ROWS, LANES), lambda i: (i, 0)), row(LANES), row(LANES)],
        out_shape=[jax.ShapeDtypeStruct((T, D), F32), jax.ShapeDtypeStruct((T * TOK_ROWS, LANES), F32),
                   jax.ShapeDtypeStruct((T, LANES), jnp.int32), jax.ShapeDtypeStruct((T, LANES), F32)],
        compiler_params=_cparams(("arbitrary",)),
        name="mix",
    )(x, onsa, oret, rg, omem, *[mw[n] for n in names])


def _prep_mix_weights(w_out, ret_norm_g, norm2_g, w_router, b_router):
    wr = jnp.pad(w_router, ((0, 0), (0, LANES - N_EXPERTS)))
    wr_hi = wr.astype(BF16)
    wr_lo = (wr - wr_hi.astype(F32)).astype(BF16)
    br = jnp.concatenate([b_router.astype(F32), jnp.full((LANES - N_EXPERTS,), NEG, F32)])[None, :]
    return dict(wout=w_out.astype(BF16), rgain=jnp.tile(ret_norm_g, 2)[None, :], g2=norm2_g[None, :],
                wr_hi=wr_hi, wr_lo=wr_lo, br=br)


MOE_TOK_TILE = 128


def _row_copy(src, i, dst, j, sem):
    return pltpu.make_async_copy(src.at[pl.ds(pl.multiple_of(i, TOK_ROWS), TOK_ROWS)],
                                 dst.at[pl.ds(pl.multiple_of(j, TOK_ROWS), TOK_ROWS)], sem)


def _dispatch_kernel(dest_ref, h_hbm, xb_in, xb_out, sem):
    del xb_in
    tile = dest_ref.shape[0] // TOP_K
    base = pl.program_id(0) * (tile * TOK_ROWS)

    def start(t, c):
        for k in range(TOP_K):
            _row_copy(h_hbm, base + t * TOK_ROWS, xb_out, dest_ref[t * TOP_K + k], sem).start()
        return c

    def wait(t, c):
        for k in range(TOP_K):
            _row_copy(h_hbm, 0, xb_out, 0, sem).wait()
        return c

    lax.fori_loop(0, tile, start, 0)
    lax.fori_loop(0, tile, wait, 0)


def _dispatch(h, dest, xb):
    T = h.shape[0] // TOK_ROWS
    hbm = pl.BlockSpec(memory_space=pl.ANY)
    n = MOE_TOK_TILE * TOP_K
    return pl.pallas_call(
        _dispatch_kernel,
        grid=(T // MOE_TOK_TILE,),
        in_specs=[pl.BlockSpec((n,), lambda i: (i,), memory_space=pltpu.SMEM), hbm, hbm],
        out_specs=hbm,
        out_shape=jax.ShapeDtypeStruct(xb.shape, xb.dtype),
        scratch_shapes=[pltpu.SemaphoreType.DMA(())],
        input_output_aliases={2: 0},
        compiler_params=_cparams(("arbitrary",)),
        name="moe_dispatch",
    )(dest.reshape(-1), h, xb)


def _moe_kernel(be_ref, nb_ref, x_ref, wup_ref, bup_ref, wdn_ref, bdn_ref, y_ref, wup_bf, wdn_bf):
    j = pl.program_id(0)

    @pl.when(j < nb_ref[0])
    def _():
        @pl.when(jnp.logical_or(j == 0, be_ref[j] != be_ref[jnp.maximum(j - 1, 0)]))
        def _():
            wup_bf[...] = wup_ref[0].astype(BF16)
            wdn_bf[...] = wdn_ref[0].astype(BF16)

        x = _load_token_tiles(x_ref, MOE_ROWS).astype(BF16)
        up = jnp.dot(x, wup_bf[...], preferred_element_type=F32) + bup_ref[0]
        x_glu = jnp.minimum(up[:, :D_FF], SWIGLU_LIMIT)
        x_lin = jnp.clip(up[:, D_FF:], -SWIGLU_LIMIT, SWIGLU_LIMIT)
        act = x_glu * jax.nn.sigmoid(SWIGLU_ALPHA * x_glu) * (x_lin + 1.0)
        _store_token_tiles(y_ref, jnp.dot(act.astype(BF16), wdn_bf[...], preferred_element_type=F32) + bdn_ref[0])

    @pl.when(j >= nb_ref[0])
    def _():
        y_ref[...] = jnp.zeros(y_ref.shape, y_ref.dtype)


def _moe_experts(xb, blk_expert, n_used, w_up, b_up, w_down, b_down):
    D = w_up.shape[1]
    blk = pl.BlockSpec((MOE_ROWS * TOK_ROWS, LANES), lambda j, be, nb: (j, 0))
    grid_spec = pltpu.PrefetchScalarGridSpec(
        num_scalar_prefetch=2,
        grid=(xb.shape[0] // (MOE_ROWS * TOK_ROWS),),
        in_specs=[blk,
                  pl.BlockSpec((1, D, 2 * D_FF), lambda j, be, nb: (be[j], 0, 0)),
                  pl.BlockSpec((1, 1, 2 * D_FF), lambda j, be, nb: (be[j], 0, 0)),
                  pl.BlockSpec((1, D_FF, D), lambda j, be, nb: (be[j], 0, 0)),
                  pl.BlockSpec((1, 1, D), lambda j, be, nb: (be[j], 0, 0))],
        out_specs=blk,
        scratch_shapes=[pltpu.VMEM((D, 2 * D_FF), BF16), pltpu.VMEM((D_FF, D), BF16)],
    )
    return pl.pallas_call(
        _moe_kernel,
        grid_spec=grid_spec,
        out_shape=jax.ShapeDtypeStruct(xb.shape, F32),
        compiler_params=_cparams(("arbitrary",)),
        name="moe_experts",
    )(blk_expert, n_used, xb, w_up, b_up[:, None, :], w_down, b_down[:, None, :])


def _combine_kernel(dest_ref, g_ref, x1_ref, yb_hbm, out_ref, buf, sem):
    tile = x1_ref.shape[0]

    def start(t, c):
        for k in range(TOP_K):
            _row_copy(yb_hbm, dest_ref[t * TOP_K + k], buf.at[k], t * TOK_ROWS, sem).start()
        return c

    def wait(t, c):
        for k in range(TOP_K):
            _row_copy(yb_hbm, 0, buf.at[k], 0, sem).wait()
        return c

    lax.fori_loop(0, tile, start, 0)
    lax.fori_loop(0, tile, wait, 0)
    acc = x1_ref[...]
    for k in range(TOP_K):
        acc = acc + g_ref[:, k:k + 1] * _load_token_tiles(buf.at[k], tile)
    out_ref[...] = acc


def _combine(x1, topg, dest, yb):
    T, D = x1.shape
    tile = MOE_TOK_TILE
    row = lambda n: pl.BlockSpec((tile, n), lambda i: (i, 0))
    return pl.pallas_call(
        _combine_kernel,
        grid=(T // tile,),
        in_specs=[pl.BlockSpec((tile * TOP_K,), lambda i: (i,), memory_space=pltpu.SMEM),
                  row(LANES), row(D), pl.BlockSpec(memory_space=pl.ANY)],
        out_specs=row(D),
        out_shape=jax.ShapeDtypeStruct((T, D), F32),
        scratch_shapes=[pltpu.VMEM((TOP_K, tile * TOK_ROWS, LANES), F32), pltpu.SemaphoreType.DMA(())],
        compiler_params=_cparams(("arbitrary",)),
        name="moe_combine",
    )(dest.reshape(-1), topg, x1, yb)


def _route(topi):
    T = topi.shape[0]
    onehot = (topi[:, :, None] == jnp.arange(N_EXPERTS, dtype=jnp.int32)).astype(jnp.int32).sum(axis=1)
    rank = jnp.cumsum(onehot, axis=0) - onehot
    counts = onehot.sum(axis=0)
    padded = (counts + MOE_ROWS - 1) // MOE_ROWS * MOE_ROWS
    pad_end = jnp.cumsum(padded)
    pad_start = pad_end - padded
    dest = pad_start[topi] + jnp.take_along_axis(rank, topi, axis=1)
    n_blk = (T * TOP_K + N_EXPERTS * (MOE_ROWS - 1) + MOE_ROWS - 1) // MOE_ROWS
    blk_start = jnp.arange(n_blk, dtype=jnp.int32) * MOE_ROWS
    blk_expert = jnp.minimum((pad_end[None, :] <= blk_start[:, None]).sum(axis=1), N_EXPERTS - 1)
    return ((dest * TOK_ROWS).astype(jnp.int32), blk_expert.astype(jnp.int32),
            (pad_end[-1:] // MOE_ROWS).astype(jnp.int32), n_blk)


SAMPLE_ROWS = 8


def _token_mixers(x, pw, cos, sin, tm, key_major):
    names = ('q', 'kvc', 'kvs', 'kvw', 'ks_bf', 'kw_bf', 'gates', 'rq', 'rk', 'rv', 'rg', 'mq', 'vs_t', 'vw_t')
    return dict(zip(names, _project(x, pw, cos, sin, tm, key_major)))


def kernel(x_prompt, x_sample, mem_prompt, cache_cmp_kv, cache_slc_kv, cache_win_kv, state_ret, cache_mem_kv, page_table, norm1_g, w_in, q_norm_g, k_norm_cmp_g, k_norm_slc_g, k_norm_win_g, cmp_pe_k, cmp_w1_k, cmp_w2_k, cmp_pe_v, cmp_w1_v, cmp_w2_v, ret_norm_g, mem_norm_g, w_mem_kv, mem_q_norm_g, mem_k_norm_g, w_out, norm2_g, w_router, b_router, w_up, b_up, w_down, b_down):
    B, S, D = x_prompt.shape
    DB, QS, _ = x_sample.shape
    n_mem = mem_prompt.shape[1]
    n_pages, page = page_table.shape[1], cache_cmp_kv.shape[2]
    past = n_pages * page
    NQ = SAMPLE_ROWS
    G, HD = NSA_GROUPS, HEAD_DIM
    win_rows = min(WINDOW, S)
    TP, TS = B * S, DB * NQ

    cos_p, sin_p = _rope_tables(jnp.arange(S, dtype=jnp.int32))
    cos_s, sin_s = _rope_tables(past + jnp.arange(NQ, dtype=jnp.int32))
    cos_s, sin_s = jnp.tile(cos_s, (DB, 1)), jnp.tile(sin_s, (DB, 1))

    xp = x_prompt.reshape(TP, D)
    xs = jnp.pad(x_sample, ((0, 0), (0, NQ - QS), (0, 0))).reshape(TS, D)
    unpad = lambda a: a.reshape(DB, NQ, -1)[:, :QS]
    outs = [[] for _ in range(9)]
    for l in range(w_in.shape[0]):
        pw = _prep_proj_weights(norm1_g[l], w_in[l], q_norm_g[l], k_norm_slc_g[l], k_norm_win_g[l], mem_q_norm_g[l])
        cw = _prep_compress_weights(cmp_pe_k[l], cmp_w1_k[l], cmp_w2_k[l], cmp_pe_v[l], cmp_w1_v[l], cmp_w2_v[l],
                                    k_norm_cmp_g[l])
        mw = _prep_mix_weights(w_out[l], ret_norm_g[l], norm2_g[l], w_router[l], b_router[l])

        t = _token_mixers(xp, pw, cos_p, sin_p, 512, True)
        kc, vct = _compress(t['kvc'].reshape(B, S, 256), cw)
        o_nsa = _nsa_prompt(B, t['q'], t['gates'], kc, vct, t['ks_bf'], t['vs_t'], t['kw_bf'], t['vw_t'])
        o_ret, r_p = _retention(t['rq'].reshape(B, S, 256), t['rk'].reshape(B, S, 256), t['rv'].reshape(B, S, 256),
                                jnp.zeros((B, 2, LANES, LANES), F32), RET_CHUNK, RET_CHUNK)
        mkv = _mem_kv(mem_prompt.reshape(B * n_mem, D), mem_norm_g[l], w_mem_kv[l], mem_k_norm_g[l])
        o_mem = _mem_attn(t['mq'].reshape(B, S, 256), mkv.reshape(B, n_mem, 512), 512)
        x1_p, h_p, topi_p, topg_p = _mix(xp, o_nsa.reshape(TP, -1), o_ret.reshape(TP, 256), t['rg'],
                                         o_mem.reshape(TP, 256), mw, 512)
        outs[0].append(t['kvc'].reshape(B, S, 2, G, HD))
        outs[1].append(t['kvs'].reshape(B, S, 2, G, HD))
        outs[2].append(t['kvw'].reshape(B, S, 2, G, HD)[:, S - win_rows:])
        outs[3].append(_pairs_to_state(r_p))
        outs[4].append(mkv.reshape(B, n_mem, 2, MEM_HEADS, HD))

        t = _token_mixers(xs, pw, cos_s, sin_s, TS, False)
        win = cache_win_kv[l]
        o_nsa = _nsa_sample(t['q'].astype(F32).reshape(DB, NQ, -1), t['gates'].reshape(DB, NQ, LANES),
                            t['kvs'].reshape(DB, NQ, 256), t['kvw'].reshape(DB, NQ, 256),
                            win.reshape(DB, win.shape[1], 256),
                            cache_cmp_kv[l].reshape(-1, page, 256), cache_slc_kv[l].reshape(-1, page, 256),
                            page_table, cw, QS)
        o_ret, r_s = _retention(t['rq'].reshape(DB, NQ, 256), t['rk'].reshape(DB, NQ, 256),
                                t['rv'].reshape(DB, NQ, 256), _state_to_pairs(state_ret[l].astype(F32)), QS, NQ)
        o_mem = _mem_attn(t['mq'].reshape(DB, NQ, 256), cache_mem_kv[l].reshape(DB, n_mem, 512), NQ)
        x1_s, h_s, topi_s, topg_s = _mix(xs, o_nsa.reshape(TS, -1).astype(BF16), o_ret.reshape(TS, 256), t['rg'],
                                         o_mem.reshape(TS, 256), mw, TS)
        kv5 = lambda a: unpad(a).reshape(DB, QS, 2, G, HD)
        outs[5].append(kv5(t['kvc']))
        outs[6].append(kv5(t['kvs']))
        outs[7].append(jnp.concatenate([win, kv5(t['kvw'])], axis=1)[:, QS:])
        outs[8].append(_pairs_to_state(r_s))

        valid = lambda a: unpad(a).reshape(DB * QS, -1)
        dest, blk_expert, n_used, n_blk = _route(jnp.concatenate([topi_p, valid(topi_s)], axis=0)[:, :TOP_K])
        xb = jnp.zeros((n_blk * MOE_ROWS * TOK_ROWS, LANES), F32)
        xb = _dispatch(h_p, dest[:TP], xb)
        h_s_valid = h_s.reshape(DB, NQ, TOK_ROWS, LANES)[:, :QS].reshape(DB * QS * TOK_ROWS, LANES)
        xb = _dispatch(h_s_valid, dest[TP:], xb)
        yb = _moe_experts(xb, blk_expert, n_used, w_up[l], b_up[l], w_down[l], b_down[l])
        xp = _combine(x1_p, topg_p, dest[:TP], yb)
        xs_new = _combine(valid(x1_s), valid(topg_s), dest[TP:], yb)
        xs = jnp.pad(xs_new.reshape(DB, QS, D), ((0, 0), (0, NQ - QS), (0, 0))).reshape(TS, D)
    y_sample = xs.reshape(DB, NQ, D)[:, :QS]
    return (xp.reshape(B, S, D), y_sample) + tuple(jnp.stack(o) for o in outs)
```

```python
import functools

import jax
import jax.numpy as jnp
from jax import lax
from jax.experimental import pallas as pl
from jax.experimental.pallas import tpu as pltpu

F32 = jnp.float32
BF16 = jnp.bfloat16

HEAD_DIM = 64
NSA_HEADS = 8
NSA_GROUPS = 2
NSA_HPG = NSA_HEADS // NSA_GROUPS
RET_HEADS = 4
MEM_HEADS = 4
CMP_BLK = 32
CMP_STRIDE = 16
CMP_HID = 256
SLC_BLK = 64
N_SEL = 16
WINDOW = 512
Q_BLK = 128
RET_CHUNK = 128
N_EXPERTS = 32
TOP_K = 4
D_FF = 1024
SWIGLU_LIMIT = 7.0
SWIGLU_ALPHA = 1.702
EPS = 1e-6
NEG = -1e30
BIG = 1e9
ROPE_BASE = 10000.0
SLC_RATIO = SLC_BLK // CMP_STRIDE
CMP_OVL = CMP_BLK // CMP_STRIDE - 1

LANES = 128
KEY_TILE = 512
MOE_ROWS = 512
VMEM_LIMIT = 56 * 1024 * 1024

C_Q = 0
C_KVC = C_Q + NSA_HEADS * LANES
C_KVS = C_KVC + 256
C_KVW = C_KVS + 256
C_GATE = C_KVW + 256
C_RQ = C_GATE + LANES
C_RK = C_RQ + 256
C_RV = C_RK + 256
C_RG = C_RV + 256
C_MQ = C_RG + 256
C_END = C_MQ + 256


def _cparams(sem):
    return pltpu.CompilerParams(dimension_semantics=sem, vmem_limit_bytes=VMEM_LIMIT)


def _lane(shape):
    return lax.broadcasted_iota(jnp.int32, shape, len(shape) - 1)


def _seg_rms(x, gain):
    lo = _lane(x.shape) < HEAD_DIM
    x2 = x * x
    s_lo = jnp.sum(jnp.where(lo, x2, 0.0), axis=-1, keepdims=True)
    s_hi = jnp.sum(jnp.where(lo, 0.0, x2), axis=-1, keepdims=True)
    ms = jnp.where(lo, s_lo, s_hi) * (1.0 / HEAD_DIM)
    return x * lax.rsqrt(ms + EPS) * gain


def _swap_halves(x):
    first = (_lane(x.shape) & (HEAD_DIM // 2)) == 0
    return jnp.where(first, pltpu.roll(x, LANES - HEAD_DIM // 2, 1), pltpu.roll(x, HEAD_DIM // 2, 1))


def _proj_kernel(x_ref, g1_ref, w_ref, cos_ref, sin_ref, qg_ref, ksg_ref, kwg_ref, mqg_ref,
                 q_out, kvc_out, kvs_out, kvw_out, ks_bf, kw_bf, gate_out,
                 rq_out, rk_out, rv_out, rg_out, mq_out, *extra, seq_tiles):
    x = x_ref[...]
    tm = x.shape[0]
    ms = jnp.mean(x * x, axis=-1, keepdims=True)
    xn = (x * lax.rsqrt(ms + EPS) * g1_ref[...]).astype(BF16)
    z = jnp.dot(xn, w_ref[...], preferred_element_type=F32)
    for i in range(NSA_HEADS):
        zq = z[:, C_Q + i * LANES:C_Q + (i + 1) * LANES]
        msq = jnp.sum(zq * zq, axis=-1, keepdims=True) * (1.0 / HEAD_DIM)
        qn = zq * lax.rsqrt(msq + EPS) * qg_ref[:, i * LANES:(i + 1) * LANES] * (HEAD_DIM ** -0.5)
        if seq_tiles:
            q_out[i * LANES:(i + 1) * LANES, :] = qn.T.astype(BF16)
        else:
            q_out[:, i * LANES:(i + 1) * LANES] = qn.astype(BF16)
    kvc_out[...] = z[:, C_KVC:C_KVC + 256]
    ks = _seg_rms(z[:, C_KVS:C_KVS + LANES], ksg_ref[...])
    vs = z[:, C_KVS + LANES:C_KVS + 256]
    kvs_out[:, 0:LANES] = ks
    kvs_out[:, LANES:256] = vs
    kw = _seg_rms(z[:, C_KVW:C_KVW + LANES], kwg_ref[...])
    vw = z[:, C_KVW + LANES:C_KVW + 256]
    kvw_out[:, 0:LANES] = kw
    kvw_out[:, LANES:256] = vw
    gates = jax.nn.sigmoid(z[:, C_GATE:C_GATE + LANES])
    ks_bf[:, 0:LANES] = ks.astype(BF16)
    if seq_tiles:
        vs_t, vw_t = extra
        pos = (pl.program_id(0) % seq_tiles) * tm + lax.broadcasted_iota(jnp.int32, (tm, LANES), 0)
        ks_bf[:, LANES:256] = jnp.where(_lane((tm, LANES)) == pos // SLC_BLK, 1.0, 0.0).astype(BF16)
        kw_bf[...] = kw.astype(BF16)
        vs_t[...] = vs.T.astype(BF16)
        vw_t[...] = vw.T.astype(BF16)
        gate_out[...] = gates.T
    else:
        ks_bf[:, LANES:256] = vs.astype(BF16)
        kw_bf[:, 0:LANES] = kw.astype(BF16)
        kw_bf[:, LANES:256] = vw.astype(BF16)
        gate_out[...] = gates
    for c in range(2):
        sl = slice(c * LANES, (c + 1) * LANES)
        cos = cos_ref[:, sl]
        sin = sin_ref[:, sl]
        rq = z[:, C_RQ + c * LANES:C_RQ + (c + 1) * LANES]
        rk = z[:, C_RK + c * LANES:C_RK + (c + 1) * LANES]
        rq_out[:, sl] = rq * cos + _swap_halves(rq) * sin
        rk_out[:, sl] = (rk * cos + _swap_halves(rk) * sin) * (HEAD_DIM ** -0.5)
        mq = z[:, C_MQ + c * LANES:C_MQ + (c + 1) * LANES]
        mq_out[:, sl] = _seg_rms(mq, mqg_ref[...]).astype(BF16)
    rv_out[...] = z[:, C_RV:C_RV + 256]
    rg_out[...] = z[:, C_RG:C_RG + 256]


def _project(x, pw, cos, sin, tm, key_major):
    T, D = x.shape
    nt = cos.shape[0] // tm
    row = lambda n: (pl.BlockSpec((tm, n), lambda i: (i, 0)), (T, n))
    col = lambda n: (pl.BlockSpec((n, tm), lambda i: (0, i)), (n, T))
    full = lambda a: pl.BlockSpec(a.shape, lambda i: (0, 0))
    tab = pl.BlockSpec((tm, 256), lambda i: (i % nt, 0))
    tok = col if key_major else row
    outs = [
        (tok(NSA_HEADS * LANES), BF16),
        (row(256), F32), (row(256), F32), (row(256), F32),
        (row(256), BF16), (row(LANES if key_major else 256), BF16),
        (tok(LANES), F32),
        (row(256), F32), (row(256), F32), (row(256), F32), (row(256), F32),
        (row(256), BF16),
    ]
    if key_major:
        outs += [(col(LANES), BF16), (col(LANES), BF16)]
    return pl.pallas_call(
        functools.partial(_proj_kernel, seq_tiles=nt if key_major else 0),
        grid=(T // tm,),
        in_specs=[row(D)[0], full(pw['g1']), full(pw['w']), tab, tab,
                  full(pw['qg']), full(pw['ksg']), full(pw['kwg']), full(pw['mqg'])],
        out_specs=[spec for (spec, _), _ in outs],
        out_shape=[jax.ShapeDtypeStruct(shape, dt) for (_, shape), dt in outs],
        compiler_params=_cparams(("arbitrary",)),
        name="proj",
    )(x, pw['g1'], pw['w'], cos, sin, pw['qg'], pw['ksg'], pw['kwg'], pw['mqg'])


def _prep_proj_weights(norm1_g, w_in, q_norm_g, k_norm_slc_g, k_norm_win_g, mem_q_norm_g):
    D = w_in.shape[0]
    sizes = (512, 256, 256, 256, 24, 256, 256, 256, 256, 256)
    parts, off = [], 0
    for n in sizes:
        parts.append(w_in[:, off:off + n])
        off += n
    wq, wkvc, wkvs, wkvw, wg, wrq, wrk, wrv, wrg, wmq = parts
    zero = jnp.zeros((D, HEAD_DIM), w_in.dtype)
    qcols, qg = [], []
    gz = jnp.zeros((HEAD_DIM,), F32)
    for i in range(NSA_HEADS):
        wh = wq[:, i * HEAD_DIM:(i + 1) * HEAD_DIM]
        if i < NSA_HPG:
            qcols += [wh, zero]
            qg += [q_norm_g, gz]
        else:
            qcols += [zero, wh]
            qg += [gz, q_norm_g]
    wgp = jnp.pad(wg, ((0, 0), (0, LANES - wg.shape[1])))
    w = jnp.concatenate(qcols + [wkvc, wkvs, wkvw, wgp, wrq, wrk, wrv, wrg, wmq], axis=1).astype(BF16)
    two = lambda g: jnp.tile(g, 2)[None, :]
    return dict(g1=norm1_g[None, :], w=w, qg=jnp.concatenate(qg)[None, :],
                ksg=two(k_norm_slc_g), kwg=two(k_norm_win_g), mqg=two(mem_q_norm_g))


def _rope_tables(pos):
    half = HEAD_DIM // 2
    inv = ROPE_BASE ** (-jnp.arange(half, dtype=F32) / half)
    ang = pos.astype(F32)[:, None] * inv[None, :]
    cos, sin = jnp.cos(ang), jnp.sin(ang)
    cos = jnp.tile(jnp.concatenate([cos, cos], axis=1), (1, RET_HEADS))
    sin = jnp.tile(jnp.concatenate([-sin, sin], axis=1), (1, RET_HEADS))
    return cos, sin


def _compress_half(src_ref, pe_ref, wa_ref, wb_ref, w2_ref, nchunks):
    a_parts, b_parts = [], []
    for r in range(CMP_STRIDE):
        xr = src_ref[pl.ds(r, nchunks, stride=CMP_STRIDE), :]
        a_parts.append((xr + pe_ref[r:r + 1, :]).astype(BF16))
        b_parts.append((xr + pe_ref[CMP_STRIDE + r:CMP_STRIDE + r + 1, :]).astype(BF16))
    ha = jnp.dot(jnp.concatenate(a_parts, axis=1), wa_ref[...], preferred_element_type=F32)
    hb = jnp.dot(jnp.concatenate(b_parts, axis=1), wb_ref[...], preferred_element_type=F32)
    h = ha + pltpu.roll(hb, nchunks - 1, 0)
    act = jax.nn.gelu(h).astype(BF16)
    return jnp.dot(act, w2_ref[...], preferred_element_type=F32)


def _compress_kernel(kv_ref, pe_ref, wa_ref, wb_ref, w2_ref, kg_ref, kc_out, vct_out):
    nchunks = kc_out.shape[1]
    y = _compress_half(kv_ref.at[0], pe_ref.at[0], wa_ref.at[0], wb_ref.at[0], w2_ref.at[0], nchunks)

    @pl.when(pl.program_id(1) == 0)
    def _():
        kc_out[0] = _seg_rms(y, kg_ref[...]).astype(BF16)

    @pl.when(pl.program_id(1) == 1)
    def _():
        vct_out[0] = y.T.astype(BF16)


def _compress(kvc, cw):
    B, T, _ = kvc.shape
    nchunks = T // CMP_STRIDE
    per_kv = lambda a: pl.BlockSpec((1,) + a.shape[1:], lambda b, j: (j,) + (0,) * (a.ndim - 1))
    return pl.pallas_call(
        _compress_kernel,
        grid=(B, 2),
        in_specs=[pl.BlockSpec((1, T, LANES), lambda b, j: (b, 0, j)),
                  per_kv(cw['pe']), per_kv(cw['wa']), per_kv(cw['wb']), per_kv(cw['w2']),
                  pl.BlockSpec((1, LANES), lambda b, j: (0, 0))],
        out_specs=[pl.BlockSpec((1, nchunks, LANES), lambda b, j: (b, 0, 0)),
                   pl.BlockSpec((1, LANES, nchunks), lambda b, j: (b, 0, 0))],
        out_shape=[jax.ShapeDtypeStruct((B, nchunks, LANES), BF16),
                   jax.ShapeDtypeStruct((B, LANES, nchunks), BF16)],
        compiler_params=_cparams(("arbitrary", "arbitrary")),
        name="compress",
    )(kvc, cw['pe'], cw['wa'], cw['wb'], cw['w2'], cw['kg'])


def _prep_compress_weights(pe_k, w1_k, w2_k, pe_v, w1_v, w2_v, k_norm_g):
    eye = jnp.eye(NSA_GROUPS, dtype=F32)

    def one(pe, w1, w2):
        w1r = w1.reshape(CMP_BLK, HEAD_DIM, CMP_HID)
        wf = jnp.einsum('gh,rdc->rgdhc', eye, w1r).reshape(CMP_BLK * LANES, NSA_GROUPS * CMP_HID)
        w2f = jnp.einsum('gh,cd->gchd', eye, w2).reshape(NSA_GROUPS * CMP_HID, LANES)
        half = CMP_STRIDE * LANES
        return jnp.tile(pe, (1, NSA_GROUPS)), wf[:half].astype(BF16), wf[half:].astype(BF16), w2f.astype(BF16)

    k, v = one(pe_k, w1_k, w2_k), one(pe_v, w1_v, w2_v)
    st = lambda i: jnp.stack([k[i], v[i]])
    return dict(pe=st(0), wa=st(1), wb=st(2), w2=st(3), kg=jnp.tile(k_norm_g, 2)[None, :])


def _dot_nt(a, b):
    return lax.dot_general(a, b, (((1,), (1,)), ((), ())), preferred_element_type=F32)


def _split3(x):
    hi = x.astype(BF16)
    r = x - hi.astype(F32)
    mid = r.astype(BF16)
    lo = (r - mid.astype(F32)).astype(BF16)
    return hi, mid, lo


def _select_bias(ps, cur, rounds, axis, cur_in_range=True):
    j = lax.broadcasted_iota(jnp.int32, ps.shape, axis)
    last = cur if cur_in_range else cur - 1
    key = jnp.where(j <= last, ps, NEG)
    for forced in (0, last, cur - 1):
        key = jnp.where(j == forced, BIG, key)
    bias = jnp.full(ps.shape, NEG, F32)
    for _ in range(rounds):
        m = jnp.max(key, axis=axis, keepdims=True)
        idx = jnp.min(jnp.where(key == m, j, LANES), axis=axis, keepdims=True)
        pick = j == idx
        bias = jnp.where(pick, 0.0, bias)
        key = jnp.where(pick, -jnp.inf, key)
    return bias


def _cmp_branch(qg, qpos, kc, vc):
    s = _dot_nt(qg, kc)
    cend = _lane((1, kc.shape[0])) * CMP_STRIDE + (CMP_BLK - 1)
    s = jnp.where(cend <= qpos, s, NEG)
    m = jnp.max(s, axis=-1, keepdims=True)
    p = jnp.exp(s - m)
    p = p * (1.0 / jnp.sum(p, axis=-1, keepdims=True))
    p = jnp.where(qpos >= CMP_BLK - 1, p, 0.0)
    return p, jnp.dot(p.astype(BF16), vc, preferred_element_type=F32)


def _sel_tile(qaug, k_tile, v_tile, blk0, mask, m_sc, l_sc, acc_sc):
    n = k_tile.shape[0]
    blk = blk0 + lax.broadcasted_iota(jnp.int32, (n, LANES), 0) // SLC_BLK
    onehot = jnp.where(_lane((n, LANES)) == blk, 1.0, 0.0).astype(BF16)
    s = _dot_nt(qaug, jnp.concatenate([k_tile, onehot], axis=1))
    if mask is not None:
        s = jnp.where(mask, s, NEG)
    _online_update(s, v_tile, m_sc, l_sc, acc_sc)


def _online_update(s, v_tile, m_sc, l_sc, acc_sc):
    m_old = m_sc[...]
    m_new = jnp.maximum(m_old, jnp.max(s, axis=-1, keepdims=True))
    alpha = jnp.exp(m_old - m_new)
    p = jnp.exp(s - m_new)
    l_sc[...] = alpha * l_sc[...] + jnp.sum(p, axis=-1, keepdims=True)
    acc_sc[...] = alpha * acc_sc[...] + jnp.dot(p.astype(BF16), v_tile, preferred_element_type=F32)
    m_sc[...] = m_new


def _gate_heads(gates, g, oc, osel, ow, nq):
    heads = []
    for h in range(NSA_HPG):
        rs = slice(h * nq, (h + 1) * nq)
        c0 = (NSA_HPG * g + h) * 3
        heads.append(gates[:, c0:c0 + 1] * oc[rs] + gates[:, c0 + 1:c0 + 2] * osel[rs]
                     + gates[:, c0 + 2:c0 + 3] * ow[rs])
    return heads


def _store_heads(heads, o_ref):
    low = _lane(heads[0].shape) < HEAD_DIM
    for pr in range(NSA_HEADS // 2):
        even, odd = heads[2 * pr], heads[2 * pr + 1]
        if pr < NSA_GROUPS:
            odd = pltpu.roll(odd, HEAD_DIM, 1)
        else:
            even = pltpu.roll(even, HEAD_DIM, 1)
        o_ref[0, :, pr * LANES:(pr + 1) * LANES] = jnp.where(low, even, odd).astype(o_ref.dtype)


def _nsa_sample_kernel(pt_ref, q_ref, gate_ref, ksn_ref, kwn_ref, win_ref, pool_c, pool_s,
                       pe_ref, wa_ref, wb_ref, w2_ref, kg_ref, mmap_ref, o_ref,
                       cbuf, sbuf, sems, m_sc, l_sc, acc_sc, *, page, n_new):
    b = pl.program_id(0)
    past = sbuf.shape[0]
    npages = past // page
    nq = q_ref.shape[1]

    def page_copies(p, pg):
        dst = pl.ds(pl.multiple_of(p * page, page), page)
        return (pltpu.make_async_copy(pool_c.at[pg, :, 0:LANES], cbuf.at[0, dst], sems.at[0]),
                pltpu.make_async_copy(pool_c.at[pg, :, LANES:256], cbuf.at[1, dst], sems.at[1]),
                pltpu.make_async_copy(pool_s.at[pg], sbuf.at[dst], sems.at[2]))

    def start(p, c):
        for cp in page_copies(p, pt_ref[b * npages + p]):
            cp.start()
        return c

    def wait(p, c):
        for cp in page_copies(p, 0):
            cp.wait()
        return c

    lax.fori_loop(0, npages, start, 0)
    lax.fori_loop(0, npages, wait, 0)

    nchunks = past // CMP_STRIDE
    kc = _seg_rms(_compress_half(cbuf.at[0], pe_ref.at[0], wa_ref.at[0], wb_ref.at[0], w2_ref.at[0], nchunks),
                  kg_ref[...]).astype(BF16)
    vc = _compress_half(cbuf.at[1], pe_ref.at[1], wa_ref.at[1], wb_ref.at[1], w2_ref.at[1], nchunks).astype(BF16)

    rows = NSA_HPG * nq
    srow = lax.broadcasted_iota(jnp.int32, (rows, 1), 0) % nq
    qpos = past + srow
    cur = (past + lax.broadcasted_iota(jnp.int32, (nq, 1), 0)) // SLC_BLK
    gates = gate_ref[0]
    pad_keys = lambda x: jnp.concatenate([x, jnp.zeros((LANES - nq, LANES), F32)], axis=0).astype(BF16)
    new_ok = _lane((1, LANES)) <= jnp.minimum(srow, n_new - 1)
    heads = []
    for g in range(NSA_GROUPS):
        qg = jnp.concatenate(
            [q_ref[0, :, (NSA_HPG * g + h) * LANES:(NSA_HPG * g + h + 1) * LANES] for h in range(NSA_HPG)],
            axis=0).astype(BF16)
        p, oc = _cmp_branch(qg, qpos, kc, vc)
        p4 = p[0:nq] + p[nq:2 * nq] + p[2 * nq:3 * nq] + p[3 * nq:4 * nq]
        ps = sum(_dot_nt(part, mmap_ref[...]) for part in _split3(p4))
        bias = _select_bias(ps, cur, N_SEL - 1, 1, cur_in_range=False)
        qaug = jnp.concatenate([qg, jnp.concatenate([bias] * NSA_HPG, axis=0).astype(BF16)], axis=1)
        m_sc[...] = jnp.full(m_sc.shape, NEG, F32)
        l_sc[...] = jnp.zeros(l_sc.shape, F32)
        acc_sc[...] = jnp.zeros(acc_sc.shape, F32)

        def body(kt, c, qaug=qaug):
            k0 = pl.multiple_of(kt * KEY_TILE, KEY_TILE)
            _sel_tile(qaug, sbuf[pl.ds(k0, KEY_TILE), 0:LANES].astype(BF16),
                      sbuf[pl.ds(k0, KEY_TILE), LANES:256].astype(BF16),
                      kt * (KEY_TILE // SLC_BLK), None, m_sc, l_sc, acc_sc)
            return c

        lax.fori_loop(0, past // KEY_TILE, body, 0)
        s_new = jnp.where(new_ok, _dot_nt(qg, pad_keys(ksn_ref[0, :, 0:LANES])), NEG)
        _online_update(s_new, pad_keys(ksn_ref[0, :, LANES:256]), m_sc, l_sc, acc_sc)
        osel = acc_sc[...] * (1.0 / l_sc[...])
        nwin = win_ref.shape[1]
        sw_old = jnp.where(_lane((1, nwin)) + (WINDOW - nwin) > srow,
                           _dot_nt(qg, win_ref[0, :, 0:LANES].astype(BF16)), NEG)
        sw_new = jnp.where(new_ok, _dot_nt(qg, pad_keys(kwn_ref[0, :, 0:LANES])), NEG)
        sw = jnp.concatenate([sw_old, sw_new], axis=1)
        mw = jnp.max(sw, axis=-1, keepdims=True)
        pw = jnp.exp(sw - mw)
        ow = (jnp.dot(pw[:, :nwin].astype(BF16), win_ref[0, :, LANES:256].astype(BF16), preferred_element_type=F32)
              + jnp.dot(pw[:, nwin:].astype(BF16), pad_keys(kwn_ref[0, :, LANES:256]), preferred_element_type=F32))
        ow = ow * (1.0 / jnp.sum(pw, axis=-1, keepdims=True))
        heads += _gate_heads(gates, g, oc, osel, ow, nq)
    _store_heads(heads, o_ref)


def _nsa_sample(q, gates, ksn, kwn, win, pool_c, pool_s, page_table, cw, n_new):
    DB, nq, _ = q.shape
    n_pages = page_table.shape[1]
    page = pool_c.shape[1]
    past = n_pages * page
    nchunks = past // CMP_STRIDE
    rows = NSA_HPG * nq
    mmap = _importance_map(nchunks)
    per_seq = lambda a: pl.BlockSpec((1,) + a.shape[1:], lambda b, pt: (b,) + (0,) * (a.ndim - 1))
    const = lambda a: pl.BlockSpec(a.shape, lambda b, pt: (0,) * a.ndim)
    hbm = pl.BlockSpec(memory_space=pl.ANY)
    grid_spec = pltpu.PrefetchScalarGridSpec(
        num_scalar_prefetch=1,
        grid=(DB,),
        in_specs=[per_seq(q), per_seq(gates), per_seq(ksn), per_seq(kwn), per_seq(win), hbm, hbm,
                  const(cw['pe']), const(cw['wa']), const(cw['wb']), const(cw['w2']), const(cw['kg']), const(mmap)],
        out_specs=pl.BlockSpec((1, nq, NSA_HEADS * HEAD_DIM), lambda b, pt: (b, 0, 0)),
        scratch_shapes=[pltpu.VMEM((2, past, LANES), F32), pltpu.VMEM((past, 256), F32),
                        pltpu.SemaphoreType.DMA((3,)),
                        pltpu.VMEM((rows, 1), F32), pltpu.VMEM((rows, 1), F32), pltpu.VMEM((rows, LANES), F32)],
    )
    return pl.pallas_call(
        functools.partial(_nsa_sample_kernel, page=page, n_new=n_new),
        grid_spec=grid_spec,
        out_shape=jax.ShapeDtypeStruct((DB, nq, NSA_HEADS * HEAD_DIM), F32),
        compiler_params=_cparams(("arbitrary",)),
        name="nsa_sample",
    )(page_table.reshape(-1), q, gates, ksn, kwn, win, pool_c, pool_s,
      cw['pe'], cw['wa'], cw['wb'], cw['w2'], cw['kg'], mmap)


def _nsa_prompt_kernel(q_ref, gate_ref, kc_ref, vct_ref, ks_ref, vst_ref, kw_ref, vwt_ref, mmap_ref, o_ref,
                       m_sc, l_sc, acc_sc):
    i = pl.program_id(1)
    s0 = i * Q_BLK
    cols = NSA_HPG * Q_BLK
    qpos = s0 + _lane((1, cols)) % Q_BLK
    cur = (s0 + _lane((1, Q_BLK))) // SLC_BLK
    kc, vct = kc_ref[0], vct_ref[0]
    ncmp = kc.shape[0]
    n_full = s0 // KEY_TILE
    w0 = pl.multiple_of(jnp.maximum(s0 - WINDOW, 0), Q_BLK)
    nw = WINDOW + Q_BLK
    dot = functools.partial(jnp.dot, preferred_element_type=F32)

    def softmax_keys(s):
        p = jnp.exp(s - jnp.max(s, axis=0, keepdims=True))
        return p, 1.0 / jnp.sum(p, axis=0, keepdims=True)

    qaugs = []
    branch = []
    for g in range(NSA_GROUPS):
        qt = jnp.concatenate([q_ref[(NSA_HPG * g + h) * LANES:(NSA_HPG * g + h + 1) * LANES, :]
                              for h in range(NSA_HPG)], axis=1)
        s = dot(kc, qt)
        n_idx = lax.broadcasted_iota(jnp.int32, (ncmp, cols), 0)
        s = jnp.where(n_idx * CMP_STRIDE + (CMP_BLK - 1) <= qpos, s, NEG)
        p, inv = softmax_keys(s)
        p = jnp.where(qpos >= CMP_BLK - 1, p * inv, 0.0)
        oc = dot(vct, p.astype(BF16))
        p4 = p[:, 0:Q_BLK] + p[:, Q_BLK:2 * Q_BLK] + p[:, 2 * Q_BLK:3 * Q_BLK] + p[:, 3 * Q_BLK:4 * Q_BLK]
        ps_t = sum(dot(mmap_ref[...], part) for part in _split3(p4))
        bias = _select_bias(ps_t, cur, N_SEL, 0).astype(BF16)
        qaugs.append(jnp.concatenate([qt, jnp.concatenate([bias] * NSA_HPG, axis=1)], axis=0))
        sw = dot(kw_ref[0, pl.ds(w0, nw), :], qt)
        d = qpos - (w0 + lax.broadcasted_iota(jnp.int32, (nw, cols), 0))
        sw = jnp.where(d >= 0, jnp.where(d < WINDOW, sw, NEG), NEG)
        pw, invw = softmax_keys(sw)
        ow = dot(vwt_ref[:, pl.ds(w0, nw)], pw.astype(BF16)) * invw
        branch.append((oc, ow))
        m_sc[g] = jnp.full(m_sc.shape[1:], NEG, F32)
        l_sc[g] = jnp.zeros(l_sc.shape[1:], F32)
        acc_sc[g] = jnp.zeros(acc_sc.shape[1:], F32)

    def tile(kt, causal):
        k0 = pl.multiple_of(kt * KEY_TILE, KEY_TILE)
        k_aug = ks_ref[0, pl.ds(k0, KEY_TILE), :]
        v_t = vst_ref[:, pl.ds(k0, KEY_TILE)]
        for g in range(NSA_GROUPS):
            s = dot(k_aug, qaugs[g])
            if causal:
                s = jnp.where(k0 + lax.broadcasted_iota(jnp.int32, s.shape, 0) <= qpos, s, NEG)
            m_old = m_sc[g]
            m_new = jnp.maximum(m_old, jnp.max(s, axis=0, keepdims=True))
            alpha = jnp.exp(m_old - m_new)
            p = jnp.exp(s - m_new)
            l_sc[g] = alpha * l_sc[g] + jnp.sum(p, axis=0, keepdims=True)
            acc_sc[g] = alpha * acc_sc[g] + dot(v_t, p.astype(BF16))
            m_sc[g] = m_new

    def body(kt, c):
        tile(kt, False)
        return c

    lax.fori_loop(0, n_full, body, 0)
    tile(n_full, True)

    pieces = []
    for g in range(NSA_GROUPS):
        oc, ow = branch[g]
        osel = acc_sc[g] * (1.0 / l_sc[g])
        for h in range(NSA_HPG):
            cs = slice(h * Q_BLK, (h + 1) * Q_BLK)
            c0 = (NSA_HPG * g + h) * 3
            o_h = (gate_ref[c0:c0 + 1, :] * oc[:, cs] + gate_ref[c0 + 1:c0 + 2, :] * osel[:, cs]
                   + gate_ref[c0 + 2:c0 + 3, :] * ow[:, cs])
            pieces.append(o_h[g * HEAD_DIM:(g + 1) * HEAD_DIM, :])
    o_ref[0] = jnp.concatenate(pieces, axis=0).T.astype(o_ref.dtype)


def _importance_map(ncmp):
    j = jnp.arange(LANES)[:, None]
    n = jnp.arange(ncmp)[None, :]
    return ((n >= SLC_RATIO * j - CMP_OVL) & (n < SLC_RATIO * j + SLC_RATIO)).astype(BF16)


def _nsa_prompt(B, qt, gates_t, kc, vct, ks_aug, vs_t, kw, vw_t):
    S = qt.shape[1] // B
    nq = S // Q_BLK
    ncmp = kc.shape[1]
    cols = NSA_HPG * Q_BLK
    mmap = _importance_map(ncmp)
    per_block = lambda n: pl.BlockSpec((n, Q_BLK), lambda b, i: (0, b * nq + i))
    seq_rows = lambda n: pl.BlockSpec((1, S, n), lambda b, i: (b, 0, 0))
    seq_cols = pl.BlockSpec((LANES, S), lambda b, i: (0, b))
    return pl.pallas_call(
        _nsa_prompt_kernel,
        grid=(B, nq),
        in_specs=[per_block(qt.shape[0]), per_block(LANES),
                  pl.BlockSpec((1, ncmp, LANES), lambda b, i: (b, 0, 0)),
                  pl.BlockSpec((1, LANES, ncmp), lambda b, i: (b, 0, 0)),
                  seq_rows(256), seq_cols, seq_rows(LANES), seq_cols,
                  pl.BlockSpec(mmap.shape, lambda b, i: (0, 0))],
        out_specs=pl.BlockSpec((1, Q_BLK, NSA_HEADS * HEAD_DIM), lambda b, i: (b, i, 0)),
        out_shape=jax.ShapeDtypeStruct((B, S, NSA_HEADS * HEAD_DIM), BF16),
        scratch_shapes=[pltpu.VMEM((NSA_GROUPS, 1, cols), F32), pltpu.VMEM((NSA_GROUPS, 1, cols), F32),
                        pltpu.VMEM((NSA_GROUPS, LANES, cols), F32)],
        compiler_params=_cparams(("arbitrary", "arbitrary")),
        name="nsa_prompt",
    )(qt, gates_t, kc, vct, ks_aug.reshape(B, S, 256), vs_t, kw.reshape(B, S, LANES), vw_t, mmap)


def _ret_kernel(rq_ref, rk_ref, rv_ref, r0_ref, dmask_ref, xi_ref, zeta_ref, dec_ref, o_ref, rout_ref, r_sc):
    @pl.when(pl.program_id(1) == 0)
    def _():
        r_sc[...] = r0_ref[0]

    C = rq_ref.shape[1]
    low = _lane((C, LANES)) < HEAD_DIM
    diag = lax.broadcasted_iota(jnp.int32, (LANES, LANES), 0) // HEAD_DIM == _lane((LANES, LANES)) // HEAD_DIM
    for pr in range(RET_HEADS // 2):
        sl = slice(pr * LANES, (pr + 1) * LANES)
        k = rk_ref[0, :, sl]
        qb, kb, vb = rq_ref[0, :, sl].astype(BF16), k.astype(BF16), rv_ref[0, :, sl].astype(BF16)
        zero = jnp.zeros_like(qb)
        s0 = _dot_nt(jnp.where(low, qb, zero), kb) * dmask_ref[2 * pr]
        s1 = _dot_nt(jnp.where(low, zero, qb), kb) * dmask_ref[2 * pr + 1]
        o = jnp.where(low, jnp.dot(s0.astype(BF16), vb, preferred_element_type=F32),
                      jnp.dot(s1.astype(BF16), vb, preferred_element_type=F32))
        r = r_sc[pr]
        o_ref[0, :, sl] = o + jnp.dot(qb, r.astype(BF16), preferred_element_type=F32) * xi_ref[:, sl]
        kz = (k * zeta_ref[:, sl]).astype(BF16)
        upd = lax.dot_general(kz, vb, (((0,), (0,)), ((), ())), preferred_element_type=F32)
        r_sc[pr] = dec_ref[:, sl] * r + jnp.where(diag, upd, 0.0)
    rout_ref[0] = r_sc[...]


def _ret_tables(c_true, c_pad):
    lg = jnp.log(1.0 - 2.0 ** (-5.0 - jnp.arange(RET_HEADS, dtype=F32)))
    idx = jnp.arange(c_pad, dtype=F32)
    diff = idx[:, None] - idx[None, :]
    dmask = jnp.where(diff >= 0, jnp.exp(jnp.maximum(diff, 0.0)[None] * lg[:, None, None]), 0.0)
    lanes = lambda a: jnp.repeat(a, HEAD_DIM, axis=-1)
    xi = lanes(jnp.exp((idx + 1.0)[:, None] * lg[None, :]))
    zeta = lanes(jnp.exp((c_true - 1.0 - idx)[:, None] * lg[None, :]))
    dec = lanes(jnp.exp(c_true * lg)[None, :])
    return dmask, xi, zeta, dec


def _retention(rq, rk, rv, r0, c_true, c_pad):
    B, T, _ = rq.shape
    dmask, xi, zeta, dec = _ret_tables(c_true, c_pad)
    row = pl.BlockSpec((1, c_pad, 256), lambda b, c: (b, c, 0))
    st = pl.BlockSpec((1, 2, LANES, LANES), lambda b, c: (b, 0, 0, 0))
    const = lambda a: pl.BlockSpec(a.shape, lambda b, c: (0,) * a.ndim)
    return pl.pallas_call(
        _ret_kernel,
        grid=(B, T // c_pad),
        in_specs=[row, row, row, st, const(dmask), const(xi), const(zeta), const(dec)],
        out_specs=[row, st],
        out_shape=[jax.ShapeDtypeStruct((B, T, 256), F32), jax.ShapeDtypeStruct((B, 2, LANES, LANES), F32)],
        scratch_shapes=[pltpu.VMEM((2, LANES, LANES), F32)],
        compiler_params=_cparams(("arbitrary", "arbitrary")),
        name="retention",
    )(rq, rk, rv, r0, dmask, xi, zeta, dec)


def _state_to_pairs(r):
    B = r.shape[0]
    r = r.reshape(B, 2, 2, HEAD_DIM, HEAD_DIM)
    eye = jnp.eye(2, dtype=r.dtype)
    return jnp.einsum('bphde,hk->bphdke', r, eye).reshape(B, 2, LANES, LANES)


def _pairs_to_state(rp):
    B = rp.shape[0]
    rp = rp.reshape(B, 2, 2, HEAD_DIM, 2, HEAD_DIM)
    return jnp.stack([rp[:, :, 0, :, 0, :], rp[:, :, 1, :, 1, :]], axis=2).reshape(B, RET_HEADS, HEAD_DIM, HEAD_DIM)


def _mem_kv_kernel(m_ref, g_ref, w_ref, kg_ref, kv_out):
    x = m_ref[...]
    ms = jnp.mean(x * x, axis=-1, keepdims=True)
    xn = (x * lax.rsqrt(ms + EPS) * g_ref[...]).astype(BF16)
    z = jnp.dot(xn, w_ref[...], preferred_element_type=F32)
    for c in range(2):
        kv_out[:, c * LANES:(c + 1) * LANES] = _seg_rms(z[:, c * LANES:(c + 1) * LANES], kg_ref[...])
    kv_out[:, 256:512] = z[:, 256:512]


def _mem_kv(mem, mem_norm_g, w_mem_kv, mem_k_norm_g):
    T, D = mem.shape
    w = w_mem_kv.astype(BF16)
    full = lambda a: pl.BlockSpec(a.shape, lambda i: (0, 0))
    g = mem_norm_g[None, :]
    kg = jnp.tile(mem_k_norm_g, 2)[None, :]
    return pl.pallas_call(
        _mem_kv_kernel,
        grid=(1,),
        in_specs=[full(mem), full(g), full(w), full(kg)],
        out_specs=pl.BlockSpec((T, 512), lambda i: (0, 0)),
        out_shape=jax.ShapeDtypeStruct((T, 512), F32),
        compiler_params=_cparams(("arbitrary",)),
        name="mem_kv",
    )(mem, g, w, kg)


def _mem_attn_kernel(mq_ref, mkv_ref, o_ref):
    q = mq_ref[0]
    rows = q.shape[0]
    low = _lane((rows, LANES)) < HEAD_DIM
    zero = jnp.zeros((rows, LANES), F32)
    for pr in range(MEM_HEADS // 2):
        sl = slice(pr * LANES, (pr + 1) * LANES)
        qp = q[:, sl].astype(F32)
        k = mkv_ref[0, :, sl].astype(BF16)
        v = mkv_ref[0, :, 256 + pr * LANES:256 + (pr + 1) * LANES].astype(BF16)
        outs = []
        for hh in range(2):
            qm = jnp.where(low, qp, zero) if hh == 0 else jnp.where(low, zero, qp)
            s = _dot_nt(qm.astype(BF16), k) * (HEAD_DIM ** -0.5)
            m = jnp.max(s, axis=-1, keepdims=True)
            p = jnp.exp(s - m)
            o = jnp.dot(p.astype(BF16), v, preferred_element_type=F32)
            outs.append(o * (1.0 / jnp.sum(p, axis=-1, keepdims=True)))
        o_ref[0, :, sl] = jnp.where(low, outs[0], outs[1]).astype(o_ref.dtype)


def _mem_attn(mq, mkv, tm):
    B, R, _ = mq.shape
    return pl.pallas_call(
        _mem_attn_kernel,
        grid=(B, R // tm),
        in_specs=[pl.BlockSpec((1, tm, 256), lambda b, i: (b, i, 0)),
                  pl.BlockSpec((1,) + mkv.shape[1:], lambda b, i: (b, 0, 0))],
        out_specs=pl.BlockSpec((1, tm, 256), lambda b, i: (b, i, 0)),
        out_shape=jax.ShapeDtypeStruct((B, R, 256), BF16),
        compiler_params=_cparams(("arbitrary", "arbitrary")),
        name="mem_attn",
    )(mq, mkv)


TOK_ROWS = 8


def _load_token_tiles(ref, n):
    return jnp.concatenate([ref[pl.ds(c, n, stride=TOK_ROWS), :] for c in range(TOK_ROWS)], axis=1)


def _store_token_tiles(ref, x):
    n = x.shape[0]
    for c in range(TOK_ROWS):
        ref[pl.ds(c, n, stride=TOK_ROWS), :] = x[:, c * LANES:(c + 1) * LANES]


def _mix_kernel(x_ref, onsa_ref, oret_ref, rg_ref, omem_ref, wout_ref, rgain_ref, g2_ref,
                wr_hi_ref, wr_lo_ref, br_ref, x1_out, h_out, topi_out, topg_out):
    parts = [onsa_ref[...]]
    for c in range(2):
        sl = slice(c * LANES, (c + 1) * LANES)
        parts.append((_seg_rms(oret_ref[:, sl], rgain_ref[...]) * jax.nn.silu(rg_ref[:, sl])).astype(BF16))
    parts.append(omem_ref[...])
    mix = jnp.concatenate(parts, axis=1)
    x1 = x_ref[...] + jnp.dot(mix, wout_ref[...], preferred_element_type=F32)
    x1_out[...] = x1
    ms = jnp.mean(x1 * x1, axis=-1, keepdims=True)
    h = x1 * lax.rsqrt(ms + EPS) * g2_ref[...]
    _store_token_tiles(h_out, h)
    h_hi, h_lo, _ = _split3(h)
    logits = (jnp.dot(h_hi, wr_hi_ref[...], preferred_element_type=F32)
              + jnp.dot(h_hi, wr_lo_ref[...], preferred_element_type=F32)
              + jnp.dot(h_lo, wr_hi_ref[...], preferred_element_type=F32)) + br_ref[...]
    lane = _lane(logits.shape)
    key = logits
    topi = jnp.zeros(logits.shape, jnp.int32)
    topv = jnp.zeros(logits.shape, F32)
    for r in range(TOP_K):
        m = jnp.max(key, axis=-1, keepdims=True)
        idx = jnp.min(jnp.where(key == m, lane, LANES), axis=-1, keepdims=True)
        if r == 0:
            m0 = m
        topi = jnp.where(lane == r, idx, topi)
        topv = jnp.where(lane == r, jnp.exp(m - m0), topv)
        key = jnp.where(lane == idx, -jnp.inf, key)
    topi_out[...] = topi
    topg_out[...] = topv * (1.0 / jnp.sum(topv, axis=-1, keepdims=True))


def _mix(x, onsa, oret, rg, omem, mw, tm):
    T, D = x.shape
    row = lambda n: pl.BlockSpec((tm, n), lambda i: (i, 0))
    full = lambda a: pl.BlockSpec(a.shape, lambda i: (0, 0))
    names = ('wout', 'rgain', 'g2', 'wr_hi', 'wr_lo', 'br')
    return pl.pallas_call(
        _mix_kernel,
        grid=(T // tm,),
        in_specs=[row(D), row(512), row(256), row(256), row(256)] + [full(mw[n]) for n in names],
        out_specs=[row(D), pl.BlockSpec((tm * TOK_ROWS, LANES), lambda i: (i, 0)), row(LANES), row(LANES)],
        out_shape=[jax.ShapeDtypeStruct((T, D), F32), jax.ShapeDtypeStruct((T * TOK_ROWS, LANES), F32),
                   jax.ShapeDtypeStruct((T, LANES), jnp.int32), jax.ShapeDtypeStruct((T, LANES), F32)],
        compiler_params=_cparams(("arbitrary",)),
        name="mix",
    )(x, onsa, oret, rg, omem, *[mw[n] for n in names])


def _prep_mix_weights(w_out, ret_norm_g, norm2_g, w_router, b_router):
    wr = jnp.pad(w_router, ((0, 0), (0, LANES - N_EXPERTS)))
    wr_hi = wr.astype(BF16)
    wr_lo = (wr - wr_hi.astype(F32)).astype(BF16)
    br = jnp.concatenate([b_router.astype(F32), jnp.full((LANES - N_EXPERTS,), NEG, F32)])[None, :]
    return dict(wout=w_out.astype(BF16), rgain=jnp.tile(ret_norm_g, 2)[None, :], g2=norm2_g[None, :],
                wr_hi=wr_hi, wr_lo=wr_lo, br=br)


MOE_TOK_TILE = 128


def _row_copy(src, i, dst, j, sem):
    return pltpu.make_async_copy(src.at[pl.ds(pl.multiple_of(i, TOK_ROWS), TOK_ROWS)],
                                 dst.at[pl.ds(pl.multiple_of(j, TOK_ROWS), TOK_ROWS)], sem)


def _dispatch_kernel(dest_ref, h_ref, xb_in, xb_out, sem):
    del xb_in
    tile = dest_ref.shape[0] // TOP_K

    def start(t, c):
        for k in range(TOP_K):
            _row_copy(h_ref, t * TOK_ROWS, xb_out, dest_ref[t * TOP_K + k], sem).start()
        return c

    def wait(t, c):
        for k in range(TOP_K):
            _row_copy(h_ref, 0, xb_out, 0, sem).wait()
        return c

    lax.fori_loop(0, tile, start, 0)
    lax.fori_loop(0, tile, wait, 0)


def _dispatch(h, dest, xb):
    T = h.shape[0] // TOK_ROWS
    hbm = pl.BlockSpec(memory_space=pl.ANY)
    n = MOE_TOK_TILE * TOP_K
    return pl.pallas_call(
        _dispatch_kernel,
        grid=(T // MOE_TOK_TILE,),
        in_specs=[pl.BlockSpec((n,), lambda i: (i,), memory_space=pltpu.SMEM),
                  pl.BlockSpec((MOE_TOK_TILE * TOK_ROWS, LANES), lambda i: (i, 0)), hbm],
        out_specs=hbm,
        out_shape=jax.ShapeDtypeStruct(xb.shape, xb.dtype),
        scratch_shapes=[pltpu.SemaphoreType.DMA(())],
        input_output_aliases={2: 0},
        compiler_params=_cparams(("arbitrary",)),
        name="moe_dispatch",
    )(dest.reshape(-1), h, xb)


def _moe_kernel(be_ref, nb_ref, x_ref, wup_ref, bup_ref, wdn_ref, bdn_ref, y_ref, wup_bf, wdn_bf):
    j = pl.program_id(0)

    @pl.when(j < nb_ref[0])
    def _():
        @pl.when(jnp.logical_or(j == 0, be_ref[j] != be_ref[jnp.maximum(j - 1, 0)]))
        def _():
            wup_bf[...] = wup_ref[0].astype(BF16)
            wdn_bf[...] = wdn_ref[0].astype(BF16)

        x = _load_token_tiles(x_ref, MOE_ROWS).astype(BF16)
        up = jnp.dot(x, wup_bf[...], preferred_element_type=F32) + bup_ref[0]
        x_glu = jnp.minimum(up[:, :D_FF], SWIGLU_LIMIT)
        x_lin = jnp.clip(up[:, D_FF:], -SWIGLU_LIMIT, SWIGLU_LIMIT)
        act = x_glu * jax.nn.sigmoid(SWIGLU_ALPHA * x_glu) * (x_lin + 1.0)
        _store_token_tiles(y_ref, jnp.dot(act.astype(BF16), wdn_bf[...], preferred_element_type=F32) + bdn_ref[0])

    @pl.when(j >= nb_ref[0])
    def _():
        y_ref[...] = jnp.zeros(y_ref.shape, y_ref.dtype)


def _moe_experts(xb, blk_expert, n_used, w_up, b_up, w_down, b_down):
    D = w_up.shape[1]
    blk = pl.BlockSpec((MOE_ROWS * TOK_ROWS, LANES), lambda j, be, nb: (j, 0))
    grid_spec = pltpu.PrefetchScalarGridSpec(
        num_scalar_prefetch=2,
        grid=(xb.shape[0] // (MOE_ROWS * TOK_ROWS),),
        in_specs=[blk,
                  pl.BlockSpec((1, D, 2 * D_FF), lambda j, be, nb: (be[j], 0, 0)),
                  pl.BlockSpec((1, 1, 2 * D_FF), lambda j, be, nb: (be[j], 0, 0)),
                  pl.BlockSpec((1, D_FF, D), lambda j, be, nb: (be[j], 0, 0)),
                  pl.BlockSpec((1, 1, D), lambda j, be, nb: (be[j], 0, 0))],
        out_specs=blk,
        scratch_shapes=[pltpu.VMEM((D, 2 * D_FF), BF16), pltpu.VMEM((D_FF, D), BF16)],
    )
    return pl.pallas_call(
        _moe_kernel,
        grid_spec=grid_spec,
        out_shape=jax.ShapeDtypeStruct(xb.shape, F32),
        compiler_params=_cparams(("arbitrary",)),
        name="moe_experts",
    )(blk_expert, n_used, xb, w_up, b_up[:, None, :], w_down, b_down[:, None, :])


def _combine_kernel(dest_ref, g_ref, x1_ref, yb_hbm, out_ref, buf, sem):
    tile = x1_ref.shape[0]

    def start(t, c):
        for k in range(TOP_K):
            _row_copy(yb_hbm, dest_ref[t * TOP_K + k], buf.at[k], t * TOK_ROWS, sem).start()
        return c

    def wait(t, c):
        for k in range(TOP_K):
            _row_copy(yb_hbm, 0, buf.at[k], 0, sem).wait()
        return c

    lax.fori_loop(0, tile, start, 0)
    lax.fori_loop(0, tile, wait, 0)
    acc = x1_ref[...]
    for k in range(TOP_K):
        acc = acc + g_ref[:, k:k + 1] * _load_token_tiles(buf.at[k], tile)
    out_ref[...] = acc


def _combine(x1, topg, dest, yb):
    T, D = x1.shape
    tile = MOE_TOK_TILE
    row = lambda n: pl.BlockSpec((tile, n), lambda i: (i, 0))
    return pl.pallas_call(
        _combine_kernel,
        grid=(T // tile,),
        in_specs=[pl.BlockSpec((tile * TOP_K,), lambda i: (i,), memory_space=pltpu.SMEM),
                  row(LANES), row(D), pl.BlockSpec(memory_space=pl.ANY)],
        out_specs=row(D),
        out_shape=jax.ShapeDtypeStruct((T, D), F32),
        scratch_shapes=[pltpu.VMEM((TOP_K, tile * TOK_ROWS, LANES), F32), pltpu.SemaphoreType.DMA(())],
        compiler_params=_cparams(("arbitrary",)),
        name="moe_combine",
    )(dest.reshape(-1), topg, x1, yb)


def _route(topi):
    T = topi.shape[0]
    onehot = (topi[:, :, None] == jnp.arange(N_EXPERTS, dtype=jnp.int32)).astype(jnp.int32).sum(axis=1)
    rank = jnp.cumsum(onehot, axis=0) - onehot
    counts = onehot.sum(axis=0)
    padded = (counts + MOE_ROWS - 1) // MOE_ROWS * MOE_ROWS
    pad_end = jnp.cumsum(padded)
    pad_start = pad_end - padded
    dest = pad_start[topi] + jnp.take_along_axis(rank, topi, axis=1)
    n_blk = (T * TOP_K + N_EXPERTS * (MOE_ROWS - 1) + MOE_ROWS - 1) // MOE_ROWS
    blk_start = jnp.arange(n_blk, dtype=jnp.int32) * MOE_ROWS
    blk_expert = jnp.minimum((pad_end[None, :] <= blk_start[:, None]).sum(axis=1), N_EXPERTS - 1)
    return ((dest * TOK_ROWS).astype(jnp.int32), blk_expert.astype(jnp.int32),
            (pad_end[-1:] // MOE_ROWS).astype(jnp.int32), n_blk)


SAMPLE_ROWS = 8


def _token_mixers(x, pw, cos, sin, tm, key_major):
    names = ('q', 'kvc', 'kvs', 'kvw', 'ks_bf', 'kw_bf', 'gates', 'rq', 'rk', 'rv', 'rg', 'mq', 'vs_t', 'vw_t')
    return dict(zip(names, _project(x, pw, cos, sin, tm, key_major)))


def kernel(x_prompt, x_sample, mem_prompt, cache_cmp_kv, cache_slc_kv, cache_win_kv, state_ret, cache_mem_kv, page_table, norm1_g, w_in, q_norm_g, k_norm_cmp_g, k_norm_slc_g, k_norm_win_g, cmp_pe_k, cmp_w1_k, cmp_w2_k, cmp_pe_v, cmp_w1_v, cmp_w2_v, ret_norm_g, mem_norm_g, w_mem_kv, mem_q_norm_g, mem_k_norm_g, w_out, norm2_g, w_router, b_router, w_up, b_up, w_down, b_down):
    B, S, D = x_prompt.shape
    DB, QS, _ = x_sample.shape
    n_mem = mem_prompt.shape[1]
    n_pages, page = page_table.shape[1], cache_cmp_kv.shape[2]
    past = n_pages * page
    NQ = SAMPLE_ROWS
    G, HD = NSA_GROUPS, HEAD_DIM
    win_rows = min(WINDOW, S)
    TP, TS = B * S, DB * NQ

    cos_p, sin_p = _rope_tables(jnp.arange(S, dtype=jnp.int32))
    cos_s, sin_s = _rope_tables(past + jnp.arange(NQ, dtype=jnp.int32))
    cos_s, sin_s = jnp.tile(cos_s, (DB, 1)), jnp.tile(sin_s, (DB, 1))

    xp = x_prompt.reshape(TP, D)
    xs = jnp.pad(x_sample, ((0, 0), (0, NQ - QS), (0, 0))).reshape(TS, D)
    unpad = lambda a: a.reshape(DB, NQ, -1)[:, :QS]
    outs = [[] for _ in range(9)]
    for l in range(w_in.shape[0]):
        pw = _prep_proj_weights(norm1_g[l], w_in[l], q_norm_g[l], k_norm_slc_g[l], k_norm_win_g[l], mem_q_norm_g[l])
        cw = _prep_compress_weights(cmp_pe_k[l], cmp_w1_k[l], cmp_w2_k[l], cmp_pe_v[l], cmp_w1_v[l], cmp_w2_v[l],
                                    k_norm_cmp_g[l])
        mw = _prep_mix_weights(w_out[l], ret_norm_g[l], norm2_g[l], w_router[l], b_router[l])

        t = _token_mixers(xp, pw, cos_p, sin_p, 512, True)
        kc, vct = _compress(t['kvc'].reshape(B, S, 256), cw)
        o_nsa = _nsa_prompt(B, t['q'], t['gates'], kc, vct, t['ks_bf'], t['vs_t'], t['kw_bf'], t['vw_t'])
        o_ret, r_p = _retention(t['rq'].reshape(B, S, 256), t['rk'].reshape(B, S, 256), t['rv'].reshape(B, S, 256),
                                jnp.zeros((B, 2, LANES, LANES), F32), RET_CHUNK, RET_CHUNK)
        mkv = _mem_kv(mem_prompt.reshape(B * n_mem, D), mem_norm_g[l], w_mem_kv[l], mem_k_norm_g[l])
        o_mem = _mem_attn(t['mq'].reshape(B, S, 256), mkv.reshape(B, n_mem, 512), 512)
        x1_p, h_p, topi_p, topg_p = _mix(xp, o_nsa.reshape(TP, -1), o_ret.reshape(TP, 256), t['rg'],
                                         o_mem.reshape(TP, 256), mw, 512)
        outs[0].append(t['kvc'].reshape(B, S, 2, G, HD))
        outs[1].append(t['kvs'].reshape(B, S, 2, G, HD))
        outs[2].append(t['kvw'].reshape(B, S, 2, G, HD)[:, S - win_rows:])
        outs[3].append(_pairs_to_state(r_p))
        outs[4].append(mkv.reshape(B, n_mem, 2, MEM_HEADS, HD))

        t = _token_mixers(xs, pw, cos_s, sin_s, TS, False)
        win = cache_win_kv[l]
        o_nsa = _nsa_sample(t['q'].astype(F32).reshape(DB, NQ, -1), t['gates'].reshape(DB, NQ, LANES),
                            t['kvs'].reshape(DB, NQ, 256), t['kvw'].reshape(DB, NQ, 256),
                            win.reshape(DB, win.shape[1], 256),
                            cache_cmp_kv[l].reshape(-1, page, 256), cache_slc_kv[l].reshape(-1, page, 256),
                            page_table, cw, QS)
        o_ret, r_s = _retention(t['rq'].reshape(DB, NQ, 256), t['rk'].reshape(DB, NQ, 256),
                                t['rv'].reshape(DB, NQ, 256), _state_to_pairs(state_ret[l].astype(F32)), QS, NQ)
        o_mem = _mem_attn(t['mq'].reshape(DB, NQ, 256), cache_mem_kv[l].reshape(DB, n_mem, 512), NQ)
        x1_s, h_s, topi_s, topg_s = _mix(xs, o_nsa.reshape(TS, -1).astype(BF16), o_ret.reshape(TS, 256), t['rg'],
                                         o_mem.reshape(TS, 256), mw, TS)
        kv5 = lambda a: unpad(a).reshape(DB, QS, 2, G, HD)
        outs[5].append(kv5(t['kvc']))
        outs[6].append(kv5(t['kvs']))
        outs[7].append(jnp.concatenate([win, kv5(t['kvw'])], axis=1)[:, QS:])
        outs[8].append(_pairs_to_state(r_s))

        valid = lambda a: unpad(a).reshape(DB * QS, -1)
        dest, blk_expert, n_used, n_blk = _route(jnp.concatenate([topi_p, valid(topi_s)], axis=0)[:, :TOP_K])
        xb = jnp.zeros((n_blk * MOE_ROWS * TOK_ROWS, LANES), F32)
        xb = _dispatch(h_p, dest[:TP], xb)
        h_s_valid = h_s.reshape(DB, NQ, TOK_ROWS, LANES)[:, :QS].reshape(DB * QS * TOK_ROWS, LANES)
        xb = _dispatch(h_s_valid, dest[TP:], xb)
        yb = _moe_experts(xb, blk_expert, n_used, w_up[l], b_up[l], w_down[l], b_down[l])
        xp = _combine(x1_p, topg_p, dest[:TP], yb)
        xs_new = _combine(valid(x1_s), valid(topg_s), dest[TP:], yb)
        xs = jnp.pad(xs_new.reshape(DB, QS, D), ((0, 0), (0, NQ - QS), (0, 0))).reshape(TS, D)
    y_sample = xs.reshape(DB, NQ, D)[:, :QS]
    return (xp.reshape(B, S, D), y_sample) + tuple(jnp.stack(o) for o in outs)
```

```python
import functools

import jax
import jax.numpy as jnp
from jax import lax
from jax.experimental import pallas as pl
from jax.experimental.pallas import tpu as pltpu

F32 = jnp.float32
BF16 = jnp.bfloat16

HEAD_DIM = 64
NSA_HEADS = 8
NSA_GROUPS = 2
NSA_HPG = NSA_HEADS // NSA_GROUPS
RET_HEADS = 4
MEM_HEADS = 4
CMP_BLK = 32
CMP_STRIDE = 16
CMP_HID = 256
SLC_BLK = 64
N_SEL = 16
WINDOW = 512
Q_BLK = 128
RET_CHUNK = 128
N_EXPERTS = 32
TOP_K = 4
D_FF = 1024
SWIGLU_LIMIT = 7.0
SWIGLU_ALPHA = 1.702
EPS = 1e-6
NEG = -1e30
BIG = 1e9
ROPE_BASE = 10000.0
SLC_RATIO = SLC_BLK // CMP_STRIDE
CMP_OVL = CMP_BLK // CMP_STRIDE - 1

LANES = 128
KEY_TILE = 512
MOE_ROWS = 512
VMEM_LIMIT = 56 * 1024 * 1024

C_Q = 0
C_KVC = C_Q + NSA_HEADS * LANES
C_KVS = C_KVC + 256
C_KVW = C_KVS + 256
C_GATE = C_KVW + 256
C_RQ = C_GATE + LANES
C_RK = C_RQ + 256
C_RV = C_RK + 256
C_RG = C_RV + 256
C_MQ = C_RG + 256
C_END = C_MQ + 256


def _cparams(sem):
    return pltpu.CompilerParams(dimension_semantics=sem, vmem_limit_bytes=VMEM_LIMIT)


def _lane(shape):
    return lax.broadcasted_iota(jnp.int32, shape, len(shape) - 1)


def _seg_rms(x, gain):
    lo = _lane(x.shape) < HEAD_DIM
    x2 = x * x
    s_lo = jnp.sum(jnp.where(lo, x2, 0.0), axis=-1, keepdims=True)
    s_hi = jnp.sum(jnp.where(lo, 0.0, x2), axis=-1, keepdims=True)
    ms = jnp.where(lo, s_lo, s_hi) * (1.0 / HEAD_DIM)
    return x * lax.rsqrt(ms + EPS) * gain


def _swap_halves(x):
    first = (_lane(x.shape) & (HEAD_DIM // 2)) == 0
    return jnp.where(first, pltpu.roll(x, LANES - HEAD_DIM // 2, 1), pltpu.roll(x, HEAD_DIM // 2, 1))


def _proj_kernel(x_ref, g1_ref, w_ref, cos_ref, sin_ref, qg_ref, ksg_ref, kwg_ref, mqg_ref,
                 q_out, kvc_out, kvs_out, kvw_out, ks_bf, kw_bf, gate_out,
                 rq_out, rk_out, rv_out, rg_out, mq_out, *extra, seq_tiles):
    x = x_ref[...]
    tm = x.shape[0]
    ms = jnp.mean(x * x, axis=-1, keepdims=True)
    xn = (x * lax.rsqrt(ms + EPS) * g1_ref[...]).astype(BF16)
    z = jnp.dot(xn, w_ref[...], preferred_element_type=F32)
    for i in range(NSA_HEADS):
        zq = z[:, C_Q + i * LANES:C_Q + (i + 1) * LANES]
        msq = jnp.sum(zq * zq, axis=-1, keepdims=True) * (1.0 / HEAD_DIM)
        qn = zq * lax.rsqrt(msq + EPS) * qg_ref[:, i * LANES:(i + 1) * LANES] * (HEAD_DIM ** -0.5)
        if seq_tiles:
            q_out[i * LANES:(i + 1) * LANES, :] = qn.T.astype(BF16)
        else:
            q_out[:, i * LANES:(i + 1) * LANES] = qn.astype(BF16)
    kvc_out[...] = z[:, C_KVC:C_KVC + 256]
    ks = _seg_rms(z[:, C_KVS:C_KVS + LANES], ksg_ref[...])
    vs = z[:, C_KVS + LANES:C_KVS + 256]
    kvs_out[:, 0:LANES] = ks
    kvs_out[:, LANES:256] = vs
    kw = _seg_rms(z[:, C_KVW:C_KVW + LANES], kwg_ref[...])
    vw = z[:, C_KVW + LANES:C_KVW + 256]
    kvw_out[:, 0:LANES] = kw
    kvw_out[:, LANES:256] = vw
    gates = jax.nn.sigmoid(z[:, C_GATE:C_GATE + LANES])
    ks_bf[:, 0:LANES] = ks.astype(BF16)
    if seq_tiles:
        vs_t, vw_t = extra
        pos = (pl.program_id(0) % seq_tiles) * tm + lax.broadcasted_iota(jnp.int32, (tm, LANES), 0)
        ks_bf[:, LANES:256] = jnp.where(_lane((tm, LANES)) == pos // SLC_BLK, 1.0, 0.0).astype(BF16)
        kw_bf[...] = kw.astype(BF16)
        vs_t[...] = vs.T.astype(BF16)
        vw_t[...] = vw.T.astype(BF16)
        gate_out[...] = gates.T
    else:
        ks_bf[:, LANES:256] = vs.astype(BF16)
        kw_bf[:, 0:LANES] = kw.astype(BF16)
        kw_bf[:, LANES:256] = vw.astype(BF16)
        gate_out[...] = gates
    for c in range(2):
        sl = slice(c * LANES, (c + 1) * LANES)
        cos = cos_ref[:, sl]
        sin = sin_ref[:, sl]
        rq = z[:, C_RQ + c * LANES:C_RQ + (c + 1) * LANES]
        rk = z[:, C_RK + c * LANES:C_RK + (c + 1) * LANES]
        rq_out[:, sl] = rq * cos + _swap_halves(rq) * sin
        rk_out[:, sl] = (rk * cos + _swap_halves(rk) * sin) * (HEAD_DIM ** -0.5)
        mq = z[:, C_MQ + c * LANES:C_MQ + (c + 1) * LANES]
        mq_out[:, sl] = _seg_rms(mq, mqg_ref[...]).astype(BF16)
    rv_out[...] = z[:, C_RV:C_RV + 256]
    rg_out[...] = z[:, C_RG:C_RG + 256]


def _project(x, pw, cos, sin, tm, key_major):
    T, D = x.shape
    nt = cos.shape[0] // tm
    row = lambda n: (pl.BlockSpec((tm, n), lambda i: (i, 0)), (T, n))
    col = lambda n: (pl.BlockSpec((n, tm), lambda i: (0, i)), (n, T))
    full = lambda a: pl.BlockSpec(a.shape, lambda i: (0, 0))
    tab = pl.BlockSpec((tm, 256), lambda i: (i % nt, 0))
    tok = col if key_major else row
    outs = [
        (tok(NSA_HEADS * LANES), BF16),
        (row(256), F32), (row(256), F32), (row(256), F32),
        (row(256), BF16), (row(LANES if key_major else 256), BF16),
        (tok(LANES), F32),
        (row(256), F32), (row(256), F32), (row(256), F32), (row(256), F32),
        (row(256), BF16),
    ]
    if key_major:
        outs += [(col(LANES), BF16), (col(LANES), BF16)]
    return pl.pallas_call(
        functools.partial(_proj_kernel, seq_tiles=nt if key_major else 0),
        grid=(T // tm,),
        in_specs=[row(D)[0], full(pw['g1']), full(pw['w']), tab, tab,
                  full(pw['qg']), full(pw['ksg']), full(pw['kwg']), full(pw['mqg'])],
        out_specs=[spec for (spec, _), _ in outs],
        out_shape=[jax.ShapeDtypeStruct(shape, dt) for (_, shape), dt in outs],
        compiler_params=_cparams(("arbitrary",)),
        name="proj",
    )(x, pw['g1'], pw['w'], cos, sin, pw['qg'], pw['ksg'], pw['kwg'], pw['mqg'])


def _prep_proj_weights(norm1_g, w_in, q_norm_g, k_norm_slc_g, k_norm_win_g, mem_q_norm_g):
    D = w_in.shape[0]
    sizes = (512, 256, 256, 256, 24, 256, 256, 256, 256, 256)
    parts, off = [], 0
    for n in sizes:
        parts.append(w_in[:, off:off + n])
        off += n
    wq, wkvc, wkvs, wkvw, wg, wrq, wrk, wrv, wrg, wmq = parts
    zero = jnp.zeros((D, HEAD_DIM), w_in.dtype)
    qcols, qg = [], []
    gz = jnp.zeros((HEAD_DIM,), F32)
    for i in range(NSA_HEADS):
        wh = wq[:, i * HEAD_DIM:(i + 1) * HEAD_DIM]
        if i < NSA_HPG:
            qcols += [wh, zero]
            qg += [q_norm_g, gz]
        else:
            qcols += [zero, wh]
            qg += [gz, q_norm_g]
    wgp = jnp.pad(wg, ((0, 0), (0, LANES - wg.shape[1])))
    w = jnp.concatenate(qcols + [wkvc, wkvs, wkvw, wgp, wrq, wrk, wrv, wrg, wmq], axis=1).astype(BF16)
    two = lambda g: jnp.tile(g, 2)[None, :]
    return dict(g1=norm1_g[None, :], w=w, qg=jnp.concatenate(qg)[None, :],
                ksg=two(k_norm_slc_g), kwg=two(k_norm_win_g), mqg=two(mem_q_norm_g))


def _rope_tables(pos):
    half = HEAD_DIM // 2
    inv = ROPE_BASE ** (-jnp.arange(half, dtype=F32) / half)
    ang = pos.astype(F32)[:, None] * inv[None, :]
    cos, sin = jnp.cos(ang), jnp.sin(ang)
    cos = jnp.tile(jnp.concatenate([cos, cos], axis=1), (1, RET_HEADS))
    sin = jnp.tile(jnp.concatenate([-sin, sin], axis=1), (1, RET_HEADS))
    return cos, sin


def _compress_half(src_ref, pe_ref, wa_ref, wb_ref, w2_ref, nchunks):
    a_parts, b_parts = [], []
    for r in range(CMP_STRIDE):
        xr = src_ref[pl.ds(r, nchunks, stride=CMP_STRIDE), :]
        a_parts.append((xr + pe_ref[r:r + 1, :]).astype(BF16))
        b_parts.append((xr + pe_ref[CMP_STRIDE + r:CMP_STRIDE + r + 1, :]).astype(BF16))
    ha = jnp.dot(jnp.concatenate(a_parts, axis=1), wa_ref[...], preferred_element_type=F32)
    hb = jnp.dot(jnp.concatenate(b_parts, axis=1), wb_ref[...], preferred_element_type=F32)
    h = ha + pltpu.roll(hb, nchunks - 1, 0)
    act = jax.nn.gelu(h).astype(BF16)
    return jnp.dot(act, w2_ref[...], preferred_element_type=F32)


def _compress_kernel(kv_ref, pe_ref, wa_ref, wb_ref, w2_ref, kg_ref, kc_out, vct_out):
    nchunks = kc_out.shape[1]
    y = _compress_half(kv_ref.at[0], pe_ref.at[0], wa_ref.at[0], wb_ref.at[0], w2_ref.at[0], nchunks)

    @pl.when(pl.program_id(1) == 0)
    def _():
        kc_out[0] = _seg_rms(y, kg_ref[...]).astype(BF16)

    @pl.when(pl.program_id(1) == 1)
    def _():
        vct_out[0] = y.T.astype(BF16)


def _compress(kvc, cw):
    B, T, _ = kvc.shape
    nchunks = T // CMP_STRIDE
    per_kv = lambda a: pl.BlockSpec((1,) + a.shape[1:], lambda b, j: (j,) + (0,) * (a.ndim - 1))
    return pl.pallas_call(
        _compress_kernel,
        grid=(B, 2),
        in_specs=[pl.BlockSpec((1, T, LANES), lambda b, j: (b, 0, j)),
                  per_kv(cw['pe']), per_kv(cw['wa']), per_kv(cw['wb']), per_kv(cw['w2']),
                  pl.BlockSpec((1, LANES), lambda b, j: (0, 0))],
        out_specs=[pl.BlockSpec((1, nchunks, LANES), lambda b, j: (b, 0, 0)),
                   pl.BlockSpec((1, LANES, nchunks), lambda b, j: (b, 0, 0))],
        out_shape=[jax.ShapeDtypeStruct((B, nchunks, LANES), BF16),
                   jax.ShapeDtypeStruct((B, LANES, nchunks), BF16)],
        compiler_params=_cparams(("arbitrary", "arbitrary")),
        name="compress",
    )(kvc, cw['pe'], cw['wa'], cw['wb'], cw['w2'], cw['kg'])


def _prep_compress_weights(pe_k, w1_k, w2_k, pe_v, w1_v, w2_v, k_norm_g):
    eye = jnp.eye(NSA_GROUPS, dtype=F32)

    def one(pe, w1, w2):
        w1r = w1.reshape(CMP_BLK, HEAD_DIM, CMP_HID)
        wf = jnp.einsum('gh,rdc->rgdhc', eye, w1r).reshape(CMP_BLK * LANES, NSA_GROUPS * CMP_HID)
        w2f = jnp.einsum('gh,cd->gchd', eye, w2).reshape(NSA_GROUPS * CMP_HID, LANES)
        half = CMP_STRIDE * LANES
        return jnp.tile(pe, (1, NSA_GROUPS)), wf[:half].astype(BF16), wf[half:].astype(BF16), w2f.astype(BF16)

    k, v = one(pe_k, w1_k, w2_k), one(pe_v, w1_v, w2_v)
    st = lambda i: jnp.stack([k[i], v[i]])
    return dict(pe=st(0), wa=st(1), wb=st(2), w2=st(3), kg=jnp.tile(k_norm_g, 2)[None, :])


def _dot_nt(a, b):
    return lax.dot_general(a, b, (((1,), (1,)), ((), ())), preferred_element_type=F32)


def _split3(x):
    hi = x.astype(BF16)
    r = x - hi.astype(F32)
    mid = r.astype(BF16)
    lo = (r - mid.astype(F32)).astype(BF16)
    return hi, mid, lo


def _select_bias(ps, cur, rounds, axis, cur_in_range=True):
    j = lax.broadcasted_iota(jnp.int32, ps.shape, axis)
    last = cur if cur_in_range else cur - 1
    key = jnp.where(j <= last, ps, NEG)
    for forced in (0, last, cur - 1):
        key = jnp.where(j == forced, BIG, key)
    bias = jnp.full(ps.shape, NEG, F32)
    for _ in range(rounds):
        m = jnp.max(key, axis=axis, keepdims=True)
        idx = jnp.min(jnp.where(key == m, j, LANES), axis=axis, keepdims=True)
        pick = j == idx
        bias = jnp.where(pick, 0.0, bias)
        key = jnp.where(pick, -jnp.inf, key)
    return bias


def _cmp_branch(qg, qpos, kc, vc):
    s = _dot_nt(qg, kc)
    cend = _lane((1, kc.shape[0])) * CMP_STRIDE + (CMP_BLK - 1)
    s = jnp.where(cend <= qpos, s, NEG)
    m = jnp.max(s, axis=-1, keepdims=True)
    p = jnp.exp(s - m)
    p = p * (1.0 / jnp.sum(p, axis=-1, keepdims=True))
    p = jnp.where(qpos >= CMP_BLK - 1, p, 0.0)
    return p, jnp.dot(p.astype(BF16), vc, preferred_element_type=F32)


def _sel_tile(qaug, k_tile, v_tile, blk0, mask, m_sc, l_sc, acc_sc):
    n = k_tile.shape[0]
    blk = blk0 + lax.broadcasted_iota(jnp.int32, (n, LANES), 0) // SLC_BLK
    onehot = jnp.where(_lane((n, LANES)) == blk, 1.0, 0.0).astype(BF16)
    s = _dot_nt(qaug, jnp.concatenate([k_tile, onehot], axis=1))
    if mask is not None:
        s = jnp.where(mask, s, NEG)
    _online_update(s, v_tile, m_sc, l_sc, acc_sc)


def _online_update(s, v_tile, m_sc, l_sc, acc_sc):
    m_old = m_sc[...]
    m_new = jnp.maximum(m_old, jnp.max(s, axis=-1, keepdims=True))
    alpha = jnp.exp(m_old - m_new)
    p = jnp.exp(s - m_new)
    l_sc[...] = alpha * l_sc[...] + jnp.sum(p, axis=-1, keepdims=True)
    acc_sc[...] = alpha * acc_sc[...] + jnp.dot(p.astype(BF16), v_tile, preferred_element_type=F32)
    m_sc[...] = m_new


def _gate_heads(gates, g, oc, osel, ow, nq):
    heads = []
    for h in range(NSA_HPG):
        rs = slice(h * nq, (h + 1) * nq)
        c0 = (NSA_HPG * g + h) * 3
        heads.append(gates[:, c0:c0 + 1] * oc[rs] + gates[:, c0 + 1:c0 + 2] * osel[rs]
                     + gates[:, c0 + 2:c0 + 3] * ow[rs])
    return heads


def _store_heads(heads, o_ref):
    low = _lane(heads[0].shape) < HEAD_DIM
    for pr in range(NSA_HEADS // 2):
        even, odd = heads[2 * pr], heads[2 * pr + 1]
        if pr < NSA_GROUPS:
            odd = pltpu.roll(odd, HEAD_DIM, 1)
        else:
            even = pltpu.roll(even, HEAD_DIM, 1)
        o_ref[0, :, pr * LANES:(pr + 1) * LANES] = jnp.where(low, even, odd).astype(o_ref.dtype)


def _nsa_decode_kernel(pt_ref, q_ref, gate_ref, ksn_ref, kwn_ref, win_ref, pool_c, pool_s,
                       pe_ref, wa_ref, wb_ref, w2_ref, kg_ref, mmap_ref, hot_ref, o_ref,
                       cbuf, sbuf, xrow, sems, *, page, n_new):
    b = pl.program_id(0)
    past = sbuf.shape[1]
    npages = past // page
    nq = q_ref.shape[1]
    dot = functools.partial(jnp.dot, preferred_element_type=F32)

    def page_copies(p, pg):
        dst = pl.ds(pl.multiple_of(p * page, page), page)
        return (pltpu.make_async_copy(pool_c.at[pg], cbuf.at[:, dst], sems.at[0]),
                pltpu.make_async_copy(pool_s.at[pg], sbuf.at[:, dst], sems.at[1]))

    def start(p, c):
        for cp in page_copies(p, pt_ref[b * npages + p]):
            cp.start()
        return c

    def wait_on(which):
        def wait(p, c):
            page_copies(p, 0)[which].wait()
            return c
        return wait

    lax.fori_loop(0, npages, start, 0)
    lax.fori_loop(0, npages, wait_on(0), 0)

    tchunk = 4 * LANES
    for half in range(2):
        for c in range(past // tchunk):
            xrow[half, c * tchunk:(c + 1) * tchunk, :] = cbuf[half * LANES:(half + 1) * LANES,
                                                              c * tchunk:(c + 1) * tchunk].T
    nchunks = past // CMP_STRIDE
    kc = _seg_rms(_compress_half(xrow.at[0], pe_ref.at[0], wa_ref.at[0], wb_ref.at[0], w2_ref.at[0], nchunks),
                  kg_ref[...]).astype(BF16)
    vc = _compress_half(xrow.at[1], pe_ref.at[1], wa_ref.at[1], wb_ref.at[1], w2_ref.at[1], nchunks).astype(BF16)

    rows = NSA_HPG * nq
    srow = lax.broadcasted_iota(jnp.int32, (rows, 1), 0) % nq
    qpos = past + srow
    cur = (past + lax.broadcasted_iota(jnp.int32, (nq, 1), 0)) // SLC_BLK
    gates = gate_ref[0]
    pad_keys = lambda x: jnp.concatenate([x, jnp.zeros((LANES - nq, LANES), F32)], axis=0).astype(BF16)
    new_ok = _lane((1, LANES)) <= jnp.minimum(srow, n_new - 1)
    ks_new, vs_new = pad_keys(ksn_ref[0, :, 0:LANES]), pad_keys(ksn_ref[0, :, LANES:256])
    kw_new, vw_new = pad_keys(kwn_ref[0, :, 0:LANES]), pad_keys(kwn_ref[0, :, LANES:256])

    lax.fori_loop(0, npages, wait_on(1), 0)
    k_aug_t = jnp.concatenate([sbuf[0:LANES, :].astype(BF16), hot_ref[...]], axis=0)
    vs_t = sbuf[LANES:256, :].astype(BF16)
    nwin = win_ref.shape[2]
    kw_t, vw_t = win_ref[0, 0:LANES, :].astype(BF16), win_ref[0, LANES:256, :].astype(BF16)

    heads = []
    for g in range(NSA_GROUPS):
        qg = jnp.concatenate(
            [q_ref[0, :, (NSA_HPG * g + h) * LANES:(NSA_HPG * g + h + 1) * LANES] for h in range(NSA_HPG)],
            axis=0).astype(BF16)
        p, oc = _cmp_branch(qg, qpos, kc, vc)
        p4 = p[0:nq] + p[nq:2 * nq] + p[2 * nq:3 * nq] + p[3 * nq:4 * nq]
        ps = sum(_dot_nt(part, mmap_ref[...]) for part in _split3(p4))
        bias = _select_bias(ps, cur, N_SEL - 1, 1, cur_in_range=False)
        qaug = jnp.concatenate([qg, jnp.concatenate([bias] * NSA_HPG, axis=0).astype(BF16)], axis=1)
        s_old = dot(qaug, k_aug_t)
        s_new = jnp.where(new_ok, _dot_nt(qg, ks_new), NEG)
        m = jnp.maximum(jnp.max(s_old, axis=-1, keepdims=True), jnp.max(s_new, axis=-1, keepdims=True))
        p_old, p_new = jnp.exp(s_old - m), jnp.exp(s_new - m)
        inv = 1.0 / (jnp.sum(p_old, axis=-1, keepdims=True) + jnp.sum(p_new, axis=-1, keepdims=True))
        osel = (_dot_nt(p_old.astype(BF16), vs_t) + dot(p_new.astype(BF16), vs_new)) * inv
        sw_old = jnp.where(_lane((1, nwin)) + (WINDOW - nwin) > srow, dot(qg, kw_t), NEG)
        sw_new = jnp.where(new_ok, _dot_nt(qg, kw_new), NEG)
        mw = jnp.maximum(jnp.max(sw_old, axis=-1, keepdims=True), jnp.max(sw_new, axis=-1, keepdims=True))
        pw_old, pw_new = jnp.exp(sw_old - mw), jnp.exp(sw_new - mw)
        invw = 1.0 / (jnp.sum(pw_old, axis=-1, keepdims=True) + jnp.sum(pw_new, axis=-1, keepdims=True))
        ow = (_dot_nt(pw_old.astype(BF16), vw_t) + dot(pw_new.astype(BF16), vw_new)) * invw
        heads += _gate_heads(gates, g, oc, osel, ow, nq)
    _store_heads(heads, o_ref)


def _nsa_decode(q, gates, ksn, kwn, win_t, pool_c, pool_s, page_table, cw, n_new):
    DB, nq, _ = q.shape
    n_pages = page_table.shape[1]
    page = pool_c.shape[2]
    past = n_pages * page
    nchunks = past // CMP_STRIDE
    mmap = _importance_map(nchunks)
    hot = (jnp.arange(LANES, dtype=jnp.int32)[:, None] == jnp.arange(past, dtype=jnp.int32)[None, :] // SLC_BLK)
    hot = hot.astype(BF16)
    per_seq = lambda a: pl.BlockSpec((1,) + a.shape[1:], lambda b, pt: (b,) + (0,) * (a.ndim - 1))
    const = lambda a: pl.BlockSpec(a.shape, lambda b, pt: (0,) * a.ndim)
    hbm = pl.BlockSpec(memory_space=pl.ANY)
    grid_spec = pltpu.PrefetchScalarGridSpec(
        num_scalar_prefetch=1,
        grid=(DB,),
        in_specs=[per_seq(q), per_seq(gates), per_seq(ksn), per_seq(kwn), per_seq(win_t), hbm, hbm,
                  const(cw['pe']), const(cw['wa']), const(cw['wb']), const(cw['w2']), const(cw['kg']), const(mmap),
                  const(hot)],
        out_specs=pl.BlockSpec((1, nq, NSA_HEADS * HEAD_DIM), lambda b, pt: (b, 0, 0)),
        scratch_shapes=[pltpu.VMEM((256, past), F32), pltpu.VMEM((256, past), F32),
                        pltpu.VMEM((2, past, LANES), F32), pltpu.SemaphoreType.DMA((2,))],
    )
    return pl.pallas_call(
        functools.partial(_nsa_decode_kernel, page=page, n_new=n_new),
        grid_spec=grid_spec,
        out_shape=jax.ShapeDtypeStruct((DB, nq, NSA_HEADS * HEAD_DIM), F32),
        compiler_params=_cparams(("arbitrary",)),
        name="nsa_sample",
    )(page_table.reshape(-1), q, gates, ksn, kwn, win_t, pool_c, pool_s,
      cw['pe'], cw['wa'], cw['wb'], cw['w2'], cw['kg'], mmap, hot)


def _nsa_sample_kernel(pt_ref, q_ref, gate_ref, ksn_ref, kwn_ref, win_ref, pool_c, pool_s,
                       pe_ref, wa_ref, wb_ref, w2_ref, kg_ref, mmap_ref, o_ref,
                       cbuf, sbuf, sems, m_sc, l_sc, acc_sc, *, page, n_new):
    b = pl.program_id(0)
    past = sbuf.shape[0]
    npages = past // page
    nq = q_ref.shape[1]

    def page_copies(p, pg):
        dst = pl.ds(pl.multiple_of(p * page, page), page)
        return (pltpu.make_async_copy(pool_c.at[pg, :, 0:LANES], cbuf.at[0, dst], sems.at[0]),
                pltpu.make_async_copy(pool_c.at[pg, :, LANES:256], cbuf.at[1, dst], sems.at[1]),
                pltpu.make_async_copy(pool_s.at[pg], sbuf.at[dst], sems.at[2]))

    def start(p, c):
        for cp in page_copies(p, pt_ref[b * npages + p]):
            cp.start()
        return c

    def wait(p, c):
        for cp in page_copies(p, 0):
            cp.wait()
        return c

    lax.fori_loop(0, npages, start, 0)
    lax.fori_loop(0, npages, wait, 0)

    nchunks = past // CMP_STRIDE
    kc = _seg_rms(_compress_half(cbuf.at[0], pe_ref.at[0], wa_ref.at[0], wb_ref.at[0], w2_ref.at[0], nchunks),
                  kg_ref[...]).astype(BF16)
    vc = _compress_half(cbuf.at[1], pe_ref.at[1], wa_ref.at[1], wb_ref.at[1], w2_ref.at[1], nchunks).astype(BF16)

    rows = NSA_HPG * nq
    srow = lax.broadcasted_iota(jnp.int32, (rows, 1), 0) % nq
    qpos = past + srow
    cur = (past + lax.broadcasted_iota(jnp.int32, (nq, 1), 0)) // SLC_BLK
    gates = gate_ref[0]
    pad_keys = lambda x: jnp.concatenate([x, jnp.zeros((LANES - nq, LANES), F32)], axis=0).astype(BF16)
    new_ok = _lane((1, LANES)) <= jnp.minimum(srow, n_new - 1)
    heads = []
    for g in range(NSA_GROUPS):
        qg = jnp.concatenate(
            [q_ref[0, :, (NSA_HPG * g + h) * LANES:(NSA_HPG * g + h + 1) * LANES] for h in range(NSA_HPG)],
            axis=0).astype(BF16)
        p, oc = _cmp_branch(qg, qpos, kc, vc)
        p4 = p[0:nq] + p[nq:2 * nq] + p[2 * nq:3 * nq] + p[3 * nq:4 * nq]
        ps = sum(_dot_nt(part, mmap_ref[...]) for part in _split3(p4))
        bias = _select_bias(ps, cur, N_SEL - 1, 1, cur_in_range=False)
        qaug = jnp.concatenate([qg, jnp.concatenate([bias] * NSA_HPG, axis=0).astype(BF16)], axis=1)
        m_sc[...] = jnp.full(m_sc.shape, NEG, F32)
        l_sc[...] = jnp.zeros(l_sc.shape, F32)
        acc_sc[...] = jnp.zeros(acc_sc.shape, F32)

        def body(kt, c, qaug=qaug):
            k0 = pl.multiple_of(kt * KEY_TILE, KEY_TILE)
            _sel_tile(qaug, sbuf[pl.ds(k0, KEY_TILE), 0:LANES].astype(BF16),
                      sbuf[pl.ds(k0, KEY_TILE), LANES:256].astype(BF16),
                      kt * (KEY_TILE // SLC_BLK), None, m_sc, l_sc, acc_sc)
            return c

        lax.fori_loop(0, past // KEY_TILE, body, 0)
        s_new = jnp.where(new_ok, _dot_nt(qg, pad_keys(ksn_ref[0, :, 0:LANES])), NEG)
        _online_update(s_new, pad_keys(ksn_ref[0, :, LANES:256]), m_sc, l_sc, acc_sc)
        osel = acc_sc[...] * (1.0 / l_sc[...])
        nwin = win_ref.shape[1]
        sw_old = jnp.where(_lane((1, nwin)) + (WINDOW - nwin) > srow,
                           _dot_nt(qg, win_ref[0, :, 0:LANES].astype(BF16)), NEG)
        sw_new = jnp.where(new_ok, _dot_nt(qg, pad_keys(kwn_ref[0, :, 0:LANES])), NEG)
        sw = jnp.concatenate([sw_old, sw_new], axis=1)
        mw = jnp.max(sw, axis=-1, keepdims=True)
        pw = jnp.exp(sw - mw)
        ow = (jnp.dot(pw[:, :nwin].astype(BF16), win_ref[0, :, LANES:256].astype(BF16), preferred_element_type=F32)
              + jnp.dot(pw[:, nwin:].astype(BF16), pad_keys(kwn_ref[0, :, LANES:256]), preferred_element_type=F32))
        ow = ow * (1.0 / jnp.sum(pw, axis=-1, keepdims=True))
        heads += _gate_heads(gates, g, oc, osel, ow, nq)
    _store_heads(heads, o_ref)


def _nsa_sample(q, gates, ksn, kwn, win, pool_c, pool_s, page_table, cw, n_new):
    DB, nq, _ = q.shape
    n_pages = page_table.shape[1]
    page = pool_c.shape[1]
    past = n_pages * page
    nchunks = past // CMP_STRIDE
    rows = NSA_HPG * nq
    mmap = _importance_map(nchunks)
    per_seq = lambda a: pl.BlockSpec((1,) + a.shape[1:], lambda b, pt: (b,) + (0,) * (a.ndim - 1))
    const = lambda a: pl.BlockSpec(a.shape, lambda b, pt: (0,) * a.ndim)
    hbm = pl.BlockSpec(memory_space=pl.ANY)
    grid_spec = pltpu.PrefetchScalarGridSpec(
        num_scalar_prefetch=1,
        grid=(DB,),
        in_specs=[per_seq(q), per_seq(gates), per_seq(ksn), per_seq(kwn), per_seq(win), hbm, hbm,
                  const(cw['pe']), const(cw['wa']), const(cw['wb']), const(cw['w2']), const(cw['kg']), const(mmap)],
        out_specs=pl.BlockSpec((1, nq, NSA_HEADS * HEAD_DIM), lambda b, pt: (b, 0, 0)),
        scratch_shapes=[pltpu.VMEM((2, past, LANES), F32), pltpu.VMEM((past, 256), F32),
                        pltpu.SemaphoreType.DMA((3,)),
                        pltpu.VMEM((rows, 1), F32), pltpu.VMEM((rows, 1), F32), pltpu.VMEM((rows, LANES), F32)],
    )
    return pl.pallas_call(
        functools.partial(_nsa_sample_kernel, page=page, n_new=n_new),
        grid_spec=grid_spec,
        out_shape=jax.ShapeDtypeStruct((DB, nq, NSA_HEADS * HEAD_DIM), F32),
        compiler_params=_cparams(("arbitrary",)),
        name="nsa_sample",
    )(page_table.reshape(-1), q, gates, ksn, kwn, win, pool_c, pool_s,
      cw['pe'], cw['wa'], cw['wb'], cw['w2'], cw['kg'], mmap)


def _nsa_prompt_kernel(q_ref, gate_ref, kc_ref, vct_ref, ks_ref, vst_ref, kw_ref, vwt_ref, mmap_ref, o_ref,
                       m_sc, l_sc, acc_sc):
    i = pl.program_id(1)
    s0 = i * Q_BLK
    cols = NSA_HPG * Q_BLK
    qpos = s0 + _lane((1, cols)) % Q_BLK
    cur = (s0 + _lane((1, Q_BLK))) // SLC_BLK
    kc, vct = kc_ref[0], vct_ref[0]
    ncmp = kc.shape[0]
    n_full = s0 // KEY_TILE
    w0 = pl.multiple_of(jnp.maximum(s0 - WINDOW, 0), Q_BLK)
    nw = WINDOW + Q_BLK
    dot = functools.partial(jnp.dot, preferred_element_type=F32)

    def softmax_keys(s):
        p = jnp.exp(s - jnp.max(s, axis=0, keepdims=True))
        return p, 1.0 / jnp.sum(p, axis=0, keepdims=True)

    qaugs = []
    branch = []
    for g in range(NSA_GROUPS):
        qt = jnp.concatenate([q_ref[(NSA_HPG * g + h) * LANES:(NSA_HPG * g + h + 1) * LANES, :]
                              for h in range(NSA_HPG)], axis=1)
        s = dot(kc, qt)
        n_idx = lax.broadcasted_iota(jnp.int32, (ncmp, cols), 0)
        s = jnp.where(n_idx * CMP_STRIDE + (CMP_BLK - 1) <= qpos, s, NEG)
        p, inv = softmax_keys(s)
        p = jnp.where(qpos >= CMP_BLK - 1, p * inv, 0.0)
        oc = dot(vct, p.astype(BF16))
        p4 = p[:, 0:Q_BLK] + p[:, Q_BLK:2 * Q_BLK] + p[:, 2 * Q_BLK:3 * Q_BLK] + p[:, 3 * Q_BLK:4 * Q_BLK]
        ps_t = sum(dot(mmap_ref[...], part) for part in _split3(p4))
        bias = _select_bias(ps_t, cur, N_SEL, 0).astype(BF16)
        qaugs.append(jnp.concatenate([qt, jnp.concatenate([bias] * NSA_HPG, axis=1)], axis=0))
        sw = dot(kw_ref[0, pl.ds(w0, nw), :], qt)
        d = qpos - (w0 + lax.broadcasted_iota(jnp.int32, (nw, cols), 0))
        sw = jnp.where(d >= 0, jnp.where(d < WINDOW, sw, NEG), NEG)
        pw, invw = softmax_keys(sw)
        ow = dot(vwt_ref[:, pl.ds(w0, nw)], pw.astype(BF16)) * invw
        branch.append((oc, ow))
        m_sc[g] = jnp.full(m_sc.shape[1:], NEG, F32)
        l_sc[g] = jnp.zeros(l_sc.shape[1:], F32)
        acc_sc[g] = jnp.zeros(acc_sc.shape[1:], F32)

    def tile(kt, causal):
        k0 = pl.multiple_of(kt * KEY_TILE, KEY_TILE)
        k_aug = ks_ref[0, pl.ds(k0, KEY_TILE), :]
        v_t = vst_ref[:, pl.ds(k0, KEY_TILE)]
        for g in range(NSA_GROUPS):
            s = dot(k_aug, qaugs[g])
            if causal:
                s = jnp.where(k0 + lax.broadcasted_iota(jnp.int32, s.shape, 0) <= qpos, s, NEG)
            m_old = m_sc[g]
            m_new = jnp.maximum(m_old, jnp.max(s, axis=0, keepdims=True))
            alpha = jnp.exp(m_old - m_new)
            p = jnp.exp(s - m_new)
            l_sc[g] = alpha * l_sc[g] + jnp.sum(p, axis=0, keepdims=True)
            acc_sc[g] = alpha * acc_sc[g] + dot(v_t, p.astype(BF16))
            m_sc[g] = m_new

    def body(kt, c):
        tile(kt, False)
        return c

    lax.fori_loop(0, n_full, body, 0)
    tile(n_full, True)

    pieces = []
    for g in range(NSA_GROUPS):
        oc, ow = branch[g]
        osel = acc_sc[g] * (1.0 / l_sc[g])
        for h in range(NSA_HPG):
            cs = slice(h * Q_BLK, (h + 1) * Q_BLK)
            c0 = (NSA_HPG * g + h) * 3
            o_h = (gate_ref[c0:c0 + 1, :] * oc[:, cs] + gate_ref[c0 + 1:c0 + 2, :] * osel[:, cs]
                   + gate_ref[c0 + 2:c0 + 3, :] * ow[:, cs])
            pieces.append(o_h[g * HEAD_DIM:(g + 1) * HEAD_DIM, :])
    o_ref[0] = jnp.concatenate(pieces, axis=0).T.astype(o_ref.dtype)


def _importance_map(ncmp):
    j = jnp.arange(LANES)[:, None]
    n = jnp.arange(ncmp)[None, :]
    return ((n >= SLC_RATIO * j - CMP_OVL) & (n < SLC_RATIO * j + SLC_RATIO)).astype(BF16)


def _nsa_prompt(B, qt, gates_t, kc, vct, ks_aug, vs_t, kw, vw_t):
    S = qt.shape[1] // B
    nq = S // Q_BLK
    ncmp = kc.shape[1]
    cols = NSA_HPG * Q_BLK
    mmap = _importance_map(ncmp)
    per_block = lambda n: pl.BlockSpec((n, Q_BLK), lambda b, i: (0, b * nq + i))
    seq_rows = lambda n: pl.BlockSpec((1, S, n), lambda b, i: (b, 0, 0))
    seq_cols = pl.BlockSpec((LANES, S), lambda b, i: (0, b))
    return pl.pallas_call(
        _nsa_prompt_kernel,
        grid=(B, nq),
        in_specs=[per_block(qt.shape[0]), per_block(LANES),
                  pl.BlockSpec((1, ncmp, LANES), lambda b, i: (b, 0, 0)),
                  pl.BlockSpec((1, LANES, ncmp), lambda b, i: (b, 0, 0)),
                  seq_rows(256), seq_cols, seq_rows(LANES), seq_cols,
                  pl.BlockSpec(mmap.shape, lambda b, i: (0, 0))],
        out_specs=pl.BlockSpec((1, Q_BLK, NSA_HEADS * HEAD_DIM), lambda b, i: (b, i, 0)),
        out_shape=jax.ShapeDtypeStruct((B, S, NSA_HEADS * HEAD_DIM), BF16),
        scratch_shapes=[pltpu.VMEM((NSA_GROUPS, 1, cols), F32), pltpu.VMEM((NSA_GROUPS, 1, cols), F32),
                        pltpu.VMEM((NSA_GROUPS, LANES, cols), F32)],
        compiler_params=_cparams(("arbitrary", "arbitrary")),
        name="nsa_prompt",
    )(qt, gates_t, kc, vct, ks_aug.reshape(B, S, 256), vs_t, kw.reshape(B, S, LANES), vw_t, mmap)


def _ret_kernel(rq_ref, rk_ref, rv_ref, r0_ref, dmask_ref, xi_ref, zeta_ref, dec_ref, o_ref, rout_ref, r_sc):
    @pl.when(pl.program_id(1) == 0)
    def _():
        r_sc[...] = r0_ref[0]

    C = rq_ref.shape[1]
    low = _lane((C, LANES)) < HEAD_DIM
    diag = lax.broadcasted_iota(jnp.int32, (LANES, LANES), 0) // HEAD_DIM == _lane((LANES, LANES)) // HEAD_DIM
    for pr in range(RET_HEADS // 2):
        sl = slice(pr * LANES, (pr + 1) * LANES)
        k = rk_ref[0, :, sl]
        qb, kb, vb = rq_ref[0, :, sl].astype(BF16), k.astype(BF16), rv_ref[0, :, sl].astype(BF16)
        zero = jnp.zeros_like(qb)
        s0 = _dot_nt(jnp.where(low, qb, zero), kb) * dmask_ref[2 * pr]
        s1 = _dot_nt(jnp.where(low, zero, qb), kb) * dmask_ref[2 * pr + 1]
        o = jnp.where(low, jnp.dot(s0.astype(BF16), vb, preferred_element_type=F32),
                      jnp.dot(s1.astype(BF16), vb, preferred_element_type=F32))
        r = r_sc[pr]
        o_ref[0, :, sl] = o + jnp.dot(qb, r.astype(BF16), preferred_element_type=F32) * xi_ref[:, sl]
        kz = (k * zeta_ref[:, sl]).astype(BF16)
        upd = lax.dot_general(kz, vb, (((0,), (0,)), ((), ())), preferred_element_type=F32)
        r_sc[pr] = dec_ref[:, sl] * r + jnp.where(diag, upd, 0.0)
    rout_ref[0] = r_sc[...]


def _ret_tables(c_true, c_pad):
    lg = jnp.log(1.0 - 2.0 ** (-5.0 - jnp.arange(RET_HEADS, dtype=F32)))
    idx = jnp.arange(c_pad, dtype=F32)
    diff = idx[:, None] - idx[None, :]
    dmask = jnp.where(diff >= 0, jnp.exp(jnp.maximum(diff, 0.0)[None] * lg[:, None, None]), 0.0)
    lanes = lambda a: jnp.repeat(a, HEAD_DIM, axis=-1)
    xi = lanes(jnp.exp((idx + 1.0)[:, None] * lg[None, :]))
    zeta = lanes(jnp.exp((c_true - 1.0 - idx)[:, None] * lg[None, :]))
    dec = lanes(jnp.exp(c_true * lg)[None, :])
    return dmask, xi, zeta, dec


def _retention(rq, rk, rv, r0, c_true, c_pad):
    B, T, _ = rq.shape
    dmask, xi, zeta, dec = _ret_tables(c_true, c_pad)
    row = pl.BlockSpec((1, c_pad, 256), lambda b, c: (b, c, 0))
    st = pl.BlockSpec((1, 2, LANES, LANES), lambda b, c: (b, 0, 0, 0))
    const = lambda a: pl.BlockSpec(a.shape, lambda b, c: (0,) * a.ndim)
    return pl.pallas_call(
        _ret_kernel,
        grid=(B, T // c_pad),
        in_specs=[row, row, row, st, const(dmask), const(xi), const(zeta), const(dec)],
        out_specs=[row, st],
        out_shape=[jax.ShapeDtypeStruct((B, T, 256), F32), jax.ShapeDtypeStruct((B, 2, LANES, LANES), F32)],
        scratch_shapes=[pltpu.VMEM((2, LANES, LANES), F32)],
        compiler_params=_cparams(("arbitrary", "arbitrary")),
        name="retention",
    )(rq, rk, rv, r0, dmask, xi, zeta, dec)


def _state_to_pairs(r):
    B = r.shape[0]
    r = r.reshape(B, 2, 2, HEAD_DIM, HEAD_DIM)
    eye = jnp.eye(2, dtype=r.dtype)
    return jnp.einsum('bphde,hk->bphdke', r, eye).reshape(B, 2, LANES, LANES)


def _pairs_to_state(rp):
    B = rp.shape[0]
    rp = rp.reshape(B, 2, 2, HEAD_DIM, 2, HEAD_DIM)
    return jnp.stack([rp[:, :, 0, :, 0, :], rp[:, :, 1, :, 1, :]], axis=2).reshape(B, RET_HEADS, HEAD_DIM, HEAD_DIM)


def _mem_kv_kernel(m_ref, g_ref, w_ref, kg_ref, kv_out):
    x = m_ref[...]
    ms = jnp.mean(x * x, axis=-1, keepdims=True)
    xn = (x * lax.rsqrt(ms + EPS) * g_ref[...]).astype(BF16)
    z = jnp.dot(xn, w_ref[...], preferred_element_type=F32)
    for c in range(2):
        kv_out[:, c * LANES:(c + 1) * LANES] = _seg_rms(z[:, c * LANES:(c + 1) * LANES], kg_ref[...])
    kv_out[:, 256:512] = z[:, 256:512]


def _mem_kv(mem, mem_norm_g, w_mem_kv, mem_k_norm_g):
    T, D = mem.shape
    w = w_mem_kv.astype(BF16)
    full = lambda a: pl.BlockSpec(a.shape, lambda i: (0, 0))
    g = mem_norm_g[None, :]
    kg = jnp.tile(mem_k_norm_g, 2)[None, :]
    return pl.pallas_call(
        _mem_kv_kernel,
        grid=(1,),
        in_specs=[full(mem), full(g), full(w), full(kg)],
        out_specs=pl.BlockSpec((T, 512), lambda i: (0, 0)),
        out_shape=jax.ShapeDtypeStruct((T, 512), F32),
        compiler_params=_cparams(("arbitrary",)),
        name="mem_kv",
    )(mem, g, w, kg)


def _mem_attn_kernel(mq_ref, mkv_ref, o_ref):
    q = mq_ref[0]
    rows = q.shape[0]
    low = _lane((rows, LANES)) < HEAD_DIM
    zero = jnp.zeros((rows, LANES), F32)
    for pr in range(MEM_HEADS // 2):
        sl = slice(pr * LANES, (pr + 1) * LANES)
        qp = q[:, sl].astype(F32)
        k = mkv_ref[0, :, sl].astype(BF16)
        v = mkv_ref[0, :, 256 + pr * LANES:256 + (pr + 1) * LANES].astype(BF16)
        outs = []
        for hh in range(2):
            qm = jnp.where(low, qp, zero) if hh == 0 else jnp.where(low, zero, qp)
            s = _dot_nt(qm.astype(BF16), k) * (HEAD_DIM ** -0.5)
            m = jnp.max(s, axis=-1, keepdims=True)
            p = jnp.exp(s - m)
            o = jnp.dot(p.astype(BF16), v, preferred_element_type=F32)
            outs.append(o * (1.0 / jnp.sum(p, axis=-1, keepdims=True)))
        o_ref[0, :, sl] = jnp.where(low, outs[0], outs[1]).astype(o_ref.dtype)


def _mem_attn(mq, mkv, tm):
    B, R, _ = mq.shape
    return pl.pallas_call(
        _mem_attn_kernel,
        grid=(B, R // tm),
        in_specs=[pl.BlockSpec((1, tm, 256), lambda b, i: (b, i, 0)),
                  pl.BlockSpec((1,) + mkv.shape[1:], lambda b, i: (b, 0, 0))],
        out_specs=pl.BlockSpec((1, tm, 256), lambda b, i: (b, i, 0)),
        out_shape=jax.ShapeDtypeStruct((B, R, 256), BF16),
        compiler_params=_cparams(("arbitrary", "arbitrary")),
        name="mem_attn",
    )(mq, mkv)


TOK_ROWS = 8


def _load_token_tiles(ref, n):
    return jnp.concatenate([ref[pl.ds(c, n, stride=TOK_ROWS), :] for c in range(TOK_ROWS)], axis=1)


def _store_token_tiles(ref, x):
    n = x.shape[0]
    for c in range(TOK_ROWS):
        ref[pl.ds(c, n, stride=TOK_ROWS), :] = x[:, c * LANES:(c + 1) * LANES]


def _mix_kernel(x_ref, onsa_ref, oret_ref, rg_ref, omem_ref, wout_ref, rgain_ref, g2_ref,
                wr_hi_ref, wr_lo_ref, br_ref, x1_out, h_out, topi_out, topg_out):
    parts = [onsa_ref[...]]
    for c in range(2):
        sl = slice(c * LANES, (c + 1) * LANES)
        parts.append((_seg_rms(oret_ref[:, sl], rgain_ref[...]) * jax.nn.silu(rg_ref[:, sl])).astype(BF16))
    parts.append(omem_ref[...])
    mix = jnp.concatenate(parts, axis=1)
    x1 = x_ref[...] + jnp.dot(mix, wout_ref[...], preferred_element_type=F32)
    x1_out[...] = x1
    ms = jnp.mean(x1 * x1, axis=-1, keepdims=True)
    h = x1 * lax.rsqrt(ms + EPS) * g2_ref[...]
    _store_token_tiles(h_out, h)
    h_hi, h_lo, _ = _split3(h)
    logits = (jnp.dot(h_hi, wr_hi_ref[...], preferred_element_type=F32)
              + jnp.dot(h_hi, wr_lo_ref[...], preferred_element_type=F32)
              + jnp.dot(h_lo, wr_hi_ref[...], preferred_element_type=F32)) + br_ref[...]
    lane = _lane(logits.shape)
    key = logits
    topi = jnp.zeros(logits.shape, jnp.int32)
    topv = jnp.zeros(logits.shape, F32)
    for r in range(TOP_K):
        m = jnp.max(key, axis=-1, keepdims=True)
        idx = jnp.min(jnp.where(key == m, lane, LANES), axis=-1, keepdims=True)
        if r == 0:
            m0 = m
        topi = jnp.where(lane == r, idx, topi)
        topv = jnp.where(lane == r, jnp.exp(m - m0), topv)
        key = jnp.where(lane == idx, -jnp.inf, key)
    topi_out[...] = topi
    topg_out[...] = topv * (1.0 / jnp.sum(topv, axis=-1, keepdims=True))


def _mix(x, onsa, oret, rg, omem, mw, tm):
    T, D = x.shape
    row = lambda n: pl.BlockSpec((tm, n), lambda i: (i, 0))
    full = lambda a: pl.BlockSpec(a.shape, lambda i: (0, 0))
    names = ('wout', 'rgain', 'g2', 'wr_hi', 'wr_lo', 'br')
    return pl.pallas_call(
        _mix_kernel,
        grid=(T // tm,),
        in_specs=[row(D), row(512), row(256), row(256), row(256)] + [full(mw[n]) for n in names],
        out_specs=[row(D), pl.BlockSpec((tm * TOK_ROWS, LANES), lambda i: (i, 0)), row(LANES), row(LANES)],
        out_shape=[jax.ShapeDtypeStruct((T, D), F32), jax.ShapeDtypeStruct((T * TOK_ROWS, LANES), F32),
                   jax.ShapeDtypeStruct((T, LANES), jnp.int32), jax.ShapeDtypeStruct((T, LANES), F32)],
        compiler_params=_cparams(("arbitrary",)),
        name="mix",
    )(x, onsa, oret, rg, omem, *[mw[n] for n in names])


def _prep_mix_weights(w_out, ret_norm_g, norm2_g, w_router, b_router):
    wr = jnp.pad(w_router, ((0, 0), (0, LANES - N_EXPERTS)))
    wr_hi = wr.astype(BF16)
    wr_lo = (wr - wr_hi.astype(F32)).astype(BF16)
    br = jnp.concatenate([b_router.astype(F32), jnp.full((LANES - N_EXPERTS,), NEG, F32)])[None, :]
    return dict(wout=w_out.astype(BF16), rgain=jnp.tile(ret_norm_g, 2)[None, :], g2=norm2_g[None, :],
                wr_hi=wr_hi, wr_lo=wr_lo, br=br)


MOE_TOK_TILE = 128


def _row_copy(src, i, dst, j, sem):
    return pltpu.make_async_copy(src.at[pl.ds(pl.multiple_of(i, TOK_ROWS), TOK_ROWS)],
                                 dst.at[pl.ds(pl.multiple_of(j, TOK_ROWS), TOK_ROWS)], sem)


def _dispatch_kernel(dest_ref, h_ref, xb_in, xb_out, sem):
    del xb_in
    tile = dest_ref.shape[0] // TOP_K

    def start(t, c):
        for k in range(TOP_K):
            _row_copy(h_ref, t * TOK_ROWS, xb_out, dest_ref[t * TOP_K + k], sem).start()
        return c

    def wait(t, c):
        for k in range(TOP_K):
            _row_copy(h_ref, 0, xb_out, 0, sem).wait()
        return c

    lax.fori_loop(0, tile, start, 0)
    lax.fori_loop(0, tile, wait, 0)


def _dispatch(h, dest, xb):
    T = h.shape[0] // TOK_ROWS
    hbm = pl.BlockSpec(memory_space=pl.ANY)
    n = MOE_TOK_TILE * TOP_K
    return pl.pallas_call(
        _dispatch_kernel,
        grid=(T // MOE_TOK_TILE,),
        in_specs=[pl.BlockSpec((n,), lambda i: (i,), memory_space=pltpu.SMEM),
                  pl.BlockSpec((MOE_TOK_TILE * TOK_ROWS, LANES), lambda i: (i, 0)), hbm],
        out_specs=hbm,
        out_shape=jax.ShapeDtypeStruct(xb.shape, xb.dtype),
        scratch_shapes=[pltpu.SemaphoreType.DMA(())],
        input_output_aliases={2: 0},
        compiler_params=_cparams(("arbitrary",)),
        name="moe_dispatch",
    )(dest.reshape(-1), h, xb)


def _moe_kernel(be_ref, nb_ref, x_ref, wup_ref, bup_ref, wdn_ref, bdn_ref, y_ref, wup_bf, wdn_bf):
    j = pl.program_id(0)

    @pl.when(j < nb_ref[0])
    def _():
        @pl.when(jnp.logical_or(j == 0, be_ref[j] != be_ref[jnp.maximum(j - 1, 0)]))
        def _():
            wup_bf[...] = wup_ref[0].astype(BF16)
            wdn_bf[...] = wdn_ref[0].astype(BF16)

        x = _load_token_tiles(x_ref, MOE_ROWS).astype(BF16)
        up = jnp.dot(x, wup_bf[...], preferred_element_type=F32) + bup_ref[0]
        x_glu = jnp.minimum(up[:, :D_FF], SWIGLU_LIMIT)
        x_lin = jnp.clip(up[:, D_FF:], -SWIGLU_LIMIT, SWIGLU_LIMIT)
        act = x_glu * jax.nn.sigmoid(SWIGLU_ALPHA * x_glu) * (x_lin + 1.0)
        _store_token_tiles(y_ref, jnp.dot(act.astype(BF16), wdn_bf[...], preferred_element_type=F32) + bdn_ref[0])

    @pl.when(j >= nb_ref[0])
    def _():
        y_ref[...] = jnp.zeros(y_ref.shape, y_ref.dtype)


def _moe_experts(xb, blk_expert, n_used, w_up, b_up, w_down, b_down):
    D = w_up.shape[1]
    blk = pl.BlockSpec((MOE_ROWS * TOK_ROWS, LANES), lambda j, be, nb: (j, 0))
    grid_spec = pltpu.PrefetchScalarGridSpec(
        num_scalar_prefetch=2,
        grid=(xb.shape[0] // (MOE_ROWS * TOK_ROWS),),
        in_specs=[blk,
                  pl.BlockSpec((1, D, 2 * D_FF), lambda j, be, nb: (be[j], 0, 0)),
                  pl.BlockSpec((1, 1, 2 * D_FF), lambda j, be, nb: (be[j], 0, 0)),
                  pl.BlockSpec((1, D_FF, D), lambda j, be, nb: (be[j], 0, 0)),
                  pl.BlockSpec((1, 1, D), lambda j, be, nb: (be[j], 0, 0))],
        out_specs=blk,
        scratch_shapes=[pltpu.VMEM((D, 2 * D_FF), BF16), pltpu.VMEM((D_FF, D), BF16)],
    )
    return pl.pallas_call(
        _moe_kernel,
        grid_spec=grid_spec,
        out_shape=jax.ShapeDtypeStruct(xb.shape, F32),
        compiler_params=_cparams(("arbitrary",)),
        name="moe_experts",
    )(blk_expert, n_used, xb, w_up, b_up[:, None, :], w_down, b_down[:, None, :])


def _combine_kernel(dest_ref, g_ref, x1_ref, yb_hbm, out_ref, buf, sem):
    tile = x1_ref.shape[0]

    def start(t, c):
        for k in range(TOP_K):
            _row_copy(yb_hbm, dest_ref[t * TOP_K + k], buf.at[k], t * TOK_ROWS, sem).start()
        return c

    def wait(t, c):
        for k in range(TOP_K):
            _row_copy(yb_hbm, 0, buf.at[k], 0, sem).wait()
        return c

    lax.fori_loop(0, tile, start, 0)
    lax.fori_loop(0, tile, wait, 0)
    acc = x1_ref[...]
    for k in range(TOP_K):
        acc = acc + g_ref[:, k:k + 1] * _load_token_tiles(buf.at[k], tile)
    out_ref[...] = acc


def _combine(x1, topg, dest, yb):
    T, D = x1.shape
    tile = MOE_TOK_TILE
    row = lambda n: pl.BlockSpec((tile, n), lambda i: (i, 0))
    return pl.pallas_call(
        _combine_kernel,
        grid=(T // tile,),
        in_specs=[pl.BlockSpec((tile * TOP_K,), lambda i: (i,), memory_space=pltpu.SMEM),
                  row(LANES), row(D), pl.BlockSpec(memory_space=pl.ANY)],
        out_specs=row(D),
        out_shape=jax.ShapeDtypeStruct((T, D), F32),
        scratch_shapes=[pltpu.VMEM((TOP_K, tile * TOK_ROWS, LANES), F32), pltpu.SemaphoreType.DMA(())],
        compiler_params=_cparams(("arbitrary",)),
        name="moe_combine",
    )(dest.reshape(-1), topg, x1, yb)


def _route(topi):
    T = topi.shape[0]
    onehot = (topi[:, :, None] == jnp.arange(N_EXPERTS, dtype=jnp.int32)).astype(jnp.int32).sum(axis=1)
    rank = jnp.cumsum(onehot, axis=0) - onehot
    counts = onehot.sum(axis=0)
    padded = (counts + MOE_ROWS - 1) // MOE_ROWS * MOE_ROWS
    pad_end = jnp.cumsum(padded)
    pad_start = pad_end - padded
    dest = pad_start[topi] + jnp.take_along_axis(rank, topi, axis=1)
    n_blk = (T * TOP_K + N_EXPERTS * (MOE_ROWS - 1) + MOE_ROWS - 1) // MOE_ROWS
    blk_start = jnp.arange(n_blk, dtype=jnp.int32) * MOE_ROWS
    blk_expert = jnp.minimum((pad_end[None, :] <= blk_start[:, None]).sum(axis=1), N_EXPERTS - 1)
    return ((dest * TOK_ROWS).astype(jnp.int32), blk_expert.astype(jnp.int32),
            (pad_end[-1:] // MOE_ROWS).astype(jnp.int32), n_blk)


SAMPLE_ROWS = 8


def _token_mixers(x, pw, cos, sin, tm, key_major):
    names = ('q', 'kvc', 'kvs', 'kvw', 'ks_bf', 'kw_bf', 'gates', 'rq', 'rk', 'rv', 'rg', 'mq', 'vs_t', 'vw_t')
    return dict(zip(names, _project(x, pw, cos, sin, tm, key_major)))


def kernel(x_prompt, x_sample, mem_prompt, cache_cmp_kv, cache_slc_kv, cache_win_kv, state_ret, cache_mem_kv, page_table, norm1_g, w_in, q_norm_g, k_norm_cmp_g, k_norm_slc_g, k_norm_win_g, cmp_pe_k, cmp_w1_k, cmp_w2_k, cmp_pe_v, cmp_w1_v, cmp_w2_v, ret_norm_g, mem_norm_g, w_mem_kv, mem_q_norm_g, mem_k_norm_g, w_out, norm2_g, w_router, b_router, w_up, b_up, w_down, b_down):
    B, S, D = x_prompt.shape
    DB, QS, _ = x_sample.shape
    n_mem = mem_prompt.shape[1]
    n_pages, page = page_table.shape[1], cache_cmp_kv.shape[2]
    past = n_pages * page
    NQ = SAMPLE_ROWS
    G, HD = NSA_GROUPS, HEAD_DIM
    win_rows = min(WINDOW, S)
    TP, TS = B * S, DB * NQ

    cos_p, sin_p = _rope_tables(jnp.arange(S, dtype=jnp.int32))
    cos_s, sin_s = _rope_tables(past + jnp.arange(NQ, dtype=jnp.int32))
    cos_s, sin_s = jnp.tile(cos_s, (DB, 1)), jnp.tile(sin_s, (DB, 1))

    xp = x_prompt.reshape(TP, D)
    xs = jnp.pad(x_sample, ((0, 0), (0, NQ - QS), (0, 0))).reshape(TS, D)
    unpad = lambda a: a.reshape(DB, NQ, -1)[:, :QS]
    outs = [[] for _ in range(9)]
    for l in range(w_in.shape[0]):
        pw = _prep_proj_weights(norm1_g[l], w_in[l], q_norm_g[l], k_norm_slc_g[l], k_norm_win_g[l], mem_q_norm_g[l])
        cw = _prep_compress_weights(cmp_pe_k[l], cmp_w1_k[l], cmp_w2_k[l], cmp_pe_v[l], cmp_w1_v[l], cmp_w2_v[l],
                                    k_norm_cmp_g[l])
        mw = _prep_mix_weights(w_out[l], ret_norm_g[l], norm2_g[l], w_router[l], b_router[l])

        t = _token_mixers(xp, pw, cos_p, sin_p, 512, True)
        kc, vct = _compress(t['kvc'].reshape(B, S, 256), cw)
        o_nsa = _nsa_prompt(B, t['q'], t['gates'], kc, vct, t['ks_bf'], t['vs_t'], t['kw_bf'], t['vw_t'])
        o_ret, r_p = _retention(t['rq'].reshape(B, S, 256), t['rk'].reshape(B, S, 256), t['rv'].reshape(B, S, 256),
                                jnp.zeros((B, 2, LANES, LANES), F32), RET_CHUNK, RET_CHUNK)
        mkv = _mem_kv(mem_prompt.reshape(B * n_mem, D), mem_norm_g[l], w_mem_kv[l], mem_k_norm_g[l])
        o_mem = _mem_attn(t['mq'].reshape(B, S, 256), mkv.reshape(B, n_mem, 512), 512)
        x1_p, h_p, topi_p, topg_p = _mix(xp, o_nsa.reshape(TP, -1), o_ret.reshape(TP, 256), t['rg'],
                                         o_mem.reshape(TP, 256), mw, 512)
        outs[0].append(t['kvc'].reshape(B, S, 2, G, HD))
        outs[1].append(t['kvs'].reshape(B, S, 2, G, HD))
        outs[2].append(t['kvw'].reshape(B, S, 2, G, HD)[:, S - win_rows:])
        outs[3].append(_pairs_to_state(r_p))
        outs[4].append(mkv.reshape(B, n_mem, 2, MEM_HEADS, HD))

        t = _token_mixers(xs, pw, cos_s, sin_s, TS, False)
        win = cache_win_kv[l]
        feat_major = lambda c: jnp.moveaxis(c, 1, -1).reshape(c.shape[0], 256, c.shape[1])
        o_nsa = _nsa_decode(t['q'].astype(F32).reshape(DB, NQ, -1), t['gates'].reshape(DB, NQ, LANES),
                            t['kvs'].reshape(DB, NQ, 256), t['kvw'].reshape(DB, NQ, 256),
                            feat_major(win), feat_major(cache_cmp_kv[l]), feat_major(cache_slc_kv[l]),
                            page_table, cw, QS)
        o_ret, r_s = _retention(t['rq'].reshape(DB, NQ, 256), t['rk'].reshape(DB, NQ, 256),
                                t['rv'].reshape(DB, NQ, 256), _state_to_pairs(state_ret[l].astype(F32)), QS, NQ)
        o_mem = _mem_attn(t['mq'].reshape(DB, NQ, 256), cache_mem_kv[l].reshape(DB, n_mem, 512), NQ)
        x1_s, h_s, topi_s, topg_s = _mix(xs, o_nsa.reshape(TS, -1).astype(BF16), o_ret.reshape(TS, 256), t['rg'],
                                         o_mem.reshape(TS, 256), mw, TS)
        kv5 = lambda a: unpad(a).reshape(DB, QS, 2, G, HD)
        outs[5].append(kv5(t['kvc']))
        outs[6].append(kv5(t['kvs']))
        outs[7].append(jnp.concatenate([win, kv5(t['kvw'])], axis=1)[:, QS:])
        outs[8].append(_pairs_to_state(r_s))

        valid = lambda a: unpad(a).reshape(DB * QS, -1)
        dest, blk_expert, n_used, n_blk = _route(jnp.concatenate([topi_p, valid(topi_s)], axis=0)[:, :TOP_K])
        xb = jnp.zeros((n_blk * MOE_ROWS * TOK_ROWS, LANES), F32)
        xb = _dispatch(h_p, dest[:TP], xb)
        h_s_valid = h_s.reshape(DB, NQ, TOK_ROWS, LANES)[:, :QS].reshape(DB * QS * TOK_ROWS, LANES)
        xb = _dispatch(h_s_valid, dest[TP:], xb)
        yb = _moe_experts(xb, blk_expert, n_used, w_up[l], b_up[l], w_down[l], b_down[l])
        xp = _combine(x1_p, topg_p, dest[:TP], yb)
        xs_new = _combine(valid(x1_s), valid(topg_s), dest[TP:], yb)
        xs = jnp.pad(xs_new.reshape(DB, QS, D), ((0, 0), (0, NQ - QS), (0, 0))).reshape(TS, D)
    y_sample = xs.reshape(DB, NQ, D)[:, :QS]
    return (xp.reshape(B, S, D), y_sample) + tuple(jnp.stack(o) for o in outs)
```

```python
import functools

import jax
import jax.numpy as jnp
from jax import lax
from jax.experimental import pallas as pl
from jax.experimental.pallas import tpu as pltpu

F32 = jnp.float32
BF16 = jnp.bfloat16

HEAD_DIM = 64
NSA_HEADS = 8
NSA_GROUPS = 2
NSA_HPG = NSA_HEADS // NSA_GROUPS
RET_HEADS = 4
MEM_HEADS = 4
CMP_BLK = 32
CMP_STRIDE = 16
CMP_HID = 256
SLC_BLK = 64
N_SEL = 16
WINDOW = 512
Q_BLK = 128
RET_CHUNK = 128
N_EXPERTS = 32
TOP_K = 4
D_FF = 1024
SWIGLU_LIMIT = 7.0
SWIGLU_ALPHA = 1.702
EPS = 1e-6
NEG = -1e30
BIG = 1e9
ROPE_BASE = 10000.0
SLC_RATIO = SLC_BLK // CMP_STRIDE
CMP_OVL = CMP_BLK // CMP_STRIDE - 1

Q_SCALE = HEAD_DIM ** -0.5 * 1.4426950408889634
LANES = 128
KEY_TILE = 512
MOE_ROWS = 512
VMEM_LIMIT = 56 * 1024 * 1024

C_Q = 0
C_KVC = C_Q + NSA_HEADS * LANES
C_KVS = C_KVC + 256
C_KVW = C_KVS + 256
C_GATE = C_KVW + 256
C_RQ = C_GATE + LANES
C_RK = C_RQ + 256
C_RV = C_RK + 256
C_RG = C_RV + 256
C_MQ = C_RG + 256
C_END = C_MQ + 256


def _cparams(sem):
    return pltpu.CompilerParams(dimension_semantics=sem, vmem_limit_bytes=VMEM_LIMIT)


def _lane(shape):
    return lax.broadcasted_iota(jnp.int32, shape, len(shape) - 1)


def _seg_rms(x, gain):
    lo = _lane(x.shape) < HEAD_DIM
    x2 = x * x
    s_lo = jnp.sum(jnp.where(lo, x2, 0.0), axis=-1, keepdims=True)
    s_hi = jnp.sum(jnp.where(lo, 0.0, x2), axis=-1, keepdims=True)
    ms = jnp.where(lo, s_lo, s_hi) * (1.0 / HEAD_DIM)
    return x * lax.rsqrt(ms + EPS) * gain


def _swap_halves(x):
    first = (_lane(x.shape) & (HEAD_DIM // 2)) == 0
    return jnp.where(first, pltpu.roll(x, LANES - HEAD_DIM // 2, 1), pltpu.roll(x, HEAD_DIM // 2, 1))


def _proj_kernel(x_ref, g1_ref, w_ref, cos_ref, sin_ref, qg_ref, ksg_ref, kwg_ref, mqg_ref,
                 q_out, kvc_out, kvs_out, kvw_out, ks_bf, kw_bf, gate_out,
                 rq_out, rk_out, rv_out, rg_out, mq_out, *extra, seq_tiles):
    x = x_ref[...]
    tm = x.shape[0]
    ms = jnp.mean(x * x, axis=-1, keepdims=True)
    xn = (x * lax.rsqrt(ms + EPS) * g1_ref[...]).astype(BF16)
    z = jnp.dot(xn, w_ref[...], preferred_element_type=F32)
    for i in range(NSA_HEADS):
        zq = z[:, C_Q + i * LANES:C_Q + (i + 1) * LANES]
        msq = jnp.sum(zq * zq, axis=-1, keepdims=True) * (1.0 / HEAD_DIM)
        qn = zq * lax.rsqrt(msq + EPS) * qg_ref[:, i * LANES:(i + 1) * LANES] * Q_SCALE
        if seq_tiles:
            q_out[i * LANES:(i + 1) * LANES, :] = qn.T.astype(BF16)
        else:
            q_out[:, i * LANES:(i + 1) * LANES] = qn.astype(BF16)
    kvc_out[...] = z[:, C_KVC:C_KVC + 256]
    ks = _seg_rms(z[:, C_KVS:C_KVS + LANES], ksg_ref[...])
    vs = z[:, C_KVS + LANES:C_KVS + 256]
    kvs_out[:, 0:LANES] = ks
    kvs_out[:, LANES:256] = vs
    kw = _seg_rms(z[:, C_KVW:C_KVW + LANES], kwg_ref[...])
    vw = z[:, C_KVW + LANES:C_KVW + 256]
    kvw_out[:, 0:LANES] = kw
    kvw_out[:, LANES:256] = vw
    gates = jax.nn.sigmoid(z[:, C_GATE:C_GATE + LANES])
    ks_bf[:, 0:LANES] = ks.astype(BF16)
    if seq_tiles:
        vs_t, vw_t = extra
        pos = (pl.program_id(0) % seq_tiles) * tm + lax.broadcasted_iota(jnp.int32, (tm, LANES), 0)
        ks_bf[:, LANES:256] = jnp.where(_lane((tm, LANES)) == pos // SLC_BLK, 1.0, 0.0).astype(BF16)
        kw_bf[...] = kw.astype(BF16)
        vs_t[...] = vs.T.astype(BF16)
        vw_t[...] = vw.T.astype(BF16)
        gate_out[...] = gates.T
    else:
        ks_bf[:, LANES:256] = vs.astype(BF16)
        kw_bf[:, 0:LANES] = kw.astype(BF16)
        kw_bf[:, LANES:256] = vw.astype(BF16)
        gate_out[...] = gates
    for c in range(2):
        sl = slice(c * LANES, (c + 1) * LANES)
        cos = cos_ref[:, sl]
        sin = sin_ref[:, sl]
        rq = z[:, C_RQ + c * LANES:C_RQ + (c + 1) * LANES]
        rk = z[:, C_RK + c * LANES:C_RK + (c + 1) * LANES]
        rq_out[:, sl] = rq * cos + _swap_halves(rq) * sin
        rk_out[:, sl] = (rk * cos + _swap_halves(rk) * sin) * (HEAD_DIM ** -0.5)
        mq = z[:, C_MQ + c * LANES:C_MQ + (c + 1) * LANES]
        mq_out[:, sl] = _seg_rms(mq, mqg_ref[...]).astype(BF16)
    rv_out[...] = z[:, C_RV:C_RV + 256]
    rg_out[...] = z[:, C_RG:C_RG + 256]


def _project(x, pw, cos, sin, tm, key_major):
    T, D = x.shape
    nt = cos.shape[0] // tm
    row = lambda n: (pl.BlockSpec((tm, n), lambda i: (i, 0)), (T, n))
    col = lambda n: (pl.BlockSpec((n, tm), lambda i: (0, i)), (n, T))
    full = lambda a: pl.BlockSpec(a.shape, lambda i: (0, 0))
    tab = pl.BlockSpec((tm, 256), lambda i: (i % nt, 0))
    tok = col if key_major else row
    outs = [
        (tok(NSA_HEADS * LANES), BF16),
        (row(256), F32), (row(256), F32), (row(256), F32),
        (row(256), BF16), (row(LANES if key_major else 256), BF16),
        (tok(LANES), F32),
        (row(256), F32), (row(256), F32), (row(256), F32), (row(256), F32),
        (row(256), BF16),
    ]
    if key_major:
        outs += [(col(LANES), BF16), (col(LANES), BF16)]
    return pl.pallas_call(
        functools.partial(_proj_kernel, seq_tiles=nt if key_major else 0),
        grid=(T // tm,),
        in_specs=[row(D)[0], full(pw['g1']), full(pw['w']), tab, tab,
                  full(pw['qg']), full(pw['ksg']), full(pw['kwg']), full(pw['mqg'])],
        out_specs=[spec for (spec, _), _ in outs],
        out_shape=[jax.ShapeDtypeStruct(shape, dt) for (_, shape), dt in outs],
        compiler_params=_cparams(("arbitrary",)),
        name="proj",
    )(x, pw['g1'], pw['w'], cos, sin, pw['qg'], pw['ksg'], pw['kwg'], pw['mqg'])


def _prep_proj_weights(norm1_g, w_in, q_norm_g, k_norm_slc_g, k_norm_win_g, mem_q_norm_g):
    D = w_in.shape[0]
    sizes = (512, 256, 256, 256, 24, 256, 256, 256, 256, 256)
    parts, off = [], 0
    for n in sizes:
        parts.append(w_in[:, off:off + n])
        off += n
    wq, wkvc, wkvs, wkvw, wg, wrq, wrk, wrv, wrg, wmq = parts
    zero = jnp.zeros((D, HEAD_DIM), w_in.dtype)
    qcols, qg = [], []
    gz = jnp.zeros((HEAD_DIM,), F32)
    for i in range(NSA_HEADS):
        wh = wq[:, i * HEAD_DIM:(i + 1) * HEAD_DIM]
        if i < NSA_HPG:
            qcols += [wh, zero]
            qg += [q_norm_g, gz]
        else:
            qcols += [zero, wh]
            qg += [gz, q_norm_g]
    wgp = jnp.pad(wg, ((0, 0), (0, LANES - wg.shape[1])))
    w = jnp.concatenate(qcols + [wkvc, wkvs, wkvw, wgp, wrq, wrk, wrv, wrg, wmq], axis=1).astype(BF16)
    two = lambda g: jnp.tile(g, 2)[None, :]
    return dict(g1=norm1_g[None, :], w=w, qg=jnp.concatenate(qg)[None, :],
                ksg=two(k_norm_slc_g), kwg=two(k_norm_win_g), mqg=two(mem_q_norm_g))


def _rope_tables(pos):
    half = HEAD_DIM // 2
    inv = ROPE_BASE ** (-jnp.arange(half, dtype=F32) / half)
    ang = pos.astype(F32)[:, None] * inv[None, :]
    cos, sin = jnp.cos(ang), jnp.sin(ang)
    cos = jnp.tile(jnp.concatenate([cos, cos], axis=1), (1, RET_HEADS))
    sin = jnp.tile(jnp.concatenate([-sin, sin], axis=1), (1, RET_HEADS))
    return cos, sin


def _compress_half(src_ref, pe_ref, wa_ref, wb_ref, w2_ref, nchunks):
    a_parts, b_parts = [], []
    for r in range(CMP_STRIDE):
        xr = src_ref[pl.ds(r, nchunks, stride=CMP_STRIDE), :]
        a_parts.append((xr + pe_ref[r:r + 1, :]).astype(BF16))
        b_parts.append((xr + pe_ref[CMP_STRIDE + r:CMP_STRIDE + r + 1, :]).astype(BF16))
    ha = jnp.dot(jnp.concatenate(a_parts, axis=1), wa_ref[...], preferred_element_type=F32)
    hb = jnp.dot(jnp.concatenate(b_parts, axis=1), wb_ref[...], preferred_element_type=F32)
    h = ha + pltpu.roll(hb, nchunks - 1, 0)
    act = jax.nn.gelu(h).astype(BF16)
    return jnp.dot(act, w2_ref[...], preferred_element_type=F32)


def _compress_kernel(kv_ref, pe_ref, wa_ref, wb_ref, w2_ref, kg_ref, kc_out, vct_out):
    nchunks = kc_out.shape[1]
    y = _compress_half(kv_ref.at[0], pe_ref.at[0], wa_ref.at[0], wb_ref.at[0], w2_ref.at[0], nchunks)

    @pl.when(pl.program_id(1) == 0)
    def _():
        kc_out[0] = _seg_rms(y, kg_ref[...]).astype(BF16)

    @pl.when(pl.program_id(1) == 1)
    def _():
        vct_out[0] = y.T.astype(BF16)


def _compress(kvc, cw):
    B, T, _ = kvc.shape
    nchunks = T // CMP_STRIDE
    per_kv = lambda a: pl.BlockSpec((1,) + a.shape[1:], lambda b, j: (j,) + (0,) * (a.ndim - 1))
    return pl.pallas_call(
        _compress_kernel,
        grid=(B, 2),
        in_specs=[pl.BlockSpec((1, T, LANES), lambda b, j: (b, 0, j)),
                  per_kv(cw['pe']), per_kv(cw['wa']), per_kv(cw['wb']), per_kv(cw['w2']),
                  pl.BlockSpec((1, LANES), lambda b, j: (0, 0))],
        out_specs=[pl.BlockSpec((1, nchunks, LANES), lambda b, j: (b, 0, 0)),
                   pl.BlockSpec((1, LANES, nchunks), lambda b, j: (b, 0, 0))],
        out_shape=[jax.ShapeDtypeStruct((B, nchunks, LANES), BF16),
                   jax.ShapeDtypeStruct((B, LANES, nchunks), BF16)],
        compiler_params=_cparams(("arbitrary", "arbitrary")),
        name="compress",
    )(kvc, cw['pe'], cw['wa'], cw['wb'], cw['w2'], cw['kg'])


def _prep_compress_weights(pe_k, w1_k, w2_k, pe_v, w1_v, w2_v, k_norm_g):
    eye = jnp.eye(NSA_GROUPS, dtype=F32)

    def one(pe, w1, w2):
        w1r = w1.reshape(CMP_BLK, HEAD_DIM, CMP_HID)
        wf = jnp.einsum('gh,rdc->rgdhc', eye, w1r).reshape(CMP_BLK * LANES, NSA_GROUPS * CMP_HID)
        w2f = jnp.einsum('gh,cd->gchd', eye, w2).reshape(NSA_GROUPS * CMP_HID, LANES)
        half = CMP_STRIDE * LANES
        return jnp.tile(pe, (1, NSA_GROUPS)), wf[:half].astype(BF16), wf[half:].astype(BF16), w2f.astype(BF16)

    k, v = one(pe_k, w1_k, w2_k), one(pe_v, w1_v, w2_v)
    st = lambda i: jnp.stack([k[i], v[i]])
    return dict(pe=st(0), wa=st(1), wb=st(2), w2=st(3), kg=jnp.tile(k_norm_g, 2)[None, :])


def _dot_nt(a, b):
    return lax.dot_general(a, b, (((1,), (1,)), ((), ())), preferred_element_type=F32)


def _split3(x):
    hi = x.astype(BF16)
    r = x - hi.astype(F32)
    mid = r.astype(BF16)
    lo = (r - mid.astype(F32)).astype(BF16)
    return hi, mid, lo


def _select_bias(ps, cur, rounds, axis, cur_in_range=True):
    j = lax.broadcasted_iota(jnp.int32, ps.shape, axis)
    last = cur if cur_in_range else cur - 1
    key = jnp.where(j <= last, ps, NEG)
    for forced in (0, last, cur - 1):
        key = jnp.where(j == forced, BIG, key)
    bias = jnp.full(ps.shape, NEG, F32)
    for _ in range(rounds):
        m = jnp.max(key, axis=axis, keepdims=True)
        idx = jnp.min(jnp.where(key == m, j, LANES), axis=axis, keepdims=True)
        pick = j == idx
        bias = jnp.where(pick, 0.0, bias)
        key = jnp.where(pick, -jnp.inf, key)
    return bias


def _cmp_branch(qg, qpos, kc, vc):
    s = _dot_nt(qg, kc)
    cend = _lane((1, kc.shape[0])) * CMP_STRIDE + (CMP_BLK - 1)
    s = jnp.where(cend <= qpos, s, NEG)
    m = jnp.max(s, axis=-1, keepdims=True)
    p = jnp.exp2(s - m)
    p = p * (1.0 / jnp.sum(p, axis=-1, keepdims=True))
    p = jnp.where(qpos >= CMP_BLK - 1, p, 0.0)
    return p, jnp.dot(p.astype(BF16), vc, preferred_element_type=F32)


def _sel_tile(qaug, k_tile, v_tile, blk0, mask, m_sc, l_sc, acc_sc):
    n = k_tile.shape[0]
    blk = blk0 + lax.broadcasted_iota(jnp.int32, (n, LANES), 0) // SLC_BLK
    onehot = jnp.where(_lane((n, LANES)) == blk, 1.0, 0.0).astype(BF16)
    s = _dot_nt(qaug, jnp.concatenate([k_tile, onehot], axis=1))
    if mask is not None:
        s = jnp.where(mask, s, NEG)
    _online_update(s, v_tile, m_sc, l_sc, acc_sc)


def _online_update(s, v_tile, m_sc, l_sc, acc_sc):
    m_old = m_sc[...]
    m_new = jnp.maximum(m_old, jnp.max(s, axis=-1, keepdims=True))
    alpha = jnp.exp(m_old - m_new)
    p = jnp.exp(s - m_new)
    l_sc[...] = alpha * l_sc[...] + jnp.sum(p, axis=-1, keepdims=True)
    acc_sc[...] = alpha * acc_sc[...] + jnp.dot(p.astype(BF16), v_tile, preferred_element_type=F32)
    m_sc[...] = m_new


def _gate_heads(gates, g, oc, osel, ow, nq):
    heads = []
    for h in range(NSA_HPG):
        rs = slice(h * nq, (h + 1) * nq)
        c0 = (NSA_HPG * g + h) * 3
        heads.append(gates[:, c0:c0 + 1] * oc[rs] + gates[:, c0 + 1:c0 + 2] * osel[rs]
                     + gates[:, c0 + 2:c0 + 3] * ow[rs])
    return heads


def _store_heads(heads, o_ref):
    low = _lane(heads[0].shape) < HEAD_DIM
    for pr in range(NSA_HEADS // 2):
        even, odd = heads[2 * pr], heads[2 * pr + 1]
        if pr < NSA_GROUPS:
            odd = pltpu.roll(odd, HEAD_DIM, 1)
        else:
            even = pltpu.roll(even, HEAD_DIM, 1)
        o_ref[0, :, pr * LANES:(pr + 1) * LANES] = jnp.where(low, even, odd).astype(o_ref.dtype)


def _nsa_decode_kernel(pt_ref, q_ref, gate_ref, ksn_ref, kwn_ref, win_ref, pool_c, pool_s,
                       pe_ref, wa_ref, wb_ref, w2_ref, kg_ref, mmap_ref, hot_ref, o_ref,
                       cbuf, sbuf, xrow, sems, *, page, n_new):
    b = pl.program_id(0)
    past = sbuf.shape[1]
    npages = past // page
    nq = q_ref.shape[1]
    dot = functools.partial(jnp.dot, preferred_element_type=F32)

    def page_copies(p, pg):
        dst = pl.ds(pl.multiple_of(p * page, page), page)
        return (pltpu.make_async_copy(pool_c.at[pg], cbuf.at[:, dst], sems.at[0]),
                pltpu.make_async_copy(pool_s.at[pg], sbuf.at[:, dst], sems.at[1]))

    def start(p, c):
        for cp in page_copies(p, pt_ref[b * npages + p]):
            cp.start()
        return c

    def wait_on(which):
        def wait(p, c):
            page_copies(p, 0)[which].wait()
            return c
        return wait

    lax.fori_loop(0, npages, start, 0)
    lax.fori_loop(0, npages, wait_on(0), 0)

    tchunk = 4 * LANES
    for half in range(2):
        for c in range(past // tchunk):
            xrow[half, c * tchunk:(c + 1) * tchunk, :] = cbuf[half * LANES:(half + 1) * LANES,
                                                              c * tchunk:(c + 1) * tchunk].T
    nchunks = past // CMP_STRIDE
    kc = _seg_rms(_compress_half(xrow.at[0], pe_ref.at[0], wa_ref.at[0], wb_ref.at[0], w2_ref.at[0], nchunks),
                  kg_ref[...]).astype(BF16)
    vc = _compress_half(xrow.at[1], pe_ref.at[1], wa_ref.at[1], wb_ref.at[1], w2_ref.at[1], nchunks).astype(BF16)

    rows = NSA_HPG * nq
    srow = lax.broadcasted_iota(jnp.int32, (rows, 1), 0) % nq
    qpos = past + srow
    cur = (past + lax.broadcasted_iota(jnp.int32, (nq, 1), 0)) // SLC_BLK
    gates = gate_ref[0]
    pad_keys = lambda x: jnp.concatenate([x, jnp.zeros((LANES - nq, LANES), F32)], axis=0).astype(BF16)
    new_ok = _lane((1, LANES)) <= jnp.minimum(srow, n_new - 1)
    ks_new, vs_new = pad_keys(ksn_ref[0, :, 0:LANES]), pad_keys(ksn_ref[0, :, LANES:256])
    kw_new, vw_new = pad_keys(kwn_ref[0, :, 0:LANES]), pad_keys(kwn_ref[0, :, LANES:256])

    lax.fori_loop(0, npages, wait_on(1), 0)
    k_aug_t = jnp.concatenate([sbuf[0:LANES, :].astype(BF16), hot_ref[...]], axis=0)
    vs_t = sbuf[LANES:256, :].astype(BF16)
    nwin = win_ref.shape[2]
    kw_t, vw_t = win_ref[0, 0:LANES, :].astype(BF16), win_ref[0, LANES:256, :].astype(BF16)

    heads = []
    for g in range(NSA_GROUPS):
        qg = jnp.concatenate(
            [q_ref[0, :, (NSA_HPG * g + h) * LANES:(NSA_HPG * g + h + 1) * LANES] for h in range(NSA_HPG)],
            axis=0).astype(BF16)
        p, oc = _cmp_branch(qg, qpos, kc, vc)
        p4 = p[0:nq] + p[nq:2 * nq] + p[2 * nq:3 * nq] + p[3 * nq:4 * nq]
        ps = sum(_dot_nt(part, mmap_ref[...]) for part in _split3(p4))
        bias = _select_bias(ps, cur, N_SEL - 1, 1, cur_in_range=False)
        qaug = jnp.concatenate([qg, jnp.concatenate([bias] * NSA_HPG, axis=0).astype(BF16)], axis=1)
        s_old = dot(qaug, k_aug_t)
        s_new = jnp.where(new_ok, _dot_nt(qg, ks_new), NEG)
        m = jnp.maximum(jnp.max(s_old, axis=-1, keepdims=True), jnp.max(s_new, axis=-1, keepdims=True))
        p_old, p_new = jnp.exp2(s_old - m), jnp.exp2(s_new - m)
        inv = 1.0 / (jnp.sum(p_old, axis=-1, keepdims=True) + jnp.sum(p_new, axis=-1, keepdims=True))
        osel = (_dot_nt(p_old.astype(BF16), vs_t) + dot(p_new.astype(BF16), vs_new)) * inv
        sw_old = jnp.where(_lane((1, nwin)) + (WINDOW - nwin) > srow, dot(qg, kw_t), NEG)
        sw_new = jnp.where(new_ok, _dot_nt(qg, kw_new), NEG)
        mw = jnp.maximum(jnp.max(sw_old, axis=-1, keepdims=True), jnp.max(sw_new, axis=-1, keepdims=True))
        pw_old, pw_new = jnp.exp2(sw_old - mw), jnp.exp2(sw_new - mw)
        invw = 1.0 / (jnp.sum(pw_old, axis=-1, keepdims=True) + jnp.sum(pw_new, axis=-1, keepdims=True))
        ow = (_dot_nt(pw_old.astype(BF16), vw_t) + dot(pw_new.astype(BF16), vw_new)) * invw
        heads += _gate_heads(gates, g, oc, osel, ow, nq)
    _store_heads(heads, o_ref)


def _nsa_decode(q, gates, ksn, kwn, win_t, pool_c, pool_s, page_table, cw, n_new):
    DB, nq, _ = q.shape
    n_pages = page_table.shape[1]
    page = pool_c.shape[2]
    past = n_pages * page
    nchunks = past // CMP_STRIDE
    mmap = _importance_map(nchunks)
    hot = (jnp.arange(LANES, dtype=jnp.int32)[:, None] == jnp.arange(past, dtype=jnp.int32)[None, :] // SLC_BLK)
    hot = hot.astype(BF16)
    per_seq = lambda a: pl.BlockSpec((1,) + a.shape[1:], lambda b, pt: (b,) + (0,) * (a.ndim - 1))
    const = lambda a: pl.BlockSpec(a.shape, lambda b, pt: (0,) * a.ndim)
    hbm = pl.BlockSpec(memory_space=pl.ANY)
    grid_spec = pltpu.PrefetchScalarGridSpec(
        num_scalar_prefetch=1,
        grid=(DB,),
        in_specs=[per_seq(q), per_seq(gates), per_seq(ksn), per_seq(kwn), per_seq(win_t), hbm, hbm,
                  const(cw['pe']), const(cw['wa']), const(cw['wb']), const(cw['w2']), const(cw['kg']), const(mmap),
                  const(hot)],
        out_specs=pl.BlockSpec((1, nq, NSA_HEADS * HEAD_DIM), lambda b, pt: (b, 0, 0)),
        scratch_shapes=[pltpu.VMEM((256, past), F32), pltpu.VMEM((256, past), F32),
                        pltpu.VMEM((2, past, LANES), F32), pltpu.SemaphoreType.DMA((2,))],
    )
    return pl.pallas_call(
        functools.partial(_nsa_decode_kernel, page=page, n_new=n_new),
        grid_spec=grid_spec,
        out_shape=jax.ShapeDtypeStruct((DB, nq, NSA_HEADS * HEAD_DIM), F32),
        compiler_params=_cparams(("arbitrary",)),
        name="nsa_sample",
    )(page_table.reshape(-1), q, gates, ksn, kwn, win_t, pool_c, pool_s,
      cw['pe'], cw['wa'], cw['wb'], cw['w2'], cw['kg'], mmap, hot)


def _nsa_sample_kernel(pt_ref, q_ref, gate_ref, ksn_ref, kwn_ref, win_ref, pool_c, pool_s,
                       pe_ref, wa_ref, wb_ref, w2_ref, kg_ref, mmap_ref, o_ref,
                       cbuf, sbuf, sems, m_sc, l_sc, acc_sc, *, page, n_new):
    b = pl.program_id(0)
    past = sbuf.shape[0]
    npages = past // page
    nq = q_ref.shape[1]

    def page_copies(p, pg):
        dst = pl.ds(pl.multiple_of(p * page, page), page)
        return (pltpu.make_async_copy(pool_c.at[pg, :, 0:LANES], cbuf.at[0, dst], sems.at[0]),
                pltpu.make_async_copy(pool_c.at[pg, :, LANES:256], cbuf.at[1, dst], sems.at[1]),
                pltpu.make_async_copy(pool_s.at[pg], sbuf.at[dst], sems.at[2]))

    def start(p, c):
        for cp in page_copies(p, pt_ref[b * npages + p]):
            cp.start()
        return c

    def wait(p, c):
        for cp in page_copies(p, 0):
            cp.wait()
        return c

    lax.fori_loop(0, npages, start, 0)
    lax.fori_loop(0, npages, wait, 0)

    nchunks = past // CMP_STRIDE
    kc = _seg_rms(_compress_half(cbuf.at[0], pe_ref.at[0], wa_ref.at[0], wb_ref.at[0], w2_ref.at[0], nchunks),
                  kg_ref[...]).astype(BF16)
    vc = _compress_half(cbuf.at[1], pe_ref.at[1], wa_ref.at[1], wb_ref.at[1], w2_ref.at[1], nchunks).astype(BF16)

    rows = NSA_HPG * nq
    srow = lax.broadcasted_iota(jnp.int32, (rows, 1), 0) % nq
    qpos = past + srow
    cur = (past + lax.broadcasted_iota(jnp.int32, (nq, 1), 0)) // SLC_BLK
    gates = gate_ref[0]
    pad_keys = lambda x: jnp.concatenate([x, jnp.zeros((LANES - nq, LANES), F32)], axis=0).astype(BF16)
    new_ok = _lane((1, LANES)) <= jnp.minimum(srow, n_new - 1)
    heads = []
    for g in range(NSA_GROUPS):
        qg = jnp.concatenate(
            [q_ref[0, :, (NSA_HPG * g + h) * LANES:(NSA_HPG * g + h + 1) * LANES] for h in range(NSA_HPG)],
            axis=0).astype(BF16)
        p, oc = _cmp_branch(qg, qpos, kc, vc)
        p4 = p[0:nq] + p[nq:2 * nq] + p[2 * nq:3 * nq] + p[3 * nq:4 * nq]
        ps = sum(_dot_nt(part, mmap_ref[...]) for part in _split3(p4))
        bias = _select_bias(ps, cur, N_SEL - 1, 1, cur_in_range=False)
        qaug = jnp.concatenate([qg, jnp.concatenate([bias] * NSA_HPG, axis=0).astype(BF16)], axis=1)
        m_sc[...] = jnp.full(m_sc.shape, NEG, F32)
        l_sc[...] = jnp.zeros(l_sc.shape, F32)
        acc_sc[...] = jnp.zeros(acc_sc.shape, F32)

        def body(kt, c, qaug=qaug):
            k0 = pl.multiple_of(kt * KEY_TILE, KEY_TILE)
            _sel_tile(qaug, sbuf[pl.ds(k0, KEY_TILE), 0:LANES].astype(BF16),
                      sbuf[pl.ds(k0, KEY_TILE), LANES:256].astype(BF16),
                      kt * (KEY_TILE // SLC_BLK), None, m_sc, l_sc, acc_sc)
            return c

        lax.fori_loop(0, past // KEY_TILE, body, 0)
        s_new = jnp.where(new_ok, _dot_nt(qg, pad_keys(ksn_ref[0, :, 0:LANES])), NEG)
        _online_update(s_new, pad_keys(ksn_ref[0, :, LANES:256]), m_sc, l_sc, acc_sc)
        osel = acc_sc[...] * (1.0 / l_sc[...])
        nwin = win_ref.shape[1]
        sw_old = jnp.where(_lane((1, nwin)) + (WINDOW - nwin) > srow,
                           _dot_nt(qg, win_ref[0, :, 0:LANES].astype(BF16)), NEG)
        sw_new = jnp.where(new_ok, _dot_nt(qg, pad_keys(kwn_ref[0, :, 0:LANES])), NEG)
        sw = jnp.concatenate([sw_old, sw_new], axis=1)
        mw = jnp.max(sw, axis=-1, keepdims=True)
        pw = jnp.exp(sw - mw)
        ow = (jnp.dot(pw[:, :nwin].astype(BF16), win_ref[0, :, LANES:256].astype(BF16), preferred_element_type=F32)
              + jnp.dot(pw[:, nwin:].astype(BF16), pad_keys(kwn_ref[0, :, LANES:256]), preferred_element_type=F32))
        ow = ow * (1.0 / jnp.sum(pw, axis=-1, keepdims=True))
        heads += _gate_heads(gates, g, oc, osel, ow, nq)
    _store_heads(heads, o_ref)


def _nsa_sample(q, gates, ksn, kwn, win, pool_c, pool_s, page_table, cw, n_new):
    DB, nq, _ = q.shape
    n_pages = page_table.shape[1]
    page = pool_c.shape[1]
    past = n_pages * page
    nchunks = past // CMP_STRIDE
    rows = NSA_HPG * nq
    mmap = _importance_map(nchunks)
    per_seq = lambda a: pl.BlockSpec((1,) + a.shape[1:], lambda b, pt: (b,) + (0,) * (a.ndim - 1))
    const = lambda a: pl.BlockSpec(a.shape, lambda b, pt: (0,) * a.ndim)
    hbm = pl.BlockSpec(memory_space=pl.ANY)
    grid_spec = pltpu.PrefetchScalarGridSpec(
        num_scalar_prefetch=1,
        grid=(DB,),
        in_specs=[per_seq(q), per_seq(gates), per_seq(ksn), per_seq(kwn), per_seq(win), hbm, hbm,
                  const(cw['pe']), const(cw['wa']), const(cw['wb']), const(cw['w2']), const(cw['kg']), const(mmap)],
        out_specs=pl.BlockSpec((1, nq, NSA_HEADS * HEAD_DIM), lambda b, pt: (b, 0, 0)),
        scratch_shapes=[pltpu.VMEM((2, past, LANES), F32), pltpu.VMEM((past, 256), F32),
                        pltpu.SemaphoreType.DMA((3,)),
                        pltpu.VMEM((rows, 1), F32), pltpu.VMEM((rows, 1), F32), pltpu.VMEM((rows, LANES), F32)],
    )
    return pl.pallas_call(
        functools.partial(_nsa_sample_kernel, page=page, n_new=n_new),
        grid_spec=grid_spec,
        out_shape=jax.ShapeDtypeStruct((DB, nq, NSA_HEADS * HEAD_DIM), F32),
        compiler_params=_cparams(("arbitrary",)),
        name="nsa_sample",
    )(page_table.reshape(-1), q, gates, ksn, kwn, win, pool_c, pool_s,
      cw['pe'], cw['wa'], cw['wb'], cw['w2'], cw['kg'], mmap)


def _nsa_prompt_kernel(q_ref, gate_ref, kc_ref, vct_ref, ks_ref, vst_ref, kw_ref, vwt_ref, mmap_ref, o_ref,
                       m_sc, l_sc, acc_sc, sa_sc, sb_sc):
    i = pl.program_id(1)
    s0 = i * Q_BLK
    cols = NSA_HPG * Q_BLK
    qpos = s0 + _lane((1, cols)) % Q_BLK
    cur = (s0 + _lane((1, Q_BLK))) // SLC_BLK
    kc, vct = kc_ref[0], vct_ref[0]
    ncmp = kc.shape[0]
    n_full = s0 // KEY_TILE
    w0 = pl.multiple_of(jnp.maximum(s0 - WINDOW, 0), Q_BLK)
    nw = WINDOW + Q_BLK
    dot = functools.partial(jnp.dot, preferred_element_type=F32)

    def softmax_keys(s):
        p = jnp.exp2(s - jnp.max(s, axis=0, keepdims=True))
        return p, 1.0 / jnp.sum(p, axis=0, keepdims=True)

    qaugs = []
    branch = []
    for g in range(NSA_GROUPS):
        qt = jnp.concatenate([q_ref[(NSA_HPG * g + h) * LANES:(NSA_HPG * g + h + 1) * LANES, :]
                              for h in range(NSA_HPG)], axis=1)
        s = dot(kc, qt)
        n_idx = lax.broadcasted_iota(jnp.int32, (ncmp, cols), 0)
        s = jnp.where(n_idx * CMP_STRIDE + (CMP_BLK - 1) <= qpos, s, NEG)
        p, inv = softmax_keys(s)
        p = jnp.where(qpos >= CMP_BLK - 1, p * inv, 0.0)
        oc = dot(vct, p.astype(BF16))
        p4 = p[:, 0:Q_BLK] + p[:, Q_BLK:2 * Q_BLK] + p[:, 2 * Q_BLK:3 * Q_BLK] + p[:, 3 * Q_BLK:4 * Q_BLK]
        ps_t = sum(dot(mmap_ref[...], part) for part in _split3(p4))
        bias = _select_bias(ps_t, cur, N_SEL, 0).astype(BF16)
        qaugs.append(jnp.concatenate([qt, jnp.concatenate([bias] * NSA_HPG, axis=1)], axis=0))
        sw = dot(kw_ref[0, pl.ds(w0, nw), :], qt)
        d = qpos - (w0 + lax.broadcasted_iota(jnp.int32, (nw, cols), 0))
        sw = jnp.where(d >= 0, jnp.where(d < WINDOW, sw, NEG), NEG)
        pw, invw = softmax_keys(sw)
        ow = dot(vwt_ref[:, pl.ds(w0, nw)], pw.astype(BF16)) * invw
        branch.append((oc, ow))
        m_sc[g] = jnp.full(m_sc.shape[1:], NEG, F32)
        l_sc[g] = jnp.zeros(l_sc.shape[1:], F32)
        acc_sc[g] = jnp.zeros(acc_sc.shape[1:], F32)

    def scores(kt, s_ref):
        k_aug = ks_ref[0, pl.ds(pl.multiple_of(kt * KEY_TILE, KEY_TILE), KEY_TILE), :]
        for g in range(NSA_GROUPS):
            s_ref[g] = dot(k_aug, qaugs[g])

    def consume(kt, s_ref, causal):
        k0 = pl.multiple_of(kt * KEY_TILE, KEY_TILE)
        v_t = vst_ref[:, pl.ds(k0, KEY_TILE)]
        for g in range(NSA_GROUPS):
            s = s_ref[g]
            if causal:
                s = jnp.where(k0 + lax.broadcasted_iota(jnp.int32, s.shape, 0) <= qpos, s, NEG)
            m_old = m_sc[g]
            m_new = jnp.maximum(m_old, jnp.max(s, axis=0, keepdims=True))
            alpha = jnp.exp2(m_old - m_new)
            p = jnp.exp2(s - m_new)
            l_sc[g] = alpha * l_sc[g] + jnp.sum(p, axis=0, keepdims=True)
            acc_sc[g] = alpha * acc_sc[g] + dot(v_t, p.astype(BF16))
            m_sc[g] = m_new

    def two_tiles(u, c):
        scores(2 * u + 1, sb_sc)
        consume(2 * u, sa_sc, False)
        scores(2 * u + 2, sa_sc)
        consume(2 * u + 1, sb_sc, False)
        return c

    scores(0, sa_sc)
    lax.fori_loop(0, n_full // 2, two_tiles, 0)

    @pl.when(n_full % 2 == 1)
    def _():
        scores(n_full, sb_sc)
        consume(n_full - 1, sa_sc, False)
        consume(n_full, sb_sc, True)

    @pl.when(n_full % 2 == 0)
    def _():
        consume(n_full, sa_sc, True)

    pieces = []
    for g in range(NSA_GROUPS):
        oc, ow = branch[g]
        osel = acc_sc[g] * (1.0 / l_sc[g])
        for h in range(NSA_HPG):
            cs = slice(h * Q_BLK, (h + 1) * Q_BLK)
            c0 = (NSA_HPG * g + h) * 3
            o_h = (gate_ref[c0:c0 + 1, :] * oc[:, cs] + gate_ref[c0 + 1:c0 + 2, :] * osel[:, cs]
                   + gate_ref[c0 + 2:c0 + 3, :] * ow[:, cs])
            pieces.append(o_h[g * HEAD_DIM:(g + 1) * HEAD_DIM, :])
    o_ref[0] = jnp.concatenate(pieces, axis=0).T.astype(o_ref.dtype)


def _importance_map(ncmp):
    j = jnp.arange(LANES)[:, None]
    n = jnp.arange(ncmp)[None, :]
    return ((n >= SLC_RATIO * j - CMP_OVL) & (n < SLC_RATIO * j + SLC_RATIO)).astype(BF16)


def _nsa_prompt(B, qt, gates_t, kc, vct, ks_aug, vs_t, kw, vw_t):
    S = qt.shape[1] // B
    nq = S // Q_BLK
    ncmp = kc.shape[1]
    cols = NSA_HPG * Q_BLK
    mmap = _importance_map(ncmp)
    per_block = lambda n: pl.BlockSpec((n, Q_BLK), lambda b, i: (0, b * nq + i))
    seq_rows = lambda n: pl.BlockSpec((1, S, n), lambda b, i: (b, 0, 0))
    seq_cols = pl.BlockSpec((LANES, S), lambda b, i: (0, b))
    return pl.pallas_call(
        _nsa_prompt_kernel,
        grid=(B, nq),
        in_specs=[per_block(qt.shape[0]), per_block(LANES),
                  pl.BlockSpec((1, ncmp, LANES), lambda b, i: (b, 0, 0)),
                  pl.BlockSpec((1, LANES, ncmp), lambda b, i: (b, 0, 0)),
                  seq_rows(256), seq_cols, seq_rows(LANES), seq_cols,
                  pl.BlockSpec(mmap.shape, lambda b, i: (0, 0))],
        out_specs=pl.BlockSpec((1, Q_BLK, NSA_HEADS * HEAD_DIM), lambda b, i: (b, i, 0)),
        out_shape=jax.ShapeDtypeStruct((B, S, NSA_HEADS * HEAD_DIM), BF16),
        scratch_shapes=[pltpu.VMEM((NSA_GROUPS, 1, cols), F32), pltpu.VMEM((NSA_GROUPS, 1, cols), F32),
                        pltpu.VMEM((NSA_GROUPS, LANES, cols), F32),
                        pltpu.VMEM((NSA_GROUPS, KEY_TILE, cols), F32), pltpu.VMEM((NSA_GROUPS, KEY_TILE, cols), F32)],
        compiler_params=_cparams(("arbitrary", "arbitrary")),
        name="nsa_prompt",
    )(qt, gates_t, kc, vct, ks_aug.reshape(B, S, 256), vs_t, kw.reshape(B, S, LANES), vw_t, mmap)


def _ret_kernel(rq_ref, rk_ref, rv_ref, r0_ref, dmask_ref, xi_ref, zeta_ref, dec_ref, o_ref, rout_ref, r_sc):
    @pl.when(pl.program_id(1) == 0)
    def _():
        r_sc[...] = r0_ref[0]

    C = rq_ref.shape[1]
    low = _lane((C, LANES)) < HEAD_DIM
    diag = lax.broadcasted_iota(jnp.int32, (LANES, LANES), 0) // HEAD_DIM == _lane((LANES, LANES)) // HEAD_DIM
    for pr in range(RET_HEADS // 2):
        sl = slice(pr * LANES, (pr + 1) * LANES)
        k = rk_ref[0, :, sl]
        qb, kb, vb = rq_ref[0, :, sl].astype(BF16), k.astype(BF16), rv_ref[0, :, sl].astype(BF16)
        zero = jnp.zeros_like(qb)
        s0 = _dot_nt(jnp.where(low, qb, zero), kb) * dmask_ref[2 * pr]
        s1 = _dot_nt(jnp.where(low, zero, qb), kb) * dmask_ref[2 * pr + 1]
        o = jnp.where(low, jnp.dot(s0.astype(BF16), vb, preferred_element_type=F32),
                      jnp.dot(s1.astype(BF16), vb, preferred_element_type=F32))
        r = r_sc[pr]
        o_ref[0, :, sl] = o + jnp.dot(qb, r.astype(BF16), preferred_element_type=F32) * xi_ref[:, sl]
        kz = (k * zeta_ref[:, sl]).astype(BF16)
        upd = lax.dot_general(kz, vb, (((0,), (0,)), ((), ())), preferred_element_type=F32)
        r_sc[pr] = dec_ref[:, sl] * r + jnp.where(diag, upd, 0.0)
    rout_ref[0] = r_sc[...]


def _ret_tables(c_true, c_pad):
    lg = jnp.log(1.0 - 2.0 ** (-5.0 - jnp.arange(RET_HEADS, dtype=F32)))
    idx = jnp.arange(c_pad, dtype=F32)
    diff = idx[:, None] - idx[None, :]
    dmask = jnp.where(diff >= 0, jnp.exp(jnp.maximum(diff, 0.0)[None] * lg[:, None, None]), 0.0)
    lanes = lambda a: jnp.repeat(a, HEAD_DIM, axis=-1)
    xi = lanes(jnp.exp((idx + 1.0)[:, None] * lg[None, :]))
    zeta = lanes(jnp.exp((c_true - 1.0 - idx)[:, None] * lg[None, :]))
    dec = lanes(jnp.exp(c_true * lg)[None, :])
    return dmask, xi, zeta, dec


def _retention(rq, rk, rv, r0, c_true, c_pad):
    B, T, _ = rq.shape
    dmask, xi, zeta, dec = _ret_tables(c_true, c_pad)
    row = pl.BlockSpec((1, c_pad, 256), lambda b, c: (b, c, 0))
    st = pl.BlockSpec((1, 2, LANES, LANES), lambda b, c: (b, 0, 0, 0))
    const = lambda a: pl.BlockSpec(a.shape, lambda b, c: (0,) * a.ndim)
    return pl.pallas_call(
        _ret_kernel,
        grid=(B, T // c_pad),
        in_specs=[row, row, row, st, const(dmask), const(xi), const(zeta), const(dec)],
        out_specs=[row, st],
        out_shape=[jax.ShapeDtypeStruct((B, T, 256), F32), jax.ShapeDtypeStruct((B, 2, LANES, LANES), F32)],
        scratch_shapes=[pltpu.VMEM((2, LANES, LANES), F32)],
        compiler_params=_cparams(("arbitrary", "arbitrary")),
        name="retention",
    )(rq, rk, rv, r0, dmask, xi, zeta, dec)


def _state_to_pairs(r):
    B = r.shape[0]
    r = r.reshape(B, 2, 2, HEAD_DIM, HEAD_DIM)
    eye = jnp.eye(2, dtype=r.dtype)
    return jnp.einsum('bphde,hk->bphdke', r, eye).reshape(B, 2, LANES, LANES)


def _pairs_to_state(rp):
    B = rp.shape[0]
    rp = rp.reshape(B, 2, 2, HEAD_DIM, 2, HEAD_DIM)
    return jnp.stack([rp[:, :, 0, :, 0, :], rp[:, :, 1, :, 1, :]], axis=2).reshape(B, RET_HEADS, HEAD_DIM, HEAD_DIM)


def _mem_kv_kernel(m_ref, g_ref, w_ref, kg_ref, kv_out):
    x = m_ref[...]
    ms = jnp.mean(x * x, axis=-1, keepdims=True)
    xn = (x * lax.rsqrt(ms + EPS) * g_ref[...]).astype(BF16)
    z = jnp.dot(xn, w_ref[...], preferred_element_type=F32)
    for c in range(2):
        kv_out[:, c * LANES:(c + 1) * LANES] = _seg_rms(z[:, c * LANES:(c + 1) * LANES], kg_ref[...])
    kv_out[:, 256:512] = z[:, 256:512]


def _mem_kv(mem, mem_norm_g, w_mem_kv, mem_k_norm_g):
    T, D = mem.shape
    w = w_mem_kv.astype(BF16)
    full = lambda a: pl.BlockSpec(a.shape, lambda i: (0, 0))
    g = mem_norm_g[None, :]
    kg = jnp.tile(mem_k_norm_g, 2)[None, :]
    return pl.pallas_call(
        _mem_kv_kernel,
        grid=(1,),
        in_specs=[full(mem), full(g), full(w), full(kg)],
        out_specs=pl.BlockSpec((T, 512), lambda i: (0, 0)),
        out_shape=jax.ShapeDtypeStruct((T, 512), F32),
        compiler_params=_cparams(("arbitrary",)),
        name="mem_kv",
    )(mem, g, w, kg)


def _mem_attn_kernel(mq_ref, mkv_ref, o_ref):
    q = mq_ref[0]
    rows = q.shape[0]
    low = _lane((rows, LANES)) < HEAD_DIM
    zero = jnp.zeros((rows, LANES), F32)
    for pr in range(MEM_HEADS // 2):
        sl = slice(pr * LANES, (pr + 1) * LANES)
        qp = q[:, sl].astype(F32)
        k = mkv_ref[0, :, sl].astype(BF16)
        v = mkv_ref[0, :, 256 + pr * LANES:256 + (pr + 1) * LANES].astype(BF16)
        outs = []
        for hh in range(2):
            qm = jnp.where(low, qp, zero) if hh == 0 else jnp.where(low, zero, qp)
            s = _dot_nt(qm.astype(BF16), k) * (HEAD_DIM ** -0.5)
            m = jnp.max(s, axis=-1, keepdims=True)
            p = jnp.exp(s - m)
            o = jnp.dot(p.astype(BF16), v, preferred_element_type=F32)
            outs.append(o * (1.0 / jnp.sum(p, axis=-1, keepdims=True)))
        o_ref[0, :, sl] = jnp.where(low, outs[0], outs[1]).astype(o_ref.dtype)


def _mem_attn(mq, mkv, tm):
    B, R, _ = mq.shape
    return pl.pallas_call(
        _mem_attn_kernel,
        grid=(B, R // tm),
        in_specs=[pl.BlockSpec((1, tm, 256), lambda b, i: (b, i, 0)),
                  pl.BlockSpec((1,) + mkv.shape[1:], lambda b, i: (b, 0, 0))],
        out_specs=pl.BlockSpec((1, tm, 256), lambda b, i: (b, i, 0)),
        out_shape=jax.ShapeDtypeStruct((B, R, 256), BF16),
        compiler_params=_cparams(("arbitrary", "arbitrary")),
        name="mem_attn",
    )(mq, mkv)


TOK_ROWS = 8


def _load_token_tiles(ref, n):
    return jnp.concatenate([ref[pl.ds(c, n, stride=TOK_ROWS), :] for c in range(TOK_ROWS)], axis=1)


def _store_token_tiles(ref, x):
    n = x.shape[0]
    for c in range(TOK_ROWS):
        ref[pl.ds(c, n, stride=TOK_ROWS), :] = x[:, c * LANES:(c + 1) * LANES]


def _mix_kernel(x_ref, onsa_ref, oret_ref, rg_ref, omem_ref, wout_ref, rgain_ref, g2_ref,
                wr_hi_ref, wr_lo_ref, br_ref, x1_out, h_out, topi_out, topg_out):
    parts = [onsa_ref[...]]
    for c in range(2):
        sl = slice(c * LANES, (c + 1) * LANES)
        parts.append((_seg_rms(oret_ref[:, sl], rgain_ref[...]) * jax.nn.silu(rg_ref[:, sl])).astype(BF16))
    parts.append(omem_ref[...])
    mix = jnp.concatenate(parts, axis=1)
    x1 = x_ref[...] + jnp.dot(mix, wout_ref[...], preferred_element_type=F32)
    x1_out[...] = x1
    ms = jnp.mean(x1 * x1, axis=-1, keepdims=True)
    h = x1 * lax.rsqrt(ms + EPS) * g2_ref[...]
    _store_token_tiles(h_out, h)
    h_hi, h_lo, _ = _split3(h)
    logits = (jnp.dot(h_hi, wr_hi_ref[...], preferred_element_type=F32)
              + jnp.dot(h_hi, wr_lo_ref[...], preferred_element_type=F32)
              + jnp.dot(h_lo, wr_hi_ref[...], preferred_element_type=F32)) + br_ref[...]
    lane = _lane(logits.shape)
    key = logits
    topi = jnp.zeros(logits.shape, jnp.int32)
    topv = jnp.zeros(logits.shape, F32)
    for r in range(TOP_K):
        m = jnp.max(key, axis=-1, keepdims=True)
        idx = jnp.min(jnp.where(key == m, lane, LANES), axis=-1, keepdims=True)
        if r == 0:
            m0 = m
        topi = jnp.where(lane == r, idx, topi)
        topv = jnp.where(lane == r, jnp.exp(m - m0), topv)
        key = jnp.where(lane == idx, -jnp.inf, key)
    topi_out[...] = topi
    topg_out[...] = topv * (1.0 / jnp.sum(topv, axis=-1, keepdims=True))


def _mix(x, onsa, oret, rg, omem, mw, tm):
    T, D = x.shape
    row = lambda n: pl.BlockSpec((tm, n), lambda i: (i, 0))
    full = lambda a: pl.BlockSpec(a.shape, lambda i: (0, 0))
    names = ('wout', 'rgain', 'g2', 'wr_hi', 'wr_lo', 'br')
    return pl.pallas_call(
        _mix_kernel,
        grid=(T // tm,),
        in_specs=[row(D), row(512), row(256), row(256), row(256)] + [full(mw[n]) for n in names],
        out_specs=[row(D), pl.BlockSpec((tm * TOK_ROWS, LANES), lambda i: (i, 0)), row(LANES), row(LANES)],
        out_shape=[jax.ShapeDtypeStruct((T, D), F32), jax.ShapeDtypeStruct((T * TOK_ROWS, LANES), F32),
                   jax.ShapeDtypeStruct((T, LANES), jnp.int32), jax.ShapeDtypeStruct((T, LANES), F32)],
        compiler_params=_cparams(("arbitrary",)),
        name="mix",
    )(x, onsa, oret, rg, omem, *[mw[n] for n in names])


def _prep_mix_weights(w_out, ret_norm_g, norm2_g, w_router, b_router):
    wr = jnp.pad(w_router, ((0, 0), (0, LANES - N_EXPERTS)))
    wr_hi = wr.astype(BF16)
    wr_lo = (wr - wr_hi.astype(F32)).astype(BF16)
    br = jnp.concatenate([b_router.astype(F32), jnp.full((LANES - N_EXPERTS,), NEG, F32)])[None, :]
    return dict(wout=w_out.astype(BF16), rgain=jnp.tile(ret_norm_g, 2)[None, :], g2=norm2_g[None, :],
                wr_hi=wr_hi, wr_lo=wr_lo, br=br)


MOE_TOK_TILE = 128


def _row_copy(src, i, dst, j, sem):
    return pltpu.make_async_copy(src.at[pl.ds(pl.multiple_of(i, TOK_ROWS), TOK_ROWS)],
                                 dst.at[pl.ds(pl.multiple_of(j, TOK_ROWS), TOK_ROWS)], sem)


def _dispatch_kernel(dest_ref, h_ref, xb_in, xb_out, sem):
    del xb_in
    tile = dest_ref.shape[0] // TOP_K

    def start(t, c):
        for k in range(TOP_K):
            _row_copy(h_ref, t * TOK_ROWS, xb_out, dest_ref[t * TOP_K + k], sem).start()
        return c

    def wait(t, c):
        for k in range(TOP_K):
            _row_copy(h_ref, 0, xb_out, 0, sem).wait()
        return c

    lax.fori_loop(0, tile, start, 0)
    lax.fori_loop(0, tile, wait, 0)


def _dispatch(h, dest, xb):
    T = h.shape[0] // TOK_ROWS
    hbm = pl.BlockSpec(memory_space=pl.ANY)
    n = MOE_TOK_TILE * TOP_K
    return pl.pallas_call(
        _dispatch_kernel,
        grid=(T // MOE_TOK_TILE,),
        in_specs=[pl.BlockSpec((n,), lambda i: (i,), memory_space=pltpu.SMEM),
                  pl.BlockSpec((MOE_TOK_TILE * TOK_ROWS, LANES), lambda i: (i, 0)), hbm],
        out_specs=hbm,
        out_shape=jax.ShapeDtypeStruct(xb.shape, xb.dtype),
        scratch_shapes=[pltpu.SemaphoreType.DMA(())],
        input_output_aliases={2: 0},
        compiler_params=_cparams(("arbitrary",)),
        name="moe_dispatch",
    )(dest.reshape(-1), h, xb)


def _moe_kernel(be_ref, nb_ref, x_ref, wup_ref, bup_ref, wdn_ref, bdn_ref, y_ref, wup_bf, wdn_bf):
    j = pl.program_id(0)

    @pl.when(j < nb_ref[0])
    def _():
        @pl.when(jnp.logical_or(j == 0, be_ref[j] != be_ref[jnp.maximum(j - 1, 0)]))
        def _():
            wup_bf[...] = wup_ref[0].astype(BF16)
            wdn_bf[...] = wdn_ref[0].astype(BF16)

        x = _load_token_tiles(x_ref, MOE_ROWS).astype(BF16)
        up = jnp.dot(x, wup_bf[...], preferred_element_type=F32) + bup_ref[0]
        x_glu = jnp.minimum(up[:, :D_FF], SWIGLU_LIMIT)
        x_lin = jnp.clip(up[:, D_FF:], -SWIGLU_LIMIT, SWIGLU_LIMIT)
        act = x_glu * jax.nn.sigmoid(SWIGLU_ALPHA * x_glu) * (x_lin + 1.0)
        _store_token_tiles(y_ref, jnp.dot(act.astype(BF16), wdn_bf[...], preferred_element_type=F32) + bdn_ref[0])

    @pl.when(j >= nb_ref[0])
    def _():
        y_ref[...] = jnp.zeros(y_ref.shape, y_ref.dtype)


def _moe_experts(xb, blk_expert, n_used, w_up, b_up, w_down, b_down):
    D = w_up.shape[1]
    blk = pl.BlockSpec((MOE_ROWS * TOK_ROWS, LANES), lambda j, be, nb: (j, 0))
    grid_spec = pltpu.PrefetchScalarGridSpec(
        num_scalar_prefetch=2,
        grid=(xb.shape[0] // (MOE_ROWS * TOK_ROWS),),
        in_specs=[blk,
                  pl.BlockSpec((1, D, 2 * D_FF), lambda j, be, nb: (be[j], 0, 0)),
                  pl.BlockSpec((1, 1, 2 * D_FF), lambda j, be, nb: (be[j], 0, 0)),
                  pl.BlockSpec((1, D_FF, D), lambda j, be, nb: (be[j], 0, 0)),
                  pl.BlockSpec((1, 1, D), lambda j, be, nb: (be[j], 0, 0))],
        out_specs=blk,
        scratch_shapes=[pltpu.VMEM((D, 2 * D_FF), BF16), pltpu.VMEM((D_FF, D), BF16)],
    )
    return pl.pallas_call(
        _moe_kernel,
        grid_spec=grid_spec,
        out_shape=jax.ShapeDtypeStruct(xb.shape, F32),
        compiler_params=_cparams(("arbitrary",)),
        name="moe_experts",
    )(blk_expert, n_used, xb, w_up, b_up[:, None, :], w_down, b_down[:, None, :])


def _combine_kernel(dest_ref, g_ref, x1_ref, yb_hbm, out_ref, buf, sem):
    tile = x1_ref.shape[0]

    def start(t, c):
        for k in range(TOP_K):
            _row_copy(yb_hbm, dest_ref[t * TOP_K + k], buf.at[k], t * TOK_ROWS, sem).start()
        return c

    def wait(t, c):
        for k in range(TOP_K):
            _row_copy(yb_hbm, 0, buf.at[k], 0, sem).wait()
        return c

    lax.fori_loop(0, tile, start, 0)
    lax.fori_loop(0, tile, wait, 0)
    acc = x1_ref[...]
    for k in range(TOP_K):
        acc = acc + g_ref[:, k:k + 1] * _load_token_tiles(buf.at[k], tile)
    out_ref[...] = acc


def _combine(x1, topg, dest, yb):
    T, D = x1.shape
    tile = MOE_TOK_TILE
    row = lambda n: pl.BlockSpec((tile, n), lambda i: (i, 0))
    return pl.pallas_call(
        _combine_kernel,
        grid=(T // tile,),
        in_specs=[pl.BlockSpec((tile * TOP_K,), lambda i: (i,), memory_space=pltpu.SMEM),
                  row(LANES), row(D), pl.BlockSpec(memory_space=pl.ANY)],
        out_specs=row(D),
        out_shape=jax.ShapeDtypeStruct((T, D), F32),
        scratch_shapes=[pltpu.VMEM((TOP_K, tile * TOK_ROWS, LANES), F32), pltpu.SemaphoreType.DMA(())],
        compiler_params=_cparams(("arbitrary",)),
        name="moe_combine",
    )(dest.reshape(-1), topg, x1, yb)


def _route(topi):
    T = topi.shape[0]
    onehot = (topi[:, :, None] == jnp.arange(N_EXPERTS, dtype=jnp.int32)).astype(jnp.int32).sum(axis=1)
    rank = jnp.cumsum(onehot, axis=0) - onehot
    counts = onehot.sum(axis=0)
    padded = (counts + MOE_ROWS - 1) // MOE_ROWS * MOE_ROWS
    pad_end = jnp.cumsum(padded)
    pad_start = pad_end - padded
    dest = pad_start[topi] + jnp.take_along_axis(rank, topi, axis=1)
    n_blk = (T * TOP_K + N_EXPERTS * (MOE_ROWS - 1) + MOE_ROWS - 1) // MOE_ROWS
    blk_start = jnp.arange(n_blk, dtype=jnp.int32) * MOE_ROWS
    blk_expert = jnp.minimum((pad_end[None, :] <= blk_start[:, None]).sum(axis=1), N_EXPERTS - 1)
    return ((dest * TOK_ROWS).astype(jnp.int32), blk_expert.astype(jnp.int32),
            (pad_end[-1:] // MOE_ROWS).astype(jnp.int32), n_blk)


SAMPLE_ROWS = 8


def _token_mixers(x, pw, cos, sin, tm, key_major):
    names = ('q', 'kvc', 'kvs', 'kvw', 'ks_bf', 'kw_bf', 'gates', 'rq', 'rk', 'rv', 'rg', 'mq', 'vs_t', 'vw_t')
    return dict(zip(names, _project(x, pw, cos, sin, tm, key_major)))


def kernel(x_prompt, x_sample, mem_prompt, cache_cmp_kv, cache_slc_kv, cache_win_kv, state_ret, cache_mem_kv, page_table, norm1_g, w_in, q_norm_g, k_norm_cmp_g, k_norm_slc_g, k_norm_win_g, cmp_pe_k, cmp_w1_k, cmp_w2_k, cmp_pe_v, cmp_w1_v, cmp_w2_v, ret_norm_g, mem_norm_g, w_mem_kv, mem_q_norm_g, mem_k_norm_g, w_out, norm2_g, w_router, b_router, w_up, b_up, w_down, b_down):
    B, S, D = x_prompt.shape
    DB, QS, _ = x_sample.shape
    n_mem = mem_prompt.shape[1]
    n_pages, page = page_table.shape[1], cache_cmp_kv.shape[2]
    past = n_pages * page
    NQ = SAMPLE_ROWS
    G, HD = NSA_GROUPS, HEAD_DIM
    win_rows = min(WINDOW, S)
    TP, TS = B * S, DB * NQ

    cos_p, sin_p = _rope_tables(jnp.arange(S, dtype=jnp.int32))
    cos_s, sin_s = _rope_tables(past + jnp.arange(NQ, dtype=jnp.int32))
    cos_s, sin_s = jnp.tile(cos_s, (DB, 1)), jnp.tile(sin_s, (DB, 1))

    xp = x_prompt.reshape(TP, D)
    xs = jnp.pad(x_sample, ((0, 0), (0, NQ - QS), (0, 0))).reshape(TS, D)
    unpad = lambda a: a.reshape(DB, NQ, -1)[:, :QS]
    outs = [[] for _ in range(9)]
    for l in range(w_in.shape[0]):
        pw = _prep_proj_weights(norm1_g[l], w_in[l], q_norm_g[l], k_norm_slc_g[l], k_norm_win_g[l], mem_q_norm_g[l])
        cw = _prep_compress_weights(cmp_pe_k[l], cmp_w1_k[l], cmp_w2_k[l], cmp_pe_v[l], cmp_w1_v[l], cmp_w2_v[l],
                                    k_norm_cmp_g[l])
        mw = _prep_mix_weights(w_out[l], ret_norm_g[l], norm2_g[l], w_router[l], b_router[l])

        t = _token_mixers(xp, pw, cos_p, sin_p, 512, True)
        kc, vct = _compress(t['kvc'].reshape(B, S, 256), cw)
        o_nsa = _nsa_prompt(B, t['q'], t['gates'], kc, vct, t['ks_bf'], t['vs_t'], t['kw_bf'], t['vw_t'])
        o_ret, r_p = _retention(t['rq'].reshape(B, S, 256), t['rk'].reshape(B, S, 256), t['rv'].reshape(B, S, 256),
                                jnp.zeros((B, 2, LANES, LANES), F32), RET_CHUNK, RET_CHUNK)
        mkv = _mem_kv(mem_prompt.reshape(B * n_mem, D), mem_norm_g[l], w_mem_kv[l], mem_k_norm_g[l])
        o_mem = _mem_attn(t['mq'].reshape(B, S, 256), mkv.reshape(B, n_mem, 512), 512)
        x1_p, h_p, topi_p, topg_p = _mix(xp, o_nsa.reshape(TP, -1), o_ret.reshape(TP, 256), t['rg'],
                                         o_mem.reshape(TP, 256), mw, 512)
        outs[0].append(t['kvc'].reshape(B, S, 2, G, HD))
        outs[1].append(t['kvs'].reshape(B, S, 2, G, HD))
        outs[2].append(t['kvw'].reshape(B, S, 2, G, HD)[:, S - win_rows:])
        outs[3].append(_pairs_to_state(r_p))
        outs[4].append(mkv.reshape(B, n_mem, 2, MEM_HEADS, HD))

        t = _token_mixers(xs, pw, cos_s, sin_s, TS, False)
        win = cache_win_kv[l]
        feat_major = lambda c: jnp.moveaxis(c, 1, -1).reshape(c.shape[0], 256, c.shape[1])
        o_nsa = _nsa_decode(t['q'].astype(F32).reshape(DB, NQ, -1), t['gates'].reshape(DB, NQ, LANES),
                            t['kvs'].reshape(DB, NQ, 256), t['kvw'].reshape(DB, NQ, 256),
                            feat_major(win), feat_major(cache_cmp_kv[l]), feat_major(cache_slc_kv[l]),
                            page_table, cw, QS)
        o_ret, r_s = _retention(t['rq'].reshape(DB, NQ, 256), t['rk'].reshape(DB, NQ, 256),
                                t['rv'].reshape(DB, NQ, 256), _state_to_pairs(state_ret[l].astype(F32)), QS, NQ)
        o_mem = _mem_attn(t['mq'].reshape(DB, NQ, 256), cache_mem_kv[l].reshape(DB, n_mem, 512), NQ)
        x1_s, h_s, topi_s, topg_s = _mix(xs, o_nsa.reshape(TS, -1).astype(BF16), o_ret.reshape(TS, 256), t['rg'],
                                         o_mem.reshape(TS, 256), mw, TS)
        kv5 = lambda a: unpad(a).reshape(DB, QS, 2, G, HD)
        outs[5].append(kv5(t['kvc']))
        outs[6].append(kv5(t['kvs']))
        outs[7].append(jnp.concatenate([win, kv5(t['kvw'])], axis=1)[:, QS:])
        outs[8].append(_pairs_to_state(r_s))

        valid = lambda a: unpad(a).reshape(DB * QS, -1)
        dest, blk_expert, n_used, n_blk = _route(jnp.concatenate([topi_p, valid(topi_s)], axis=0)[:, :TOP_K])
        xb = jnp.zeros((n_blk * MOE_ROWS * TOK_ROWS, LANES), F32)
        xb = _dispatch(h_p, dest[:TP], xb)
        h_s_valid = h_s.reshape(DB, NQ, TOK_ROWS, LANES)[:, :QS].reshape(DB * QS * TOK_ROWS, LANES)
        xb = _dispatch(h_s_valid, dest[TP:], xb)
        yb = _moe_experts(xb, blk_expert, n_used, w_up[l], b_up[l], w_down[l], b_down[l])
        xp = _combine(x1_p, topg_p, dest[:TP], yb)
        xs_new = _combine(valid(x1_s), valid(topg_s), dest[TP:], yb)
        xs = jnp.pad(xs_new.reshape(DB, QS, D), ((0, 0), (0, NQ - QS), (0, 0))).reshape(TS, D)
    y_sample = xs.reshape(DB, NQ, D)[:, :QS]
    return (xp.reshape(B, S, D), y_sample) + tuple(jnp.stack(o) for o in outs)
```

```python
import functools

import jax
import jax.numpy as jnp
from jax import lax
from jax.experimental import pallas as pl
from jax.experimental.pallas import tpu as pltpu

F32 = jnp.float32
BF16 = jnp.bfloat16

HEAD_DIM = 64
NSA_HEADS = 8
NSA_GROUPS = 2
NSA_HPG = NSA_HEADS // NSA_GROUPS
RET_HEADS = 4
MEM_HEADS = 4
CMP_BLK = 32
CMP_STRIDE = 16
CMP_HID = 256
SLC_BLK = 64
N_SEL = 16
WINDOW = 512
Q_BLK = 128
RET_CHUNK = 128
N_EXPERTS = 32
TOP_K = 4
D_FF = 1024
SWIGLU_LIMIT = 7.0
SWIGLU_ALPHA = 1.702
EPS = 1e-6
NEG = -1e30
BIG = 1e9
ROPE_BASE = 10000.0
SLC_RATIO = SLC_BLK // CMP_STRIDE
CMP_OVL = CMP_BLK // CMP_STRIDE - 1

Q_SCALE = HEAD_DIM ** -0.5 * 1.4426950408889634
LANES = 128
KEY_TILE = 512
MOE_ROWS = 512
VMEM_LIMIT = 56 * 1024 * 1024

C_Q = 0
C_KVC = C_Q + NSA_HEADS * LANES
C_KVS = C_KVC + 256
C_KVW = C_KVS + 256
C_GATE = C_KVW + 256
C_RQ = C_GATE + LANES
C_RK = C_RQ + 256
C_RV = C_RK + 256
C_RG = C_RV + 256
C_MQ = C_RG + 256
C_END = C_MQ + 256


def _cparams(sem):
    return pltpu.CompilerParams(dimension_semantics=sem, vmem_limit_bytes=VMEM_LIMIT)


def _lane(shape):
    return lax.broadcasted_iota(jnp.int32, shape, len(shape) - 1)


def _seg_rms(x, gain):
    lo = _lane(x.shape) < HEAD_DIM
    x2 = x * x
    s_lo = jnp.sum(jnp.where(lo, x2, 0.0), axis=-1, keepdims=True)
    s_hi = jnp.sum(jnp.where(lo, 0.0, x2), axis=-1, keepdims=True)
    ms = jnp.where(lo, s_lo, s_hi) * (1.0 / HEAD_DIM)
    return x * lax.rsqrt(ms + EPS) * gain


def _swap_halves(x):
    first = (_lane(x.shape) & (HEAD_DIM // 2)) == 0
    return jnp.where(first, pltpu.roll(x, LANES - HEAD_DIM // 2, 1), pltpu.roll(x, HEAD_DIM // 2, 1))


def _proj_kernel(x_ref, g1_ref, w_ref, cos_ref, sin_ref, qg_ref, ksg_ref, kwg_ref, mqg_ref,
                 q_out, kvc_out, kvs_out, kvw_out, ks_bf, kw_bf, gate_out,
                 rq_out, rk_out, rv_out, rg_out, mq_out, *extra, seq_tiles):
    x = x_ref[...]
    tm = x.shape[0]
    ms = jnp.mean(x * x, axis=-1, keepdims=True)
    xn = (x * lax.rsqrt(ms + EPS) * g1_ref[...]).astype(BF16)
    z = jnp.dot(xn, w_ref[...], preferred_element_type=F32)
    for i in range(NSA_HEADS):
        zq = z[:, C_Q + i * LANES:C_Q + (i + 1) * LANES]
        msq = jnp.sum(zq * zq, axis=-1, keepdims=True) * (1.0 / HEAD_DIM)
        qn = zq * lax.rsqrt(msq + EPS) * qg_ref[:, i * LANES:(i + 1) * LANES] * Q_SCALE
        if seq_tiles:
            q_out[i * LANES:(i + 1) * LANES, :] = qn.T.astype(BF16)
        else:
            q_out[:, i * LANES:(i + 1) * LANES] = qn.astype(BF16)
    kvc_out[...] = z[:, C_KVC:C_KVC + 256]
    ks = _seg_rms(z[:, C_KVS:C_KVS + LANES], ksg_ref[...])
    vs = z[:, C_KVS + LANES:C_KVS + 256]
    kvs_out[:, 0:LANES] = ks
    kvs_out[:, LANES:256] = vs
    kw = _seg_rms(z[:, C_KVW:C_KVW + LANES], kwg_ref[...])
    vw = z[:, C_KVW + LANES:C_KVW + 256]
    kvw_out[:, 0:LANES] = kw
    kvw_out[:, LANES:256] = vw
    gates = jax.nn.sigmoid(z[:, C_GATE:C_GATE + LANES])
    ks_bf[:, 0:LANES] = ks.astype(BF16)
    if seq_tiles:
        vs_t, vw_t = extra
        pos = (pl.program_id(0) % seq_tiles) * tm + lax.broadcasted_iota(jnp.int32, (tm, LANES), 0)
        ks_bf[:, LANES:256] = jnp.where(_lane((tm, LANES)) == pos // SLC_BLK, 1.0, 0.0).astype(BF16)
        kw_bf[...] = kw.astype(BF16)
        vs_t[...] = vs.T.astype(BF16)
        vw_t[...] = vw.T.astype(BF16)
        gate_out[...] = gates.T
    else:
        ks_bf[:, LANES:256] = vs.astype(BF16)
        kw_bf[:, 0:LANES] = kw.astype(BF16)
        kw_bf[:, LANES:256] = vw.astype(BF16)
        gate_out[...] = gates
    for c in range(2):
        sl = slice(c * LANES, (c + 1) * LANES)
        cos = cos_ref[:, sl]
        sin = sin_ref[:, sl]
        rq = z[:, C_RQ + c * LANES:C_RQ + (c + 1) * LANES]
        rk = z[:, C_RK + c * LANES:C_RK + (c + 1) * LANES]
        rq_out[:, sl] = rq * cos + _swap_halves(rq) * sin
        rk_out[:, sl] = (rk * cos + _swap_halves(rk) * sin) * (HEAD_DIM ** -0.5)
        mq = z[:, C_MQ + c * LANES:C_MQ + (c + 1) * LANES]
        mq_out[:, sl] = _seg_rms(mq, mqg_ref[...]).astype(BF16)
    rv_out[...] = z[:, C_RV:C_RV + 256]
    rg_out[...] = z[:, C_RG:C_RG + 256]


def _project(x, pw, cos, sin, tm, key_major):
    T, D = x.shape
    nt = cos.shape[0] // tm
    row = lambda n: (pl.BlockSpec((tm, n), lambda i: (i, 0)), (T, n))
    col = lambda n: (pl.BlockSpec((n, tm), lambda i: (0, i)), (n, T))
    full = lambda a: pl.BlockSpec(a.shape, lambda i: (0, 0))
    tab = pl.BlockSpec((tm, 256), lambda i: (i % nt, 0))
    tok = col if key_major else row
    outs = [
        (tok(NSA_HEADS * LANES), BF16),
        (row(256), F32), (row(256), F32), (row(256), F32),
        (row(256), BF16), (row(LANES if key_major else 256), BF16),
        (tok(LANES), F32),
        (row(256), F32), (row(256), F32), (row(256), F32), (row(256), F32),
        (row(256), BF16),
    ]
    if key_major:
        outs += [(col(LANES), BF16), (col(LANES), BF16)]
    return pl.pallas_call(
        functools.partial(_proj_kernel, seq_tiles=nt if key_major else 0),
        grid=(T // tm,),
        in_specs=[row(D)[0], full(pw['g1']), full(pw['w']), tab, tab,
                  full(pw['qg']), full(pw['ksg']), full(pw['kwg']), full(pw['mqg'])],
        out_specs=[spec for (spec, _), _ in outs],
        out_shape=[jax.ShapeDtypeStruct(shape, dt) for (_, shape), dt in outs],
        compiler_params=_cparams(("arbitrary",)),
        name="proj",
    )(x, pw['g1'], pw['w'], cos, sin, pw['qg'], pw['ksg'], pw['kwg'], pw['mqg'])


def _prep_proj_weights(norm1_g, w_in, q_norm_g, k_norm_slc_g, k_norm_win_g, mem_q_norm_g):
    D = w_in.shape[0]
    sizes = (512, 256, 256, 256, 24, 256, 256, 256, 256, 256)
    parts, off = [], 0
    for n in sizes:
        parts.append(w_in[:, off:off + n])
        off += n
    wq, wkvc, wkvs, wkvw, wg, wrq, wrk, wrv, wrg, wmq = parts
    zero = jnp.zeros((D, HEAD_DIM), w_in.dtype)
    qcols, qg = [], []
    gz = jnp.zeros((HEAD_DIM,), F32)
    for i in range(NSA_HEADS):
        wh = wq[:, i * HEAD_DIM:(i + 1) * HEAD_DIM]
        if i < NSA_HPG:
            qcols += [wh, zero]
            qg += [q_norm_g, gz]
        else:
            qcols += [zero, wh]
            qg += [gz, q_norm_g]
    wgp = jnp.pad(wg, ((0, 0), (0, LANES - wg.shape[1])))
    w = jnp.concatenate(qcols + [wkvc, wkvs, wkvw, wgp, wrq, wrk, wrv, wrg, wmq], axis=1).astype(BF16)
    two = lambda g: jnp.tile(g, 2)[None, :]
    return dict(g1=norm1_g[None, :], w=w, qg=jnp.concatenate(qg)[None, :],
                ksg=two(k_norm_slc_g), kwg=two(k_norm_win_g), mqg=two(mem_q_norm_g))


def _rope_tables(pos):
    half = HEAD_DIM // 2
    inv = ROPE_BASE ** (-jnp.arange(half, dtype=F32) / half)
    ang = pos.astype(F32)[:, None] * inv[None, :]
    cos, sin = jnp.cos(ang), jnp.sin(ang)
    cos = jnp.tile(jnp.concatenate([cos, cos], axis=1), (1, RET_HEADS))
    sin = jnp.tile(jnp.concatenate([-sin, sin], axis=1), (1, RET_HEADS))
    return cos, sin


def _compress_half(src_ref, pe_ref, wa_ref, wb_ref, w2_ref, nchunks):
    a_parts, b_parts = [], []
    for r in range(CMP_STRIDE):
        xr = src_ref[pl.ds(r, nchunks, stride=CMP_STRIDE), :]
        a_parts.append((xr + pe_ref[r:r + 1, :]).astype(BF16))
        b_parts.append((xr + pe_ref[CMP_STRIDE + r:CMP_STRIDE + r + 1, :]).astype(BF16))
    ha = jnp.dot(jnp.concatenate(a_parts, axis=1), wa_ref[...], preferred_element_type=F32)
    hb = jnp.dot(jnp.concatenate(b_parts, axis=1), wb_ref[...], preferred_element_type=F32)
    h = ha + pltpu.roll(hb, nchunks - 1, 0)
    act = jax.nn.gelu(h).astype(BF16)
    return jnp.dot(act, w2_ref[...], preferred_element_type=F32)


def _compress_kernel(kv_ref, pe_ref, wa_ref, wb_ref, w2_ref, kg_ref, kc_out, vct_out):
    nchunks = kc_out.shape[1]
    y = _compress_half(kv_ref.at[0], pe_ref.at[0], wa_ref.at[0], wb_ref.at[0], w2_ref.at[0], nchunks)

    @pl.when(pl.program_id(1) == 0)
    def _():
        kc_out[0] = _seg_rms(y, kg_ref[...]).astype(BF16)

    @pl.when(pl.program_id(1) == 1)
    def _():
        vct_out[0] = y.T.astype(BF16)


def _compress(kvc, cw):
    B, T, _ = kvc.shape
    nchunks = T // CMP_STRIDE
    per_kv = lambda a: pl.BlockSpec((1,) + a.shape[1:], lambda b, j: (j,) + (0,) * (a.ndim - 1))
    return pl.pallas_call(
        _compress_kernel,
        grid=(B, 2),
        in_specs=[pl.BlockSpec((1, T, LANES), lambda b, j: (b, 0, j)),
                  per_kv(cw['pe']), per_kv(cw['wa']), per_kv(cw['wb']), per_kv(cw['w2']),
                  pl.BlockSpec((1, LANES), lambda b, j: (0, 0))],
        out_specs=[pl.BlockSpec((1, nchunks, LANES), lambda b, j: (b, 0, 0)),
                   pl.BlockSpec((1, LANES, nchunks), lambda b, j: (b, 0, 0))],
        out_shape=[jax.ShapeDtypeStruct((B, nchunks, LANES), BF16),
                   jax.ShapeDtypeStruct((B, LANES, nchunks), BF16)],
        compiler_params=_cparams(("arbitrary", "arbitrary")),
        name="compress",
    )(kvc, cw['pe'], cw['wa'], cw['wb'], cw['w2'], cw['kg'])


def _prep_compress_weights(pe_k, w1_k, w2_k, pe_v, w1_v, w2_v, k_norm_g):
    eye = jnp.eye(NSA_GROUPS, dtype=F32)

    def one(pe, w1, w2):
        w1r = w1.reshape(CMP_BLK, HEAD_DIM, CMP_HID)
        wf = jnp.einsum('gh,rdc->rgdhc', eye, w1r).reshape(CMP_BLK * LANES, NSA_GROUPS * CMP_HID)
        w2f = jnp.einsum('gh,cd->gchd', eye, w2).reshape(NSA_GROUPS * CMP_HID, LANES)
        half = CMP_STRIDE * LANES
        return jnp.tile(pe, (1, NSA_GROUPS)), wf[:half].astype(BF16), wf[half:].astype(BF16), w2f.astype(BF16)

    k, v = one(pe_k, w1_k, w2_k), one(pe_v, w1_v, w2_v)
    st = lambda i: jnp.stack([k[i], v[i]])
    return dict(pe=st(0), wa=st(1), wb=st(2), w2=st(3), kg=jnp.tile(k_norm_g, 2)[None, :])


def _dot_nt(a, b):
    return lax.dot_general(a, b, (((1,), (1,)), ((), ())), preferred_element_type=F32)


def _split3(x):
    hi = x.astype(BF16)
    r = x - hi.astype(F32)
    mid = r.astype(BF16)
    lo = (r - mid.astype(F32)).astype(BF16)
    return hi, mid, lo


def _select_bias(ps, cur, rounds, axis, cur_in_range=True):
    j = lax.broadcasted_iota(jnp.int32, ps.shape, axis)
    last = cur if cur_in_range else cur - 1
    key = jnp.where(j <= last, ps, NEG)
    for forced in (0, last, cur - 1):
        key = jnp.where(j == forced, BIG, key)
    bias = jnp.full(ps.shape, NEG, F32)
    for _ in range(rounds):
        m = jnp.max(key, axis=axis, keepdims=True)
        idx = jnp.min(jnp.where(key == m, j, LANES), axis=axis, keepdims=True)
        pick = j == idx
        bias = jnp.where(pick, 0.0, bias)
        key = jnp.where(pick, -jnp.inf, key)
    return bias


def _cmp_branch(qg, qpos, kc, vc):
    s = _dot_nt(qg, kc)
    cend = _lane((1, kc.shape[0])) * CMP_STRIDE + (CMP_BLK - 1)
    s = jnp.where(cend <= qpos, s, NEG)
    m = jnp.max(s, axis=-1, keepdims=True)
    p = jnp.exp2(s - m)
    p = p * (1.0 / jnp.sum(p, axis=-1, keepdims=True))
    p = jnp.where(qpos >= CMP_BLK - 1, p, 0.0)
    return p, jnp.dot(p.astype(BF16), vc, preferred_element_type=F32)


def _sel_tile(qaug, k_tile, v_tile, blk0, mask, m_sc, l_sc, acc_sc):
    n = k_tile.shape[0]
    blk = blk0 + lax.broadcasted_iota(jnp.int32, (n, LANES), 0) // SLC_BLK
    onehot = jnp.where(_lane((n, LANES)) == blk, 1.0, 0.0).astype(BF16)
    s = _dot_nt(qaug, jnp.concatenate([k_tile, onehot], axis=1))
    if mask is not None:
        s = jnp.where(mask, s, NEG)
    _online_update(s, v_tile, m_sc, l_sc, acc_sc)


def _online_update(s, v_tile, m_sc, l_sc, acc_sc):
    m_old = m_sc[...]
    m_new = jnp.maximum(m_old, jnp.max(s, axis=-1, keepdims=True))
    alpha = jnp.exp(m_old - m_new)
    p = jnp.exp(s - m_new)
    l_sc[...] = alpha * l_sc[...] + jnp.sum(p, axis=-1, keepdims=True)
    acc_sc[...] = alpha * acc_sc[...] + jnp.dot(p.astype(BF16), v_tile, preferred_element_type=F32)
    m_sc[...] = m_new


def _gate_heads(gates, g, oc, osel, ow, nq):
    heads = []
    for h in range(NSA_HPG):
        rs = slice(h * nq, (h + 1) * nq)
        c0 = (NSA_HPG * g + h) * 3
        heads.append(gates[:, c0:c0 + 1] * oc[rs] + gates[:, c0 + 1:c0 + 2] * osel[rs]
                     + gates[:, c0 + 2:c0 + 3] * ow[rs])
    return heads


def _store_heads(heads, o_ref):
    low = _lane(heads[0].shape) < HEAD_DIM
    for pr in range(NSA_HEADS // 2):
        even, odd = heads[2 * pr], heads[2 * pr + 1]
        if pr < NSA_GROUPS:
            odd = pltpu.roll(odd, HEAD_DIM, 1)
        else:
            even = pltpu.roll(even, HEAD_DIM, 1)
        o_ref[0, :, pr * LANES:(pr + 1) * LANES] = jnp.where(low, even, odd).astype(o_ref.dtype)


def _nsa_decode_kernel(pt_ref, q_ref, gate_ref, ksn_ref, kwn_ref, win_ref, pool_c, pool_s,
                       pe_ref, wa_ref, wb_ref, w2_ref, kg_ref, mmap_ref, hot_ref, o_ref,
                       cbuf, sbuf, xrow, sems, *, page, n_new):
    b = pl.program_id(0)
    past = sbuf.shape[1]
    npages = past // page
    nq = q_ref.shape[1]
    dot = functools.partial(jnp.dot, preferred_element_type=F32)

    def page_copies(p, pg):
        dst = pl.ds(pl.multiple_of(p * page, page), page)
        return (pltpu.make_async_copy(pool_c.at[pg], cbuf.at[:, dst], sems.at[0]),
                pltpu.make_async_copy(pool_s.at[pg], sbuf.at[:, dst], sems.at[1]))

    def start(p, c):
        for cp in page_copies(p, pt_ref[b * npages + p]):
            cp.start()
        return c

    def wait_on(which):
        def wait(p, c):
            page_copies(p, 0)[which].wait()
            return c
        return wait

    lax.fori_loop(0, npages, start, 0)
    lax.fori_loop(0, npages, wait_on(0), 0)

    tchunk = 4 * LANES
    for half in range(2):
        for c in range(past // tchunk):
            xrow[half, c * tchunk:(c + 1) * tchunk, :] = cbuf[half * LANES:(half + 1) * LANES,
                                                              c * tchunk:(c + 1) * tchunk].T
    nchunks = past // CMP_STRIDE
    kc = _seg_rms(_compress_half(xrow.at[0], pe_ref.at[0], wa_ref.at[0], wb_ref.at[0], w2_ref.at[0], nchunks),
                  kg_ref[...]).astype(BF16)
    vc = _compress_half(xrow.at[1], pe_ref.at[1], wa_ref.at[1], wb_ref.at[1], w2_ref.at[1], nchunks).astype(BF16)

    rows = NSA_HPG * nq
    srow = lax.broadcasted_iota(jnp.int32, (rows, 1), 0) % nq
    qpos = past + srow
    cur = (past + lax.broadcasted_iota(jnp.int32, (nq, 1), 0)) // SLC_BLK
    gates = gate_ref[0]
    pad_keys = lambda x: jnp.concatenate([x, jnp.zeros((LANES - nq, LANES), F32)], axis=0).astype(BF16)
    new_ok = _lane((1, LANES)) <= jnp.minimum(srow, n_new - 1)
    ks_new, vs_new = pad_keys(ksn_ref[0, :, 0:LANES]), pad_keys(ksn_ref[0, :, LANES:256])
    kw_new, vw_new = pad_keys(kwn_ref[0, :, 0:LANES]), pad_keys(kwn_ref[0, :, LANES:256])

    lax.fori_loop(0, npages, wait_on(1), 0)
    k_aug_t = jnp.concatenate([sbuf[0:LANES, :].astype(BF16), hot_ref[...]], axis=0)
    vs_t = sbuf[LANES:256, :].astype(BF16)
    nwin = win_ref.shape[2]
    kw_t, vw_t = win_ref[0, 0:LANES, :].astype(BF16), win_ref[0, LANES:256, :].astype(BF16)

    qgs, ocs, p4s = [], [], []
    for g in range(NSA_GROUPS):
        qg = jnp.concatenate(
            [q_ref[0, :, (NSA_HPG * g + h) * LANES:(NSA_HPG * g + h + 1) * LANES] for h in range(NSA_HPG)],
            axis=0).astype(BF16)
        p, oc = _cmp_branch(qg, qpos, kc, vc)
        qgs.append(qg)
        ocs.append(oc)
        p4s.append(p[0:nq] + p[nq:2 * nq] + p[2 * nq:3 * nq] + p[3 * nq:4 * nq])
    p4 = jnp.concatenate(p4s, axis=0)
    ps_t = sum(_dot_nt(mmap_ref[...], part) for part in _split3(p4))
    cur = (past + _lane((1, NSA_GROUPS * nq)) % nq) // SLC_BLK
    bias_t = _select_bias(ps_t, cur, N_SEL - 1, 0, cur_in_range=False)
    bias_all = jnp.concatenate([bias_t, jnp.zeros((LANES, LANES - NSA_GROUPS * nq), F32)], axis=1).T

    heads = []
    for g in range(NSA_GROUPS):
        qg, oc = qgs[g], ocs[g]
        bias = bias_all[g * nq:(g + 1) * nq]
        qaug = jnp.concatenate([qg, jnp.concatenate([bias] * NSA_HPG, axis=0).astype(BF16)], axis=1)
        s_old = dot(qaug, k_aug_t)
        s_new = jnp.where(new_ok, _dot_nt(qg, ks_new), NEG)
        m = jnp.maximum(jnp.max(s_old, axis=-1, keepdims=True), jnp.max(s_new, axis=-1, keepdims=True))
        p_old, p_new = jnp.exp2(s_old - m), jnp.exp2(s_new - m)
        inv = 1.0 / (jnp.sum(p_old, axis=-1, keepdims=True) + jnp.sum(p_new, axis=-1, keepdims=True))
        osel = (_dot_nt(p_old.astype(BF16), vs_t) + dot(p_new.astype(BF16), vs_new)) * inv
        sw_old = jnp.where(_lane((1, nwin)) + (WINDOW - nwin) > srow, dot(qg, kw_t), NEG)
        sw_new = jnp.where(new_ok, _dot_nt(qg, kw_new), NEG)
        mw = jnp.maximum(jnp.max(sw_old, axis=-1, keepdims=True), jnp.max(sw_new, axis=-1, keepdims=True))
        pw_old, pw_new = jnp.exp2(sw_old - mw), jnp.exp2(sw_new - mw)
        invw = 1.0 / (jnp.sum(pw_old, axis=-1, keepdims=True) + jnp.sum(pw_new, axis=-1, keepdims=True))
        ow = (_dot_nt(pw_old.astype(BF16), vw_t) + dot(pw_new.astype(BF16), vw_new)) * invw
        heads += _gate_heads(gates, g, oc, osel, ow, nq)
    _store_heads(heads, o_ref)


def _nsa_decode(q, gates, ksn, kwn, win_t, pool_c, pool_s, page_table, cw, n_new):
    DB, nq, _ = q.shape
    n_pages = page_table.shape[1]
    page = pool_c.shape[2]
    past = n_pages * page
    nchunks = past // CMP_STRIDE
    mmap = _importance_map(nchunks)
    hot = (jnp.arange(LANES, dtype=jnp.int32)[:, None] == jnp.arange(past, dtype=jnp.int32)[None, :] // SLC_BLK)
    hot = hot.astype(BF16)
    per_seq = lambda a: pl.BlockSpec((1,) + a.shape[1:], lambda b, pt: (b,) + (0,) * (a.ndim - 1))
    const = lambda a: pl.BlockSpec(a.shape, lambda b, pt: (0,) * a.ndim)
    hbm = pl.BlockSpec(memory_space=pl.ANY)
    grid_spec = pltpu.PrefetchScalarGridSpec(
        num_scalar_prefetch=1,
        grid=(DB,),
        in_specs=[per_seq(q), per_seq(gates), per_seq(ksn), per_seq(kwn), per_seq(win_t), hbm, hbm,
                  const(cw['pe']), const(cw['wa']), const(cw['wb']), const(cw['w2']), const(cw['kg']), const(mmap),
                  const(hot)],
        out_specs=pl.BlockSpec((1, nq, NSA_HEADS * HEAD_DIM), lambda b, pt: (b, 0, 0)),
        scratch_shapes=[pltpu.VMEM((256, past), F32), pltpu.VMEM((256, past), F32),
                        pltpu.VMEM((2, past, LANES), F32), pltpu.SemaphoreType.DMA((2,))],
    )
    return pl.pallas_call(
        functools.partial(_nsa_decode_kernel, page=page, n_new=n_new),
        grid_spec=grid_spec,
        out_shape=jax.ShapeDtypeStruct((DB, nq, NSA_HEADS * HEAD_DIM), F32),
        compiler_params=_cparams(("arbitrary",)),
        name="nsa_sample",
    )(page_table.reshape(-1), q, gates, ksn, kwn, win_t, pool_c, pool_s,
      cw['pe'], cw['wa'], cw['wb'], cw['w2'], cw['kg'], mmap, hot)


def _nsa_sample_kernel(pt_ref, q_ref, gate_ref, ksn_ref, kwn_ref, win_ref, pool_c, pool_s,
                       pe_ref, wa_ref, wb_ref, w2_ref, kg_ref, mmap_ref, o_ref,
                       cbuf, sbuf, sems, m_sc, l_sc, acc_sc, *, page, n_new):
    b = pl.program_id(0)
    past = sbuf.shape[0]
    npages = past // page
    nq = q_ref.shape[1]

    def page_copies(p, pg):
        dst = pl.ds(pl.multiple_of(p * page, page), page)
        return (pltpu.make_async_copy(pool_c.at[pg, :, 0:LANES], cbuf.at[0, dst], sems.at[0]),
                pltpu.make_async_copy(pool_c.at[pg, :, LANES:256], cbuf.at[1, dst], sems.at[1]),
                pltpu.make_async_copy(pool_s.at[pg], sbuf.at[dst], sems.at[2]))

    def start(p, c):
        for cp in page_copies(p, pt_ref[b * npages + p]):
            cp.start()
        return c

    def wait(p, c):
        for cp in page_copies(p, 0):
            cp.wait()
        return c

    lax.fori_loop(0, npages, start, 0)
    lax.fori_loop(0, npages, wait, 0)

    nchunks = past // CMP_STRIDE
    kc = _seg_rms(_compress_half(cbuf.at[0], pe_ref.at[0], wa_ref.at[0], wb_ref.at[0], w2_ref.at[0], nchunks),
                  kg_ref[...]).astype(BF16)
    vc = _compress_half(cbuf.at[1], pe_ref.at[1], wa_ref.at[1], wb_ref.at[1], w2_ref.at[1], nchunks).astype(BF16)

    rows = NSA_HPG * nq
    srow = lax.broadcasted_iota(jnp.int32, (rows, 1), 0) % nq
    qpos = past + srow
    cur = (past + lax.broadcasted_iota(jnp.int32, (nq, 1), 0)) // SLC_BLK
    gates = gate_ref[0]
    pad_keys = lambda x: jnp.concatenate([x, jnp.zeros((LANES - nq, LANES), F32)], axis=0).astype(BF16)
    new_ok = _lane((1, LANES)) <= jnp.minimum(srow, n_new - 1)
    heads = []
    for g in range(NSA_GROUPS):
        qg = jnp.concatenate(
            [q_ref[0, :, (NSA_HPG * g + h) * LANES:(NSA_HPG * g + h + 1) * LANES] for h in range(NSA_HPG)],
            axis=0).astype(BF16)
        p, oc = _cmp_branch(qg, qpos, kc, vc)
        p4 = p[0:nq] + p[nq:2 * nq] + p[2 * nq:3 * nq] + p[3 * nq:4 * nq]
        ps = sum(_dot_nt(part, mmap_ref[...]) for part in _split3(p4))
        bias = _select_bias(ps, cur, N_SEL - 1, 1, cur_in_range=False)
        qaug = jnp.concatenate([qg, jnp.concatenate([bias] * NSA_HPG, axis=0).astype(BF16)], axis=1)
        m_sc[...] = jnp.full(m_sc.shape, NEG, F32)
        l_sc[...] = jnp.zeros(l_sc.shape, F32)
        acc_sc[...] = jnp.zeros(acc_sc.shape, F32)

        def body(kt, c, qaug=qaug):
            k0 = pl.multiple_of(kt * KEY_TILE, KEY_TILE)
            _sel_tile(qaug, sbuf[pl.ds(k0, KEY_TILE), 0:LANES].astype(BF16),
                      sbuf[pl.ds(k0, KEY_TILE), LANES:256].astype(BF16),
                      kt * (KEY_TILE // SLC_BLK), None, m_sc, l_sc, acc_sc)
            return c

        lax.fori_loop(0, past // KEY_TILE, body, 0)
        s_new = jnp.where(new_ok, _dot_nt(qg, pad_keys(ksn_ref[0, :, 0:LANES])), NEG)
        _online_update(s_new, pad_keys(ksn_ref[0, :, LANES:256]), m_sc, l_sc, acc_sc)
        osel = acc_sc[...] * (1.0 / l_sc[...])
        nwin = win_ref.shape[1]
        sw_old = jnp.where(_lane((1, nwin)) + (WINDOW - nwin) > srow,
                           _dot_nt(qg, win_ref[0, :, 0:LANES].astype(BF16)), NEG)
        sw_new = jnp.where(new_ok, _dot_nt(qg, pad_keys(kwn_ref[0, :, 0:LANES])), NEG)
        sw = jnp.concatenate([sw_old, sw_new], axis=1)
        mw = jnp.max(sw, axis=-1, keepdims=True)
        pw = jnp.exp(sw - mw)
        ow = (jnp.dot(pw[:, :nwin].astype(BF16), win_ref[0, :, LANES:256].astype(BF16), preferred_element_type=F32)
              + jnp.dot(pw[:, nwin:].astype(BF16), pad_keys(kwn_ref[0, :, LANES:256]), preferred_element_type=F32))
        ow = ow * (1.0 / jnp.sum(pw, axis=-1, keepdims=True))
        heads += _gate_heads(gates, g, oc, osel, ow, nq)
    _store_heads(heads, o_ref)


def _nsa_sample(q, gates, ksn, kwn, win, pool_c, pool_s, page_table, cw, n_new):
    DB, nq, _ = q.shape
    n_pages = page_table.shape[1]
    page = pool_c.shape[1]
    past = n_pages * page
    nchunks = past // CMP_STRIDE
    rows = NSA_HPG * nq
    mmap = _importance_map(nchunks)
    per_seq = lambda a: pl.BlockSpec((1,) + a.shape[1:], lambda b, pt: (b,) + (0,) * (a.ndim - 1))
    const = lambda a: pl.BlockSpec(a.shape, lambda b, pt: (0,) * a.ndim)
    hbm = pl.BlockSpec(memory_space=pl.ANY)
    grid_spec = pltpu.PrefetchScalarGridSpec(
        num_scalar_prefetch=1,
        grid=(DB,),
        in_specs=[per_seq(q), per_seq(gates), per_seq(ksn), per_seq(kwn), per_seq(win), hbm, hbm,
                  const(cw['pe']), const(cw['wa']), const(cw['wb']), const(cw['w2']), const(cw['kg']), const(mmap)],
        out_specs=pl.BlockSpec((1, nq, NSA_HEADS * HEAD_DIM), lambda b, pt: (b, 0, 0)),
        scratch_shapes=[pltpu.VMEM((2, past, LANES), F32), pltpu.VMEM((past, 256), F32),
                        pltpu.SemaphoreType.DMA((3,)),
                        pltpu.VMEM((rows, 1), F32), pltpu.VMEM((rows, 1), F32), pltpu.VMEM((rows, LANES), F32)],
    )
    return pl.pallas_call(
        functools.partial(_nsa_sample_kernel, page=page, n_new=n_new),
        grid_spec=grid_spec,
        out_shape=jax.ShapeDtypeStruct((DB, nq, NSA_HEADS * HEAD_DIM), F32),
        compiler_params=_cparams(("arbitrary",)),
        name="nsa_sample",
    )(page_table.reshape(-1), q, gates, ksn, kwn, win, pool_c, pool_s,
      cw['pe'], cw['wa'], cw['wb'], cw['w2'], cw['kg'], mmap)


def _nsa_prompt_kernel(q_ref, gate_ref, kc_ref, vct_ref, ks_ref, vst_ref, kw_ref, vwt_ref, mmap_ref, o_ref,
                       m_sc, l_sc, acc_sc, sa_sc, sb_sc):
    i = pl.program_id(1)
    s0 = i * Q_BLK
    cols = NSA_HPG * Q_BLK
    qpos = s0 + _lane((1, cols)) % Q_BLK
    cur = (s0 + _lane((1, Q_BLK))) // SLC_BLK
    kc, vct = kc_ref[0], vct_ref[0]
    ncmp = kc.shape[0]
    n_full = s0 // KEY_TILE
    w0 = pl.multiple_of(jnp.maximum(s0 - WINDOW, 0), Q_BLK)
    nw = WINDOW + Q_BLK
    dot = functools.partial(jnp.dot, preferred_element_type=F32)

    def softmax_keys(s):
        p = jnp.exp2(s - jnp.max(s, axis=0, keepdims=True))
        return p, 1.0 / jnp.sum(p, axis=0, keepdims=True)

    qaugs = []
    branch = []
    for g in range(NSA_GROUPS):
        qt = jnp.concatenate([q_ref[(NSA_HPG * g + h) * LANES:(NSA_HPG * g + h + 1) * LANES, :]
                              for h in range(NSA_HPG)], axis=1)
        s = dot(kc, qt)
        n_idx = lax.broadcasted_iota(jnp.int32, (ncmp, cols), 0)
        s = jnp.where(n_idx * CMP_STRIDE + (CMP_BLK - 1) <= qpos, s, NEG)
        p, inv = softmax_keys(s)
        p = jnp.where(qpos >= CMP_BLK - 1, p * inv, 0.0)
        oc = dot(vct, p.astype(BF16))
        p4 = p[:, 0:Q_BLK] + p[:, Q_BLK:2 * Q_BLK] + p[:, 2 * Q_BLK:3 * Q_BLK] + p[:, 3 * Q_BLK:4 * Q_BLK]
        ps_t = sum(dot(mmap_ref[...], part) for part in _split3(p4))
        bias = _select_bias(ps_t, cur, N_SEL, 0).astype(BF16)
        qaugs.append(jnp.concatenate([qt, jnp.concatenate([bias] * NSA_HPG, axis=1)], axis=0))
        sw = dot(kw_ref[0, pl.ds(w0, nw), :], qt)
        d = qpos - (w0 + lax.broadcasted_iota(jnp.int32, (nw, cols), 0))
        sw = jnp.where(d >= 0, jnp.where(d < WINDOW, sw, NEG), NEG)
        pw, invw = softmax_keys(sw)
        ow = dot(vwt_ref[:, pl.ds(w0, nw)], pw.astype(BF16)) * invw
        branch.append((oc, ow))
        m_sc[g] = jnp.full(m_sc.shape[1:], NEG, F32)
        l_sc[g] = jnp.zeros(l_sc.shape[1:], F32)
        acc_sc[g] = jnp.zeros(acc_sc.shape[1:], F32)

    def scores(kt, s_ref):
        k_aug = ks_ref[0, pl.ds(pl.multiple_of(kt * KEY_TILE, KEY_TILE), KEY_TILE), :]
        for g in range(NSA_GROUPS):
            s_ref[g] = dot(k_aug, qaugs[g])

    def consume(kt, s_ref, causal):
        k0 = pl.multiple_of(kt * KEY_TILE, KEY_TILE)
        v_t = vst_ref[:, pl.ds(k0, KEY_TILE)]
        for g in range(NSA_GROUPS):
            s = s_ref[g]
            if causal:
                s = jnp.where(k0 + lax.broadcasted_iota(jnp.int32, s.shape, 0) <= qpos, s, NEG)
            m_old = m_sc[g]
            m_new = jnp.maximum(m_old, jnp.max(s, axis=0, keepdims=True))
            alpha = jnp.exp2(m_old - m_new)
            p = jnp.exp2(s - m_new)
            l_sc[g] = alpha * l_sc[g] + jnp.sum(p, axis=0, keepdims=True)
            acc_sc[g] = alpha * acc_sc[g] + dot(v_t, p.astype(BF16))
            m_sc[g] = m_new

    def two_tiles(u, c):
        scores(2 * u + 1, sb_sc)
        consume(2 * u, sa_sc, False)
        scores(2 * u + 2, sa_sc)
        consume(2 * u + 1, sb_sc, False)
        return c

    scores(0, sa_sc)
    lax.fori_loop(0, n_full // 2, two_tiles, 0)

    @pl.when(n_full % 2 == 1)
    def _():
        scores(n_full, sb_sc)
        consume(n_full - 1, sa_sc, False)
        consume(n_full, sb_sc, True)

    @pl.when(n_full % 2 == 0)
    def _():
        consume(n_full, sa_sc, True)

    pieces = []
    for g in range(NSA_GROUPS):
        oc, ow = branch[g]
        osel = acc_sc[g] * (1.0 / l_sc[g])
        for h in range(NSA_HPG):
            cs = slice(h * Q_BLK, (h + 1) * Q_BLK)
            c0 = (NSA_HPG * g + h) * 3
            o_h = (gate_ref[c0:c0 + 1, :] * oc[:, cs] + gate_ref[c0 + 1:c0 + 2, :] * osel[:, cs]
                   + gate_ref[c0 + 2:c0 + 3, :] * ow[:, cs])
            pieces.append(o_h[g * HEAD_DIM:(g + 1) * HEAD_DIM, :])
    o_ref[0] = jnp.concatenate(pieces, axis=0).T.astype(o_ref.dtype)


def _importance_map(ncmp):
    j = jnp.arange(LANES)[:, None]
    n = jnp.arange(ncmp)[None, :]
    return ((n >= SLC_RATIO * j - CMP_OVL) & (n < SLC_RATIO * j + SLC_RATIO)).astype(BF16)


def _nsa_prompt(B, qt, gates_t, kc, vct, ks_aug, vs_t, kw, vw_t):
    S = qt.shape[1] // B
    nq = S // Q_BLK
    ncmp = kc.shape[1]
    cols = NSA_HPG * Q_BLK
    mmap = _importance_map(ncmp)
    per_block = lambda n: pl.BlockSpec((n, Q_BLK), lambda b, i: (0, b * nq + i))
    seq_rows = lambda n: pl.BlockSpec((1, S, n), lambda b, i: (b, 0, 0))
    seq_cols = pl.BlockSpec((LANES, S), lambda b, i: (0, b))
    return pl.pallas_call(
        _nsa_prompt_kernel,
        grid=(B, nq),
        in_specs=[per_block(qt.shape[0]), per_block(LANES),
                  pl.BlockSpec((1, ncmp, LANES), lambda b, i: (b, 0, 0)),
                  pl.BlockSpec((1, LANES, ncmp), lambda b, i: (b, 0, 0)),
                  seq_rows(256), seq_cols, seq_rows(LANES), seq_cols,
                  pl.BlockSpec(mmap.shape, lambda b, i: (0, 0))],
        out_specs=pl.BlockSpec((1, Q_BLK, NSA_HEADS * HEAD_DIM), lambda b, i: (b, i, 0)),
        out_shape=jax.ShapeDtypeStruct((B, S, NSA_HEADS * HEAD_DIM), BF16),
        scratch_shapes=[pltpu.VMEM((NSA_GROUPS, 1, cols), F32), pltpu.VMEM((NSA_GROUPS, 1, cols), F32),
                        pltpu.VMEM((NSA_GROUPS, LANES, cols), F32),
                        pltpu.VMEM((NSA_GROUPS, KEY_TILE, cols), F32), pltpu.VMEM((NSA_GROUPS, KEY_TILE, cols), F32)],
        compiler_params=_cparams(("arbitrary", "arbitrary")),
        name="nsa_prompt",
    )(qt, gates_t, kc, vct, ks_aug.reshape(B, S, 256), vs_t, kw.reshape(B, S, LANES), vw_t, mmap)


def _ret_kernel(rq_ref, rk_ref, rv_ref, r0_ref, dmask_ref, xi_ref, zeta_ref, dec_ref, o_ref, rout_ref, r_sc):
    @pl.when(pl.program_id(1) == 0)
    def _():
        r_sc[...] = r0_ref[0]

    C = rq_ref.shape[1]
    low = _lane((C, LANES)) < HEAD_DIM
    diag = lax.broadcasted_iota(jnp.int32, (LANES, LANES), 0) // HEAD_DIM == _lane((LANES, LANES)) // HEAD_DIM
    for pr in range(RET_HEADS // 2):
        sl = slice(pr * LANES, (pr + 1) * LANES)
        k = rk_ref[0, :, sl]
        qb, kb, vb = rq_ref[0, :, sl].astype(BF16), k.astype(BF16), rv_ref[0, :, sl].astype(BF16)
        zero = jnp.zeros_like(qb)
        s0 = _dot_nt(jnp.where(low, qb, zero), kb) * dmask_ref[2 * pr]
        s1 = _dot_nt(jnp.where(low, zero, qb), kb) * dmask_ref[2 * pr + 1]
        o = jnp.where(low, jnp.dot(s0.astype(BF16), vb, preferred_element_type=F32),
                      jnp.dot(s1.astype(BF16), vb, preferred_element_type=F32))
        r = r_sc[pr]
        o_ref[0, :, sl] = o + jnp.dot(qb, r.astype(BF16), preferred_element_type=F32) * xi_ref[:, sl]
        kz = (k * zeta_ref[:, sl]).astype(BF16)
        upd = lax.dot_general(kz, vb, (((0,), (0,)), ((), ())), preferred_element_type=F32)
        r_sc[pr] = dec_ref[:, sl] * r + jnp.where(diag, upd, 0.0)
    rout_ref[0] = r_sc[...]


def _ret_tables(c_true, c_pad):
    lg = jnp.log(1.0 - 2.0 ** (-5.0 - jnp.arange(RET_HEADS, dtype=F32)))
    idx = jnp.arange(c_pad, dtype=F32)
    diff = idx[:, None] - idx[None, :]
    dmask = jnp.where(diff >= 0, jnp.exp(jnp.maximum(diff, 0.0)[None] * lg[:, None, None]), 0.0)
    lanes = lambda a: jnp.repeat(a, HEAD_DIM, axis=-1)
    xi = lanes(jnp.exp((idx + 1.0)[:, None] * lg[None, :]))
    zeta = lanes(jnp.exp((c_true - 1.0 - idx)[:, None] * lg[None, :]))
    dec = lanes(jnp.exp(c_true * lg)[None, :])
    return dmask, xi, zeta, dec


def _retention(rq, rk, rv, r0, c_true, c_pad):
    B, T, _ = rq.shape
    dmask, xi, zeta, dec = _ret_tables(c_true, c_pad)
    row = pl.BlockSpec((1, c_pad, 256), lambda b, c: (b, c, 0))
    st = pl.BlockSpec((1, 2, LANES, LANES), lambda b, c: (b, 0, 0, 0))
    const = lambda a: pl.BlockSpec(a.shape, lambda b, c: (0,) * a.ndim)
    return pl.pallas_call(
        _ret_kernel,
        grid=(B, T // c_pad),
        in_specs=[row, row, row, st, const(dmask), const(xi), const(zeta), const(dec)],
        out_specs=[row, st],
        out_shape=[jax.ShapeDtypeStruct((B, T, 256), F32), jax.ShapeDtypeStruct((B, 2, LANES, LANES), F32)],
        scratch_shapes=[pltpu.VMEM((2, LANES, LANES), F32)],
        compiler_params=_cparams(("arbitrary", "arbitrary")),
        name="retention",
    )(rq, rk, rv, r0, dmask, xi, zeta, dec)


def _state_to_pairs(r):
    B = r.shape[0]
    r = r.reshape(B, 2, 2, HEAD_DIM, HEAD_DIM)
    eye = jnp.eye(2, dtype=r.dtype)
    return jnp.einsum('bphde,hk->bphdke', r, eye).reshape(B, 2, LANES, LANES)


def _pairs_to_state(rp):
    B = rp.shape[0]
    rp = rp.reshape(B, 2, 2, HEAD_DIM, 2, HEAD_DIM)
    return jnp.stack([rp[:, :, 0, :, 0, :], rp[:, :, 1, :, 1, :]], axis=2).reshape(B, RET_HEADS, HEAD_DIM, HEAD_DIM)


def _mem_kv_kernel(m_ref, g_ref, w_ref, kg_ref, kv_out):
    x = m_ref[...]
    ms = jnp.mean(x * x, axis=-1, keepdims=True)
    xn = (x * lax.rsqrt(ms + EPS) * g_ref[...]).astype(BF16)
    z = jnp.dot(xn, w_ref[...], preferred_element_type=F32)
    for c in range(2):
        kv_out[:, c * LANES:(c + 1) * LANES] = _seg_rms(z[:, c * LANES:(c + 1) * LANES], kg_ref[...])
    kv_out[:, 256:512] = z[:, 256:512]


def _mem_kv(mem, mem_norm_g, w_mem_kv, mem_k_norm_g):
    T, D = mem.shape
    w = w_mem_kv.astype(BF16)
    full = lambda a: pl.BlockSpec(a.shape, lambda i: (0, 0))
    g = mem_norm_g[None, :]
    kg = jnp.tile(mem_k_norm_g, 2)[None, :]
    return pl.pallas_call(
        _mem_kv_kernel,
        grid=(1,),
        in_specs=[full(mem), full(g), full(w), full(kg)],
        out_specs=pl.BlockSpec((T, 512), lambda i: (0, 0)),
        out_shape=jax.ShapeDtypeStruct((T, 512), F32),
        compiler_params=_cparams(("arbitrary",)),
        name="mem_kv",
    )(mem, g, w, kg)


def _mem_attn_kernel(mq_ref, mkv_ref, o_ref):
    q = mq_ref[0]
    rows = q.shape[0]
    low = _lane((rows, LANES)) < HEAD_DIM
    zero = jnp.zeros((rows, LANES), F32)
    for pr in range(MEM_HEADS // 2):
        sl = slice(pr * LANES, (pr + 1) * LANES)
        qp = q[:, sl].astype(F32)
        k = mkv_ref[0, :, sl].astype(BF16)
        v = mkv_ref[0, :, 256 + pr * LANES:256 + (pr + 1) * LANES].astype(BF16)
        outs = []
        for hh in range(2):
            qm = jnp.where(low, qp, zero) if hh == 0 else jnp.where(low, zero, qp)
            s = _dot_nt(qm.astype(BF16), k) * (HEAD_DIM ** -0.5)
            m = jnp.max(s, axis=-1, keepdims=True)
            p = jnp.exp(s - m)
            o = jnp.dot(p.astype(BF16), v, preferred_element_type=F32)
            outs.append(o * (1.0 / jnp.sum(p, axis=-1, keepdims=True)))
        o_ref[0, :, sl] = jnp.where(low, outs[0], outs[1]).astype(o_ref.dtype)


def _mem_attn(mq, mkv, tm):
    B, R, _ = mq.shape
    return pl.pallas_call(
        _mem_attn_kernel,
        grid=(B, R // tm),
        in_specs=[pl.BlockSpec((1, tm, 256), lambda b, i: (b, i, 0)),
                  pl.BlockSpec((1,) + mkv.shape[1:], lambda b, i: (b, 0, 0))],
        out_specs=pl.BlockSpec((1, tm, 256), lambda b, i: (b, i, 0)),
        out_shape=jax.ShapeDtypeStruct((B, R, 256), BF16),
        compiler_params=_cparams(("arbitrary", "arbitrary")),
        name="mem_attn",
    )(mq, mkv)


TOK_ROWS = 8


def _load_token_tiles(ref, n):
    return jnp.concatenate([ref[pl.ds(c, n, stride=TOK_ROWS), :] for c in range(TOK_ROWS)], axis=1)


def _store_token_tiles(ref, x):
    n = x.shape[0]
    for c in range(TOK_ROWS):
        ref[pl.ds(c, n, stride=TOK_ROWS), :] = x[:, c * LANES:(c + 1) * LANES]


def _mix_kernel(x_ref, onsa_ref, oret_ref, rg_ref, omem_ref, wout_ref, rgain_ref, g2_ref,
                wr_hi_ref, wr_lo_ref, br_ref, x1_out, h_out, topi_out, topg_out):
    parts = [onsa_ref[...]]
    for c in range(2):
        sl = slice(c * LANES, (c + 1) * LANES)
        parts.append((_seg_rms(oret_ref[:, sl], rgain_ref[...]) * jax.nn.silu(rg_ref[:, sl])).astype(BF16))
    parts.append(omem_ref[...])
    mix = jnp.concatenate(parts, axis=1)
    x1 = x_ref[...] + jnp.dot(mix, wout_ref[...], preferred_element_type=F32)
    x1_out[...] = x1
    ms = jnp.mean(x1 * x1, axis=-1, keepdims=True)
    h = x1 * lax.rsqrt(ms + EPS) * g2_ref[...]
    _store_token_tiles(h_out, h)
    h_hi, h_lo, _ = _split3(h)
    logits = (jnp.dot(h_hi, wr_hi_ref[...], preferred_element_type=F32)
              + jnp.dot(h_hi, wr_lo_ref[...], preferred_element_type=F32)
              + jnp.dot(h_lo, wr_hi_ref[...], preferred_element_type=F32)) + br_ref[...]
    lane = _lane(logits.shape)
    key = logits
    topi = jnp.zeros(logits.shape, jnp.int32)
    topv = jnp.zeros(logits.shape, F32)
    for r in range(TOP_K):
        m = jnp.max(key, axis=-1, keepdims=True)
        idx = jnp.min(jnp.where(key == m, lane, LANES), axis=-1, keepdims=True)
        if r == 0:
            m0 = m
        topi = jnp.where(lane == r, idx, topi)
        topv = jnp.where(lane == r, jnp.exp(m - m0), topv)
        key = jnp.where(lane == idx, -jnp.inf, key)
    topi_out[...] = topi
    topg_out[...] = topv * (1.0 / jnp.sum(topv, axis=-1, keepdims=True))


def _mix(x, onsa, oret, rg, omem, mw, tm):
    T, D = x.shape
    row = lambda n: pl.BlockSpec((tm, n), lambda i: (i, 0))
    full = lambda a: pl.BlockSpec(a.shape, lambda i: (0, 0))
    names = ('wout', 'rgain', 'g2', 'wr_hi', 'wr_lo', 'br')
    return pl.pallas_call(
        _mix_kernel,
        grid=(T // tm,),
        in_specs=[row(D), row(512), row(256), row(256), row(256)] + [full(mw[n]) for n in names],
        out_specs=[row(D), pl.BlockSpec((tm * TOK_ROWS, LANES), lambda i: (i, 0)), row(LANES), row(LANES)],
        out_shape=[jax.ShapeDtypeStruct((T, D), F32), jax.ShapeDtypeStruct((T * TOK_ROWS, LANES), F32),
                   jax.ShapeDtypeStruct((T, LANES), jnp.int32), jax.ShapeDtypeStruct((T, LANES), F32)],
        compiler_params=_cparams(("arbitrary",)),
        name="mix",
    )(x, onsa, oret, rg, omem, *[mw[n] for n in names])


def _prep_mix_weights(w_out, ret_norm_g, norm2_g, w_router, b_router):
    wr = jnp.pad(w_router, ((0, 0), (0, LANES - N_EXPERTS)))
    wr_hi = wr.astype(BF16)
    wr_lo = (wr - wr_hi.astype(F32)).astype(BF16)
    br = jnp.concatenate([b_router.astype(F32), jnp.full((LANES - N_EXPERTS,), NEG, F32)])[None, :]
    return dict(wout=w_out.astype(BF16), rgain=jnp.tile(ret_norm_g, 2)[None, :], g2=norm2_g[None, :],
                wr_hi=wr_hi, wr_lo=wr_lo, br=br)


MOE_TOK_TILE = 128


def _row_copy(src, i, dst, j, sem):
    return pltpu.make_async_copy(src.at[pl.ds(pl.multiple_of(i, TOK_ROWS), TOK_ROWS)],
                                 dst.at[pl.ds(pl.multiple_of(j, TOK_ROWS), TOK_ROWS)], sem)


def _dispatch_kernel(dest_ref, h_ref, xb_in, xb_out, sem):
    del xb_in
    tile = dest_ref.shape[0] // TOP_K

    def start(t, c):
        for k in range(TOP_K):
            _row_copy(h_ref, t * TOK_ROWS, xb_out, dest_ref[t * TOP_K + k], sem).start()
        return c

    def wait(t, c):
        for k in range(TOP_K):
            _row_copy(h_ref, 0, xb_out, 0, sem).wait()
        return c

    lax.fori_loop(0, tile, start, 0, unroll=8)
    lax.fori_loop(0, tile, wait, 0, unroll=8)


def _dispatch(h, dest, xb):
    T = h.shape[0] // TOK_ROWS
    hbm = pl.BlockSpec(memory_space=pl.ANY)
    n = MOE_TOK_TILE * TOP_K
    return pl.pallas_call(
        _dispatch_kernel,
        grid=(T // MOE_TOK_TILE,),
        in_specs=[pl.BlockSpec((n,), lambda i: (i,), memory_space=pltpu.SMEM),
                  pl.BlockSpec((MOE_TOK_TILE * TOK_ROWS, LANES), lambda i: (i, 0)), hbm],
        out_specs=hbm,
        out_shape=jax.ShapeDtypeStruct(xb.shape, xb.dtype),
        scratch_shapes=[pltpu.SemaphoreType.DMA(())],
        input_output_aliases={2: 0},
        compiler_params=_cparams(("arbitrary",)),
        name="moe_dispatch",
    )(dest.reshape(-1), h, xb)


def _moe_kernel(be_ref, nb_ref, x_ref, wup_ref, bup_ref, wdn_ref, bdn_ref, y_ref, wup_bf, wdn_bf):
    j = pl.program_id(0)

    @pl.when(j < nb_ref[0])
    def _():
        @pl.when(jnp.logical_or(j == 0, be_ref[j] != be_ref[jnp.maximum(j - 1, 0)]))
        def _():
            wup_bf[...] = wup_ref[0].astype(BF16)
            wdn_bf[...] = wdn_ref[0].astype(BF16)

        x = _load_token_tiles(x_ref, MOE_ROWS).astype(BF16)
        up = jnp.dot(x, wup_bf[...], preferred_element_type=F32) + bup_ref[0]
        x_glu = jnp.minimum(up[:, :D_FF], SWIGLU_LIMIT)
        x_lin = jnp.clip(up[:, D_FF:], -SWIGLU_LIMIT, SWIGLU_LIMIT)
        act = x_glu * jax.nn.sigmoid(SWIGLU_ALPHA * x_glu) * (x_lin + 1.0)
        _store_token_tiles(y_ref, jnp.dot(act.astype(BF16), wdn_bf[...], preferred_element_type=F32) + bdn_ref[0])

    @pl.when(j >= nb_ref[0])
    def _():
        y_ref[...] = jnp.zeros(y_ref.shape, y_ref.dtype)


def _moe_experts(xb, blk_expert, n_used, w_up, b_up, w_down, b_down):
    D = w_up.shape[1]
    blk = pl.BlockSpec((MOE_ROWS * TOK_ROWS, LANES), lambda j, be, nb: (j, 0))
    grid_spec = pltpu.PrefetchScalarGridSpec(
        num_scalar_prefetch=2,
        grid=(xb.shape[0] // (MOE_ROWS * TOK_ROWS),),
        in_specs=[blk,
                  pl.BlockSpec((1, D, 2 * D_FF), lambda j, be, nb: (be[j], 0, 0)),
                  pl.BlockSpec((1, 1, 2 * D_FF), lambda j, be, nb: (be[j], 0, 0)),
                  pl.BlockSpec((1, D_FF, D), lambda j, be, nb: (be[j], 0, 0)),
                  pl.BlockSpec((1, 1, D), lambda j, be, nb: (be[j], 0, 0))],
        out_specs=blk,
        scratch_shapes=[pltpu.VMEM((D, 2 * D_FF), BF16), pltpu.VMEM((D_FF, D), BF16)],
    )
    return pl.pallas_call(
        _moe_kernel,
        grid_spec=grid_spec,
        out_shape=jax.ShapeDtypeStruct(xb.shape, F32),
        compiler_params=_cparams(("arbitrary",)),
        name="moe_experts",
    )(blk_expert, n_used, xb, w_up, b_up[:, None, :], w_down, b_down[:, None, :])


def _combine_kernel(dest_ref, g_ref, x1_ref, yb_hbm, out_ref, buf, sem):
    tile = x1_ref.shape[0]

    def start(t, c):
        for k in range(TOP_K):
            _row_copy(yb_hbm, dest_ref[t * TOP_K + k], buf.at[k], t * TOK_ROWS, sem).start()
        return c

    def wait(t, c):
        for k in range(TOP_K):
            _row_copy(yb_hbm, 0, buf.at[k], 0, sem).wait()
        return c

    lax.fori_loop(0, tile, start, 0, unroll=8)
    lax.fori_loop(0, tile, wait, 0, unroll=8)
    acc = x1_ref[...]
    for k in range(TOP_K):
        acc = acc + g_ref[:, k:k + 1] * _load_token_tiles(buf.at[k], tile)
    out_ref[...] = acc


def _combine(x1, topg, dest, yb):
    T, D = x1.shape
    tile = MOE_TOK_TILE
    row = lambda n: pl.BlockSpec((tile, n), lambda i: (i, 0))
    return pl.pallas_call(
        _combine_kernel,
        grid=(T // tile,),
        in_specs=[pl.BlockSpec((tile * TOP_K,), lambda i: (i,), memory_space=pltpu.SMEM),
                  row(LANES), row(D), pl.BlockSpec(memory_space=pl.ANY)],
        out_specs=row(D),
        out_shape=jax.ShapeDtypeStruct((T, D), F32),
        scratch_shapes=[pltpu.VMEM((TOP_K, tile * TOK_ROWS, LANES), F32), pltpu.SemaphoreType.DMA(())],
        compiler_params=_cparams(("arbitrary",)),
        name="moe_combine",
    )(dest.reshape(-1), topg, x1, yb)


def _route(topi):
    T = topi.shape[0]
    onehot = (topi[:, :, None] == jnp.arange(N_EXPERTS, dtype=jnp.int32)).astype(jnp.int32).sum(axis=1)
    rank = jnp.cumsum(onehot, axis=0) - onehot
    counts = onehot.sum(axis=0)
    padded = (counts + MOE_ROWS - 1) // MOE_ROWS * MOE_ROWS
    pad_end = jnp.cumsum(padded)
    pad_start = pad_end - padded
    dest = pad_start[topi] + jnp.take_along_axis(rank, topi, axis=1)
    n_blk = (T * TOP_K + N_EXPERTS * (MOE_ROWS - 1) + MOE_ROWS - 1) // MOE_ROWS
    blk_start = jnp.arange(n_blk, dtype=jnp.int32) * MOE_ROWS
    blk_expert = jnp.minimum((pad_end[None, :] <= blk_start[:, None]).sum(axis=1), N_EXPERTS - 1)
    return ((dest * TOK_ROWS).astype(jnp.int32), blk_expert.astype(jnp.int32),
            (pad_end[-1:] // MOE_ROWS).astype(jnp.int32), n_blk)


SAMPLE_ROWS = 8


def _token_mixers(x, pw, cos, sin, tm, key_major):
    names = ('q', 'kvc', 'kvs', 'kvw', 'ks_bf', 'kw_bf', 'gates', 'rq', 'rk', 'rv', 'rg', 'mq', 'vs_t', 'vw_t')
    return dict(zip(names, _project(x, pw, cos, sin, tm, key_major)))


def kernel(x_prompt, x_sample, mem_prompt, cache_cmp_kv, cache_slc_kv, cache_win_kv, state_ret, cache_mem_kv, page_table, norm1_g, w_in, q_norm_g, k_norm_cmp_g, k_norm_slc_g, k_norm_win_g, cmp_pe_k, cmp_w1_k, cmp_w2_k, cmp_pe_v, cmp_w1_v, cmp_w2_v, ret_norm_g, mem_norm_g, w_mem_kv, mem_q_norm_g, mem_k_norm_g, w_out, norm2_g, w_router, b_router, w_up, b_up, w_down, b_down):
    B, S, D = x_prompt.shape
    DB, QS, _ = x_sample.shape
    n_mem = mem_prompt.shape[1]
    n_pages, page = page_table.shape[1], cache_cmp_kv.shape[2]
    past = n_pages * page
    NQ = SAMPLE_ROWS
    G, HD = NSA_GROUPS, HEAD_DIM
    win_rows = min(WINDOW, S)
    TP, TS = B * S, DB * NQ

    cos_p, sin_p = _rope_tables(jnp.arange(S, dtype=jnp.int32))
    cos_s, sin_s = _rope_tables(past + jnp.arange(NQ, dtype=jnp.int32))
    cos_s, sin_s = jnp.tile(cos_s, (DB, 1)), jnp.tile(sin_s, (DB, 1))

    xp = x_prompt.reshape(TP, D)
    xs = jnp.pad(x_sample, ((0, 0), (0, NQ - QS), (0, 0))).reshape(TS, D)
    unpad = lambda a: a.reshape(DB, NQ, -1)[:, :QS]
    outs = [[] for _ in range(9)]
    for l in range(w_in.shape[0]):
        pw = _prep_proj_weights(norm1_g[l], w_in[l], q_norm_g[l], k_norm_slc_g[l], k_norm_win_g[l], mem_q_norm_g[l])
        cw = _prep_compress_weights(cmp_pe_k[l], cmp_w1_k[l], cmp_w2_k[l], cmp_pe_v[l], cmp_w1_v[l], cmp_w2_v[l],
                                    k_norm_cmp_g[l])
        mw = _prep_mix_weights(w_out[l], ret_norm_g[l], norm2_g[l], w_router[l], b_router[l])

        t = _token_mixers(xp, pw, cos_p, sin_p, 512, True)
        kc, vct = _compress(t['kvc'].reshape(B, S, 256), cw)
        o_nsa = _nsa_prompt(B, t['q'], t['gates'], kc, vct, t['ks_bf'], t['vs_t'], t['kw_bf'], t['vw_t'])
        o_ret, r_p = _retention(t['rq'].reshape(B, S, 256), t['rk'].reshape(B, S, 256), t['rv'].reshape(B, S, 256),
                                jnp.zeros((B, 2, LANES, LANES), F32), RET_CHUNK, RET_CHUNK)
        mkv = _mem_kv(mem_prompt.reshape(B * n_mem, D), mem_norm_g[l], w_mem_kv[l], mem_k_norm_g[l])
        o_mem = _mem_attn(t['mq'].reshape(B, S, 256), mkv.reshape(B, n_mem, 512), 512)
        x1_p, h_p, topi_p, topg_p = _mix(xp, o_nsa.reshape(TP, -1), o_ret.reshape(TP, 256), t['rg'],
                                         o_mem.reshape(TP, 256), mw, 512)
        outs[0].append(t['kvc'].reshape(B, S, 2, G, HD))
        outs[1].append(t['kvs'].reshape(B, S, 2, G, HD))
        outs[2].append(t['kvw'].reshape(B, S, 2, G, HD)[:, S - win_rows:])
        outs[3].append(_pairs_to_state(r_p))
        outs[4].append(mkv.reshape(B, n_mem, 2, MEM_HEADS, HD))

        t = _token_mixers(xs, pw, cos_s, sin_s, TS, False)
        win = cache_win_kv[l]
        feat_major = lambda c: jnp.moveaxis(c, 1, -1).reshape(c.shape[0], 256, c.shape[1])
        o_nsa = _nsa_decode(t['q'].astype(F32).reshape(DB, NQ, -1), t['gates'].reshape(DB, NQ, LANES),
                            t['kvs'].reshape(DB, NQ, 256), t['kvw'].reshape(DB, NQ, 256),
                            feat_major(win), feat_major(cache_cmp_kv[l]), feat_major(cache_slc_kv[l]),
                            page_table, cw, QS)
        o_ret, r_s = _retention(t['rq'].reshape(DB, NQ, 256), t['rk'].reshape(DB, NQ, 256),
                                t['rv'].reshape(DB, NQ, 256), _state_to_pairs(state_ret[l].astype(F32)), QS, NQ)
        o_mem = _mem_attn(t['mq'].reshape(DB, NQ, 256), cache_mem_kv[l].reshape(DB, n_mem, 512), NQ)
        x1_s, h_s, topi_s, topg_s = _mix(xs, o_nsa.reshape(TS, -1).astype(BF16), o_ret.reshape(TS, 256), t['rg'],
                                         o_mem.reshape(TS, 256), mw, TS)
        kv5 = lambda a: unpad(a).reshape(DB, QS, 2, G, HD)
        outs[5].append(kv5(t['kvc']))
        outs[6].append(kv5(t['kvs']))
        outs[7].append(jnp.concatenate([win, kv5(t['kvw'])], axis=1)[:, QS:])
        outs[8].append(_pairs_to_state(r_s))

        valid = lambda a: unpad(a).reshape(DB * QS, -1)
        dest, blk_expert, n_used, n_blk = _route(jnp.concatenate([topi_p, valid(topi_s)], axis=0)[:, :TOP_K])
        xb = jnp.zeros((n_blk * MOE_ROWS * TOK_ROWS, LANES), F32)
        xb = _dispatch(h_p, dest[:TP], xb)
        h_s_valid = h_s.reshape(DB, NQ, TOK_ROWS, LANES)[:, :QS].reshape(DB * QS * TOK_ROWS, LANES)
        xb = _dispatch(h_s_valid, dest[TP:], xb)
        yb = _moe_experts(xb, blk_expert, n_used, w_up[l], b_up[l], w_down[l], b_down[l])
        xp = _combine(x1_p, topg_p, dest[:TP], yb)
        xs_new = _combine(valid(x1_s), valid(topg_s), dest[TP:], yb)
        xs = jnp.pad(xs_new.reshape(DB, QS, D), ((0, 0), (0, NQ - QS), (0, 0))).reshape(TS, D)
    y_sample = xs.reshape(DB, NQ, D)[:, :QS]
    return (xp.reshape(B, S, D), y_sample) + tuple(jnp.stack(o) for o in outs)
```

```python
import functools

import jax
import jax.numpy as jnp
from jax import lax
from jax.experimental import pallas as pl
from jax.experimental.pallas import tpu as pltpu

F32 = jnp.float32
BF16 = jnp.bfloat16

HEAD_DIM = 64
NSA_HEADS = 8
NSA_GROUPS = 2
NSA_HPG = NSA_HEADS // NSA_GROUPS
RET_HEADS = 4
MEM_HEADS = 4
CMP_BLK = 32
CMP_STRIDE = 16
CMP_HID = 256
SLC_BLK = 64
N_SEL = 16
WINDOW = 512
Q_BLK = 128
RET_CHUNK = 128
N_EXPERTS = 32
TOP_K = 4
D_FF = 1024
SWIGLU_LIMIT = 7.0
SWIGLU_ALPHA = 1.702
EPS = 1e-6
NEG = -1e30
BIG = 1e9
ROPE_BASE = 10000.0
SLC_RATIO = SLC_BLK // CMP_STRIDE
CMP_OVL = CMP_BLK // CMP_STRIDE - 1

Q_SCALE = HEAD_DIM ** -0.5 * 1.4426950408889634
LANES = 128
KEY_TILE = 512
MOE_ROWS = 512
VMEM_LIMIT = 56 * 1024 * 1024

C_Q = 0
C_KVC = C_Q + NSA_HEADS * LANES
C_KVS = C_KVC + 256
C_KVW = C_KVS + 256
C_GATE = C_KVW + 256
C_RQ = C_GATE + LANES
C_RK = C_RQ + 256
C_RV = C_RK + 256
C_RG = C_RV + 256
C_MQ = C_RG + 256
C_END = C_MQ + 256


def _cparams(sem):
    return pltpu.CompilerParams(dimension_semantics=sem, vmem_limit_bytes=VMEM_LIMIT)


def _lane(shape):
    return lax.broadcasted_iota(jnp.int32, shape, len(shape) - 1)


def _seg_rms(x, gain):
    lo = _lane(x.shape) < HEAD_DIM
    x2 = x * x
    s_lo = jnp.sum(jnp.where(lo, x2, 0.0), axis=-1, keepdims=True)
    s_hi = jnp.sum(jnp.where(lo, 0.0, x2), axis=-1, keepdims=True)
    ms = jnp.where(lo, s_lo, s_hi) * (1.0 / HEAD_DIM)
    return x * lax.rsqrt(ms + EPS) * gain


def _swap_halves(x):
    first = (_lane(x.shape) & (HEAD_DIM // 2)) == 0
    return jnp.where(first, pltpu.roll(x, LANES - HEAD_DIM // 2, 1), pltpu.roll(x, HEAD_DIM // 2, 1))


def _proj_kernel(x_ref, g1_ref, w_ref, cos_ref, sin_ref, qg_ref, ksg_ref, kwg_ref, mqg_ref,
                 q_out, kvc_out, kvs_out, kvw_out, ks_bf, kw_bf, gate_out,
                 rq_out, rk_out, rv_out, rg_out, mq_out, *extra, seq_tiles):
    x = x_ref[...]
    tm = x.shape[0]
    ms = jnp.mean(x * x, axis=-1, keepdims=True)
    xn = (x * lax.rsqrt(ms + EPS) * g1_ref[...]).astype(BF16)
    z = jnp.dot(xn, w_ref[...], preferred_element_type=F32)
    for i in range(NSA_HEADS):
        zq = z[:, C_Q + i * LANES:C_Q + (i + 1) * LANES]
        msq = jnp.sum(zq * zq, axis=-1, keepdims=True) * (1.0 / HEAD_DIM)
        qn = zq * lax.rsqrt(msq + EPS) * qg_ref[:, i * LANES:(i + 1) * LANES] * Q_SCALE
        if seq_tiles:
            q_out[i * LANES:(i + 1) * LANES, :] = qn.T.astype(BF16)
        else:
            q_out[:, i * LANES:(i + 1) * LANES] = qn.astype(BF16)
    kvc_out[...] = z[:, C_KVC:C_KVC + 256]
    ks = _seg_rms(z[:, C_KVS:C_KVS + LANES], ksg_ref[...])
    vs = z[:, C_KVS + LANES:C_KVS + 256]
    kvs_out[:, 0:LANES] = ks
    kvs_out[:, LANES:256] = vs
    kw = _seg_rms(z[:, C_KVW:C_KVW + LANES], kwg_ref[...])
    vw = z[:, C_KVW + LANES:C_KVW + 256]
    kvw_out[:, 0:LANES] = kw
    kvw_out[:, LANES:256] = vw
    gates = jax.nn.sigmoid(z[:, C_GATE:C_GATE + LANES])
    ks_bf[:, 0:LANES] = ks.astype(BF16)
    if seq_tiles:
        vs_t, vw_t = extra
        pos = (pl.program_id(0) % seq_tiles) * tm + lax.broadcasted_iota(jnp.int32, (tm, LANES), 0)
        ks_bf[:, LANES:256] = jnp.where(_lane((tm, LANES)) == pos // SLC_BLK, 1.0, 0.0).astype(BF16)
        kw_bf[...] = kw.astype(BF16)
        vs_t[...] = vs.T.astype(BF16)
        vw_t[...] = vw.T.astype(BF16)
        gate_out[...] = gates.T
    else:
        ks_bf[:, LANES:256] = vs.astype(BF16)
        kw_bf[:, 0:LANES] = kw.astype(BF16)
        kw_bf[:, LANES:256] = vw.astype(BF16)
        gate_out[...] = gates
    for c in range(2):
        sl = slice(c * LANES, (c + 1) * LANES)
        cos = cos_ref[:, sl]
        sin = sin_ref[:, sl]
        rq = z[:, C_RQ + c * LANES:C_RQ + (c + 1) * LANES]
        rk = z[:, C_RK + c * LANES:C_RK + (c + 1) * LANES]
        rq_out[:, sl] = rq * cos + _swap_halves(rq) * sin
        rk_out[:, sl] = (rk * cos + _swap_halves(rk) * sin) * (HEAD_DIM ** -0.5)
        mq = z[:, C_MQ + c * LANES:C_MQ + (c + 1) * LANES]
        mq_out[:, sl] = _seg_rms(mq, mqg_ref[...]).astype(BF16)
    rv_out[...] = z[:, C_RV:C_RV + 256]
    rg_out[...] = z[:, C_RG:C_RG + 256]


def _project(x, pw, cos, sin, tm, key_major):
    T, D = x.shape
    nt = cos.shape[0] // tm
    row = lambda n: (pl.BlockSpec((tm, n), lambda i: (i, 0)), (T, n))
    col = lambda n: (pl.BlockSpec((n, tm), lambda i: (0, i)), (n, T))
    full = lambda a: pl.BlockSpec(a.shape, lambda i: (0, 0))
    tab = pl.BlockSpec((tm, 256), lambda i: (i % nt, 0))
    tok = col if key_major else row
    outs = [
        (tok(NSA_HEADS * LANES), BF16),
        (row(256), F32), (row(256), F32), (row(256), F32),
        (row(256), BF16), (row(LANES if key_major else 256), BF16),
        (tok(LANES), F32),
        (row(256), F32), (row(256), F32), (row(256), F32), (row(256), F32),
        (row(256), BF16),
    ]
    if key_major:
        outs += [(col(LANES), BF16), (col(LANES), BF16)]
    return pl.pallas_call(
        functools.partial(_proj_kernel, seq_tiles=nt if key_major else 0),
        grid=(T // tm,),
        in_specs=[row(D)[0], full(pw['g1']), full(pw['w']), tab, tab,
                  full(pw['qg']), full(pw['ksg']), full(pw['kwg']), full(pw['mqg'])],
        out_specs=[spec for (spec, _), _ in outs],
        out_shape=[jax.ShapeDtypeStruct(shape, dt) for (_, shape), dt in outs],
        compiler_params=_cparams(("arbitrary",)),
        name="proj",
    )(x, pw['g1'], pw['w'], cos, sin, pw['qg'], pw['ksg'], pw['kwg'], pw['mqg'])


def _prep_proj_weights(norm1_g, w_in, q_norm_g, k_norm_slc_g, k_norm_win_g, mem_q_norm_g):
    D = w_in.shape[0]
    sizes = (512, 256, 256, 256, 24, 256, 256, 256, 256, 256)
    parts, off = [], 0
    for n in sizes:
        parts.append(w_in[:, off:off + n])
        off += n
    wq, wkvc, wkvs, wkvw, wg, wrq, wrk, wrv, wrg, wmq = parts
    zero = jnp.zeros((D, HEAD_DIM), w_in.dtype)
    qcols, qg = [], []
    gz = jnp.zeros((HEAD_DIM,), F32)
    for i in range(NSA_HEADS):
        wh = wq[:, i * HEAD_DIM:(i + 1) * HEAD_DIM]
        if i < NSA_HPG:
            qcols += [wh, zero]
            qg += [q_norm_g, gz]
        else:
            qcols += [zero, wh]
            qg += [gz, q_norm_g]
    wgp = jnp.pad(wg, ((0, 0), (0, LANES - wg.shape[1])))
    w = jnp.concatenate(qcols + [wkvc, wkvs, wkvw, wgp, wrq, wrk, wrv, wrg, wmq], axis=1).astype(BF16)
    two = lambda g: jnp.tile(g, 2)[None, :]
    return dict(g1=norm1_g[None, :], w=w, qg=jnp.concatenate(qg)[None, :],
                ksg=two(k_norm_slc_g), kwg=two(k_norm_win_g), mqg=two(mem_q_norm_g))


def _rope_tables(pos):
    half = HEAD_DIM // 2
    inv = ROPE_BASE ** (-jnp.arange(half, dtype=F32) / half)
    ang = pos.astype(F32)[:, None] * inv[None, :]
    cos, sin = jnp.cos(ang), jnp.sin(ang)
    cos = jnp.tile(jnp.concatenate([cos, cos], axis=1), (1, RET_HEADS))
    sin = jnp.tile(jnp.concatenate([-sin, sin], axis=1), (1, RET_HEADS))
    return cos, sin


def _compress_half(src_ref, pe_ref, wa_ref, wb_ref, w2_ref, nchunks):
    a_parts, b_parts = [], []
    for r in range(CMP_STRIDE):
        xr = src_ref[pl.ds(r, nchunks, stride=CMP_STRIDE), :]
        a_parts.append((xr + pe_ref[r:r + 1, :]).astype(BF16))
        b_parts.append((xr + pe_ref[CMP_STRIDE + r:CMP_STRIDE + r + 1, :]).astype(BF16))
    ha = jnp.dot(jnp.concatenate(a_parts, axis=1), wa_ref[...], preferred_element_type=F32)
    hb = jnp.dot(jnp.concatenate(b_parts, axis=1), wb_ref[...], preferred_element_type=F32)
    h = ha + pltpu.roll(hb, nchunks - 1, 0)
    act = jax.nn.gelu(h).astype(BF16)
    return jnp.dot(act, w2_ref[...], preferred_element_type=F32)


def _compress_kernel(kv_ref, pe_ref, wa_ref, wb_ref, w2_ref, kg_ref, kc_out, vct_out):
    nchunks = kc_out.shape[1]
    y = _compress_half(kv_ref.at[0], pe_ref.at[0], wa_ref.at[0], wb_ref.at[0], w2_ref.at[0], nchunks)

    @pl.when(pl.program_id(1) == 0)
    def _():
        kc_out[0] = _seg_rms(y, kg_ref[...]).astype(BF16)

    @pl.when(pl.program_id(1) == 1)
    def _():
        vct_out[0] = y.T.astype(BF16)


def _compress(kvc, cw):
    B, T, _ = kvc.shape
    nchunks = T // CMP_STRIDE
    per_kv = lambda a: pl.BlockSpec((1,) + a.shape[1:], lambda b, j: (j,) + (0,) * (a.ndim - 1))
    return pl.pallas_call(
        _compress_kernel,
        grid=(B, 2),
        in_specs=[pl.BlockSpec((1, T, LANES), lambda b, j: (b, 0, j)),
                  per_kv(cw['pe']), per_kv(cw['wa']), per_kv(cw['wb']), per_kv(cw['w2']),
                  pl.BlockSpec((1, LANES), lambda b, j: (0, 0))],
        out_specs=[pl.BlockSpec((1, nchunks, LANES), lambda b, j: (b, 0, 0)),
                   pl.BlockSpec((1, LANES, nchunks), lambda b, j: (b, 0, 0))],
        out_shape=[jax.ShapeDtypeStruct((B, nchunks, LANES), BF16),
                   jax.ShapeDtypeStruct((B, LANES, nchunks), BF16)],
        compiler_params=_cparams(("arbitrary", "arbitrary")),
        name="compress",
    )(kvc, cw['pe'], cw['wa'], cw['wb'], cw['w2'], cw['kg'])


def _prep_compress_weights(pe_k, w1_k, w2_k, pe_v, w1_v, w2_v, k_norm_g):
    eye = jnp.eye(NSA_GROUPS, dtype=F32)

    def one(pe, w1, w2):
        w1r = w1.reshape(CMP_BLK, HEAD_DIM, CMP_HID)
        wf = jnp.einsum('gh,rdc->rgdhc', eye, w1r).reshape(CMP_BLK * LANES, NSA_GROUPS * CMP_HID)
        w2f = jnp.einsum('gh,cd->gchd', eye, w2).reshape(NSA_GROUPS * CMP_HID, LANES)
        half = CMP_STRIDE * LANES
        return jnp.tile(pe, (1, NSA_GROUPS)), wf[:half].astype(BF16), wf[half:].astype(BF16), w2f.astype(BF16)

    k, v = one(pe_k, w1_k, w2_k), one(pe_v, w1_v, w2_v)
    st = lambda i: jnp.stack([k[i], v[i]])
    return dict(pe=st(0), wa=st(1), wb=st(2), w2=st(3), kg=jnp.tile(k_norm_g, 2)[None, :])


def _dot_nt(a, b):
    return lax.dot_general(a, b, (((1,), (1,)), ((), ())), preferred_element_type=F32)


def _split3(x):
    hi = x.astype(BF16)
    r = x - hi.astype(F32)
    mid = r.astype(BF16)
    lo = (r - mid.astype(F32)).astype(BF16)
    return hi, mid, lo


def _select_bias(ps, cur, rounds, axis, cur_in_range=True):
    j = lax.broadcasted_iota(jnp.int32, ps.shape, axis)
    last = cur if cur_in_range else cur - 1
    key = jnp.where(j <= last, ps, NEG)
    for forced in (0, last, cur - 1):
        key = jnp.where(j == forced, BIG, key)
    bias = jnp.full(ps.shape, NEG, F32)
    for _ in range(rounds):
        m = jnp.max(key, axis=axis, keepdims=True)
        idx = jnp.min(jnp.where(key == m, j, LANES), axis=axis, keepdims=True)
        pick = j == idx
        bias = jnp.where(pick, 0.0, bias)
        key = jnp.where(pick, -jnp.inf, key)
    return bias


def _cmp_branch(qg, qpos, kc, vc):
    s = _dot_nt(qg, kc)
    cend = _lane((1, kc.shape[0])) * CMP_STRIDE + (CMP_BLK - 1)
    s = jnp.where(cend <= qpos, s, NEG)
    m = jnp.max(s, axis=-1, keepdims=True)
    p = jnp.exp2(s - m)
    p = p * (1.0 / jnp.sum(p, axis=-1, keepdims=True))
    p = jnp.where(qpos >= CMP_BLK - 1, p, 0.0)
    return p, jnp.dot(p.astype(BF16), vc, preferred_element_type=F32)


def _gate_heads(gates, g, oc, osel, ow, nq):
    heads = []
    for h in range(NSA_HPG):
        rs = slice(h * nq, (h + 1) * nq)
        c0 = (NSA_HPG * g + h) * 3
        heads.append(gates[:, c0:c0 + 1] * oc[rs] + gates[:, c0 + 1:c0 + 2] * osel[rs]
                     + gates[:, c0 + 2:c0 + 3] * ow[rs])
    return heads


def _store_heads(heads, o_ref):
    low = _lane(heads[0].shape) < HEAD_DIM
    for pr in range(NSA_HEADS // 2):
        even, odd = heads[2 * pr], heads[2 * pr + 1]
        if pr < NSA_GROUPS:
            odd = pltpu.roll(odd, HEAD_DIM, 1)
        else:
            even = pltpu.roll(even, HEAD_DIM, 1)
        o_ref[0, :, pr * LANES:(pr + 1) * LANES] = jnp.where(low, even, odd).astype(o_ref.dtype)


def _nsa_decode_kernel(pt_ref, q_ref, gate_ref, ksn_ref, kwn_ref, win_ref, pool_c, pool_s,
                       pe_ref, wa_ref, wb_ref, w2_ref, kg_ref, mmap_ref, hot_ref, o_ref,
                       cbuf, sbuf, xrow, sems, *, page, n_new):
    b = pl.program_id(0)
    past = sbuf.shape[1]
    npages = past // page
    nq = q_ref.shape[1]
    dot = functools.partial(jnp.dot, preferred_element_type=F32)

    slot = b % 2

    def cmp_copy(seq, p, sl):
        dst = pl.ds(pl.multiple_of(p * page, page), page)
        return pltpu.make_async_copy(pool_c.at[pt_ref[seq * npages + p]], cbuf.at[sl, :, dst], sems.at[sl])

    def slc_copy(seq, p):
        dst = pl.ds(pl.multiple_of(p * page, page), page)
        return pltpu.make_async_copy(pool_s.at[pt_ref[seq * npages + p]], sbuf.at[:, dst], sems.at[2])

    def for_pages(fn):
        def body(p, c):
            fn(p)
            return c
        lax.fori_loop(0, npages, body, 0)

    @pl.when(b == 0)
    def _():
        for_pages(lambda p: cmp_copy(0, p, 0).start())

    for_pages(lambda p: slc_copy(b, p).start())

    @pl.when(b + 1 < pl.num_programs(0))
    def _():
        for_pages(lambda p: cmp_copy(b + 1, p, 1 - slot).start())

    for_pages(lambda p: cmp_copy(b, p, slot).wait())

    tchunk = 4 * LANES
    for half in range(2):
        for c in range(past // tchunk):
            xrow[half, c * tchunk:(c + 1) * tchunk, :] = cbuf[slot, half * LANES:(half + 1) * LANES,
                                                              c * tchunk:(c + 1) * tchunk].T
    nchunks = past // CMP_STRIDE
    kc = _seg_rms(_compress_half(xrow.at[0], pe_ref.at[0], wa_ref.at[0], wb_ref.at[0], w2_ref.at[0], nchunks),
                  kg_ref[...]).astype(BF16)
    vc = _compress_half(xrow.at[1], pe_ref.at[1], wa_ref.at[1], wb_ref.at[1], w2_ref.at[1], nchunks).astype(BF16)

    rows = NSA_HPG * nq
    srow = lax.broadcasted_iota(jnp.int32, (rows, 1), 0) % nq
    qpos = past + srow
    gates = gate_ref[0]
    pad_keys = lambda x: jnp.concatenate([x, jnp.zeros((LANES - nq, LANES), F32)], axis=0).astype(BF16)
    new_ok = _lane((1, LANES)) <= jnp.minimum(srow, n_new - 1)
    ks_new, vs_new = pad_keys(ksn_ref[0, :, 0:LANES]), pad_keys(ksn_ref[0, :, LANES:256])
    kw_new, vw_new = pad_keys(kwn_ref[0, :, 0:LANES]), pad_keys(kwn_ref[0, :, LANES:256])

    for_pages(lambda p: slc_copy(b, p).wait())
    k_aug_t = jnp.concatenate([sbuf[0:LANES, :].astype(BF16), hot_ref[...]], axis=0)
    vs_t = sbuf[LANES:256, :].astype(BF16)
    nwin = win_ref.shape[2]
    kw_t, vw_t = win_ref[0, 0:LANES, :].astype(BF16), win_ref[0, LANES:256, :].astype(BF16)

    qgs, ocs, p4s = [], [], []
    for g in range(NSA_GROUPS):
        qg = jnp.concatenate(
            [q_ref[0, :, (NSA_HPG * g + h) * LANES:(NSA_HPG * g + h + 1) * LANES] for h in range(NSA_HPG)],
            axis=0).astype(BF16)
        p, oc = _cmp_branch(qg, qpos, kc, vc)
        qgs.append(qg)
        ocs.append(oc)
        p4s.append(p[0:nq] + p[nq:2 * nq] + p[2 * nq:3 * nq] + p[3 * nq:4 * nq])
    p4 = jnp.concatenate(p4s, axis=0)
    ps_t = sum(_dot_nt(mmap_ref[...], part) for part in _split3(p4))
    cur = (past + _lane((1, NSA_GROUPS * nq)) % nq) // SLC_BLK
    bias_t = _select_bias(ps_t, cur, N_SEL - 1, 0, cur_in_range=False)
    bias_all = jnp.concatenate([bias_t, jnp.zeros((LANES, LANES - NSA_GROUPS * nq), F32)], axis=1).T

    heads = []
    for g in range(NSA_GROUPS):
        qg, oc = qgs[g], ocs[g]
        bias = bias_all[g * nq:(g + 1) * nq]
        qaug = jnp.concatenate([qg, jnp.concatenate([bias] * NSA_HPG, axis=0).astype(BF16)], axis=1)
        s_old = dot(qaug, k_aug_t)
        s_new = jnp.where(new_ok, _dot_nt(qg, ks_new), NEG)
        m = jnp.maximum(jnp.max(s_old, axis=-1, keepdims=True), jnp.max(s_new, axis=-1, keepdims=True))
        p_old, p_new = jnp.exp2(s_old - m), jnp.exp2(s_new - m)
        inv = 1.0 / (jnp.sum(p_old, axis=-1, keepdims=True) + jnp.sum(p_new, axis=-1, keepdims=True))
        osel = (_dot_nt(p_old.astype(BF16), vs_t) + dot(p_new.astype(BF16), vs_new)) * inv
        sw_old = jnp.where(_lane((1, nwin)) + (WINDOW - nwin) > srow, dot(qg, kw_t), NEG)
        sw_new = jnp.where(new_ok, _dot_nt(qg, kw_new), NEG)
        mw = jnp.maximum(jnp.max(sw_old, axis=-1, keepdims=True), jnp.max(sw_new, axis=-1, keepdims=True))
        pw_old, pw_new = jnp.exp2(sw_old - mw), jnp.exp2(sw_new - mw)
        invw = 1.0 / (jnp.sum(pw_old, axis=-1, keepdims=True) + jnp.sum(pw_new, axis=-1, keepdims=True))
        ow = (_dot_nt(pw_old.astype(BF16), vw_t) + dot(pw_new.astype(BF16), vw_new)) * invw
        heads += _gate_heads(gates, g, oc, osel, ow, nq)
    _store_heads(heads, o_ref)


def _nsa_decode(q, gates, ksn, kwn, win_t, pool_c, pool_s, page_table, cw, n_new):
    DB, nq, _ = q.shape
    n_pages = page_table.shape[1]
    page = pool_c.shape[2]
    past = n_pages * page
    nchunks = past // CMP_STRIDE
    mmap = _importance_map(nchunks)
    hot = (jnp.arange(LANES, dtype=jnp.int32)[:, None] == jnp.arange(past, dtype=jnp.int32)[None, :] // SLC_BLK)
    hot = hot.astype(BF16)
    per_seq = lambda a: pl.BlockSpec((1,) + a.shape[1:], lambda b, pt: (b,) + (0,) * (a.ndim - 1))
    const = lambda a: pl.BlockSpec(a.shape, lambda b, pt: (0,) * a.ndim, pipeline_mode=pl.Buffered(1))
    hbm = pl.BlockSpec(memory_space=pl.ANY)
    grid_spec = pltpu.PrefetchScalarGridSpec(
        num_scalar_prefetch=1,
        grid=(DB,),
        in_specs=[per_seq(q), per_seq(gates), per_seq(ksn), per_seq(kwn), per_seq(win_t), hbm, hbm,
                  const(cw['pe']), const(cw['wa']), const(cw['wb']), const(cw['w2']), const(cw['kg']), const(mmap),
                  const(hot)],
        out_specs=pl.BlockSpec((1, nq, NSA_HEADS * HEAD_DIM), lambda b, pt: (b, 0, 0)),
        scratch_shapes=[pltpu.VMEM((2, 256, past), F32), pltpu.VMEM((256, past), F32),
                        pltpu.VMEM((2, past, LANES), F32), pltpu.SemaphoreType.DMA((3,))],
    )
    return pl.pallas_call(
        functools.partial(_nsa_decode_kernel, page=page, n_new=n_new),
        grid_spec=grid_spec,
        out_shape=jax.ShapeDtypeStruct((DB, nq, NSA_HEADS * HEAD_DIM), F32),
        compiler_params=_cparams(("arbitrary",)),
        name="nsa_sample",
    )(page_table.reshape(-1), q, gates, ksn, kwn, win_t, pool_c, pool_s,
      cw['pe'], cw['wa'], cw['wb'], cw['w2'], cw['kg'], mmap, hot)


def _nsa_prompt_kernel(q_ref, gate_ref, kc_ref, vct_ref, ks_ref, vst_ref, kw_ref, vwt_ref, mmap_ref, o_ref,
                       m_sc, l_sc, acc_sc, sa_sc, sb_sc):
    i = pl.program_id(1)
    s0 = i * Q_BLK
    cols = NSA_HPG * Q_BLK
    qpos = s0 + _lane((1, cols)) % Q_BLK
    cur = (s0 + _lane((1, Q_BLK))) // SLC_BLK
    kc, vct = kc_ref[0], vct_ref[0]
    ncmp = kc.shape[0]
    n_full = s0 // KEY_TILE
    w0 = pl.multiple_of(jnp.maximum(s0 - WINDOW, 0), Q_BLK)
    nw = WINDOW + Q_BLK
    dot = functools.partial(jnp.dot, preferred_element_type=F32)

    def softmax_keys(s):
        p = jnp.exp2(s - jnp.max(s, axis=0, keepdims=True))
        return p, 1.0 / jnp.sum(p, axis=0, keepdims=True)

    qaugs = []
    branch = []
    n_idx = lax.broadcasted_iota(jnp.int32, (ncmp, cols), 0)
    n_last = lax.shift_right_arithmetic(qpos - (CMP_BLK - 1), CMP_STRIDE.bit_length() - 1)
    r_idx = lax.broadcasted_iota(jnp.int32, (nw, cols), 0)
    r_last = qpos - w0
    for g in range(NSA_GROUPS):
        vg = slice(g * HEAD_DIM, (g + 1) * HEAD_DIM)
        qt = jnp.concatenate([q_ref[(NSA_HPG * g + h) * LANES:(NSA_HPG * g + h + 1) * LANES, :]
                              for h in range(NSA_HPG)], axis=1)
        s = jnp.where(n_idx <= n_last, dot(kc, qt), NEG)
        p, inv = softmax_keys(s)
        p = p * jnp.where(n_last >= 0, inv, 0.0)
        oc = dot(vct[vg], p.astype(BF16))
        p4 = p[:, 0:Q_BLK] + p[:, Q_BLK:2 * Q_BLK] + p[:, 2 * Q_BLK:3 * Q_BLK] + p[:, 3 * Q_BLK:4 * Q_BLK]
        ps_t = sum(dot(mmap_ref[...], part) for part in _split3(p4))
        bias = _select_bias(ps_t, cur, N_SEL, 0).astype(BF16)
        qaugs.append(jnp.concatenate([qt, jnp.concatenate([bias] * NSA_HPG, axis=1)], axis=0))
        sw = dot(kw_ref[0, pl.ds(w0, nw), :], qt)
        sw = jnp.where(r_idx <= r_last, jnp.where(r_idx > r_last - WINDOW, sw, NEG), NEG)
        pw, invw = softmax_keys(sw)
        ow = dot(vwt_ref[pl.ds(g * HEAD_DIM, HEAD_DIM), pl.ds(w0, nw)], pw.astype(BF16)) * invw
        branch.append((oc, ow))
        m_sc[g] = jnp.full(m_sc.shape[1:], NEG, F32)
        l_sc[g] = jnp.zeros(l_sc.shape[1:], F32)
        acc_sc[g] = jnp.zeros(acc_sc.shape[1:], F32)

    def scores(kt, s_ref):
        k_aug = ks_ref[0, pl.ds(pl.multiple_of(kt * KEY_TILE, KEY_TILE), KEY_TILE), :]
        for g in range(NSA_GROUPS):
            s_ref[g] = dot(k_aug, qaugs[g])

    def consume(kt, s_ref, causal):
        k0 = pl.multiple_of(kt * KEY_TILE, KEY_TILE)
        v_t = vst_ref[:, pl.ds(k0, KEY_TILE)]
        for g in range(NSA_GROUPS):
            s = s_ref[g]
            if causal:
                s = jnp.where(k0 + lax.broadcasted_iota(jnp.int32, s.shape, 0) <= qpos, s, NEG)
            m_old = m_sc[g]
            m_new = jnp.maximum(m_old, jnp.max(s, axis=0, keepdims=True))
            alpha = jnp.exp2(m_old - m_new)
            p = jnp.exp2(s - m_new)
            l_sc[g] = alpha * l_sc[g] + jnp.sum(p, axis=0, keepdims=True)
            acc_sc[g] = alpha * acc_sc[g] + dot(v_t[g * HEAD_DIM:(g + 1) * HEAD_DIM], p.astype(BF16))
            m_sc[g] = m_new

    def two_tiles(u, c):
        scores(2 * u + 1, sb_sc)
        consume(2 * u, sa_sc, False)
        scores(2 * u + 2, sa_sc)
        consume(2 * u + 1, sb_sc, False)
        return c

    scores(0, sa_sc)
    lax.fori_loop(0, n_full // 2, two_tiles, 0)

    @pl.when(n_full % 2 == 1)
    def _():
        scores(n_full, sb_sc)
        consume(n_full - 1, sa_sc, False)
        consume(n_full, sb_sc, True)

    @pl.when(n_full % 2 == 0)
    def _():
        consume(n_full, sa_sc, True)

    pieces = []
    for g in range(NSA_GROUPS):
        oc, ow = branch[g]
        osel = acc_sc[g] * (1.0 / l_sc[g])
        for h in range(NSA_HPG):
            cs = slice(h * Q_BLK, (h + 1) * Q_BLK)
            c0 = (NSA_HPG * g + h) * 3
            pieces.append(gate_ref[c0:c0 + 1, :] * oc[:, cs] + gate_ref[c0 + 1:c0 + 2, :] * osel[:, cs]
                          + gate_ref[c0 + 2:c0 + 3, :] * ow[:, cs])
    o_ref[0] = jnp.concatenate(pieces, axis=0).T.astype(o_ref.dtype)


def _importance_map(ncmp):
    j = jnp.arange(LANES)[:, None]
    n = jnp.arange(ncmp)[None, :]
    return ((n >= SLC_RATIO * j - CMP_OVL) & (n < SLC_RATIO * j + SLC_RATIO)).astype(BF16)


def _nsa_prompt(B, qt, gates_t, kc, vct, ks_aug, vs_t, kw, vw_t):
    S = qt.shape[1] // B
    nq = S // Q_BLK
    ncmp = kc.shape[1]
    cols = NSA_HPG * Q_BLK
    mmap = _importance_map(ncmp)
    per_block = lambda n: pl.BlockSpec((n, Q_BLK), lambda b, i: (0, b * nq + i))
    seq_rows = lambda n: pl.BlockSpec((1, S, n), lambda b, i: (b, 0, 0))
    seq_cols = pl.BlockSpec((LANES, S), lambda b, i: (0, b))
    return pl.pallas_call(
        _nsa_prompt_kernel,
        grid=(B, nq),
        in_specs=[per_block(qt.shape[0]), per_block(LANES),
                  pl.BlockSpec((1, ncmp, LANES), lambda b, i: (b, 0, 0)),
                  pl.BlockSpec((1, LANES, ncmp), lambda b, i: (b, 0, 0)),
                  seq_rows(256), seq_cols, seq_rows(LANES), seq_cols,
                  pl.BlockSpec(mmap.shape, lambda b, i: (0, 0))],
        out_specs=pl.BlockSpec((1, Q_BLK, NSA_HEADS * HEAD_DIM), lambda b, i: (b, i, 0)),
        out_shape=jax.ShapeDtypeStruct((B, S, NSA_HEADS * HEAD_DIM), BF16),
        scratch_shapes=[pltpu.VMEM((NSA_GROUPS, 1, cols), F32), pltpu.VMEM((NSA_GROUPS, 1, cols), F32),
                        pltpu.VMEM((NSA_GROUPS, HEAD_DIM, cols), F32),
                        pltpu.VMEM((NSA_GROUPS, KEY_TILE, cols), F32), pltpu.VMEM((NSA_GROUPS, KEY_TILE, cols), F32)],
        compiler_params=_cparams(("arbitrary", "arbitrary")),
        name="nsa_prompt",
    )(qt, gates_t, kc, vct, ks_aug.reshape(B, S, 256), vs_t, kw.reshape(B, S, LANES), vw_t, mmap)


def _ret_kernel(rq_ref, rk_ref, rv_ref, r0_ref, dmask_ref, xi_ref, zeta_ref, dec_ref, o_ref, rout_ref, r_sc):
    @pl.when(pl.program_id(1) == 0)
    def _():
        r_sc[...] = r0_ref[0]

    C = rq_ref.shape[1]
    low = _lane((C, LANES)) < HEAD_DIM
    diag = lax.broadcasted_iota(jnp.int32, (LANES, LANES), 0) // HEAD_DIM == _lane((LANES, LANES)) // HEAD_DIM
    for pr in range(RET_HEADS // 2):
        sl = slice(pr * LANES, (pr + 1) * LANES)
        k = rk_ref[0, :, sl]
        qb, kb, vb = rq_ref[0, :, sl].astype(BF16), k.astype(BF16), rv_ref[0, :, sl].astype(BF16)
        zero = jnp.zeros_like(qb)
        s0 = _dot_nt(jnp.where(low, qb, zero), kb) * dmask_ref[2 * pr]
        s1 = _dot_nt(jnp.where(low, zero, qb), kb) * dmask_ref[2 * pr + 1]
        o = jnp.where(low, jnp.dot(s0.astype(BF16), vb, preferred_element_type=F32),
                      jnp.dot(s1.astype(BF16), vb, preferred_element_type=F32))
        r = r_sc[pr]
        o_ref[0, :, sl] = o + jnp.dot(qb, r.astype(BF16), preferred_element_type=F32) * xi_ref[:, sl]
        kz = (k * zeta_ref[:, sl]).astype(BF16)
        upd = lax.dot_general(kz, vb, (((0,), (0,)), ((), ())), preferred_element_type=F32)
        r_sc[pr] = dec_ref[:, sl] * r + jnp.where(diag, upd, 0.0)
    rout_ref[0] = r_sc[...]


def _ret_tables(c_true, c_pad):
    lg = jnp.log(1.0 - 2.0 ** (-5.0 - jnp.arange(RET_HEADS, dtype=F32)))
    idx = jnp.arange(c_pad, dtype=F32)
    diff = idx[:, None] - idx[None, :]
    dmask = jnp.where(diff >= 0, jnp.exp(jnp.maximum(diff, 0.0)[None] * lg[:, None, None]), 0.0)
    lanes = lambda a: jnp.repeat(a, HEAD_DIM, axis=-1)
    xi = lanes(jnp.exp((idx + 1.0)[:, None] * lg[None, :]))
    zeta = lanes(jnp.exp((c_true - 1.0 - idx)[:, None] * lg[None, :]))
    dec = lanes(jnp.exp(c_true * lg)[None, :])
    return dmask, xi, zeta, dec


def _retention(rq, rk, rv, r0, c_true, c_pad):
    B, T, _ = rq.shape
    dmask, xi, zeta, dec = _ret_tables(c_true, c_pad)
    row = pl.BlockSpec((1, c_pad, 256), lambda b, c: (b, c, 0))
    st = pl.BlockSpec((1, 2, LANES, LANES), lambda b, c: (b, 0, 0, 0))
    const = lambda a: pl.BlockSpec(a.shape, lambda b, c: (0,) * a.ndim)
    return pl.pallas_call(
        _ret_kernel,
        grid=(B, T // c_pad),
        in_specs=[row, row, row, st, const(dmask), const(xi), const(zeta), const(dec)],
        out_specs=[row, st],
        out_shape=[jax.ShapeDtypeStruct((B, T, 256), F32), jax.ShapeDtypeStruct((B, 2, LANES, LANES), F32)],
        scratch_shapes=[pltpu.VMEM((2, LANES, LANES), F32)],
        compiler_params=_cparams(("arbitrary", "arbitrary")),
        name="retention",
    )(rq, rk, rv, r0, dmask, xi, zeta, dec)


def _state_to_pairs(r):
    B = r.shape[0]
    r = r.reshape(B, 2, 2, HEAD_DIM, HEAD_DIM)
    eye = jnp.eye(2, dtype=r.dtype)
    return jnp.einsum('bphde,hk->bphdke', r, eye).reshape(B, 2, LANES, LANES)


def _pairs_to_state(rp):
    B = rp.shape[0]
    rp = rp.reshape(B, 2, 2, HEAD_DIM, 2, HEAD_DIM)
    return jnp.stack([rp[:, :, 0, :, 0, :], rp[:, :, 1, :, 1, :]], axis=2).reshape(B, RET_HEADS, HEAD_DIM, HEAD_DIM)


def _mem_kv_kernel(m_ref, g_ref, w_ref, kg_ref, kv_out):
    x = m_ref[...]
    ms = jnp.mean(x * x, axis=-1, keepdims=True)
    xn = (x * lax.rsqrt(ms + EPS) * g_ref[...]).astype(BF16)
    z = jnp.dot(xn, w_ref[...], preferred_element_type=F32)
    for c in range(2):
        kv_out[:, c * LANES:(c + 1) * LANES] = _seg_rms(z[:, c * LANES:(c + 1) * LANES], kg_ref[...])
    kv_out[:, 256:512] = z[:, 256:512]


def _mem_kv(mem, mem_norm_g, w_mem_kv, mem_k_norm_g):
    T, D = mem.shape
    w = w_mem_kv.astype(BF16)
    full = lambda a: pl.BlockSpec(a.shape, lambda i: (0, 0))
    g = mem_norm_g[None, :]
    kg = jnp.tile(mem_k_norm_g, 2)[None, :]
    return pl.pallas_call(
        _mem_kv_kernel,
        grid=(1,),
        in_specs=[full(mem), full(g), full(w), full(kg)],
        out_specs=pl.BlockSpec((T, 512), lambda i: (0, 0)),
        out_shape=jax.ShapeDtypeStruct((T, 512), F32),
        compiler_params=_cparams(("arbitrary",)),
        name="mem_kv",
    )(mem, g, w, kg)


def _mem_attn_kernel(mq_ref, mkv_ref, o_ref):
    q = mq_ref[0]
    rows = q.shape[0]
    low = _lane((rows, LANES)) < HEAD_DIM
    zero = jnp.zeros((rows, LANES), F32)
    for pr in range(MEM_HEADS // 2):
        sl = slice(pr * LANES, (pr + 1) * LANES)
        qp = q[:, sl].astype(F32)
        k = mkv_ref[0, :, sl].astype(BF16)
        v = mkv_ref[0, :, 256 + pr * LANES:256 + (pr + 1) * LANES].astype(BF16)
        outs = []
        for hh in range(2):
            qm = jnp.where(low, qp, zero) if hh == 0 else jnp.where(low, zero, qp)
            s = _dot_nt(qm.astype(BF16), k) * (HEAD_DIM ** -0.5)
            m = jnp.max(s, axis=-1, keepdims=True)
            p = jnp.exp(s - m)
            o = jnp.dot(p.astype(BF16), v, preferred_element_type=F32)
            outs.append(o * (1.0 / jnp.sum(p, axis=-1, keepdims=True)))
        o_ref[0, :, sl] = jnp.where(low, outs[0], outs[1]).astype(o_ref.dtype)


def _mem_attn(mq, mkv, tm):
    B, R, _ = mq.shape
    return pl.pallas_call(
        _mem_attn_kernel,
        grid=(B, R // tm),
        in_specs=[pl.BlockSpec((1, tm, 256), lambda b, i: (b, i, 0)),
                  pl.BlockSpec((1,) + mkv.shape[1:], lambda b, i: (b, 0, 0))],
        out_specs=pl.BlockSpec((1, tm, 256), lambda b, i: (b, i, 0)),
        out_shape=jax.ShapeDtypeStruct((B, R, 256), BF16),
        compiler_params=_cparams(("arbitrary", "arbitrary")),
        name="mem_attn",
    )(mq, mkv)


TOK_ROWS = 8


def _load_token_tiles(ref, n):
    return jnp.concatenate([ref[pl.ds(c, n, stride=TOK_ROWS), :] for c in range(TOK_ROWS)], axis=1)


def _store_token_tiles(ref, x):
    n = x.shape[0]
    for c in range(TOK_ROWS):
        ref[pl.ds(c, n, stride=TOK_ROWS), :] = x[:, c * LANES:(c + 1) * LANES]


def _mix_kernel(x_ref, onsa_ref, oret_ref, rg_ref, omem_ref, wout_ref, rgain_ref, g2_ref,
                wr_hi_ref, wr_lo_ref, br_ref, x1_out, h_out, topi_out, topg_out):
    parts = [onsa_ref[...]]
    for c in range(2):
        sl = slice(c * LANES, (c + 1) * LANES)
        parts.append((_seg_rms(oret_ref[:, sl], rgain_ref[...]) * jax.nn.silu(rg_ref[:, sl])).astype(BF16))
    parts.append(omem_ref[...])
    mix = jnp.concatenate(parts, axis=1)
    x1 = x_ref[...] + jnp.dot(mix, wout_ref[...], preferred_element_type=F32)
    x1_out[...] = x1
    ms = jnp.mean(x1 * x1, axis=-1, keepdims=True)
    h = x1 * lax.rsqrt(ms + EPS) * g2_ref[...]
    _store_token_tiles(h_out, h)
    h_hi, h_lo, _ = _split3(h)
    logits = (jnp.dot(h_hi, wr_hi_ref[...], preferred_element_type=F32)
              + jnp.dot(h_hi, wr_lo_ref[...], preferred_element_type=F32)
              + jnp.dot(h_lo, wr_hi_ref[...], preferred_element_type=F32)) + br_ref[...]
    lane = _lane(logits.shape)
    key = logits
    topi = jnp.zeros(logits.shape, jnp.int32)
    topv = jnp.zeros(logits.shape, F32)
    for r in range(TOP_K):
        m = jnp.max(key, axis=-1, keepdims=True)
        idx = jnp.min(jnp.where(key == m, lane, LANES), axis=-1, keepdims=True)
        if r == 0:
            m0 = m
        topi = jnp.where(lane == r, idx, topi)
        topv = jnp.where(lane == r, jnp.exp(m - m0), topv)
        key = jnp.where(lane == idx, -jnp.inf, key)
    topi_out[...] = topi
    topg_out[...] = topv * (1.0 / jnp.sum(topv, axis=-1, keepdims=True))


def _mix(x, onsa, oret, rg, omem, mw, tm):
    T, D = x.shape
    row = lambda n: pl.BlockSpec((tm, n), lambda i: (i, 0))
    full = lambda a: pl.BlockSpec(a.shape, lambda i: (0, 0))
    names = ('wout', 'rgain', 'g2', 'wr_hi', 'wr_lo', 'br')
    return pl.pallas_call(
        _mix_kernel,
        grid=(T // tm,),
        in_specs=[row(D), row(512), row(256), row(256), row(256)] + [full(mw[n]) for n in names],
        out_specs=[row(D), pl.BlockSpec((tm * TOK_ROWS, LANES), lambda i: (i, 0)), row(LANES), row(LANES)],
        out_shape=[jax.ShapeDtypeStruct((T, D), F32), jax.ShapeDtypeStruct((T * TOK_ROWS, LANES), F32),
                   jax.ShapeDtypeStruct((T, LANES), jnp.int32), jax.ShapeDtypeStruct((T, LANES), F32)],
        compiler_params=_cparams(("arbitrary",)),
        name="mix",
    )(x, onsa, oret, rg, omem, *[mw[n] for n in names])


def _prep_mix_weights(w_out, ret_norm_g, norm2_g, w_router, b_router):
    wr = jnp.pad(w_router, ((0, 0), (0, LANES - N_EXPERTS)))
    wr_hi = wr.astype(BF16)
    wr_lo = (wr - wr_hi.astype(F32)).astype(BF16)
    br = jnp.concatenate([b_router.astype(F32), jnp.full((LANES - N_EXPERTS,), NEG, F32)])[None, :]
    return dict(wout=w_out.astype(BF16), rgain=jnp.tile(ret_norm_g, 2)[None, :], g2=norm2_g[None, :],
                wr_hi=wr_hi, wr_lo=wr_lo, br=br)


MOE_TOK_TILE = 128


def _row_copy(src, i, dst, j, sem):
    return pltpu.make_async_copy(src.at[pl.ds(pl.multiple_of(i, TOK_ROWS), TOK_ROWS)],
                                 dst.at[pl.ds(pl.multiple_of(j, TOK_ROWS), TOK_ROWS)], sem)


def _dispatch_kernel(dest_ref, h_ref, xb_in, xb_out, sem):
    del xb_in
    tile = dest_ref.shape[0] // TOP_K

    def start(t, c):
        for k in range(TOP_K):
            _row_copy(h_ref, t * TOK_ROWS, xb_out, dest_ref[t * TOP_K + k], sem).start()
        return c

    def wait(t, c):
        for k in range(TOP_K):
            _row_copy(h_ref, 0, xb_out, 0, sem).wait()
        return c

    lax.fori_loop(0, tile, start, 0, unroll=8)
    lax.fori_loop(0, tile, wait, 0, unroll=8)


def _dispatch(h, dest, xb):
    T = h.shape[0] // TOK_ROWS
    hbm = pl.BlockSpec(memory_space=pl.ANY)
    n = MOE_TOK_TILE * TOP_K
    return pl.pallas_call(
        _dispatch_kernel,
        grid=(T // MOE_TOK_TILE,),
        in_specs=[pl.BlockSpec((n,), lambda i: (i,), memory_space=pltpu.SMEM),
                  pl.BlockSpec((MOE_TOK_TILE * TOK_ROWS, LANES), lambda i: (i, 0)), hbm],
        out_specs=hbm,
        out_shape=jax.ShapeDtypeStruct(xb.shape, xb.dtype),
        scratch_shapes=[pltpu.SemaphoreType.DMA(())],
        input_output_aliases={2: 0},
        compiler_params=_cparams(("arbitrary",)),
        name="moe_dispatch",
    )(dest.reshape(-1), h, xb)


def _moe_kernel(be_ref, nb_ref, x_ref, wup_ref, bup_ref, wdn_ref, bdn_ref, y_ref, wup_bf, wdn_bf):
    j = pl.program_id(0)

    @pl.when(j < nb_ref[0])
    def _():
        @pl.when(jnp.logical_or(j == 0, be_ref[j] != be_ref[jnp.maximum(j - 1, 0)]))
        def _():
            wup_bf[...] = wup_ref[0].astype(BF16)
            wdn_bf[...] = wdn_ref[0].astype(BF16)

        x = _load_token_tiles(x_ref, MOE_ROWS).astype(BF16)
        up = jnp.dot(x, wup_bf[...], preferred_element_type=F32) + bup_ref[0]
        x_glu = jnp.minimum(up[:, :D_FF], SWIGLU_LIMIT)
        x_lin = jnp.clip(up[:, D_FF:], -SWIGLU_LIMIT, SWIGLU_LIMIT)
        act = x_glu * jax.nn.sigmoid(SWIGLU_ALPHA * x_glu) * (x_lin + 1.0)
        _store_token_tiles(y_ref, jnp.dot(act.astype(BF16), wdn_bf[...], preferred_element_type=F32) + bdn_ref[0])

    @pl.when(j >= nb_ref[0])
    def _():
        y_ref[...] = jnp.zeros(y_ref.shape, y_ref.dtype)


def _moe_experts(xb, blk_expert, n_used, w_up, b_up, w_down, b_down):
    D = w_up.shape[1]
    blk = pl.BlockSpec((MOE_ROWS * TOK_ROWS, LANES), lambda j, be, nb: (j, 0))
    grid_spec = pltpu.PrefetchScalarGridSpec(
        num_scalar_prefetch=2,
        grid=(xb.shape[0] // (MOE_ROWS * TOK_ROWS),),
        in_specs=[blk,
                  pl.BlockSpec((1, D, 2 * D_FF), lambda j, be, nb: (be[j], 0, 0)),
                  pl.BlockSpec((1, 1, 2 * D_FF), lambda j, be, nb: (be[j], 0, 0)),
                  pl.BlockSpec((1, D_FF, D), lambda j, be, nb: (be[j], 0, 0)),
                  pl.BlockSpec((1, 1, D), lambda j, be, nb: (be[j], 0, 0))],
        out_specs=blk,
        scratch_shapes=[pltpu.VMEM((D, 2 * D_FF), BF16), pltpu.VMEM((D_FF, D), BF16)],
    )
    return pl.pallas_call(
        _moe_kernel,
        grid_spec=grid_spec,
        out_shape=jax.ShapeDtypeStruct(xb.shape, F32),
        compiler_params=_cparams(("arbitrary",)),
        name="moe_experts",
    )(blk_expert, n_used, xb, w_up, b_up[:, None, :], w_down, b_down[:, None, :])


def _combine_kernel(dest_ref, g_ref, x1_ref, yb_hbm, out_ref, buf, sem):
    tile = x1_ref.shape[0]

    def start(t, c):
        for k in range(TOP_K):
            _row_copy(yb_hbm, dest_ref[t * TOP_K + k], buf.at[k], t * TOK_ROWS, sem).start()
        return c

    def wait(t, c):
        for k in range(TOP_K):
            _row_copy(yb_hbm, 0, buf.at[k], 0, sem).wait()
        return c

    lax.fori_loop(0, tile, start, 0, unroll=8)
    lax.fori_loop(0, tile, wait, 0, unroll=8)
    acc = x1_ref[...]
    for k in range(TOP_K):
        acc = acc + g_ref[:, k:k + 1] * _load_token_tiles(buf.at[k], tile)
    out_ref[...] = acc


def _combine(x1, topg, dest, yb):
    T, D = x1.shape
    tile = MOE_TOK_TILE
    row = lambda n: pl.BlockSpec((tile, n), lambda i: (i, 0))
    return pl.pallas_call(
        _combine_kernel,
        grid=(T // tile,),
        in_specs=[pl.BlockSpec((tile * TOP_K,), lambda i: (i,), memory_space=pltpu.SMEM),
                  row(LANES), row(D), pl.BlockSpec(memory_space=pl.ANY)],
        out_specs=row(D),
        out_shape=jax.ShapeDtypeStruct((T, D), F32),
        scratch_shapes=[pltpu.VMEM((TOP_K, tile * TOK_ROWS, LANES), F32), pltpu.SemaphoreType.DMA(())],
        compiler_params=_cparams(("arbitrary",)),
        name="moe_combine",
    )(dest.reshape(-1), topg, x1, yb)


def _route(topi):
    T = topi.shape[0]
    onehot = (topi[:, :, None] == jnp.arange(N_EXPERTS, dtype=jnp.int32)).astype(jnp.int32).sum(axis=1)
    rank = jnp.cumsum(onehot, axis=0) - onehot
    counts = onehot.sum(axis=0)
    padded = (counts + MOE_ROWS - 1) // MOE_ROWS * MOE_ROWS
    pad_end = jnp.cumsum(padded)
    pad_start = pad_end - padded
    dest = pad_start[topi] + jnp.take_along_axis(rank, topi, axis=1)
    n_blk = (T * TOP_K + N_EXPERTS * (MOE_ROWS - 1) + MOE_ROWS - 1) // MOE_ROWS
    blk_start = jnp.arange(n_blk, dtype=jnp.int32) * MOE_ROWS
    blk_expert = jnp.minimum((pad_end[None, :] <= blk_start[:, None]).sum(axis=1), N_EXPERTS - 1)
    return ((dest * TOK_ROWS).astype(jnp.int32), blk_expert.astype(jnp.int32),
            (pad_end[-1:] // MOE_ROWS).astype(jnp.int32), n_blk)


SAMPLE_ROWS = 8


def _token_mixers(x, pw, cos, sin, tm, key_major):
    names = ('q', 'kvc', 'kvs', 'kvw', 'ks_bf', 'kw_bf', 'gates', 'rq', 'rk', 'rv', 'rg', 'mq', 'vs_t', 'vw_t')
    return dict(zip(names, _project(x, pw, cos, sin, tm, key_major)))


def kernel(x_prompt, x_sample, mem_prompt, cache_cmp_kv, cache_slc_kv, cache_win_kv, state_ret, cache_mem_kv, page_table, norm1_g, w_in, q_norm_g, k_norm_cmp_g, k_norm_slc_g, k_norm_win_g, cmp_pe_k, cmp_w1_k, cmp_w2_k, cmp_pe_v, cmp_w1_v, cmp_w2_v, ret_norm_g, mem_norm_g, w_mem_kv, mem_q_norm_g, mem_k_norm_g, w_out, norm2_g, w_router, b_router, w_up, b_up, w_down, b_down):
    B, S, D = x_prompt.shape
    DB, QS, _ = x_sample.shape
    n_mem = mem_prompt.shape[1]
    n_pages, page = page_table.shape[1], cache_cmp_kv.shape[2]
    past = n_pages * page
    NQ = SAMPLE_ROWS
    G, HD = NSA_GROUPS, HEAD_DIM
    win_rows = min(WINDOW, S)
    TP, TS = B * S, DB * NQ

    cos_p, sin_p = _rope_tables(jnp.arange(S, dtype=jnp.int32))
    cos_s, sin_s = _rope_tables(past + jnp.arange(NQ, dtype=jnp.int32))
    cos_s, sin_s = jnp.tile(cos_s, (DB, 1)), jnp.tile(sin_s, (DB, 1))

    xp = x_prompt.reshape(TP, D)
    xs = jnp.pad(x_sample, ((0, 0), (0, NQ - QS), (0, 0))).reshape(TS, D)
    unpad = lambda a: a.reshape(DB, NQ, -1)[:, :QS]
    outs = [[] for _ in range(9)]
    for l in range(w_in.shape[0]):
        pw = _prep_proj_weights(norm1_g[l], w_in[l], q_norm_g[l], k_norm_slc_g[l], k_norm_win_g[l], mem_q_norm_g[l])
        cw = _prep_compress_weights(cmp_pe_k[l], cmp_w1_k[l], cmp_w2_k[l], cmp_pe_v[l], cmp_w1_v[l], cmp_w2_v[l],
                                    k_norm_cmp_g[l])
        mw = _prep_mix_weights(w_out[l], ret_norm_g[l], norm2_g[l], w_router[l], b_router[l])

        t = _token_mixers(xp, pw, cos_p, sin_p, 512, True)
        kc, vct = _compress(t['kvc'].reshape(B, S, 256), cw)
        o_nsa = _nsa_prompt(B, t['q'], t['gates'], kc, vct, t['ks_bf'], t['vs_t'], t['kw_bf'], t['vw_t'])
        o_ret, r_p = _retention(t['rq'].reshape(B, S, 256), t['rk'].reshape(B, S, 256), t['rv'].reshape(B, S, 256),
                                jnp.zeros((B, 2, LANES, LANES), F32), RET_CHUNK, RET_CHUNK)
        mkv = _mem_kv(mem_prompt.reshape(B * n_mem, D), mem_norm_g[l], w_mem_kv[l], mem_k_norm_g[l])
        o_mem = _mem_attn(t['mq'].reshape(B, S, 256), mkv.reshape(B, n_mem, 512), 512)
        x1_p, h_p, topi_p, topg_p = _mix(xp, o_nsa.reshape(TP, -1), o_ret.reshape(TP, 256), t['rg'],
                                         o_mem.reshape(TP, 256), mw, 512)
        outs[0].append(t['kvc'].reshape(B, S, 2, G, HD))
        outs[1].append(t['kvs'].reshape(B, S, 2, G, HD))
        outs[2].append(t['kvw'].reshape(B, S, 2, G, HD)[:, S - win_rows:])
        outs[3].append(_pairs_to_state(r_p))
        outs[4].append(mkv.reshape(B, n_mem, 2, MEM_HEADS, HD))

        t = _token_mixers(xs, pw, cos_s, sin_s, TS, False)
        win = cache_win_kv[l]
        feat_major = lambda c: jnp.moveaxis(c, 1, -1).reshape(c.shape[0], 256, c.shape[1])
        o_nsa = _nsa_decode(t['q'].astype(F32).reshape(DB, NQ, -1), t['gates'].reshape(DB, NQ, LANES),
                            t['kvs'].reshape(DB, NQ, 256), t['kvw'].reshape(DB, NQ, 256),
                            feat_major(win), feat_major(cache_cmp_kv[l]), feat_major(cache_slc_kv[l]),
                            page_table, cw, QS)
        o_ret, r_s = _retention(t['rq'].reshape(DB, NQ, 256), t['rk'].reshape(DB, NQ, 256),
                                t['rv'].reshape(DB, NQ, 256), _state_to_pairs(state_ret[l].astype(F32)), QS, NQ)
        o_mem = _mem_attn(t['mq'].reshape(DB, NQ, 256), cache_mem_kv[l].reshape(DB, n_mem, 512), NQ)
        x1_s, h_s, topi_s, topg_s = _mix(xs, o_nsa.reshape(TS, -1).astype(BF16), o_ret.reshape(TS, 256), t['rg'],
                                         o_mem.reshape(TS, 256), mw, TS)
        kv5 = lambda a: unpad(a).reshape(DB, QS, 2, G, HD)
        outs[5].append(kv5(t['kvc']))
        outs[6].append(kv5(t['kvs']))
        outs[7].append(jnp.concatenate([win, kv5(t['kvw'])], axis=1)[:, QS:])
        outs[8].append(_pairs_to_state(r_s))

        valid = lambda a: unpad(a).reshape(DB * QS, -1)
        dest, blk_expert, n_used, n_blk = _route(jnp.concatenate([topi_p, valid(topi_s)], axis=0)[:, :TOP_K])
        xb = jnp.zeros((n_blk * MOE_ROWS * TOK_ROWS, LANES), F32)
        xb = _dispatch(h_p, dest[:TP], xb)
        h_s_valid = h_s.reshape(DB, NQ, TOK_ROWS, LANES)[:, :QS].reshape(DB * QS * TOK_ROWS, LANES)
        xb = _dispatch(h_s_valid, dest[TP:], xb)
        yb = _moe_experts(xb, blk_expert, n_used, w_up[l], b_up[l], w_down[l], b_down[l])
        xp = _combine(x1_p, topg_p, dest[:TP], yb)
        xs_new = _combine(valid(x1_s), valid(topg_s), dest[TP:], yb)
        xs = jnp.pad(xs_new.reshape(DB, QS, D), ((0, 0), (0, NQ - QS), (0, 0))).reshape(TS, D)
    y_sample = xs.reshape(DB, NQ, D)[:, :QS]
    return (xp.reshape(B, S, D), y_sample) + tuple(jnp.stack(o) for o in outs)
```

```python
import functools

import jax
import jax.numpy as jnp
from jax import lax
from jax.experimental import pallas as pl
from jax.experimental.pallas import tpu as pltpu

F32 = jnp.float32
BF16 = jnp.bfloat16

HEAD_DIM = 64
NSA_HEADS = 8
NSA_GROUPS = 2
NSA_HPG = NSA_HEADS // NSA_GROUPS
RET_HEADS = 4
MEM_HEADS = 4
CMP_BLK = 32
CMP_STRIDE = 16
CMP_HID = 256
SLC_BLK = 64
N_SEL = 16
WINDOW = 512
Q_BLK = 128
RET_CHUNK = 128
N_EXPERTS = 32
TOP_K = 4
D_FF = 1024
SWIGLU_LIMIT = 7.0
SWIGLU_ALPHA = 1.702
EPS = 1e-6
NEG = -1e30
BIG = 1e9
ROPE_BASE = 10000.0
SLC_RATIO = SLC_BLK // CMP_STRIDE
CMP_OVL = CMP_BLK // CMP_STRIDE - 1

Q_SCALE = HEAD_DIM ** -0.5 * 1.4426950408889634
LANES = 128
KEY_TILE = 512
MOE_ROWS = 512
VMEM_LIMIT = 56 * 1024 * 1024

C_Q = 0
C_KVC = C_Q + NSA_HEADS * LANES
C_KVS = C_KVC + 256
C_KVW = C_KVS + 256
C_GATE = C_KVW + 256
C_RQ = C_GATE + LANES
C_RK = C_RQ + 256
C_RV = C_RK + 256
C_RG = C_RV + 256
C_MQ = C_RG + 256
C_END = C_MQ + 256


def _cparams(sem):
    return pltpu.CompilerParams(dimension_semantics=sem, vmem_limit_bytes=VMEM_LIMIT)


def _lane(shape):
    return lax.broadcasted_iota(jnp.int32, shape, len(shape) - 1)


def _seg_rms(x, gain):
    lo = _lane(x.shape) < HEAD_DIM
    x2 = x * x
    s_lo = jnp.sum(jnp.where(lo, x2, 0.0), axis=-1, keepdims=True)
    s_hi = jnp.sum(jnp.where(lo, 0.0, x2), axis=-1, keepdims=True)
    ms = jnp.where(lo, s_lo, s_hi) * (1.0 / HEAD_DIM)
    return x * lax.rsqrt(ms + EPS) * gain


def _swap_halves(x):
    first = (_lane(x.shape) & (HEAD_DIM // 2)) == 0
    return jnp.where(first, pltpu.roll(x, LANES - HEAD_DIM // 2, 1), pltpu.roll(x, HEAD_DIM // 2, 1))


def _proj_kernel(x_ref, g1_ref, w_ref, cos_ref, sin_ref, qg_ref, ksg_ref, kwg_ref, mqg_ref,
                 q_out, kvc_out, kvs_out, kvw_out, ks_bf, kw_bf, gate_out,
                 rq_out, rk_out, rv_out, rg_out, mq_out, *extra, seq_tiles):
    x = x_ref[...]
    tm = x.shape[0]
    ms = jnp.mean(x * x, axis=-1, keepdims=True)
    xn = (x * lax.rsqrt(ms + EPS) * g1_ref[...]).astype(BF16)
    z = jnp.dot(xn, w_ref[...], preferred_element_type=F32)
    for i in range(NSA_HEADS):
        zq = z[:, C_Q + i * LANES:C_Q + (i + 1) * LANES]
        msq = jnp.sum(zq * zq, axis=-1, keepdims=True) * (1.0 / HEAD_DIM)
        qn = zq * lax.rsqrt(msq + EPS) * qg_ref[:, i * LANES:(i + 1) * LANES] * Q_SCALE
        if seq_tiles:
            q_out[i * LANES:(i + 1) * LANES, :] = qn.T.astype(BF16)
        else:
            q_out[:, i * LANES:(i + 1) * LANES] = qn.astype(BF16)
    kvc_out[...] = z[:, C_KVC:C_KVC + 256]
    ks = _seg_rms(z[:, C_KVS:C_KVS + LANES], ksg_ref[...])
    vs = z[:, C_KVS + LANES:C_KVS + 256]
    kvs_out[:, 0:LANES] = ks
    kvs_out[:, LANES:256] = vs
    kw = _seg_rms(z[:, C_KVW:C_KVW + LANES], kwg_ref[...])
    vw = z[:, C_KVW + LANES:C_KVW + 256]
    kvw_out[:, 0:LANES] = kw
    kvw_out[:, LANES:256] = vw
    gates = jax.nn.sigmoid(z[:, C_GATE:C_GATE + LANES])
    ks_bf[:, 0:LANES] = ks.astype(BF16)
    if seq_tiles:
        vs_t, vw_t = extra
        pos = (pl.program_id(0) % seq_tiles) * tm + lax.broadcasted_iota(jnp.int32, (tm, LANES), 0)
        ks_bf[:, LANES:256] = jnp.where(_lane((tm, LANES)) == pos // SLC_BLK, 1.0, 0.0).astype(BF16)
        kw_bf[...] = kw.astype(BF16)
        vs_t[...] = vs.T.astype(BF16)
        vw_t[...] = vw.T.astype(BF16)
        gate_out[...] = gates.T
    else:
        ks_bf[:, LANES:256] = vs.astype(BF16)
        kw_bf[:, 0:LANES] = kw.astype(BF16)
        kw_bf[:, LANES:256] = vw.astype(BF16)
        gate_out[...] = gates
    for c in range(2):
        sl = slice(c * LANES, (c + 1) * LANES)
        cos = cos_ref[:, sl]
        sin = sin_ref[:, sl]
        rq = z[:, C_RQ + c * LANES:C_RQ + (c + 1) * LANES]
        rk = z[:, C_RK + c * LANES:C_RK + (c + 1) * LANES]
        rq_out[:, sl] = rq * cos + _swap_halves(rq) * sin
        rk_out[:, sl] = (rk * cos + _swap_halves(rk) * sin) * (HEAD_DIM ** -0.5)
        mq = z[:, C_MQ + c * LANES:C_MQ + (c + 1) * LANES]
        mq_out[:, sl] = _seg_rms(mq, mqg_ref[...]).astype(BF16)
    rv_out[...] = z[:, C_RV:C_RV + 256]
    rg_out[...] = z[:, C_RG:C_RG + 256]


def _project(x, pw, cos, sin, tm, key_major):
    T, D = x.shape
    nt = cos.shape[0] // tm
    row = lambda n: (pl.BlockSpec((tm, n), lambda i: (i, 0)), (T, n))
    col = lambda n: (pl.BlockSpec((n, tm), lambda i: (0, i)), (n, T))
    full = lambda a: pl.BlockSpec(a.shape, lambda i: (0, 0))
    tab = pl.BlockSpec((tm, 256), lambda i: (i % nt, 0))
    tok = col if key_major else row
    outs = [
        (tok(NSA_HEADS * LANES), BF16),
        (row(256), F32), (row(256), F32), (row(256), F32),
        (row(256), BF16), (row(LANES if key_major else 256), BF16),
        (tok(LANES), F32),
        (row(256), F32), (row(256), F32), (row(256), F32), (row(256), F32),
        (row(256), BF16),
    ]
    if key_major:
        outs += [(col(LANES), BF16), (col(LANES), BF16)]
    return pl.pallas_call(
        functools.partial(_proj_kernel, seq_tiles=nt if key_major else 0),
        grid=(T // tm,),
        in_specs=[row(D)[0], full(pw['g1']), full(pw['w']), tab, tab,
                  full(pw['qg']), full(pw['ksg']), full(pw['kwg']), full(pw['mqg'])],
        out_specs=[spec for (spec, _), _ in outs],
        out_shape=[jax.ShapeDtypeStruct(shape, dt) for (_, shape), dt in outs],
        compiler_params=_cparams(("arbitrary",)),
        name="proj",
    )(x, pw['g1'], pw['w'], cos, sin, pw['qg'], pw['ksg'], pw['kwg'], pw['mqg'])


def _prep_proj_weights(norm1_g, w_in, q_norm_g, k_norm_slc_g, k_norm_win_g, mem_q_norm_g):
    D = w_in.shape[0]
    sizes = (512, 256, 256, 256, 24, 256, 256, 256, 256, 256)
    parts, off = [], 0
    for n in sizes:
        parts.append(w_in[:, off:off + n])
        off += n
    wq, wkvc, wkvs, wkvw, wg, wrq, wrk, wrv, wrg, wmq = parts
    zero = jnp.zeros((D, HEAD_DIM), w_in.dtype)
    qcols, qg = [], []
    gz = jnp.zeros((HEAD_DIM,), F32)
    for i in range(NSA_HEADS):
        wh = wq[:, i * HEAD_DIM:(i + 1) * HEAD_DIM]
        if i < NSA_HPG:
            qcols += [wh, zero]
            qg += [q_norm_g, gz]
        else:
            qcols += [zero, wh]
            qg += [gz, q_norm_g]
    wgp = jnp.pad(wg, ((0, 0), (0, LANES - wg.shape[1])))
    w = jnp.concatenate(qcols + [wkvc, wkvs, wkvw, wgp, wrq, wrk, wrv, wrg, wmq], axis=1).astype(BF16)
    two = lambda g: jnp.tile(g, 2)[None, :]
    return dict(g1=norm1_g[None, :], w=w, qg=jnp.concatenate(qg)[None, :],
                ksg=two(k_norm_slc_g), kwg=two(k_norm_win_g), mqg=two(mem_q_norm_g))


def _rope_tables(pos):
    half = HEAD_DIM // 2
    inv = ROPE_BASE ** (-jnp.arange(half, dtype=F32) / half)
    ang = pos.astype(F32)[:, None] * inv[None, :]
    cos, sin = jnp.cos(ang), jnp.sin(ang)
    cos = jnp.tile(jnp.concatenate([cos, cos], axis=1), (1, RET_HEADS))
    sin = jnp.tile(jnp.concatenate([-sin, sin], axis=1), (1, RET_HEADS))
    return cos, sin


def _compress_half(src_ref, pe_ref, wa_ref, wb_ref, w2_ref, nchunks):
    a_parts, b_parts = [], []
    for r in range(CMP_STRIDE):
        xr = src_ref[pl.ds(r, nchunks, stride=CMP_STRIDE), :]
        a_parts.append((xr + pe_ref[r:r + 1, :]).astype(BF16))
        b_parts.append((xr + pe_ref[CMP_STRIDE + r:CMP_STRIDE + r + 1, :]).astype(BF16))
    ha = jnp.dot(jnp.concatenate(a_parts, axis=1), wa_ref[...], preferred_element_type=F32)
    hb = jnp.dot(jnp.concatenate(b_parts, axis=1), wb_ref[...], preferred_element_type=F32)
    h = ha + pltpu.roll(hb, nchunks - 1, 0)
    act = jax.nn.gelu(h).astype(BF16)
    return jnp.dot(act, w2_ref[...], preferred_element_type=F32)


def _compress_kernel(kv_ref, pe_ref, wa_ref, wb_ref, w2_ref, kg_ref, kc_out, vct_out):
    nchunks = kc_out.shape[1]
    y = _compress_half(kv_ref.at[0], pe_ref.at[0], wa_ref.at[0], wb_ref.at[0], w2_ref.at[0], nchunks)

    @pl.when(pl.program_id(1) == 0)
    def _():
        kc_out[0] = _seg_rms(y, kg_ref[...]).astype(BF16)

    @pl.when(pl.program_id(1) == 1)
    def _():
        vct_out[0] = y.T.astype(BF16)


def _compress(kvc, cw):
    B, T, _ = kvc.shape
    nchunks = T // CMP_STRIDE
    per_kv = lambda a: pl.BlockSpec((1,) + a.shape[1:], lambda b, j: (j,) + (0,) * (a.ndim - 1))
    return pl.pallas_call(
        _compress_kernel,
        grid=(B, 2),
        in_specs=[pl.BlockSpec((1, T, LANES), lambda b, j: (b, 0, j)),
                  per_kv(cw['pe']), per_kv(cw['wa']), per_kv(cw['wb']), per_kv(cw['w2']),
                  pl.BlockSpec((1, LANES), lambda b, j: (0, 0))],
        out_specs=[pl.BlockSpec((1, nchunks, LANES), lambda b, j: (b, 0, 0)),
                   pl.BlockSpec((1, LANES, nchunks), lambda b, j: (b, 0, 0))],
        out_shape=[jax.ShapeDtypeStruct((B, nchunks, LANES), BF16),
                   jax.ShapeDtypeStruct((B, LANES, nchunks), BF16)],
        compiler_params=_cparams(("arbitrary", "arbitrary")),
        name="compress",
    )(kvc, cw['pe'], cw['wa'], cw['wb'], cw['w2'], cw['kg'])


def _prep_compress_weights(pe_k, w1_k, w2_k, pe_v, w1_v, w2_v, k_norm_g):
    eye = jnp.eye(NSA_GROUPS, dtype=F32)

    def one(pe, w1, w2):
        w1r = w1.reshape(CMP_BLK, HEAD_DIM, CMP_HID)
        wf = jnp.einsum('gh,rdc->rgdhc', eye, w1r).reshape(CMP_BLK * LANES, NSA_GROUPS * CMP_HID)
        w2f = jnp.einsum('gh,cd->gchd', eye, w2).reshape(NSA_GROUPS * CMP_HID, LANES)
        half = CMP_STRIDE * LANES
        return jnp.tile(pe, (1, NSA_GROUPS)), wf[:half].astype(BF16), wf[half:].astype(BF16), w2f.astype(BF16)

    k, v = one(pe_k, w1_k, w2_k), one(pe_v, w1_v, w2_v)
    st = lambda i: jnp.stack([k[i], v[i]])
    return dict(pe=st(0), wa=st(1), wb=st(2), w2=st(3), kg=jnp.tile(k_norm_g, 2)[None, :])


def _dot_nt(a, b):
    return lax.dot_general(a, b, (((1,), (1,)), ((), ())), preferred_element_type=F32)


def _split3(x):
    hi = x.astype(BF16)
    r = x - hi.astype(F32)
    mid = r.astype(BF16)
    lo = (r - mid.astype(F32)).astype(BF16)
    return hi, mid, lo


def _select_bias(ps, cur, rounds, axis, cur_in_range=True):
    j = lax.broadcasted_iota(jnp.int32, ps.shape, axis)
    last = cur if cur_in_range else cur - 1
    key = jnp.where(j <= last, ps, NEG)
    for forced in (0, last, cur - 1):
        key = jnp.where(j == forced, BIG, key)
    bias = jnp.full(ps.shape, NEG, F32)
    for _ in range(rounds):
        m = jnp.max(key, axis=axis, keepdims=True)
        idx = jnp.min(jnp.where(key == m, j, LANES), axis=axis, keepdims=True)
        pick = j == idx
        bias = jnp.where(pick, 0.0, bias)
        key = jnp.where(pick, -jnp.inf, key)
    return bias


def _cmp_branch(qg, qpos, kc, vc):
    s = _dot_nt(qg, kc)
    cend = _lane((1, kc.shape[0])) * CMP_STRIDE + (CMP_BLK - 1)
    s = jnp.where(cend <= qpos, s, NEG)
    m = jnp.max(s, axis=-1, keepdims=True)
    p = jnp.exp2(s - m)
    p = p * (1.0 / jnp.sum(p, axis=-1, keepdims=True))
    p = jnp.where(qpos >= CMP_BLK - 1, p, 0.0)
    return p, jnp.dot(p.astype(BF16), vc, preferred_element_type=F32)


def _gate_heads(gates, g, oc, osel, ow, nq):
    heads = []
    for h in range(NSA_HPG):
        rs = slice(h * nq, (h + 1) * nq)
        c0 = (NSA_HPG * g + h) * 3
        heads.append(gates[:, c0:c0 + 1] * oc[rs] + gates[:, c0 + 1:c0 + 2] * osel[rs]
                     + gates[:, c0 + 2:c0 + 3] * ow[rs])
    return heads


def _store_heads(heads, o_ref):
    low = _lane(heads[0].shape) < HEAD_DIM
    for pr in range(NSA_HEADS // 2):
        even, odd = heads[2 * pr], heads[2 * pr + 1]
        if pr < NSA_GROUPS:
            odd = pltpu.roll(odd, HEAD_DIM, 1)
        else:
            even = pltpu.roll(even, HEAD_DIM, 1)
        o_ref[0, :, pr * LANES:(pr + 1) * LANES] = jnp.where(low, even, odd).astype(o_ref.dtype)


def _nsa_decode_kernel(pt_ref, q_ref, gate_ref, ksn_ref, kwn_ref, win_ref, pool_c, pool_s,
                       pe_ref, wa_ref, wb_ref, w2_ref, kg_ref, mmap_ref, hot_ref, o_ref,
                       cbuf, sbuf, xrow, sems, *, page, n_new):
    b = pl.program_id(0)
    past = sbuf.shape[1]
    npages = past // page
    nq = q_ref.shape[1]
    dot = functools.partial(jnp.dot, preferred_element_type=F32)

    slot = b % 2

    def cmp_copy(seq, p, sl):
        dst = pl.ds(pl.multiple_of(p * page, page), page)
        return pltpu.make_async_copy(pool_c.at[pt_ref[seq * npages + p]], cbuf.at[sl, :, dst], sems.at[sl])

    def slc_copy(seq, p):
        dst = pl.ds(pl.multiple_of(p * page, page), page)
        return pltpu.make_async_copy(pool_s.at[pt_ref[seq * npages + p]], sbuf.at[:, dst], sems.at[2])

    def for_pages(fn):
        def body(p, c):
            fn(p)
            return c
        lax.fori_loop(0, npages, body, 0)

    @pl.when(b == 0)
    def _():
        for_pages(lambda p: cmp_copy(0, p, 0).start())

    for_pages(lambda p: slc_copy(b, p).start())

    @pl.when(b + 1 < pl.num_programs(0))
    def _():
        for_pages(lambda p: cmp_copy(b + 1, p, 1 - slot).start())

    for_pages(lambda p: cmp_copy(b, p, slot).wait())

    tchunk = 4 * LANES
    for half in range(2):
        for c in range(past // tchunk):
            xrow[half, c * tchunk:(c + 1) * tchunk, :] = cbuf[slot, half * LANES:(half + 1) * LANES,
                                                              c * tchunk:(c + 1) * tchunk].T
    nchunks = past // CMP_STRIDE
    kc = _seg_rms(_compress_half(xrow.at[0], pe_ref.at[0], wa_ref.at[0], wb_ref.at[0], w2_ref.at[0], nchunks),
                  kg_ref[...]).astype(BF16)
    vc = _compress_half(xrow.at[1], pe_ref.at[1], wa_ref.at[1], wb_ref.at[1], w2_ref.at[1], nchunks).astype(BF16)

    rows = NSA_HPG * nq
    srow = lax.broadcasted_iota(jnp.int32, (rows, 1), 0) % nq
    qpos = past + srow
    gates = gate_ref[0]
    pad_keys = lambda x: jnp.concatenate([x, jnp.zeros((LANES - nq, LANES), F32)], axis=0).astype(BF16)
    new_ok = _lane((1, LANES)) <= jnp.minimum(srow, n_new - 1)
    ks_new, vs_new = pad_keys(ksn_ref[0, :, 0:LANES]), pad_keys(ksn_ref[0, :, LANES:256])
    kw_new, vw_new = pad_keys(kwn_ref[0, :, 0:LANES]), pad_keys(kwn_ref[0, :, LANES:256])

    for_pages(lambda p: slc_copy(b, p).wait())
    k_aug_t = jnp.concatenate([sbuf[0:LANES, :].astype(BF16), hot_ref[...]], axis=0)
    vs_t = sbuf[LANES:256, :].astype(BF16)
    nwin = win_ref.shape[2]
    kw_t, vw_t = win_ref[0, 0:LANES, :].astype(BF16), win_ref[0, LANES:256, :].astype(BF16)

    qgs, ocs, p4s = [], [], []
    for g in range(NSA_GROUPS):
        qg = jnp.concatenate(
            [q_ref[0, :, (NSA_HPG * g + h) * LANES:(NSA_HPG * g + h + 1) * LANES] for h in range(NSA_HPG)],
            axis=0).astype(BF16)
        p, oc = _cmp_branch(qg, qpos, kc, vc)
        qgs.append(qg)
        ocs.append(oc)
        p4s.append(p[0:nq] + p[nq:2 * nq] + p[2 * nq:3 * nq] + p[3 * nq:4 * nq])
    p4 = jnp.concatenate(p4s, axis=0)
    ps_t = sum(_dot_nt(mmap_ref[...], part) for part in _split3(p4))
    cur = (past + _lane((1, NSA_GROUPS * nq)) % nq) // SLC_BLK
    bias_t = _select_bias(ps_t, cur, N_SEL - 1, 0, cur_in_range=False)
    bias_all = jnp.concatenate([bias_t, jnp.zeros((LANES, LANES - NSA_GROUPS * nq), F32)], axis=1).T

    heads = []
    for g in range(NSA_GROUPS):
        qg, oc = qgs[g], ocs[g]
        bias = bias_all[g * nq:(g + 1) * nq]
        qaug = jnp.concatenate([qg, jnp.concatenate([bias] * NSA_HPG, axis=0).astype(BF16)], axis=1)
        s_old = dot(qaug, k_aug_t)
        s_new = jnp.where(new_ok, _dot_nt(qg, ks_new), NEG)
        m = jnp.maximum(jnp.max(s_old, axis=-1, keepdims=True), jnp.max(s_new, axis=-1, keepdims=True))
        p_old, p_new = jnp.exp2(s_old - m), jnp.exp2(s_new - m)
        inv = 1.0 / (jnp.sum(p_old, axis=-1, keepdims=True) + jnp.sum(p_new, axis=-1, keepdims=True))
        osel = (_dot_nt(p_old.astype(BF16), vs_t) + dot(p_new.astype(BF16), vs_new)) * inv
        sw_old = jnp.where(_lane((1, nwin)) + (WINDOW - nwin) > srow, dot(qg, kw_t), NEG)
        sw_new = jnp.where(new_ok, _dot_nt(qg, kw_new), NEG)
        mw = jnp.maximum(jnp.max(sw_old, axis=-1, keepdims=True), jnp.max(sw_new, axis=-1, keepdims=True))
        pw_old, pw_new = jnp.exp2(sw_old - mw), jnp.exp2(sw_new - mw)
        invw = 1.0 / (jnp.sum(pw_old, axis=-1, keepdims=True) + jnp.sum(pw_new, axis=-1, keepdims=True))
        ow = (_dot_nt(pw_old.astype(BF16), vw_t) + dot(pw_new.astype(BF16), vw_new)) * invw
        heads += _gate_heads(gates, g, oc, osel, ow, nq)
    _store_heads(heads, o_ref)


def _nsa_decode(q, gates, ksn, kwn, win_t, pool_c, pool_s, page_table, cw, n_new):
    DB, nq, _ = q.shape
    n_pages = page_table.shape[1]
    page = pool_c.shape[2]
    past = n_pages * page
    nchunks = past // CMP_STRIDE
    mmap = _importance_map(nchunks)
    hot = (jnp.arange(LANES, dtype=jnp.int32)[:, None] == jnp.arange(past, dtype=jnp.int32)[None, :] // SLC_BLK)
    hot = hot.astype(BF16)
    per_seq = lambda a: pl.BlockSpec((1,) + a.shape[1:], lambda b, pt: (b,) + (0,) * (a.ndim - 1))
    const = lambda a: pl.BlockSpec(a.shape, lambda b, pt: (0,) * a.ndim, pipeline_mode=pl.Buffered(1))
    hbm = pl.BlockSpec(memory_space=pl.ANY)
    grid_spec = pltpu.PrefetchScalarGridSpec(
        num_scalar_prefetch=1,
        grid=(DB,),
        in_specs=[per_seq(q), per_seq(gates), per_seq(ksn), per_seq(kwn), per_seq(win_t), hbm, hbm,
                  const(cw['pe']), const(cw['wa']), const(cw['wb']), const(cw['w2']), const(cw['kg']), const(mmap),
                  const(hot)],
        out_specs=pl.BlockSpec((1, nq, NSA_HEADS * HEAD_DIM), lambda b, pt: (b, 0, 0)),
        scratch_shapes=[pltpu.VMEM((2, 256, past), F32), pltpu.VMEM((256, past), F32),
                        pltpu.VMEM((2, past, LANES), F32), pltpu.SemaphoreType.DMA((3,))],
    )
    return pl.pallas_call(
        functools.partial(_nsa_decode_kernel, page=page, n_new=n_new),
        grid_spec=grid_spec,
        out_shape=jax.ShapeDtypeStruct((DB, nq, NSA_HEADS * HEAD_DIM), F32),
        compiler_params=_cparams(("arbitrary",)),
        name="nsa_sample",
    )(page_table.reshape(-1), q, gates, ksn, kwn, win_t, pool_c, pool_s,
      cw['pe'], cw['wa'], cw['wb'], cw['w2'], cw['kg'], mmap, hot)


def _nsa_prompt_kernel(q_ref, gate_ref, kc_ref, vct_ref, ks_ref, vst_ref, kw_ref, vwt_ref, mmap_ref, o_ref,
                       m_sc, l_sc, acc_sc, sa_sc, sb_sc):
    i = pl.program_id(1)
    s0 = i * Q_BLK
    cols = NSA_HPG * Q_BLK
    qpos = s0 + _lane((1, cols)) % Q_BLK
    cur = (s0 + _lane((1, Q_BLK))) // SLC_BLK
    kc, vct = kc_ref[0], vct_ref[0]
    ncmp = kc.shape[0]
    n_full = s0 // KEY_TILE
    w0 = pl.multiple_of(jnp.maximum(s0 - WINDOW, 0), Q_BLK)
    nw = WINDOW + Q_BLK
    dot = functools.partial(jnp.dot, preferred_element_type=F32)

    def softmax_keys(s):
        p = jnp.exp2(s - jnp.max(s, axis=0, keepdims=True))
        return p, 1.0 / jnp.sum(p, axis=0, keepdims=True)

    qaugs = []
    branch = []
    n_idx = lax.broadcasted_iota(jnp.int32, (ncmp, cols), 0)
    n_last = lax.shift_right_arithmetic(qpos - (CMP_BLK - 1), CMP_STRIDE.bit_length() - 1)
    r_idx = lax.broadcasted_iota(jnp.int32, (nw, cols), 0)
    r_last = qpos - w0
    for g in range(NSA_GROUPS):
        vg = slice(g * HEAD_DIM, (g + 1) * HEAD_DIM)
        qt = jnp.concatenate([q_ref[(NSA_HPG * g + h) * LANES:(NSA_HPG * g + h + 1) * LANES, :]
                              for h in range(NSA_HPG)], axis=1)
        s = jnp.where(n_idx <= n_last, dot(kc, qt), NEG)
        p, inv = softmax_keys(s)
        p = p * jnp.where(n_last >= 0, inv, 0.0)
        oc = dot(vct[vg], p.astype(BF16))
        p4 = p[:, 0:Q_BLK] + p[:, Q_BLK:2 * Q_BLK] + p[:, 2 * Q_BLK:3 * Q_BLK] + p[:, 3 * Q_BLK:4 * Q_BLK]
        ps_t = sum(dot(mmap_ref[...], part) for part in _split3(p4))
        bias = _select_bias(ps_t, cur, N_SEL, 0).astype(BF16)
        qaugs.append(jnp.concatenate([qt, jnp.concatenate([bias] * NSA_HPG, axis=1)], axis=0))
        sw = dot(kw_ref[0, pl.ds(w0, nw), :], qt)
        sw = jnp.where(r_idx <= r_last, jnp.where(r_idx > r_last - WINDOW, sw, NEG), NEG)
        pw, invw = softmax_keys(sw)
        ow = dot(vwt_ref[pl.ds(g * HEAD_DIM, HEAD_DIM), pl.ds(w0, nw)], pw.astype(BF16)) * invw
        branch.append((oc, ow))
        m_sc[g] = jnp.full(m_sc.shape[1:], NEG, F32)
        l_sc[g] = jnp.zeros(l_sc.shape[1:], F32)
        acc_sc[g] = jnp.zeros(acc_sc.shape[1:], F32)

    def scores(kt, s_ref):
        k_aug = ks_ref[0, pl.ds(pl.multiple_of(kt * KEY_TILE, KEY_TILE), KEY_TILE), :]
        for g in range(NSA_GROUPS):
            s_ref[g] = dot(k_aug, qaugs[g])

    def consume(kt, s_ref, causal):
        k0 = pl.multiple_of(kt * KEY_TILE, KEY_TILE)
        v_t = vst_ref[:, pl.ds(k0, KEY_TILE)]
        for g in range(NSA_GROUPS):
            s = s_ref[g]
            if causal:
                s = jnp.where(k0 + lax.broadcasted_iota(jnp.int32, s.shape, 0) <= qpos, s, NEG)
            m_old = m_sc[g]
            m_new = jnp.maximum(m_old, jnp.max(s, axis=0, keepdims=True))
            alpha = jnp.exp2(m_old - m_new)
            p = jnp.exp2(s - m_new)
            l_sc[g] = alpha * l_sc[g] + jnp.sum(p, axis=0, keepdims=True)
            acc_sc[g] = alpha * acc_sc[g] + dot(v_t[g * HEAD_DIM:(g + 1) * HEAD_DIM], p.astype(BF16))
            m_sc[g] = m_new

    def two_tiles(u, c):
        scores(2 * u + 1, sb_sc)
        consume(2 * u, sa_sc, False)
        scores(2 * u + 2, sa_sc)
        consume(2 * u + 1, sb_sc, False)
        return c

    scores(0, sa_sc)
    lax.fori_loop(0, n_full // 2, two_tiles, 0)

    @pl.when(n_full % 2 == 1)
    def _():
        scores(n_full, sb_sc)
        consume(n_full - 1, sa_sc, False)
        consume(n_full, sb_sc, True)

    @pl.when(n_full % 2 == 0)
    def _():
        consume(n_full, sa_sc, True)

    pieces = []
    for g in range(NSA_GROUPS):
        oc, ow = branch[g]
        osel = acc_sc[g] * (1.0 / l_sc[g])
        for h in range(NSA_HPG):
            cs = slice(h * Q_BLK, (h + 1) * Q_BLK)
            c0 = (NSA_HPG * g + h) * 3
            pieces.append(gate_ref[c0:c0 + 1, :] * oc[:, cs] + gate_ref[c0 + 1:c0 + 2, :] * osel[:, cs]
                          + gate_ref[c0 + 2:c0 + 3, :] * ow[:, cs])
    o_ref[0] = jnp.concatenate(pieces, axis=0).T.astype(o_ref.dtype)


def _importance_map(ncmp):
    j = jnp.arange(LANES)[:, None]
    n = jnp.arange(ncmp)[None, :]
    return ((n >= SLC_RATIO * j - CMP_OVL) & (n < SLC_RATIO * j + SLC_RATIO)).astype(BF16)


def _nsa_prompt(B, qt, gates_t, kc, vct, ks_aug, vs_t, kw, vw_t):
    S = qt.shape[1] // B
    nq = S // Q_BLK
    ncmp = kc.shape[1]
    cols = NSA_HPG * Q_BLK
    mmap = _importance_map(ncmp)
    per_block = lambda n: pl.BlockSpec((n, Q_BLK), lambda b, i: (0, b * nq + i))
    seq_rows = lambda n: pl.BlockSpec((1, S, n), lambda b, i: (b, 0, 0))
    seq_cols = pl.BlockSpec((LANES, S), lambda b, i: (0, b))
    return pl.pallas_call(
        _nsa_prompt_kernel,
        grid=(B, nq),
        in_specs=[per_block(qt.shape[0]), per_block(LANES),
                  pl.BlockSpec((1, ncmp, LANES), lambda b, i: (b, 0, 0)),
                  pl.BlockSpec((1, LANES, ncmp), lambda b, i: (b, 0, 0)),
                  seq_rows(256), seq_cols, seq_rows(LANES), seq_cols,
                  pl.BlockSpec(mmap.shape, lambda b, i: (0, 0))],
        out_specs=pl.BlockSpec((1, Q_BLK, NSA_HEADS * HEAD_DIM), lambda b, i: (b, i, 0)),
        out_shape=jax.ShapeDtypeStruct((B, S, NSA_HEADS * HEAD_DIM), BF16),
        scratch_shapes=[pltpu.VMEM((NSA_GROUPS, 1, cols), F32), pltpu.VMEM((NSA_GROUPS, 1, cols), F32),
                        pltpu.VMEM((NSA_GROUPS, HEAD_DIM, cols), F32),
                        pltpu.VMEM((NSA_GROUPS, KEY_TILE, cols), F32), pltpu.VMEM((NSA_GROUPS, KEY_TILE, cols), F32)],
        compiler_params=_cparams(("arbitrary", "arbitrary")),
        name="nsa_prompt",
    )(qt, gates_t, kc, vct, ks_aug.reshape(B, S, 256), vs_t, kw.reshape(B, S, LANES), vw_t, mmap)


def _ret_kernel(rq_ref, rk_ref, rv_ref, r0_ref, dmask_ref, xi_ref, zeta_ref, dec_ref, o_ref, rout_ref, r_sc):
    @pl.when(pl.program_id(1) == 0)
    def _():
        r_sc[...] = r0_ref[0]

    C = rq_ref.shape[1]
    low = _lane((C, LANES)) < HEAD_DIM
    diag = lax.broadcasted_iota(jnp.int32, (LANES, LANES), 0) // HEAD_DIM == _lane((LANES, LANES)) // HEAD_DIM
    for pr in range(RET_HEADS // 2):
        sl = slice(pr * LANES, (pr + 1) * LANES)
        k = rk_ref[0, :, sl]
        qb, kb, vb = rq_ref[0, :, sl].astype(BF16), k.astype(BF16), rv_ref[0, :, sl].astype(BF16)
        zero = jnp.zeros_like(qb)
        s0 = _dot_nt(jnp.where(low, qb, zero), kb) * dmask_ref[2 * pr]
        s1 = _dot_nt(jnp.where(low, zero, qb), kb) * dmask_ref[2 * pr + 1]
        o = jnp.where(low, jnp.dot(s0.astype(BF16), vb, preferred_element_type=F32),
                      jnp.dot(s1.astype(BF16), vb, preferred_element_type=F32))
        r = r_sc[pr]
        o_ref[0, :, sl] = o + jnp.dot(qb, r.astype(BF16), preferred_element_type=F32) * xi_ref[:, sl]
        kz = (k * zeta_ref[:, sl]).astype(BF16)
        upd = lax.dot_general(kz, vb, (((0,), (0,)), ((), ())), preferred_element_type=F32)
        r_sc[pr] = dec_ref[:, sl] * r + jnp.where(diag, upd, 0.0)
    rout_ref[0] = r_sc[...]


def _ret_tables(c_true, c_pad):
    lg = jnp.log(1.0 - 2.0 ** (-5.0 - jnp.arange(RET_HEADS, dtype=F32)))
    idx = jnp.arange(c_pad, dtype=F32)
    diff = idx[:, None] - idx[None, :]
    dmask = jnp.where(diff >= 0, jnp.exp(jnp.maximum(diff, 0.0)[None] * lg[:, None, None]), 0.0)
    lanes = lambda a: jnp.repeat(a, HEAD_DIM, axis=-1)
    xi = lanes(jnp.exp((idx + 1.0)[:, None] * lg[None, :]))
    zeta = lanes(jnp.exp((c_true - 1.0 - idx)[:, None] * lg[None, :]))
    dec = lanes(jnp.exp(c_true * lg)[None, :])
    return dmask, xi, zeta, dec


def _retention(rq, rk, rv, r0, c_true, c_pad):
    B, T, _ = rq.shape
    dmask, xi, zeta, dec = _ret_tables(c_true, c_pad)
    row = pl.BlockSpec((1, c_pad, 256), lambda b, c: (b, c, 0))
    st = pl.BlockSpec((1, 2, LANES, LANES), lambda b, c: (b, 0, 0, 0))
    const = lambda a: pl.BlockSpec(a.shape, lambda b, c: (0,) * a.ndim)
    return pl.pallas_call(
        _ret_kernel,
        grid=(B, T // c_pad),
        in_specs=[row, row, row, st, const(dmask), const(xi), const(zeta), const(dec)],
        out_specs=[row, st],
        out_shape=[jax.ShapeDtypeStruct((B, T, 256), F32), jax.ShapeDtypeStruct((B, 2, LANES, LANES), F32)],
        scratch_shapes=[pltpu.VMEM((2, LANES, LANES), F32)],
        compiler_params=_cparams(("arbitrary", "arbitrary")),
        name="retention",
    )(rq, rk, rv, r0, dmask, xi, zeta, dec)


def _state_to_pairs(r):
    B = r.shape[0]
    r = r.reshape(B, 2, 2, HEAD_DIM, HEAD_DIM)
    eye = jnp.eye(2, dtype=r.dtype)
    return jnp.einsum('bphde,hk->bphdke', r, eye).reshape(B, 2, LANES, LANES)


def _pairs_to_state(rp):
    B = rp.shape[0]
    rp = rp.reshape(B, 2, 2, HEAD_DIM, 2, HEAD_DIM)
    return jnp.stack([rp[:, :, 0, :, 0, :], rp[:, :, 1, :, 1, :]], axis=2).reshape(B, RET_HEADS, HEAD_DIM, HEAD_DIM)


def _mem_kv_kernel(m_ref, g_ref, w_ref, kg_ref, kv_out):
    x = m_ref[...]
    ms = jnp.mean(x * x, axis=-1, keepdims=True)
    xn = (x * lax.rsqrt(ms + EPS) * g_ref[...]).astype(BF16)
    z = jnp.dot(xn, w_ref[...], preferred_element_type=F32)
    for c in range(2):
        kv_out[:, c * LANES:(c + 1) * LANES] = _seg_rms(z[:, c * LANES:(c + 1) * LANES], kg_ref[...])
    kv_out[:, 256:512] = z[:, 256:512]


def _mem_kv(mem, mem_norm_g, w_mem_kv, mem_k_norm_g):
    T, D = mem.shape
    w = w_mem_kv.astype(BF16)
    full = lambda a: pl.BlockSpec(a.shape, lambda i: (0, 0))
    g = mem_norm_g[None, :]
    kg = jnp.tile(mem_k_norm_g, 2)[None, :]
    return pl.pallas_call(
        _mem_kv_kernel,
        grid=(1,),
        in_specs=[full(mem), full(g), full(w), full(kg)],
        out_specs=pl.BlockSpec((T, 512), lambda i: (0, 0)),
        out_shape=jax.ShapeDtypeStruct((T, 512), F32),
        compiler_params=_cparams(("arbitrary",)),
        name="mem_kv",
    )(mem, g, w, kg)


def _mem_attn_kernel(mq_ref, mkv_ref, o_ref):
    q = mq_ref[0]
    rows = q.shape[0]
    low = _lane((rows, LANES)) < HEAD_DIM
    zero = jnp.zeros((rows, LANES), F32)
    for pr in range(MEM_HEADS // 2):
        sl = slice(pr * LANES, (pr + 1) * LANES)
        qp = q[:, sl].astype(F32)
        k = mkv_ref[0, :, sl].astype(BF16)
        v = mkv_ref[0, :, 256 + pr * LANES:256 + (pr + 1) * LANES].astype(BF16)
        outs = []
        for hh in range(2):
            qm = jnp.where(low, qp, zero) if hh == 0 else jnp.where(low, zero, qp)
            s = _dot_nt(qm.astype(BF16), k) * (HEAD_DIM ** -0.5)
            m = jnp.max(s, axis=-1, keepdims=True)
            p = jnp.exp(s - m)
            o = jnp.dot(p.astype(BF16), v, preferred_element_type=F32)
            outs.append(o * (1.0 / jnp.sum(p, axis=-1, keepdims=True)))
        o_ref[0, :, sl] = jnp.where(low, outs[0], outs[1]).astype(o_ref.dtype)


def _mem_attn(mq, mkv, tm):
    B, R, _ = mq.shape
    return pl.pallas_call(
        _mem_attn_kernel,
        grid=(B, R // tm),
        in_specs=[pl.BlockSpec((1, tm, 256), lambda b, i: (b, i, 0)),
                  pl.BlockSpec((1,) + mkv.shape[1:], lambda b, i: (b, 0, 0))],
        out_specs=pl.BlockSpec((1, tm, 256), lambda b, i: (b, i, 0)),
        out_shape=jax.ShapeDtypeStruct((B, R, 256), BF16),
        compiler_params=_cparams(("arbitrary", "arbitrary")),
        name="mem_attn",
    )(mq, mkv)


TOK_ROWS = 8


def _load_token_tiles(ref, n):
    return jnp.concatenate([ref[pl.ds(c, n, stride=TOK_ROWS), :] for c in range(TOK_ROWS)], axis=1)


def _store_token_tiles(ref, x):
    n = x.shape[0]
    for c in range(TOK_ROWS):
        ref[pl.ds(c, n, stride=TOK_ROWS), :] = x[:, c * LANES:(c + 1) * LANES]


def _mix_kernel(x_ref, onsa_ref, oret_ref, rg_ref, omem_ref, wout_ref, rgain_ref, g2_ref,
                wr_hi_ref, wr_lo_ref, br_ref, x1_out, h_out, topi_out, topg_out):
    parts = [onsa_ref[...]]
    for c in range(2):
        sl = slice(c * LANES, (c + 1) * LANES)
        parts.append((_seg_rms(oret_ref[:, sl], rgain_ref[...]) * jax.nn.silu(rg_ref[:, sl])).astype(BF16))
    parts.append(omem_ref[...])
    mix = jnp.concatenate(parts, axis=1)
    x1 = x_ref[...] + jnp.dot(mix, wout_ref[...], preferred_element_type=F32)
    x1_out[...] = x1
    ms = jnp.mean(x1 * x1, axis=-1, keepdims=True)
    h = x1 * lax.rsqrt(ms + EPS) * g2_ref[...]
    _store_token_tiles(h_out, h)
    h_hi, h_lo, _ = _split3(h)
    logits = (jnp.dot(h_hi, wr_hi_ref[...], preferred_element_type=F32)
              + jnp.dot(h_hi, wr_lo_ref[...], preferred_element_type=F32)
              + jnp.dot(h_lo, wr_hi_ref[...], preferred_element_type=F32)) + br_ref[...]
    lane = _lane(logits.shape)
    key = logits
    topi = jnp.zeros(logits.shape, jnp.int32)
    topv = jnp.zeros(logits.shape, F32)
    for r in range(TOP_K):
        m = jnp.max(key, axis=-1, keepdims=True)
        idx = jnp.min(jnp.where(key == m, lane, LANES), axis=-1, keepdims=True)
        if r == 0:
            m0 = m
        topi = jnp.where(lane == r, idx, topi)
        topv = jnp.where(lane == r, jnp.exp(m - m0), topv)
        key = jnp.where(lane == idx, -jnp.inf, key)
    topi_out[...] = topi
    topg_out[...] = topv * (1.0 / jnp.sum(topv, axis=-1, keepdims=True))


def _mix(x, onsa, oret, rg, omem, mw, tm):
    T, D = x.shape
    row = lambda n: pl.BlockSpec((tm, n), lambda i: (i, 0))
    full = lambda a: pl.BlockSpec(a.shape, lambda i: (0, 0))
    names = ('wout', 'rgain', 'g2', 'wr_hi', 'wr_lo', 'br')
    return pl.pallas_call(
        _mix_kernel,
        grid=(T // tm,),
        in_specs=[row(D), row(512), row(256), row(256), row(256)] + [full(mw[n]) for n in names],
        out_specs=[row(D), pl.BlockSpec((tm * TOK_ROWS, LANES), lambda i: (i, 0)), row(LANES), row(LANES)],
        out_shape=[jax.ShapeDtypeStruct((T, D), F32), jax.ShapeDtypeStruct((T * TOK_ROWS, LANES), F32),
                   jax.ShapeDtypeStruct((T, LANES), jnp.int32), jax.ShapeDtypeStruct((T, LANES), F32)],
        compiler_params=_cparams(("arbitrary",)),
        name="mix",
    )(x, onsa, oret, rg, omem, *[mw[n] for n in names])


def _prep_mix_weights(w_out, ret_norm_g, norm2_g, w_router, b_router):
    wr = jnp.pad(w_router, ((0, 0), (0, LANES - N_EXPERTS)))
    wr_hi = wr.astype(BF16)
    wr_lo = (wr - wr_hi.astype(F32)).astype(BF16)
    br = jnp.concatenate([b_router.astype(F32), jnp.full((LANES - N_EXPERTS,), NEG, F32)])[None, :]
    return dict(wout=w_out.astype(BF16), rgain=jnp.tile(ret_norm_g, 2)[None, :], g2=norm2_g[None, :],
                wr_hi=wr_hi, wr_lo=wr_lo, br=br)


MOE_TOK_TILE = 128


def _row_copy(src, i, dst, j, sem):
    return pltpu.make_async_copy(src.at[pl.ds(pl.multiple_of(i, TOK_ROWS), TOK_ROWS)],
                                 dst.at[pl.ds(pl.multiple_of(j, TOK_ROWS), TOK_ROWS)], sem)


def _dispatch_kernel(pad_ref, dest_ref, h_ref, *rest, first):
    if first:
        xb_out, sem, zeros, zsem = rest
        blk_rows = zeros.shape[0]

        @pl.when(pl.program_id(0) == 0)
        def _():
            zeros[...] = jnp.zeros(zeros.shape, zeros.dtype)

            def pad_copy(e):
                row = pl.multiple_of(pad_ref[e], blk_rows)
                return pltpu.make_async_copy(zeros, xb_out.at[pl.ds(row, blk_rows)], zsem)

            for e in range(N_EXPERTS):
                @pl.when(pad_ref[N_EXPERTS + e] > 0)
                def _(e=e):
                    pad_copy(e).start()
            for e in range(N_EXPERTS):
                @pl.when(pad_ref[N_EXPERTS + e] > 0)
                def _(e=e):
                    pad_copy(e).wait()
    else:
        _, xb_out, sem = rest
    tile = dest_ref.shape[0] // TOP_K

    def start(t, c):
        for k in range(TOP_K):
            _row_copy(h_ref, t * TOK_ROWS, xb_out, dest_ref[t * TOP_K + k], sem).start()
        return c

    def wait(t, c):
        for k in range(TOP_K):
            _row_copy(h_ref, 0, xb_out, 0, sem).wait()
        return c

    lax.fori_loop(0, tile, start, 0, unroll=8)
    lax.fori_loop(0, tile, wait, 0, unroll=8)


def _dispatch(h, dest, pad_info, xb=None, xb_rows=None):
    T = h.shape[0] // TOK_ROWS
    first = xb is None
    hbm = pl.BlockSpec(memory_space=pl.ANY)
    n = MOE_TOK_TILE * TOP_K
    blk_rows = MOE_ROWS * TOK_ROWS
    grid_spec = pltpu.PrefetchScalarGridSpec(
        num_scalar_prefetch=1,
        grid=(T // MOE_TOK_TILE,),
        in_specs=[pl.BlockSpec((n,), lambda i, pad: (i,), memory_space=pltpu.SMEM),
                  pl.BlockSpec((MOE_TOK_TILE * TOK_ROWS, LANES), lambda i, pad: (i, 0))] + ([] if first else [hbm]),
        out_specs=hbm,
        scratch_shapes=[pltpu.SemaphoreType.DMA(())] + (
            [pltpu.VMEM((blk_rows, LANES), F32), pltpu.SemaphoreType.DMA(())] if first else []),
    )
    shape = (xb_rows, LANES) if first else xb.shape
    return pl.pallas_call(
        functools.partial(_dispatch_kernel, first=first),
        grid_spec=grid_spec,
        out_shape=jax.ShapeDtypeStruct(shape, F32),
        input_output_aliases={} if first else {3: 0},
        compiler_params=_cparams(("arbitrary",)),
        name="moe_dispatch",
    )(pad_info, dest.reshape(-1), h, *([] if first else [xb]))


def _moe_kernel(be_ref, nb_ref, x_ref, wup_ref, bup_ref, wdn_ref, bdn_ref, y_ref, wup_bf, wdn_bf):
    j = pl.program_id(0)

    @pl.when(j < nb_ref[0])
    def _():
        @pl.when(jnp.logical_or(j == 0, be_ref[j] != be_ref[jnp.maximum(j - 1, 0)]))
        def _():
            wup_bf[...] = wup_ref[0].astype(BF16)
            wdn_bf[...] = wdn_ref[0].astype(BF16)

        x = _load_token_tiles(x_ref, MOE_ROWS).astype(BF16)
        up = jnp.dot(x, wup_bf[...], preferred_element_type=F32) + bup_ref[0]
        x_glu = jnp.minimum(up[:, :D_FF], SWIGLU_LIMIT)
        x_lin = jnp.clip(up[:, D_FF:], -SWIGLU_LIMIT, SWIGLU_LIMIT)
        act = x_glu * jax.nn.sigmoid(SWIGLU_ALPHA * x_glu) * (x_lin + 1.0)
        _store_token_tiles(y_ref, jnp.dot(act.astype(BF16), wdn_bf[...], preferred_element_type=F32) + bdn_ref[0])


def _moe_experts(xb, blk_expert, n_used, w_up, b_up, w_down, b_down):
    D = w_up.shape[1]
    blk = pl.BlockSpec((MOE_ROWS * TOK_ROWS, LANES), lambda j, be, nb: (jnp.minimum(j, nb[0] - 1), 0))
    grid_spec = pltpu.PrefetchScalarGridSpec(
        num_scalar_prefetch=2,
        grid=(xb.shape[0] // (MOE_ROWS * TOK_ROWS),),
        in_specs=[blk,
                  pl.BlockSpec((1, D, 2 * D_FF), lambda j, be, nb: (be[j], 0, 0)),
                  pl.BlockSpec((1, 1, 2 * D_FF), lambda j, be, nb: (be[j], 0, 0)),
                  pl.BlockSpec((1, D_FF, D), lambda j, be, nb: (be[j], 0, 0)),
                  pl.BlockSpec((1, 1, D), lambda j, be, nb: (be[j], 0, 0))],
        out_specs=blk,
        scratch_shapes=[pltpu.VMEM((D, 2 * D_FF), BF16), pltpu.VMEM((D_FF, D), BF16)],
    )
    return pl.pallas_call(
        _moe_kernel,
        grid_spec=grid_spec,
        out_shape=jax.ShapeDtypeStruct(xb.shape, F32),
        compiler_params=_cparams(("arbitrary",)),
        name="moe_experts",
    )(blk_expert, n_used, xb, w_up, b_up[:, None, :], w_down, b_down[:, None, :])


def _combine_kernel(dest_ref, g_ref, x1_ref, yb_hbm, out_ref, buf, sem):
    tile = x1_ref.shape[0]

    def start(t, c):
        for k in range(TOP_K):
            _row_copy(yb_hbm, dest_ref[t * TOP_K + k], buf.at[k], t * TOK_ROWS, sem).start()
        return c

    def wait(t, c):
        for k in range(TOP_K):
            _row_copy(yb_hbm, 0, buf.at[k], 0, sem).wait()
        return c

    lax.fori_loop(0, tile, start, 0, unroll=8)
    lax.fori_loop(0, tile, wait, 0, unroll=8)
    acc = x1_ref[...]
    for k in range(TOP_K):
        acc = acc + g_ref[:, k:k + 1] * _load_token_tiles(buf.at[k], tile)
    out_ref[...] = acc


def _combine(x1, topg, dest, yb):
    T, D = x1.shape
    tile = MOE_TOK_TILE
    row = lambda n: pl.BlockSpec((tile, n), lambda i: (i, 0))
    return pl.pallas_call(
        _combine_kernel,
        grid=(T // tile,),
        in_specs=[pl.BlockSpec((tile * TOP_K,), lambda i: (i,), memory_space=pltpu.SMEM),
                  row(LANES), row(D), pl.BlockSpec(memory_space=pl.ANY)],
        out_specs=row(D),
        out_shape=jax.ShapeDtypeStruct((T, D), F32),
        scratch_shapes=[pltpu.VMEM((TOP_K, tile * TOK_ROWS, LANES), F32), pltpu.SemaphoreType.DMA(())],
        compiler_params=_cparams(("arbitrary",)),
        name="moe_combine",
    )(dest.reshape(-1), topg, x1, yb)


def _route(topi):
    T = topi.shape[0]
    onehot = (topi[:, :, None] == jnp.arange(N_EXPERTS, dtype=jnp.int32)).astype(jnp.int32).sum(axis=1)
    rank = jnp.cumsum(onehot, axis=0) - onehot
    counts = onehot.sum(axis=0)
    padded = (counts + MOE_ROWS - 1) // MOE_ROWS * MOE_ROWS
    pad_end = jnp.cumsum(padded)
    pad_start = pad_end - padded
    chosen = topi[:, :, None] == jnp.arange(N_EXPERTS, dtype=jnp.int32)
    dest = jnp.where(chosen, (pad_start[None, :] + rank)[:, None, :], 0).sum(axis=-1)
    n_blk = (T * TOP_K + N_EXPERTS * (MOE_ROWS - 1) + MOE_ROWS - 1) // MOE_ROWS
    blk_start = jnp.arange(n_blk, dtype=jnp.int32) * MOE_ROWS
    blk_expert = jnp.minimum((pad_end[None, :] <= blk_start[:, None]).sum(axis=1), N_EXPERTS - 1)
    pad_info = jnp.concatenate([(pad_end - MOE_ROWS) * TOK_ROWS, padded]).astype(jnp.int32)
    return ((dest * TOK_ROWS).astype(jnp.int32), blk_expert.astype(jnp.int32),
            (pad_end[-1:] // MOE_ROWS).astype(jnp.int32), pad_info, n_blk)


SAMPLE_ROWS = 8


def _token_mixers(x, pw, cos, sin, tm, key_major):
    names = ('q', 'kvc', 'kvs', 'kvw', 'ks_bf', 'kw_bf', 'gates', 'rq', 'rk', 'rv', 'rg', 'mq', 'vs_t', 'vw_t')
    return dict(zip(names, _project(x, pw, cos, sin, tm, key_major)))


def kernel(x_prompt, x_sample, mem_prompt, cache_cmp_kv, cache_slc_kv, cache_win_kv, state_ret, cache_mem_kv, page_table, norm1_g, w_in, q_norm_g, k_norm_cmp_g, k_norm_slc_g, k_norm_win_g, cmp_pe_k, cmp_w1_k, cmp_w2_k, cmp_pe_v, cmp_w1_v, cmp_w2_v, ret_norm_g, mem_norm_g, w_mem_kv, mem_q_norm_g, mem_k_norm_g, w_out, norm2_g, w_router, b_router, w_up, b_up, w_down, b_down):
    B, S, D = x_prompt.shape
    DB, QS, _ = x_sample.shape
    n_mem = mem_prompt.shape[1]
    n_pages, page = page_table.shape[1], cache_cmp_kv.shape[2]
    past = n_pages * page
    NQ = SAMPLE_ROWS
    G, HD = NSA_GROUPS, HEAD_DIM
    win_rows = min(WINDOW, S)
    TP, TS = B * S, DB * NQ

    cos_p, sin_p = _rope_tables(jnp.arange(S, dtype=jnp.int32))
    cos_s, sin_s = _rope_tables(past + jnp.arange(NQ, dtype=jnp.int32))
    cos_s, sin_s = jnp.tile(cos_s, (DB, 1)), jnp.tile(sin_s, (DB, 1))

    xp = x_prompt.reshape(TP, D)
    xs = jnp.pad(x_sample, ((0, 0), (0, NQ - QS), (0, 0))).reshape(TS, D)
    unpad = lambda a: a.reshape(DB, NQ, -1)[:, :QS]
    outs = [[] for _ in range(9)]
    for l in range(w_in.shape[0]):
        pw = _prep_proj_weights(norm1_g[l], w_in[l], q_norm_g[l], k_norm_slc_g[l], k_norm_win_g[l], mem_q_norm_g[l])
        cw = _prep_compress_weights(cmp_pe_k[l], cmp_w1_k[l], cmp_w2_k[l], cmp_pe_v[l], cmp_w1_v[l], cmp_w2_v[l],
                                    k_norm_cmp_g[l])
        mw = _prep_mix_weights(w_out[l], ret_norm_g[l], norm2_g[l], w_router[l], b_router[l])

        t = _token_mixers(xp, pw, cos_p, sin_p, 512, True)
        kc, vct = _compress(t['kvc'].reshape(B, S, 256), cw)
        o_nsa = _nsa_prompt(B, t['q'], t['gates'], kc, vct, t['ks_bf'], t['vs_t'], t['kw_bf'], t['vw_t'])
        o_ret, r_p = _retention(t['rq'].reshape(B, S, 256), t['rk'].reshape(B, S, 256), t['rv'].reshape(B, S, 256),
                                jnp.zeros((B, 2, LANES, LANES), F32), RET_CHUNK, RET_CHUNK)
        mkv = _mem_kv(mem_prompt.reshape(B * n_mem, D), mem_norm_g[l], w_mem_kv[l], mem_k_norm_g[l])
        o_mem = _mem_attn(t['mq'].reshape(B, S, 256), mkv.reshape(B, n_mem, 512), 512)
        x1_p, h_p, topi_p, topg_p = _mix(xp, o_nsa.reshape(TP, -1), o_ret.reshape(TP, 256), t['rg'],
                                         o_mem.reshape(TP, 256), mw, 512)
        outs[0].append(t['kvc'].reshape(B, S, 2, G, HD))
        outs[1].append(t['kvs'].reshape(B, S, 2, G, HD))
        outs[2].append(t['kvw'].reshape(B, S, 2, G, HD)[:, S - win_rows:])
        outs[3].append(_pairs_to_state(r_p))
        outs[4].append(mkv.reshape(B, n_mem, 2, MEM_HEADS, HD))

        t = _token_mixers(xs, pw, cos_s, sin_s, TS, False)
        win = cache_win_kv[l]
        feat_major = lambda c: jnp.moveaxis(c, 1, -1).reshape(c.shape[0], 256, c.shape[1])
        o_nsa = _nsa_decode(t['q'].astype(F32).reshape(DB, NQ, -1), t['gates'].reshape(DB, NQ, LANES),
                            t['kvs'].reshape(DB, NQ, 256), t['kvw'].reshape(DB, NQ, 256),
                            feat_major(win), feat_major(cache_cmp_kv[l]), feat_major(cache_slc_kv[l]),
                            page_table, cw, QS)
        o_ret, r_s = _retention(t['rq'].reshape(DB, NQ, 256), t['rk'].reshape(DB, NQ, 256),
                                t['rv'].reshape(DB, NQ, 256), _state_to_pairs(state_ret[l].astype(F32)), QS, NQ)
        o_mem = _mem_attn(t['mq'].reshape(DB, NQ, 256), cache_mem_kv[l].reshape(DB, n_mem, 512), NQ)
        x1_s, h_s, topi_s, topg_s = _mix(xs, o_nsa.reshape(TS, -1).astype(BF16), o_ret.reshape(TS, 256), t['rg'],
                                         o_mem.reshape(TS, 256), mw, TS)
        kv5 = lambda a: unpad(a).reshape(DB, QS, 2, G, HD)
        outs[5].append(kv5(t['kvc']))
        outs[6].append(kv5(t['kvs']))
        outs[7].append(jnp.concatenate([win, kv5(t['kvw'])], axis=1)[:, QS:])
        outs[8].append(_pairs_to_state(r_s))

        valid = lambda a: unpad(a).reshape(DB * QS, -1)
        dest, blk_expert, n_used, pad_info, n_blk = _route(
            jnp.concatenate([topi_p, valid(topi_s)], axis=0)[:, :TOP_K])
        xb = _dispatch(h_p, dest[:TP], pad_info, xb_rows=n_blk * MOE_ROWS * TOK_ROWS)
        h_s_valid = h_s.reshape(DB, NQ, TOK_ROWS, LANES)[:, :QS].reshape(DB * QS * TOK_ROWS, LANES)
        xb = _dispatch(h_s_valid, dest[TP:], pad_info, xb=xb)
        yb = _moe_experts(xb, blk_expert, n_used, w_up[l], b_up[l], w_down[l], b_down[l])
        xp = _combine(x1_p, topg_p, dest[:TP], yb)
        xs_new = _combine(valid(x1_s), valid(topg_s), dest[TP:], yb)
        xs = jnp.pad(xs_new.reshape(DB, QS, D), ((0, 0), (0, NQ - QS), (0, 0))).reshape(TS, D)
    y_sample = xs.reshape(DB, NQ, D)[:, :QS]
    return (xp.reshape(B, S, D), y_sample) + tuple(jnp.stack(o) for o in outs)
```

```python
import functools

import jax
import jax.numpy as jnp
from jax import lax
from jax.experimental import pallas as pl
from jax.experimental.pallas import tpu as pltpu

F32 = jnp.float32
BF16 = jnp.bfloat16

HEAD_DIM = 64
NSA_HEADS = 8
NSA_GROUPS = 2
NSA_HPG = NSA_HEADS // NSA_GROUPS
RET_HEADS = 4
MEM_HEADS = 4
CMP_BLK = 32
CMP_STRIDE = 16
CMP_HID = 256
SLC_BLK = 64
N_SEL = 16
WINDOW = 512
Q_BLK = 128
RET_CHUNK = 128
N_EXPERTS = 32
TOP_K = 4
D_FF = 1024
SWIGLU_LIMIT = 7.0
SWIGLU_ALPHA = 1.702
EPS = 1e-6
NEG = -1e30
BIG = 1e9
ROPE_BASE = 10000.0
SLC_RATIO = SLC_BLK // CMP_STRIDE
CMP_OVL = CMP_BLK // CMP_STRIDE - 1

Q_SCALE = HEAD_DIM ** -0.5 * 1.4426950408889634
LANES = 128
KEY_TILE = 512
MOE_ROWS = 512
VMEM_LIMIT = 56 * 1024 * 1024

C_Q = 0
C_KVC = C_Q + NSA_HEADS * LANES
C_KVS = C_KVC + 256
C_KVW = C_KVS + 256
C_GATE = C_KVW + 256
C_RQ = C_GATE + LANES
C_RK = C_RQ + 256
C_RV = C_RK + 256
C_RG = C_RV + 256
C_MQ = C_RG + 256
C_END = C_MQ + 256


def _cparams(sem):
    return pltpu.CompilerParams(dimension_semantics=sem, vmem_limit_bytes=VMEM_LIMIT)


def _lane(shape):
    return lax.broadcasted_iota(jnp.int32, shape, len(shape) - 1)


def _seg_rms(x, gain):
    lo = _lane(x.shape) < HEAD_DIM
    x2 = x * x
    s_lo = jnp.sum(jnp.where(lo, x2, 0.0), axis=-1, keepdims=True)
    s_hi = jnp.sum(jnp.where(lo, 0.0, x2), axis=-1, keepdims=True)
    ms = jnp.where(lo, s_lo, s_hi) * (1.0 / HEAD_DIM)
    return x * lax.rsqrt(ms + EPS) * gain


def _swap_halves(x):
    first = (_lane(x.shape) & (HEAD_DIM // 2)) == 0
    return jnp.where(first, pltpu.roll(x, LANES - HEAD_DIM // 2, 1), pltpu.roll(x, HEAD_DIM // 2, 1))


def _proj_kernel(x_ref, g1_ref, w_ref, cos_ref, sin_ref, qg_ref, ksg_ref, kwg_ref, mqg_ref,
                 q_out, kvc_out, kvs_out, kvw_out, ks_bf, kw_bf, gate_out,
                 rq_out, rk_out, rv_out, rg_out, mq_out, *extra, seq_tiles):
    x = x_ref[...]
    tm = x.shape[0]
    ms = jnp.mean(x * x, axis=-1, keepdims=True)
    xn = (x * lax.rsqrt(ms + EPS) * g1_ref[...]).astype(BF16)
    z = jnp.dot(xn, w_ref[...], preferred_element_type=F32)
    for i in range(NSA_HEADS):
        zq = z[:, C_Q + i * LANES:C_Q + (i + 1) * LANES]
        msq = jnp.sum(zq * zq, axis=-1, keepdims=True) * (1.0 / HEAD_DIM)
        qn = zq * lax.rsqrt(msq + EPS) * qg_ref[:, i * LANES:(i + 1) * LANES] * Q_SCALE
        if seq_tiles:
            q_out[i * LANES:(i + 1) * LANES, :] = qn.T.astype(BF16)
        else:
            q_out[:, i * LANES:(i + 1) * LANES] = qn.astype(BF16)
    kvc_out[...] = z[:, C_KVC:C_KVC + 256]
    ks = _seg_rms(z[:, C_KVS:C_KVS + LANES], ksg_ref[...])
    vs = z[:, C_KVS + LANES:C_KVS + 256]
    kvs_out[:, 0:LANES] = ks
    kvs_out[:, LANES:256] = vs
    kw = _seg_rms(z[:, C_KVW:C_KVW + LANES], kwg_ref[...])
    vw = z[:, C_KVW + LANES:C_KVW + 256]
    kvw_out[:, 0:LANES] = kw
    kvw_out[:, LANES:256] = vw
    gates = jax.nn.sigmoid(z[:, C_GATE:C_GATE + LANES])
    ks_bf[:, 0:LANES] = ks.astype(BF16)
    if seq_tiles:
        vs_t, vw_t = extra
        pos = (pl.program_id(0) % seq_tiles) * tm + lax.broadcasted_iota(jnp.int32, (tm, LANES), 0)
        ks_bf[:, LANES:256] = jnp.where(_lane((tm, LANES)) == pos // SLC_BLK, 1.0, 0.0).astype(BF16)
        kw_bf[...] = kw.astype(BF16)
        vs_t[...] = vs.T.astype(BF16)
        vw_t[...] = vw.T.astype(BF16)
        gate_out[...] = gates.T
    else:
        ks_bf[:, LANES:256] = vs.astype(BF16)
        kw_bf[:, 0:LANES] = kw.astype(BF16)
        kw_bf[:, LANES:256] = vw.astype(BF16)
        gate_out[...] = gates
    for c in range(2):
        sl = slice(c * LANES, (c + 1) * LANES)
        cos = cos_ref[:, sl]
        sin = sin_ref[:, sl]
        rq = z[:, C_RQ + c * LANES:C_RQ + (c + 1) * LANES]
        rk = z[:, C_RK + c * LANES:C_RK + (c + 1) * LANES]
        rq_out[:, sl] = rq * cos + _swap_halves(rq) * sin
        rk_out[:, sl] = (rk * cos + _swap_halves(rk) * sin) * (HEAD_DIM ** -0.5)
        mq = z[:, C_MQ + c * LANES:C_MQ + (c + 1) * LANES]
        mq_out[:, sl] = _seg_rms(mq, mqg_ref[...]).astype(BF16)
    rv_out[...] = z[:, C_RV:C_RV + 256]
    rg_out[...] = z[:, C_RG:C_RG + 256]


def _project(x, pw, cos, sin, tm, key_major):
    T, D = x.shape
    nt = cos.shape[0] // tm
    row = lambda n: (pl.BlockSpec((tm, n), lambda i: (i, 0)), (T, n))
    col = lambda n: (pl.BlockSpec((n, tm), lambda i: (0, i)), (n, T))
    full = lambda a: pl.BlockSpec(a.shape, lambda i: (0, 0))
    tab = pl.BlockSpec((tm, 256), lambda i: (i % nt, 0))
    tok = col if key_major else row
    outs = [
        (tok(NSA_HEADS * LANES), BF16),
        (row(256), F32), (row(256), F32), (row(256), F32),
        (row(256), BF16), (row(LANES if key_major else 256), BF16),
        (tok(LANES), F32),
        (row(256), F32), (row(256), F32), (row(256), F32), (row(256), F32),
        (row(256), BF16),
    ]
    if key_major:
        outs += [(col(LANES), BF16), (col(LANES), BF16)]
    return pl.pallas_call(
        functools.partial(_proj_kernel, seq_tiles=nt if key_major else 0),
        grid=(T // tm,),
        in_specs=[row(D)[0], full(pw['g1']), full(pw['w']), tab, tab,
                  full(pw['qg']), full(pw['ksg']), full(pw['kwg']), full(pw['mqg'])],
        out_specs=[spec for (spec, _), _ in outs],
        out_shape=[jax.ShapeDtypeStruct(shape, dt) for (_, shape), dt in outs],
        compiler_params=_cparams(("arbitrary",)),
        name="proj",
    )(x, pw['g1'], pw['w'], cos, sin, pw['qg'], pw['ksg'], pw['kwg'], pw['mqg'])


def _prep_proj_weights(norm1_g, w_in, q_norm_g, k_norm_slc_g, k_norm_win_g, mem_q_norm_g):
    D = w_in.shape[0]
    sizes = (512, 256, 256, 256, 24, 256, 256, 256, 256, 256)
    parts, off = [], 0
    for n in sizes:
        parts.append(w_in[:, off:off + n])
        off += n
    wq, wkvc, wkvs, wkvw, wg, wrq, wrk, wrv, wrg, wmq = parts
    zero = jnp.zeros((D, HEAD_DIM), w_in.dtype)
    qcols, qg = [], []
    gz = jnp.zeros((HEAD_DIM,), F32)
    for i in range(NSA_HEADS):
        wh = wq[:, i * HEAD_DIM:(i + 1) * HEAD_DIM]
        if i < NSA_HPG:
            qcols += [wh, zero]
            qg += [q_norm_g, gz]
        else:
            qcols += [zero, wh]
            qg += [gz, q_norm_g]
    wgp = jnp.pad(wg, ((0, 0), (0, LANES - wg.shape[1])))
    w = jnp.concatenate(qcols + [wkvc, wkvs, wkvw, wgp, wrq, wrk, wrv, wrg, wmq], axis=1).astype(BF16)
    two = lambda g: jnp.tile(g, 2)[None, :]
    return dict(g1=norm1_g[None, :], w=w, qg=jnp.concatenate(qg)[None, :],
                ksg=two(k_norm_slc_g), kwg=two(k_norm_win_g), mqg=two(mem_q_norm_g))


def _rope_tables(pos):
    half = HEAD_DIM // 2
    inv = ROPE_BASE ** (-jnp.arange(half, dtype=F32) / half)
    ang = pos.astype(F32)[:, None] * inv[None, :]
    cos, sin = jnp.cos(ang), jnp.sin(ang)
    cos = jnp.tile(jnp.concatenate([cos, cos], axis=1), (1, RET_HEADS))
    sin = jnp.tile(jnp.concatenate([-sin, sin], axis=1), (1, RET_HEADS))
    return cos, sin


def _compress_half(src_ref, pe_ref, wa_ref, wb_ref, w2_ref, nchunks):
    a_parts, b_parts = [], []
    for r in range(CMP_STRIDE):
        xr = src_ref[pl.ds(r, nchunks, stride=CMP_STRIDE), :]
        a_parts.append((xr + pe_ref[r:r + 1, :]).astype(BF16))
        b_parts.append((xr + pe_ref[CMP_STRIDE + r:CMP_STRIDE + r + 1, :]).astype(BF16))
    ha = jnp.dot(jnp.concatenate(a_parts, axis=1), wa_ref[...], preferred_element_type=F32)
    hb = jnp.dot(jnp.concatenate(b_parts, axis=1), wb_ref[...], preferred_element_type=F32)
    h = ha + pltpu.roll(hb, nchunks - 1, 0)
    act = jax.nn.gelu(h).astype(BF16)
    return jnp.dot(act, w2_ref[...], preferred_element_type=F32)


def _compress_kernel(kv_ref, pe_ref, wa_ref, wb_ref, w2_ref, kg_ref, kc_out, vct_out):
    nchunks = kc_out.shape[1]
    y = _compress_half(kv_ref.at[0], pe_ref.at[0], wa_ref.at[0], wb_ref.at[0], w2_ref.at[0], nchunks)

    @pl.when(pl.program_id(1) == 0)
    def _():
        kc_out[0] = _seg_rms(y, kg_ref[...]).astype(BF16)

    @pl.when(pl.program_id(1) == 1)
    def _():
        vct_out[0] = y.T.astype(BF16)


def _compress(kvc, cw):
    B, T, _ = kvc.shape
    nchunks = T // CMP_STRIDE
    per_kv = lambda a: pl.BlockSpec((1,) + a.shape[1:], lambda b, j: (j,) + (0,) * (a.ndim - 1))
    return pl.pallas_call(
        _compress_kernel,
        grid=(B, 2),
        in_specs=[pl.BlockSpec((1, T, LANES), lambda b, j: (b, 0, j)),
                  per_kv(cw['pe']), per_kv(cw['wa']), per_kv(cw['wb']), per_kv(cw['w2']),
                  pl.BlockSpec((1, LANES), lambda b, j: (0, 0))],
        out_specs=[pl.BlockSpec((1, nchunks, LANES), lambda b, j: (b, 0, 0)),
                   pl.BlockSpec((1, LANES, nchunks), lambda b, j: (b, 0, 0))],
        out_shape=[jax.ShapeDtypeStruct((B, nchunks, LANES), BF16),
                   jax.ShapeDtypeStruct((B, LANES, nchunks), BF16)],
        compiler_params=_cparams(("arbitrary", "arbitrary")),
        name="compress",
    )(kvc, cw['pe'], cw['wa'], cw['wb'], cw['w2'], cw['kg'])


def _prep_compress_weights(pe_k, w1_k, w2_k, pe_v, w1_v, w2_v, k_norm_g):
    eye = jnp.eye(NSA_GROUPS, dtype=F32)

    def one(pe, w1, w2):
        w1r = w1.reshape(CMP_BLK, HEAD_DIM, CMP_HID)
        wf = jnp.einsum('gh,rdc->rgdhc', eye, w1r).reshape(CMP_BLK * LANES, NSA_GROUPS * CMP_HID)
        w2f = jnp.einsum('gh,cd->gchd', eye, w2).reshape(NSA_GROUPS * CMP_HID, LANES)
        half = CMP_STRIDE * LANES
        return jnp.tile(pe, (1, NSA_GROUPS)), wf[:half].astype(BF16), wf[half:].astype(BF16), w2f.astype(BF16)

    k, v = one(pe_k, w1_k, w2_k), one(pe_v, w1_v, w2_v)
    st = lambda i: jnp.stack([k[i], v[i]])
    return dict(pe=st(0), wa=st(1), wb=st(2), w2=st(3), kg=jnp.tile(k_norm_g, 2)[None, :])


def _dot_nt(a, b):
    return lax.dot_general(a, b, (((1,), (1,)), ((), ())), preferred_element_type=F32)


def _split3(x):
    hi = x.astype(BF16)
    r = x - hi.astype(F32)
    mid = r.astype(BF16)
    lo = (r - mid.astype(F32)).astype(BF16)
    return hi, mid, lo


def _select_bias(ps, cur, rounds, axis, cur_in_range=True):
    j = lax.broadcasted_iota(jnp.int32, ps.shape, axis)
    last = cur if cur_in_range else cur - 1
    key = jnp.where(j <= last, ps, NEG)
    for forced in (0, last, cur - 1):
        key = jnp.where(j == forced, BIG, key)
    bias = jnp.full(ps.shape, NEG, F32)
    for _ in range(rounds):
        m = jnp.max(key, axis=axis, keepdims=True)
        idx = jnp.min(jnp.where(key == m, j, LANES), axis=axis, keepdims=True)
        pick = j == idx
        bias = jnp.where(pick, 0.0, bias)
        key = jnp.where(pick, -jnp.inf, key)
    return bias


def _cmp_branch(qg, qpos, kc, vc):
    s = _dot_nt(qg, kc)
    cend = _lane((1, kc.shape[0])) * CMP_STRIDE + (CMP_BLK - 1)
    s = jnp.where(cend <= qpos, s, NEG)
    m = jnp.max(s, axis=-1, keepdims=True)
    p = jnp.exp2(s - m)
    p = p * (1.0 / jnp.sum(p, axis=-1, keepdims=True))
    p = jnp.where(qpos >= CMP_BLK - 1, p, 0.0)
    return p, jnp.dot(p.astype(BF16), vc, preferred_element_type=F32)


def _gate_heads(gates, g, oc, osel, ow, nq):
    heads = []
    for h in range(NSA_HPG):
        rs = slice(h * nq, (h + 1) * nq)
        c0 = (NSA_HPG * g + h) * 3
        heads.append(gates[:, c0:c0 + 1] * oc[rs] + gates[:, c0 + 1:c0 + 2] * osel[rs]
                     + gates[:, c0 + 2:c0 + 3] * ow[rs])
    return heads


def _store_heads(heads, o_ref):
    low = _lane(heads[0].shape) < HEAD_DIM
    for pr in range(NSA_HEADS // 2):
        even, odd = heads[2 * pr], heads[2 * pr + 1]
        if pr < NSA_GROUPS:
            odd = pltpu.roll(odd, HEAD_DIM, 1)
        else:
            even = pltpu.roll(even, HEAD_DIM, 1)
        o_ref[0, :, pr * LANES:(pr + 1) * LANES] = jnp.where(low, even, odd).astype(o_ref.dtype)


def _nsa_decode_kernel(pt_ref, q_ref, gate_ref, ksn_ref, kwn_ref, win_ref, pool_c, pool_s,
                       pe_ref, wa_ref, wb_ref, w2_ref, kg_ref, mmap_ref, hot_ref, o_ref,
                       cbuf, sbuf, xrow, sems, *, page, n_new):
    b = pl.program_id(0)
    past = sbuf.shape[1]
    npages = past // page
    nq = q_ref.shape[1]
    dot = functools.partial(jnp.dot, preferred_element_type=F32)

    slot = b % 2

    def cmp_copy(seq, p, sl):
        dst = pl.ds(pl.multiple_of(p * page, page), page)
        return pltpu.make_async_copy(pool_c.at[pt_ref[seq * npages + p]], cbuf.at[sl, :, dst], sems.at[sl])

    def slc_copy(seq, p):
        dst = pl.ds(pl.multiple_of(p * page, page), page)
        return pltpu.make_async_copy(pool_s.at[pt_ref[seq * npages + p]], sbuf.at[:, dst], sems.at[2])

    def for_pages(fn):
        def body(p, c):
            fn(p)
            return c
        lax.fori_loop(0, npages, body, 0)

    @pl.when(b == 0)
    def _():
        for_pages(lambda p: cmp_copy(0, p, 0).start())

    for_pages(lambda p: slc_copy(b, p).start())

    @pl.when(b + 1 < pl.num_programs(0))
    def _():
        for_pages(lambda p: cmp_copy(b + 1, p, 1 - slot).start())

    for_pages(lambda p: cmp_copy(b, p, slot).wait())

    tchunk = 4 * LANES
    for half in range(2):
        for c in range(past // tchunk):
            xrow[half, c * tchunk:(c + 1) * tchunk, :] = cbuf[slot, half * LANES:(half + 1) * LANES,
                                                              c * tchunk:(c + 1) * tchunk].T
    nchunks = past // CMP_STRIDE
    kc = _seg_rms(_compress_half(xrow.at[0], pe_ref.at[0], wa_ref.at[0], wb_ref.at[0], w2_ref.at[0], nchunks),
                  kg_ref[...]).astype(BF16)
    vc = _compress_half(xrow.at[1], pe_ref.at[1], wa_ref.at[1], wb_ref.at[1], w2_ref.at[1], nchunks).astype(BF16)

    rows = NSA_HPG * nq
    srow = lax.broadcasted_iota(jnp.int32, (rows, 1), 0) % nq
    qpos = past + srow
    gates = gate_ref[0]
    pad_keys = lambda x: jnp.concatenate([x, jnp.zeros((LANES - nq, LANES), F32)], axis=0).astype(BF16)
    new_ok = _lane((1, LANES)) <= jnp.minimum(srow, n_new - 1)
    ks_new, vs_new = pad_keys(ksn_ref[0, :, 0:LANES]), pad_keys(ksn_ref[0, :, LANES:256])
    kw_new, vw_new = pad_keys(kwn_ref[0, :, 0:LANES]), pad_keys(kwn_ref[0, :, LANES:256])

    for_pages(lambda p: slc_copy(b, p).wait())
    k_aug_t = jnp.concatenate([sbuf[0:LANES, :].astype(BF16), hot_ref[...]], axis=0)
    vs_t = sbuf[LANES:256, :].astype(BF16)
    nwin = win_ref.shape[2]
    kw_t, vw_t = win_ref[0, 0:LANES, :].astype(BF16), win_ref[0, LANES:256, :].astype(BF16)

    qgs, ocs, p4s = [], [], []
    for g in range(NSA_GROUPS):
        qg = jnp.concatenate(
            [q_ref[0, :, (NSA_HPG * g + h) * LANES:(NSA_HPG * g + h + 1) * LANES] for h in range(NSA_HPG)],
            axis=0).astype(BF16)
        p, oc = _cmp_branch(qg, qpos, kc, vc)
        qgs.append(qg)
        ocs.append(oc)
        p4s.append(p[0:nq] + p[nq:2 * nq] + p[2 * nq:3 * nq] + p[3 * nq:4 * nq])
    p4 = jnp.concatenate(p4s, axis=0)
    ps_t = sum(_dot_nt(mmap_ref[...], part) for part in _split3(p4))
    cur = (past + _lane((1, NSA_GROUPS * nq)) % nq) // SLC_BLK
    bias_t = _select_bias(ps_t, cur, N_SEL - 1, 0, cur_in_range=False)
    bias_all = jnp.concatenate([bias_t, jnp.zeros((LANES, LANES - NSA_GROUPS * nq), F32)], axis=1).T

    heads = []
    for g in range(NSA_GROUPS):
        qg, oc = qgs[g], ocs[g]
        bias = bias_all[g * nq:(g + 1) * nq]
        qaug = jnp.concatenate([qg, jnp.concatenate([bias] * NSA_HPG, axis=0).astype(BF16)], axis=1)
        s_old = dot(qaug, k_aug_t)
        s_new = jnp.where(new_ok, _dot_nt(qg, ks_new), NEG)
        m = jnp.maximum(jnp.max(s_old, axis=-1, keepdims=True), jnp.max(s_new, axis=-1, keepdims=True))
        p_old, p_new = jnp.exp2(s_old - m), jnp.exp2(s_new - m)
        inv = 1.0 / (jnp.sum(p_old, axis=-1, keepdims=True) + jnp.sum(p_new, axis=-1, keepdims=True))
        osel = (_dot_nt(p_old.astype(BF16), vs_t) + dot(p_new.astype(BF16), vs_new)) * inv
        sw_old = jnp.where(_lane((1, nwin)) + (WINDOW - nwin) > srow, dot(qg, kw_t), NEG)
        sw_new = jnp.where(new_ok, _dot_nt(qg, kw_new), NEG)
        mw = jnp.maximum(jnp.max(sw_old, axis=-1, keepdims=True), jnp.max(sw_new, axis=-1, keepdims=True))
        pw_old, pw_new = jnp.exp2(sw_old - mw), jnp.exp2(sw_new - mw)
        invw = 1.0 / (jnp.sum(pw_old, axis=-1, keepdims=True) + jnp.sum(pw_new, axis=-1, keepdims=True))
        ow = (_dot_nt(pw_old.astype(BF16), vw_t) + dot(pw_new.astype(BF16), vw_new)) * invw
        heads += _gate_heads(gates, g, oc, osel, ow, nq)
    _store_heads(heads, o_ref)


def _nsa_decode(q, gates, ksn, kwn, win_t, pool_c, pool_s, page_table, cw, n_new):
    DB, nq, _ = q.shape
    n_pages = page_table.shape[1]
    page = pool_c.shape[2]
    past = n_pages * page
    nchunks = past // CMP_STRIDE
    mmap = _importance_map(nchunks)
    hot = (jnp.arange(LANES, dtype=jnp.int32)[:, None] == jnp.arange(past, dtype=jnp.int32)[None, :] // SLC_BLK)
    hot = hot.astype(BF16)
    per_seq = lambda a: pl.BlockSpec((1,) + a.shape[1:], lambda b, pt: (b,) + (0,) * (a.ndim - 1))
    const = lambda a: pl.BlockSpec(a.shape, lambda b, pt: (0,) * a.ndim, pipeline_mode=pl.Buffered(1))
    hbm = pl.BlockSpec(memory_space=pl.ANY)
    grid_spec = pltpu.PrefetchScalarGridSpec(
        num_scalar_prefetch=1,
        grid=(DB,),
        in_specs=[per_seq(q), per_seq(gates), per_seq(ksn), per_seq(kwn), per_seq(win_t), hbm, hbm,
                  const(cw['pe']), const(cw['wa']), const(cw['wb']), const(cw['w2']), const(cw['kg']), const(mmap),
                  const(hot)],
        out_specs=pl.BlockSpec((1, nq, NSA_HEADS * HEAD_DIM), lambda b, pt: (b, 0, 0)),
        scratch_shapes=[pltpu.VMEM((2, 256, past), F32), pltpu.VMEM((256, past), F32),
                        pltpu.VMEM((2, past, LANES), F32), pltpu.SemaphoreType.DMA((3,))],
    )
    return pl.pallas_call(
        functools.partial(_nsa_decode_kernel, page=page, n_new=n_new),
        grid_spec=grid_spec,
        out_shape=jax.ShapeDtypeStruct((DB, nq, NSA_HEADS * HEAD_DIM), F32),
        compiler_params=_cparams(("arbitrary",)),
        name="nsa_sample",
    )(page_table.reshape(-1), q, gates, ksn, kwn, win_t, pool_c, pool_s,
      cw['pe'], cw['wa'], cw['wb'], cw['w2'], cw['kg'], mmap, hot)


def _nsa_prompt_kernel(q_ref, gate_ref, kc_ref, vct_ref, ks_ref, vst_ref, kw_ref, vwt_ref, mmap_ref, o_ref,
                       m_sc, l_sc, acc_sc, sa_sc, sb_sc):
    i = pl.program_id(1)
    s0 = i * Q_BLK
    cols = NSA_HPG * Q_BLK
    qpos = s0 + _lane((1, cols)) % Q_BLK
    cur = (s0 + _lane((1, Q_BLK))) // SLC_BLK
    kc, vct = kc_ref[0], vct_ref[0]
    ncmp = kc.shape[0]
    n_full = s0 // KEY_TILE
    w0 = pl.multiple_of(jnp.maximum(s0 - WINDOW, 0), Q_BLK)
    nw = WINDOW + Q_BLK
    dot = functools.partial(jnp.dot, preferred_element_type=F32)

    def softmax_keys(s):
        p = jnp.exp2(s - jnp.max(s, axis=0, keepdims=True))
        return p, 1.0 / jnp.sum(p, axis=0, keepdims=True)

    qaugs = []
    branch = []
    n_idx = lax.broadcasted_iota(jnp.int32, (ncmp, cols), 0)
    n_last = lax.shift_right_arithmetic(qpos - (CMP_BLK - 1), CMP_STRIDE.bit_length() - 1)
    r_idx = lax.broadcasted_iota(jnp.int32, (nw, cols), 0)
    r_last = qpos - w0
    for g in range(NSA_GROUPS):
        vg = slice(g * HEAD_DIM, (g + 1) * HEAD_DIM)
        qt = jnp.concatenate([q_ref[(NSA_HPG * g + h) * LANES:(NSA_HPG * g + h + 1) * LANES, :]
                              for h in range(NSA_HPG)], axis=1)
        s = jnp.where(n_idx <= n_last, dot(kc, qt), NEG)
        p, inv = softmax_keys(s)
        p = p * jnp.where(n_last >= 0, inv, 0.0)
        oc = dot(vct[vg], p.astype(BF16))
        p4 = p[:, 0:Q_BLK] + p[:, Q_BLK:2 * Q_BLK] + p[:, 2 * Q_BLK:3 * Q_BLK] + p[:, 3 * Q_BLK:4 * Q_BLK]
        ps_t = sum(dot(mmap_ref[...], part) for part in _split3(p4))
        bias = _select_bias(ps_t, cur, N_SEL, 0).astype(BF16)
        qaugs.append(jnp.concatenate([qt, jnp.concatenate([bias] * NSA_HPG, axis=1)], axis=0))
        sw = dot(kw_ref[0, pl.ds(w0, nw), :], qt)
        sw = jnp.where(r_idx <= r_last, jnp.where(r_idx > r_last - WINDOW, sw, NEG), NEG)
        pw, invw = softmax_keys(sw)
        ow = dot(vwt_ref[pl.ds(g * HEAD_DIM, HEAD_DIM), pl.ds(w0, nw)], pw.astype(BF16)) * invw
        branch.append((oc, ow))
        m_sc[g] = jnp.full(m_sc.shape[1:], NEG, F32)
        l_sc[g] = jnp.zeros(l_sc.shape[1:], F32)
        acc_sc[g] = jnp.zeros(acc_sc.shape[1:], F32)

    def scores(kt, s_ref):
        k_aug = ks_ref[0, pl.ds(pl.multiple_of(kt * KEY_TILE, KEY_TILE), KEY_TILE), :]
        for g in range(NSA_GROUPS):
            s_ref[g] = dot(k_aug, qaugs[g])

    def consume(kt, s_ref, causal):
        k0 = pl.multiple_of(kt * KEY_TILE, KEY_TILE)
        v_t = vst_ref[:, pl.ds(k0, KEY_TILE)]
        for g in range(NSA_GROUPS):
            s = s_ref[g]
            if causal:
                s = jnp.where(k0 + lax.broadcasted_iota(jnp.int32, s.shape, 0) <= qpos, s, NEG)
            m_old = m_sc[g]
            m_new = jnp.maximum(m_old, jnp.max(s, axis=0, keepdims=True))
            alpha = jnp.exp2(m_old - m_new)
            p = jnp.exp2(s - m_new)
            l_sc[g] = alpha * l_sc[g] + jnp.sum(p, axis=0, keepdims=True)
            acc_sc[g] = alpha * acc_sc[g] + dot(v_t[g * HEAD_DIM:(g + 1) * HEAD_DIM], p.astype(BF16))
            m_sc[g] = m_new

    def two_tiles(u, c):
        scores(2 * u + 1, sb_sc)
        consume(2 * u, sa_sc, False)
        scores(2 * u + 2, sa_sc)
        consume(2 * u + 1, sb_sc, False)
        return c

    scores(0, sa_sc)
    lax.fori_loop(0, n_full // 2, two_tiles, 0)

    @pl.when(n_full % 2 == 1)
    def _():
        scores(n_full, sb_sc)
        consume(n_full - 1, sa_sc, False)
        consume(n_full, sb_sc, True)

    @pl.when(n_full % 2 == 0)
    def _():
        consume(n_full, sa_sc, True)

    pieces = []
    for g in range(NSA_GROUPS):
        oc, ow = branch[g]
        osel = acc_sc[g] * (1.0 / l_sc[g])
        for h in range(NSA_HPG):
            cs = slice(h * Q_BLK, (h + 1) * Q_BLK)
            c0 = (NSA_HPG * g + h) * 3
            pieces.append(gate_ref[c0:c0 + 1, :] * oc[:, cs] + gate_ref[c0 + 1:c0 + 2, :] * osel[:, cs]
                          + gate_ref[c0 + 2:c0 + 3, :] * ow[:, cs])
    o_ref[0] = jnp.concatenate(pieces, axis=0).T.astype(o_ref.dtype)


def _importance_map(ncmp):
    j = jnp.arange(LANES)[:, None]
    n = jnp.arange(ncmp)[None, :]
    return ((n >= SLC_RATIO * j - CMP_OVL) & (n < SLC_RATIO * j + SLC_RATIO)).astype(BF16)


def _nsa_prompt(B, qt, gates_t, kc, vct, ks_aug, vs_t, kw, vw_t):
    S = qt.shape[1] // B
    nq = S // Q_BLK
    ncmp = kc.shape[1]
    cols = NSA_HPG * Q_BLK
    mmap = _importance_map(ncmp)
    per_block = lambda n: pl.BlockSpec((n, Q_BLK), lambda b, i: (0, b * nq + i))
    seq_rows = lambda n: pl.BlockSpec((1, S, n), lambda b, i: (b, 0, 0))
    seq_cols = pl.BlockSpec((LANES, S), lambda b, i: (0, b))
    return pl.pallas_call(
        _nsa_prompt_kernel,
        grid=(B, nq),
        in_specs=[per_block(qt.shape[0]), per_block(LANES),
                  pl.BlockSpec((1, ncmp, LANES), lambda b, i: (b, 0, 0)),
                  pl.BlockSpec((1, LANES, ncmp), lambda b, i: (b, 0, 0)),
                  seq_rows(256), seq_cols, seq_rows(LANES), seq_cols,
                  pl.BlockSpec(mmap.shape, lambda b, i: (0, 0))],
        out_specs=pl.BlockSpec((1, Q_BLK, NSA_HEADS * HEAD_DIM), lambda b, i: (b, i, 0)),
        out_shape=jax.ShapeDtypeStruct((B, S, NSA_HEADS * HEAD_DIM), BF16),
        scratch_shapes=[pltpu.VMEM((NSA_GROUPS, 1, cols), F32), pltpu.VMEM((NSA_GROUPS, 1, cols), F32),
                        pltpu.VMEM((NSA_GROUPS, HEAD_DIM, cols), F32),
                        pltpu.VMEM((NSA_GROUPS, KEY_TILE, cols), F32), pltpu.VMEM((NSA_GROUPS, KEY_TILE, cols), F32)],
        compiler_params=_cparams(("arbitrary", "arbitrary")),
        name="nsa_prompt",
    )(qt, gates_t, kc, vct, ks_aug.reshape(B, S, 256), vs_t, kw.reshape(B, S, LANES), vw_t, mmap)


def _ret_kernel(rq_ref, rk_ref, rv_ref, r0_ref, dmask_ref, xi_ref, zeta_ref, dec_ref, o_ref, rout_ref, r_sc):
    @pl.when(pl.program_id(1) == 0)
    def _():
        r_sc[...] = r0_ref[0]

    C = rq_ref.shape[1]
    low = _lane((C, LANES)) < HEAD_DIM
    diag = lax.broadcasted_iota(jnp.int32, (LANES, LANES), 0) // HEAD_DIM == _lane((LANES, LANES)) // HEAD_DIM
    for pr in range(RET_HEADS // 2):
        sl = slice(pr * LANES, (pr + 1) * LANES)
        k = rk_ref[0, :, sl]
        qb, kb, vb = rq_ref[0, :, sl].astype(BF16), k.astype(BF16), rv_ref[0, :, sl].astype(BF16)
        zero = jnp.zeros_like(qb)
        s0 = _dot_nt(jnp.where(low, qb, zero), kb) * dmask_ref[2 * pr]
        s1 = _dot_nt(jnp.where(low, zero, qb), kb) * dmask_ref[2 * pr + 1]
        o = jnp.where(low, jnp.dot(s0.astype(BF16), vb, preferred_element_type=F32),
                      jnp.dot(s1.astype(BF16), vb, preferred_element_type=F32))
        r = r_sc[pr]
        o_ref[0, :, sl] = o + jnp.dot(qb, r.astype(BF16), preferred_element_type=F32) * xi_ref[:, sl]
        kz = (k * zeta_ref[:, sl]).astype(BF16)
        upd = lax.dot_general(kz, vb, (((0,), (0,)), ((), ())), preferred_element_type=F32)
        r_sc[pr] = dec_ref[:, sl] * r + jnp.where(diag, upd, 0.0)
    rout_ref[0] = r_sc[...]


def _ret_tables(c_true, c_pad):
    lg = jnp.log(1.0 - 2.0 ** (-5.0 - jnp.arange(RET_HEADS, dtype=F32)))
    idx = jnp.arange(c_pad, dtype=F32)
    diff = idx[:, None] - idx[None, :]
    dmask = jnp.where(diff >= 0, jnp.exp(jnp.maximum(diff, 0.0)[None] * lg[:, None, None]), 0.0)
    lanes = lambda a: jnp.repeat(a, HEAD_DIM, axis=-1)
    xi = lanes(jnp.exp((idx + 1.0)[:, None] * lg[None, :]))
    zeta = lanes(jnp.exp((c_true - 1.0 - idx)[:, None] * lg[None, :]))
    dec = lanes(jnp.exp(c_true * lg)[None, :])
    return dmask, xi, zeta, dec


def _retention(rq, rk, rv, r0, c_true, c_pad):
    B, T, _ = rq.shape
    dmask, xi, zeta, dec = _ret_tables(c_true, c_pad)
    row = pl.BlockSpec((1, c_pad, 256), lambda b, c: (b, c, 0))
    st = pl.BlockSpec((1, 2, LANES, LANES), lambda b, c: (b, 0, 0, 0))
    const = lambda a: pl.BlockSpec(a.shape, lambda b, c: (0,) * a.ndim)
    return pl.pallas_call(
        _ret_kernel,
        grid=(B, T // c_pad),
        in_specs=[row, row, row, st, const(dmask), const(xi), const(zeta), const(dec)],
        out_specs=[row, st],
        out_shape=[jax.ShapeDtypeStruct((B, T, 256), F32), jax.ShapeDtypeStruct((B, 2, LANES, LANES), F32)],
        scratch_shapes=[pltpu.VMEM((2, LANES, LANES), F32)],
        compiler_params=_cparams(("arbitrary", "arbitrary")),
        name="retention",
    )(rq, rk, rv, r0, dmask, xi, zeta, dec)


def _state_to_pairs(r):
    B = r.shape[0]
    r = r.reshape(B, 2, 2, HEAD_DIM, HEAD_DIM)
    eye = jnp.eye(2, dtype=r.dtype)
    return jnp.einsum('bphde,hk->bphdke', r, eye).reshape(B, 2, LANES, LANES)


def _pairs_to_state(rp):
    B = rp.shape[0]
    rp = rp.reshape(B, 2, 2, HEAD_DIM, 2, HEAD_DIM)
    return jnp.stack([rp[:, :, 0, :, 0, :], rp[:, :, 1, :, 1, :]], axis=2).reshape(B, RET_HEADS, HEAD_DIM, HEAD_DIM)


def _mem_kv_kernel(m_ref, g_ref, w_ref, kg_ref, kv_out):
    x = m_ref[...]
    ms = jnp.mean(x * x, axis=-1, keepdims=True)
    xn = (x * lax.rsqrt(ms + EPS) * g_ref[...]).astype(BF16)
    z = jnp.dot(xn, w_ref[...], preferred_element_type=F32)
    for c in range(2):
        kv_out[:, c * LANES:(c + 1) * LANES] = _seg_rms(z[:, c * LANES:(c + 1) * LANES], kg_ref[...])
    kv_out[:, 256:512] = z[:, 256:512]


def _mem_kv(mem, mem_norm_g, w_mem_kv, mem_k_norm_g):
    T, D = mem.shape
    w = w_mem_kv.astype(BF16)
    full = lambda a: pl.BlockSpec(a.shape, lambda i: (0, 0))
    g = mem_norm_g[None, :]
    kg = jnp.tile(mem_k_norm_g, 2)[None, :]
    return pl.pallas_call(
        _mem_kv_kernel,
        grid=(1,),
        in_specs=[full(mem), full(g), full(w), full(kg)],
        out_specs=pl.BlockSpec((T, 512), lambda i: (0, 0)),
        out_shape=jax.ShapeDtypeStruct((T, 512), F32),
        compiler_params=_cparams(("arbitrary",)),
        name="mem_kv",
    )(mem, g, w, kg)


def _mem_attn_kernel(mq_ref, mkv_ref, o_ref):
    q = mq_ref[0]
    rows = q.shape[0]
    low = _lane((rows, LANES)) < HEAD_DIM
    zero = jnp.zeros((rows, LANES), F32)
    for pr in range(MEM_HEADS // 2):
        sl = slice(pr * LANES, (pr + 1) * LANES)
        qp = q[:, sl].astype(F32)
        k = mkv_ref[0, :, sl].astype(BF16)
        v = mkv_ref[0, :, 256 + pr * LANES:256 + (pr + 1) * LANES].astype(BF16)
        outs = []
        for hh in range(2):
            qm = jnp.where(low, qp, zero) if hh == 0 else jnp.where(low, zero, qp)
            s = _dot_nt(qm.astype(BF16), k) * (HEAD_DIM ** -0.5)
            m = jnp.max(s, axis=-1, keepdims=True)
            p = jnp.exp(s - m)
            o = jnp.dot(p.astype(BF16), v, preferred_element_type=F32)
            outs.append(o * (1.0 / jnp.sum(p, axis=-1, keepdims=True)))
        o_ref[0, :, sl] = jnp.where(low, outs[0], outs[1]).astype(o_ref.dtype)


def _mem_attn(mq, mkv, tm):
    B, R, _ = mq.shape
    return pl.pallas_call(
        _mem_attn_kernel,
        grid=(B, R // tm),
        in_specs=[pl.BlockSpec((1, tm, 256), lambda b, i: (b, i, 0)),
                  pl.BlockSpec((1,) + mkv.shape[1:], lambda b, i: (b, 0, 0))],
        out_specs=pl.BlockSpec((1, tm, 256), lambda b, i: (b, i, 0)),
        out_shape=jax.ShapeDtypeStruct((B, R, 256), BF16),
        compiler_params=_cparams(("arbitrary", "arbitrary")),
        name="mem_attn",
    )(mq, mkv)


TOK_ROWS = 8


def _load_token_tiles(ref, n):
    return jnp.concatenate([ref[pl.ds(c, n, stride=TOK_ROWS), :] for c in range(TOK_ROWS)], axis=1)


def _store_token_tiles(ref, x):
    n = x.shape[0]
    for c in range(TOK_ROWS):
        ref[pl.ds(c, n, stride=TOK_ROWS), :] = x[:, c * LANES:(c + 1) * LANES]


def _mix_kernel(x_ref, onsa_ref, oret_ref, rg_ref, omem_ref, wout_ref, rgain_ref, g2_ref,
                wr_hi_ref, wr_lo_ref, br_ref, x1_out, h_out, topi_out, topg_out):
    parts = [onsa_ref[...]]
    for c in range(2):
        sl = slice(c * LANES, (c + 1) * LANES)
        parts.append((_seg_rms(oret_ref[:, sl], rgain_ref[...]) * jax.nn.silu(rg_ref[:, sl])).astype(BF16))
    parts.append(omem_ref[...])
    mix = jnp.concatenate(parts, axis=1)
    x1 = x_ref[...] + jnp.dot(mix, wout_ref[...], preferred_element_type=F32)
    x1_out[...] = x1
    ms = jnp.mean(x1 * x1, axis=-1, keepdims=True)
    h = x1 * lax.rsqrt(ms + EPS) * g2_ref[...]
    _store_token_tiles(h_out, h)
    h_hi, h_lo, _ = _split3(h)
    logits = (jnp.dot(h_hi, wr_hi_ref[...], preferred_element_type=F32)
              + jnp.dot(h_hi, wr_lo_ref[...], preferred_element_type=F32)
              + jnp.dot(h_lo, wr_hi_ref[...], preferred_element_type=F32)) + br_ref[...]
    lane = _lane(logits.shape)
    key = logits
    topi = jnp.zeros(logits.shape, jnp.int32)
    topv = jnp.zeros(logits.shape, F32)
    for r in range(TOP_K):
        m = jnp.max(key, axis=-1, keepdims=True)
        idx = jnp.min(jnp.where(key == m, lane, LANES), axis=-1, keepdims=True)
        if r == 0:
            m0 = m
        topi = jnp.where(lane == r, idx, topi)
        topv = jnp.where(lane == r, jnp.exp(m - m0), topv)
        key = jnp.where(lane == idx, -jnp.inf, key)
    topi_out[...] = topi
    topg_out[...] = topv * (1.0 / jnp.sum(topv, axis=-1, keepdims=True))


def _mix(x, onsa, oret, rg, omem, mw, tm):
    T, D = x.shape
    row = lambda n: pl.BlockSpec((tm, n), lambda i: (i, 0))
    full = lambda a: pl.BlockSpec(a.shape, lambda i: (0, 0))
    names = ('wout', 'rgain', 'g2', 'wr_hi', 'wr_lo', 'br')
    return pl.pallas_call(
        _mix_kernel,
        grid=(T // tm,),
        in_specs=[row(D), row(512), row(256), row(256), row(256)] + [full(mw[n]) for n in names],
        out_specs=[row(D), pl.BlockSpec((tm * TOK_ROWS, LANES), lambda i: (i, 0)), row(LANES), row(LANES)],
        out_shape=[jax.ShapeDtypeStruct((T, D), F32), jax.ShapeDtypeStruct((T * TOK_ROWS, LANES), F32),
                   jax.ShapeDtypeStruct((T, LANES), jnp.int32), jax.ShapeDtypeStruct((T, LANES), F32)],
        compiler_params=_cparams(("arbitrary",)),
        name="mix",
    )(x, onsa, oret, rg, omem, *[mw[n] for n in names])


def _prep_mix_weights(w_out, ret_norm_g, norm2_g, w_router, b_router):
    wr = jnp.pad(w_router, ((0, 0), (0, LANES - N_EXPERTS)))
    wr_hi = wr.astype(BF16)
    wr_lo = (wr - wr_hi.astype(F32)).astype(BF16)
    br = jnp.concatenate([b_router.astype(F32), jnp.full((LANES - N_EXPERTS,), NEG, F32)])[None, :]
    return dict(wout=w_out.astype(BF16), rgain=jnp.tile(ret_norm_g, 2)[None, :], g2=norm2_g[None, :],
                wr_hi=wr_hi, wr_lo=wr_lo, br=br)


MOE_TOK_TILE = 128


def _row_copy(src, i, dst, j, sem):
    return pltpu.make_async_copy(src.at[pl.ds(pl.multiple_of(i, TOK_ROWS), TOK_ROWS)],
                                 dst.at[pl.ds(pl.multiple_of(j, TOK_ROWS), TOK_ROWS)], sem)


def _dispatch_kernel(pad_ref, dest_ref, h_ref, dest2_ref, h2_ref, xb_out, sem, zeros, zsem):
    blk_rows = zeros.shape[0]
    n_blk = xb_out.shape[0] // blk_rows

    @pl.when(pl.program_id(0) == 0)
    def _():
        zeros[...] = jnp.zeros(zeros.shape, zeros.dtype)

        def zero_block(row):
            return pltpu.make_async_copy(zeros, xb_out.at[pl.ds(pl.multiple_of(row, blk_rows), blk_rows)], zsem)

        def each_partial_block(fn):
            for e in range(N_EXPERTS):
                @pl.when(pad_ref[N_EXPERTS + e] > 0)
                def _(e=e):
                    fn(pad_ref[e])

            def body(j, c):
                fn(j * blk_rows)
                return c
            lax.fori_loop(pad_ref[2 * N_EXPERTS], n_blk, body, 0)

        each_partial_block(lambda row: zero_block(row).start())
        each_partial_block(lambda row: zero_block(row).wait())

    def scatter(d_ref, src_ref):
        tile = d_ref.shape[0] // TOP_K

        def start(t, c):
            for k in range(TOP_K):
                _row_copy(src_ref, t * TOK_ROWS, xb_out, d_ref[t * TOP_K + k], sem).start()
            return c

        def wait(t, c):
            for k in range(TOP_K):
                _row_copy(src_ref, 0, xb_out, 0, sem).wait()
            return c

        lax.fori_loop(0, tile, start, 0, unroll=8)
        lax.fori_loop(0, tile, wait, 0, unroll=8)

    scatter(dest_ref, h_ref)

    @pl.when(pl.program_id(0) == pl.num_programs(0) - 1)
    def _():
        scatter(dest2_ref, h2_ref)


def _dispatch(h, dest, h2, dest2, pad_info, xb_rows):
    T = h.shape[0] // TOK_ROWS
    n = MOE_TOK_TILE * TOP_K
    blk_rows = MOE_ROWS * TOK_ROWS
    grid_spec = pltpu.PrefetchScalarGridSpec(
        num_scalar_prefetch=1,
        grid=(T // MOE_TOK_TILE,),
        in_specs=[pl.BlockSpec((n,), lambda i, pad: (i,), memory_space=pltpu.SMEM),
                  pl.BlockSpec((MOE_TOK_TILE * TOK_ROWS, LANES), lambda i, pad: (i, 0)),
                  pl.BlockSpec(memory_space=pltpu.SMEM),
                  pl.BlockSpec(h2.shape, lambda i, pad: (0, 0))],
        out_specs=pl.BlockSpec(memory_space=pl.ANY),
        scratch_shapes=[pltpu.SemaphoreType.DMA(()), pltpu.VMEM((blk_rows, LANES), F32),
                        pltpu.SemaphoreType.DMA(())],
    )
    return pl.pallas_call(
        _dispatch_kernel,
        grid_spec=grid_spec,
        out_shape=jax.ShapeDtypeStruct((xb_rows, LANES), F32),
        compiler_params=_cparams(("arbitrary",)),
        name="moe_dispatch",
    )(pad_info, dest.reshape(-1), h, dest2.reshape(-1), h2)


def _moe_kernel(be_ref, nb_ref, x_ref, wup_ref, bup_ref, wdn_ref, bdn_ref, y_ref, wup_bf, wdn_bf):
    j = pl.program_id(0)

    @pl.when(j < nb_ref[0])
    def _():
        @pl.when(jnp.logical_or(j == 0, be_ref[j] != be_ref[jnp.maximum(j - 1, 0)]))
        def _():
            wup_bf[...] = wup_ref[0].astype(BF16)
            wdn_bf[...] = wdn_ref[0].astype(BF16)

        x = _load_token_tiles(x_ref, MOE_ROWS).astype(BF16)
        up = jnp.dot(x, wup_bf[...], preferred_element_type=F32) + bup_ref[0]
        x_glu = jnp.minimum(up[:, :D_FF], SWIGLU_LIMIT)
        x_lin = jnp.clip(up[:, D_FF:], -SWIGLU_LIMIT, SWIGLU_LIMIT)
        act = x_glu * jax.nn.sigmoid(SWIGLU_ALPHA * x_glu) * (x_lin + 1.0)
        _store_token_tiles(y_ref, jnp.dot(act.astype(BF16), wdn_bf[...], preferred_element_type=F32) + bdn_ref[0])

    @pl.when(j >= nb_ref[0])
    def _():
        y_ref[...] = jnp.zeros(y_ref.shape, y_ref.dtype)


def _moe_experts(xb, blk_expert, n_used, w_up, b_up, w_down, b_down):
    D = w_up.shape[1]
    blk = pl.BlockSpec((MOE_ROWS * TOK_ROWS, LANES), lambda j, be, nb: (j, 0))
    grid_spec = pltpu.PrefetchScalarGridSpec(
        num_scalar_prefetch=2,
        grid=(xb.shape[0] // (MOE_ROWS * TOK_ROWS),),
        in_specs=[blk,
                  pl.BlockSpec((1, D, 2 * D_FF), lambda j, be, nb: (be[j], 0, 0)),
                  pl.BlockSpec((1, 1, 2 * D_FF), lambda j, be, nb: (be[j], 0, 0)),
                  pl.BlockSpec((1, D_FF, D), lambda j, be, nb: (be[j], 0, 0)),
                  pl.BlockSpec((1, 1, D), lambda j, be, nb: (be[j], 0, 0))],
        out_specs=blk,
        scratch_shapes=[pltpu.VMEM((D, 2 * D_FF), BF16), pltpu.VMEM((D_FF, D), BF16)],
    )
    return pl.pallas_call(
        _moe_kernel,
        grid_spec=grid_spec,
        out_shape=jax.ShapeDtypeStruct(xb.shape, F32),
        compiler_params=_cparams(("arbitrary",)),
        name="moe_experts",
    )(blk_expert, n_used, xb, w_up, b_up[:, None, :], w_down, b_down[:, None, :])


def _combine_kernel(dest_ref, g_ref, x1_ref, yb_hbm, out_ref, buf, sem):
    tile = x1_ref.shape[0]

    def start(t, c):
        for k in range(TOP_K):
            _row_copy(yb_hbm, dest_ref[t * TOP_K + k], buf.at[k], t * TOK_ROWS, sem).start()
        return c

    def wait(t, c):
        for k in range(TOP_K):
            _row_copy(yb_hbm, 0, buf.at[k], 0, sem).wait()
        return c

    lax.fori_loop(0, tile, start, 0, unroll=8)
    lax.fori_loop(0, tile, wait, 0, unroll=8)
    acc = x1_ref[...]
    for k in range(TOP_K):
        acc = acc + g_ref[:, k:k + 1] * _load_token_tiles(buf.at[k], tile)
    out_ref[...] = acc


def _combine(x1, topg, dest, yb):
    T, D = x1.shape
    tile = MOE_TOK_TILE
    row = lambda n: pl.BlockSpec((tile, n), lambda i: (i, 0))
    return pl.pallas_call(
        _combine_kernel,
        grid=(T // tile,),
        in_specs=[pl.BlockSpec((tile * TOP_K,), lambda i: (i,), memory_space=pltpu.SMEM),
                  row(LANES), row(D), pl.BlockSpec(memory_space=pl.ANY)],
        out_specs=row(D),
        out_shape=jax.ShapeDtypeStruct((T, D), F32),
        scratch_shapes=[pltpu.VMEM((TOP_K, tile * TOK_ROWS, LANES), F32), pltpu.SemaphoreType.DMA(())],
        compiler_params=_cparams(("arbitrary",)),
        name="moe_combine",
    )(dest.reshape(-1), topg, x1, yb)


def _route(topi):
    T = topi.shape[0]
    onehot = (topi[:, :, None] == jnp.arange(N_EXPERTS, dtype=jnp.int32)).astype(jnp.int32).sum(axis=1)
    rank = jnp.cumsum(onehot, axis=0) - onehot
    counts = onehot.sum(axis=0)
    padded = (counts + MOE_ROWS - 1) // MOE_ROWS * MOE_ROWS
    pad_end = jnp.cumsum(padded)
    pad_start = pad_end - padded
    chosen = topi[:, :, None] == jnp.arange(N_EXPERTS, dtype=jnp.int32)
    dest = jnp.where(chosen, (pad_start[None, :] + rank)[:, None, :], 0).sum(axis=-1)
    n_blk = (T * TOP_K + N_EXPERTS * (MOE_ROWS - 1) + MOE_ROWS - 1) // MOE_ROWS
    blk_start = jnp.arange(n_blk, dtype=jnp.int32) * MOE_ROWS
    blk_expert = jnp.minimum((pad_end[None, :] <= blk_start[:, None]).sum(axis=1), N_EXPERTS - 1)
    n_used = pad_end[-1:] // MOE_ROWS
    pad_info = jnp.concatenate([(pad_end - MOE_ROWS) * TOK_ROWS, padded, n_used]).astype(jnp.int32)
    return (dest * TOK_ROWS).astype(jnp.int32), blk_expert.astype(jnp.int32), n_used.astype(jnp.int32), pad_info, n_blk


SAMPLE_ROWS = 8


def _token_mixers(x, pw, cos, sin, tm, key_major):
    names = ('q', 'kvc', 'kvs', 'kvw', 'ks_bf', 'kw_bf', 'gates', 'rq', 'rk', 'rv', 'rg', 'mq', 'vs_t', 'vw_t')
    return dict(zip(names, _project(x, pw, cos, sin, tm, key_major)))


def kernel(x_prompt, x_sample, mem_prompt, cache_cmp_kv, cache_slc_kv, cache_win_kv, state_ret, cache_mem_kv, page_table, norm1_g, w_in, q_norm_g, k_norm_cmp_g, k_norm_slc_g, k_norm_win_g, cmp_pe_k, cmp_w1_k, cmp_w2_k, cmp_pe_v, cmp_w1_v, cmp_w2_v, ret_norm_g, mem_norm_g, w_mem_kv, mem_q_norm_g, mem_k_norm_g, w_out, norm2_g, w_router, b_router, w_up, b_up, w_down, b_down):
    B, S, D = x_prompt.shape
    DB, QS, _ = x_sample.shape
    n_mem = mem_prompt.shape[1]
    n_pages, page = page_table.shape[1], cache_cmp_kv.shape[2]
    past = n_pages * page
    NQ = SAMPLE_ROWS
    G, HD = NSA_GROUPS, HEAD_DIM
    win_rows = min(WINDOW, S)
    TP, TS = B * S, DB * NQ

    cos_p, sin_p = _rope_tables(jnp.arange(S, dtype=jnp.int32))
    cos_s, sin_s = _rope_tables(past + jnp.arange(NQ, dtype=jnp.int32))
    cos_s, sin_s = jnp.tile(cos_s, (DB, 1)), jnp.tile(sin_s, (DB, 1))

    xp = x_prompt.reshape(TP, D)
    xs = jnp.pad(x_sample, ((0, 0), (0, NQ - QS), (0, 0))).reshape(TS, D)
    unpad = lambda a: a.reshape(DB, NQ, -1)[:, :QS]
    outs = [[] for _ in range(9)]
    for l in range(w_in.shape[0]):
        pw = _prep_proj_weights(norm1_g[l], w_in[l], q_norm_g[l], k_norm_slc_g[l], k_norm_win_g[l], mem_q_norm_g[l])
        cw = _prep_compress_weights(cmp_pe_k[l], cmp_w1_k[l], cmp_w2_k[l], cmp_pe_v[l], cmp_w1_v[l], cmp_w2_v[l],
                                    k_norm_cmp_g[l])
        mw = _prep_mix_weights(w_out[l], ret_norm_g[l], norm2_g[l], w_router[l], b_router[l])

        t = _token_mixers(xp, pw, cos_p, sin_p, 512, True)
        kc, vct = _compress(t['kvc'].reshape(B, S, 256), cw)
        o_nsa = _nsa_prompt(B, t['q'], t['gates'], kc, vct, t['ks_bf'], t['vs_t'], t['kw_bf'], t['vw_t'])
        o_ret, r_p = _retention(t['rq'].reshape(B, S, 256), t['rk'].reshape(B, S, 256), t['rv'].reshape(B, S, 256),
                                jnp.zeros((B, 2, LANES, LANES), F32), RET_CHUNK, RET_CHUNK)
        mkv = _mem_kv(mem_prompt.reshape(B * n_mem, D), mem_norm_g[l], w_mem_kv[l], mem_k_norm_g[l])
        o_mem = _mem_attn(t['mq'].reshape(B, S, 256), mkv.reshape(B, n_mem, 512), 512)
        x1_p, h_p, topi_p, topg_p = _mix(xp, o_nsa.reshape(TP, -1), o_ret.reshape(TP, 256), t['rg'],
                                         o_mem.reshape(TP, 256), mw, 512)
        outs[0].append(t['kvc'].reshape(B, S, 2, G, HD))
        outs[1].append(t['kvs'].reshape(B, S, 2, G, HD))
        outs[2].append(t['kvw'].reshape(B, S, 2, G, HD)[:, S - win_rows:])
        outs[3].append(_pairs_to_state(r_p))
        outs[4].append(mkv.reshape(B, n_mem, 2, MEM_HEADS, HD))

        t = _token_mixers(xs, pw, cos_s, sin_s, TS, False)
        win = cache_win_kv[l]
        feat_major = lambda c: jnp.moveaxis(c, 1, -1).reshape(c.shape[0], 256, c.shape[1])
        o_nsa = _nsa_decode(t['q'].astype(F32).reshape(DB, NQ, -1), t['gates'].reshape(DB, NQ, LANES),
                            t['kvs'].reshape(DB, NQ, 256), t['kvw'].reshape(DB, NQ, 256),
                            feat_major(win), feat_major(cache_cmp_kv[l]), feat_major(cache_slc_kv[l]),
                            page_table, cw, QS)
        o_ret, r_s = _retention(t['rq'].reshape(DB, NQ, 256), t['rk'].reshape(DB, NQ, 256),
                                t['rv'].reshape(DB, NQ, 256), _state_to_pairs(state_ret[l].astype(F32)), QS, NQ)
        o_mem = _mem_attn(t['mq'].reshape(DB, NQ, 256), cache_mem_kv[l].reshape(DB, n_mem, 512), NQ)
        x1_s, h_s, topi_s, topg_s = _mix(xs, o_nsa.reshape(TS, -1).astype(BF16), o_ret.reshape(TS, 256), t['rg'],
                                         o_mem.reshape(TS, 256), mw, TS)
        kv5 = lambda a: unpad(a).reshape(DB, QS, 2, G, HD)
        outs[5].append(kv5(t['kvc']))
        outs[6].append(kv5(t['kvs']))
        outs[7].append(jnp.concatenate([win, kv5(t['kvw'])], axis=1)[:, QS:])
        outs[8].append(_pairs_to_state(r_s))

        valid = lambda a: unpad(a).reshape(DB * QS, -1)
        dest, blk_expert, n_used, pad_info, n_blk = _route(
            jnp.concatenate([topi_p, valid(topi_s)], axis=0)[:, :TOP_K])
        h_s_valid = h_s.reshape(DB, NQ, TOK_ROWS, LANES)[:, :QS].reshape(DB * QS * TOK_ROWS, LANES)
        xb = _dispatch(h_p, dest[:TP], h_s_valid, dest[TP:], pad_info, n_blk * MOE_ROWS * TOK_ROWS)
        yb = _moe_experts(xb, blk_expert, n_used, w_up[l], b_up[l], w_down[l], b_down[l])
        xp = _combine(x1_p, topg_p, dest[:TP], yb)
        xs_new = _combine(valid(x1_s), valid(topg_s), dest[TP:], yb)
        xs = jnp.pad(xs_new.reshape(DB, QS, D), ((0, 0), (0, NQ - QS), (0, 0))).reshape(TS, D)
    y_sample = xs.reshape(DB, NQ, D)[:, :QS]
    return (xp.reshape(B, S, D), y_sample) + tuple(jnp.stack(o) for o in outs)
```

```python
import functools

import jax
import jax.numpy as jnp
from jax import lax
from jax.experimental import pallas as pl
from jax.experimental.pallas import tpu as pltpu

F32 = jnp.float32
BF16 = jnp.bfloat16

HEAD_DIM = 64
NSA_HEADS = 8
NSA_GROUPS = 2
NSA_HPG = NSA_HEADS // NSA_GROUPS
RET_HEADS = 4
MEM_HEADS = 4
CMP_BLK = 32
CMP_STRIDE = 16
CMP_HID = 256
SLC_BLK = 64
N_SEL = 16
WINDOW = 512
Q_BLK = 128
RET_CHUNK = 128
N_EXPERTS = 32
TOP_K = 4
D_FF = 1024
SWIGLU_LIMIT = 7.0
SWIGLU_ALPHA = 1.702
EPS = 1e-6
NEG = -1e30
BIG = 1e9
ROPE_BASE = 10000.0
SLC_RATIO = SLC_BLK // CMP_STRIDE
CMP_OVL = CMP_BLK // CMP_STRIDE - 1

Q_SCALE = HEAD_DIM ** -0.5 * 1.4426950408889634
LANES = 128
KEY_TILE = 512
MOE_ROWS = 512
VMEM_LIMIT = 56 * 1024 * 1024

C_Q = 0
C_KVC = C_Q + NSA_HEADS * LANES
C_KVS = C_KVC + 256
C_KVW = C_KVS + 256
C_GATE = C_KVW + 256
C_RQ = C_GATE + LANES
C_RK = C_RQ + 256
C_RV = C_RK + 256
C_RG = C_RV + 256
C_MQ = C_RG + 256
C_END = C_MQ + 256


def _cparams(sem):
    return pltpu.CompilerParams(dimension_semantics=sem, vmem_limit_bytes=VMEM_LIMIT)


def _lane(shape):
    return lax.broadcasted_iota(jnp.int32, shape, len(shape) - 1)


def _seg_rms(x, gain):
    lo = _lane(x.shape) < HEAD_DIM
    x2 = x * x
    s_lo = jnp.sum(jnp.where(lo, x2, 0.0), axis=-1, keepdims=True)
    s_hi = jnp.sum(jnp.where(lo, 0.0, x2), axis=-1, keepdims=True)
    ms = jnp.where(lo, s_lo, s_hi) * (1.0 / HEAD_DIM)
    return x * lax.rsqrt(ms + EPS) * gain


def _swap_halves(x):
    first = (_lane(x.shape) & (HEAD_DIM // 2)) == 0
    return jnp.where(first, pltpu.roll(x, LANES - HEAD_DIM // 2, 1), pltpu.roll(x, HEAD_DIM // 2, 1))


def _proj_kernel(x_ref, g1_ref, w_ref, cos_ref, sin_ref, qg_ref, ksg_ref, kwg_ref, mqg_ref,
                 q_out, kvc_out, kvs_out, kvw_out, ks_bf, kw_bf, gate_out,
                 rq_out, rk_out, rv_out, rg_out, mq_out, *extra, seq_tiles):
    x = x_ref[...]
    tm = x.shape[0]
    ms = jnp.mean(x * x, axis=-1, keepdims=True)
    xn = (x * lax.rsqrt(ms + EPS) * g1_ref[...]).astype(BF16)
    z = jnp.dot(xn, w_ref[...], preferred_element_type=F32)
    for i in range(NSA_HEADS):
        zq = z[:, C_Q + i * LANES:C_Q + (i + 1) * LANES]
        msq = jnp.sum(zq * zq, axis=-1, keepdims=True) * (1.0 / HEAD_DIM)
        qn = zq * lax.rsqrt(msq + EPS) * qg_ref[:, i * LANES:(i + 1) * LANES] * Q_SCALE
        if seq_tiles:
            q_out[i * LANES:(i + 1) * LANES, :] = qn.T.astype(BF16)
        else:
            q_out[:, i * LANES:(i + 1) * LANES] = qn.astype(BF16)
    kvc_out[...] = z[:, C_KVC:C_KVC + 256]
    ks = _seg_rms(z[:, C_KVS:C_KVS + LANES], ksg_ref[...])
    vs = z[:, C_KVS + LANES:C_KVS + 256]
    kvs_out[:, 0:LANES] = ks
    kvs_out[:, LANES:256] = vs
    kw = _seg_rms(z[:, C_KVW:C_KVW + LANES], kwg_ref[...])
    vw = z[:, C_KVW + LANES:C_KVW + 256]
    kvw_out[:, 0:LANES] = kw
    kvw_out[:, LANES:256] = vw
    gates = jax.nn.sigmoid(z[:, C_GATE:C_GATE + LANES])
    ks_bf[:, 0:LANES] = ks.astype(BF16)
    if seq_tiles:
        vs_t, vw_t = extra
        pos = (pl.program_id(0) % seq_tiles) * tm + lax.broadcasted_iota(jnp.int32, (tm, LANES), 0)
        ks_bf[:, LANES:256] = jnp.where(_lane((tm, LANES)) == pos // SLC_BLK, 1.0, 0.0).astype(BF16)
        kw_bf[...] = kw.astype(BF16)
        vs_t[...] = vs.T.astype(BF16)
        vw_t[...] = vw.T.astype(BF16)
        gate_out[...] = gates.T
    else:
        ks_bf[:, LANES:256] = vs.astype(BF16)
        kw_bf[:, 0:LANES] = kw.astype(BF16)
        kw_bf[:, LANES:256] = vw.astype(BF16)
        gate_out[...] = gates
    for c in range(2):
        sl = slice(c * LANES, (c + 1) * LANES)
        cos = cos_ref[:, sl]
        sin = sin_ref[:, sl]
        rq = z[:, C_RQ + c * LANES:C_RQ + (c + 1) * LANES]
        rk = z[:, C_RK + c * LANES:C_RK + (c + 1) * LANES]
        rq_out[:, sl] = rq * cos + _swap_halves(rq) * sin
        rk_out[:, sl] = (rk * cos + _swap_halves(rk) * sin) * (HEAD_DIM ** -0.5)
        mq = z[:, C_MQ + c * LANES:C_MQ + (c + 1) * LANES]
        mq_out[:, sl] = _seg_rms(mq, mqg_ref[...]).astype(BF16)
    rv_out[...] = z[:, C_RV:C_RV + 256]
    rg_out[...] = z[:, C_RG:C_RG + 256]


def _project(x, pw, cos, sin, tm, key_major):
    T, D = x.shape
    nt = cos.shape[0] // tm
    row = lambda n: (pl.BlockSpec((tm, n), lambda i: (i, 0)), (T, n))
    col = lambda n: (pl.BlockSpec((n, tm), lambda i: (0, i)), (n, T))
    full = lambda a: pl.BlockSpec(a.shape, lambda i: (0, 0))
    tab = pl.BlockSpec((tm, 256), lambda i: (i % nt, 0))
    tok = col if key_major else row
    outs = [
        (tok(NSA_HEADS * LANES), BF16),
        (row(256), F32), (row(256), F32), (row(256), F32),
        (row(256), BF16), (row(LANES if key_major else 256), BF16),
        (tok(LANES), F32),
        (row(256), F32), (row(256), F32), (row(256), F32), (row(256), F32),
        (row(256), BF16),
    ]
    if key_major:
        outs += [(col(LANES), BF16), (col(LANES), BF16)]
    return pl.pallas_call(
        functools.partial(_proj_kernel, seq_tiles=nt if key_major else 0),
        grid=(T // tm,),
        in_specs=[row(D)[0], full(pw['g1']), full(pw['w']), tab, tab,
                  full(pw['qg']), full(pw['ksg']), full(pw['kwg']), full(pw['mqg'])],
        out_specs=[spec for (spec, _), _ in outs],
        out_shape=[jax.ShapeDtypeStruct(shape, dt) for (_, shape), dt in outs],
        compiler_params=_cparams(("arbitrary",)),
        name="proj",
    )(x, pw['g1'], pw['w'], cos, sin, pw['qg'], pw['ksg'], pw['kwg'], pw['mqg'])


def _prep_proj_weights(norm1_g, w_in, q_norm_g, k_norm_slc_g, k_norm_win_g, mem_q_norm_g):
    D = w_in.shape[0]
    sizes = (512, 256, 256, 256, 24, 256, 256, 256, 256, 256)
    parts, off = [], 0
    for n in sizes:
        parts.append(w_in[:, off:off + n])
        off += n
    wq, wkvc, wkvs, wkvw, wg, wrq, wrk, wrv, wrg, wmq = parts
    zero = jnp.zeros((D, HEAD_DIM), w_in.dtype)
    qcols, qg = [], []
    gz = jnp.zeros((HEAD_DIM,), F32)
    for i in range(NSA_HEADS):
        wh = wq[:, i * HEAD_DIM:(i + 1) * HEAD_DIM]
        if i < NSA_HPG:
            qcols += [wh, zero]
            qg += [q_norm_g, gz]
        else:
            qcols += [zero, wh]
            qg += [gz, q_norm_g]
    wgp = jnp.pad(wg, ((0, 0), (0, LANES - wg.shape[1])))
    w = jnp.concatenate(qcols + [wkvc, wkvs, wkvw, wgp, wrq, wrk, wrv, wrg, wmq], axis=1).astype(BF16)
    two = lambda g: jnp.tile(g, 2)[None, :]
    return dict(g1=norm1_g[None, :], w=w, qg=jnp.concatenate(qg)[None, :],
                ksg=two(k_norm_slc_g), kwg=two(k_norm_win_g), mqg=two(mem_q_norm_g))


def _rope_tables(pos):
    half = HEAD_DIM // 2
    inv = ROPE_BASE ** (-jnp.arange(half, dtype=F32) / half)
    ang = pos.astype(F32)[:, None] * inv[None, :]
    cos, sin = jnp.cos(ang), jnp.sin(ang)
    cos = jnp.tile(jnp.concatenate([cos, cos], axis=1), (1, RET_HEADS))
    sin = jnp.tile(jnp.concatenate([-sin, sin], axis=1), (1, RET_HEADS))
    return cos, sin


def _compress_half(src_ref, pe_ref, wa_ref, wb_ref, w2_ref, nchunks):
    a_parts, b_parts = [], []
    for r in range(CMP_STRIDE):
        xr = src_ref[pl.ds(r, nchunks, stride=CMP_STRIDE), :]
        a_parts.append((xr + pe_ref[r:r + 1, :]).astype(BF16))
        b_parts.append((xr + pe_ref[CMP_STRIDE + r:CMP_STRIDE + r + 1, :]).astype(BF16))
    ha = jnp.dot(jnp.concatenate(a_parts, axis=1), wa_ref[...], preferred_element_type=F32)
    hb = jnp.dot(jnp.concatenate(b_parts, axis=1), wb_ref[...], preferred_element_type=F32)
    h = ha + pltpu.roll(hb, nchunks - 1, 0)
    act = jax.nn.gelu(h).astype(BF16)
    return jnp.dot(act, w2_ref[...], preferred_element_type=F32)


def _compress_kernel(kv_ref, pe_ref, wa_ref, wb_ref, w2_ref, kg_ref, kc_out, vct_out):
    nchunks = kc_out.shape[1]
    y = _compress_half(kv_ref.at[0], pe_ref.at[0], wa_ref.at[0], wb_ref.at[0], w2_ref.at[0], nchunks)

    @pl.when(pl.program_id(1) == 0)
    def _():
        kc_out[0] = _seg_rms(y, kg_ref[...]).astype(BF16)

    @pl.when(pl.program_id(1) == 1)
    def _():
        vct_out[0] = y.T.astype(BF16)


def _compress(kvc, cw):
    B, T, _ = kvc.shape
    nchunks = T // CMP_STRIDE
    per_kv = lambda a: pl.BlockSpec((1,) + a.shape[1:], lambda b, j: (j,) + (0,) * (a.ndim - 1))
    return pl.pallas_call(
        _compress_kernel,
        grid=(B, 2),
        in_specs=[pl.BlockSpec((1, T, LANES), lambda b, j: (b, 0, j)),
                  per_kv(cw['pe']), per_kv(cw['wa']), per_kv(cw['wb']), per_kv(cw['w2']),
                  pl.BlockSpec((1, LANES), lambda b, j: (0, 0))],
        out_specs=[pl.BlockSpec((1, nchunks, LANES), lambda b, j: (b, 0, 0)),
                   pl.BlockSpec((1, LANES, nchunks), lambda b, j: (b, 0, 0))],
        out_shape=[jax.ShapeDtypeStruct((B, nchunks, LANES), BF16),
                   jax.ShapeDtypeStruct((B, LANES, nchunks), BF16)],
        compiler_params=_cparams(("arbitrary", "arbitrary")),
        name="compress",
    )(kvc, cw['pe'], cw['wa'], cw['wb'], cw['w2'], cw['kg'])


def _prep_compress_weights(pe_k, w1_k, w2_k, pe_v, w1_v, w2_v, k_norm_g):
    eye = jnp.eye(NSA_GROUPS, dtype=F32)

    def one(pe, w1, w2):
        w1r = w1.reshape(CMP_BLK, HEAD_DIM, CMP_HID)
        wf = jnp.einsum('gh,rdc->rgdhc', eye, w1r).reshape(CMP_BLK * LANES, NSA_GROUPS * CMP_HID)
        w2f = jnp.einsum('gh,cd->gchd', eye, w2).reshape(NSA_GROUPS * CMP_HID, LANES)
        half = CMP_STRIDE * LANES
        return jnp.tile(pe, (1, NSA_GROUPS)), wf[:half].astype(BF16), wf[half:].astype(BF16), w2f.astype(BF16)

    k, v = one(pe_k, w1_k, w2_k), one(pe_v, w1_v, w2_v)
    st = lambda i: jnp.stack([k[i], v[i]])
    return dict(pe=st(0), wa=st(1), wb=st(2), w2=st(3), kg=jnp.tile(k_norm_g, 2)[None, :])


def _dot_nt(a, b):
    return lax.dot_general(a, b, (((1,), (1,)), ((), ())), preferred_element_type=F32)


def _split3(x):
    hi = x.astype(BF16)
    r = x - hi.astype(F32)
    mid = r.astype(BF16)
    lo = (r - mid.astype(F32)).astype(BF16)
    return hi, mid, lo


def _select_bias(ps, cur, rounds, axis, cur_in_range=True):
    j = lax.broadcasted_iota(jnp.int32, ps.shape, axis)
    last = cur if cur_in_range else cur - 1
    key = jnp.where(j <= last, ps, NEG)
    for forced in (0, last, cur - 1):
        key = jnp.where(j == forced, BIG, key)
    bias = jnp.full(ps.shape, NEG, F32)
    for _ in range(rounds):
        m = jnp.max(key, axis=axis, keepdims=True)
        idx = jnp.min(jnp.where(key == m, j, LANES), axis=axis, keepdims=True)
        pick = j == idx
        bias = jnp.where(pick, 0.0, bias)
        key = jnp.where(pick, -jnp.inf, key)
    return bias


def _cmp_branch(qg, qpos, kc, vc):
    s = _dot_nt(qg, kc)
    cend = _lane((1, kc.shape[0])) * CMP_STRIDE + (CMP_BLK - 1)
    s = jnp.where(cend <= qpos, s, NEG)
    m = jnp.max(s, axis=-1, keepdims=True)
    p = jnp.exp2(s - m)
    p = p * (1.0 / jnp.sum(p, axis=-1, keepdims=True))
    p = jnp.where(qpos >= CMP_BLK - 1, p, 0.0)
    return p, jnp.dot(p.astype(BF16), vc, preferred_element_type=F32)


def _gate_heads(gates, g, oc, osel, ow, nq):
    heads = []
    for h in range(NSA_HPG):
        rs = slice(h * nq, (h + 1) * nq)
        c0 = (NSA_HPG * g + h) * 3
        heads.append(gates[:, c0:c0 + 1] * oc[rs] + gates[:, c0 + 1:c0 + 2] * osel[rs]
                     + gates[:, c0 + 2:c0 + 3] * ow[rs])
    return heads


def _store_heads(heads, o_ref):
    low = _lane(heads[0].shape) < HEAD_DIM
    for pr in range(NSA_HEADS // 2):
        even, odd = heads[2 * pr], heads[2 * pr + 1]
        if pr < NSA_GROUPS:
            odd = pltpu.roll(odd, HEAD_DIM, 1)
        else:
            even = pltpu.roll(even, HEAD_DIM, 1)
        o_ref[0, :, pr * LANES:(pr + 1) * LANES] = jnp.where(low, even, odd).astype(o_ref.dtype)


def _nsa_decode_kernel(pt_ref, q_ref, gate_ref, ksn_ref, kwn_ref, win_ref, pool_c, pool_s,
                       pe_ref, wa_ref, wb_ref, w2_ref, kg_ref, mmap_ref, hot_ref, o_ref,
                       cbuf, sbuf, xrow, sems, *, page, n_new):
    b = pl.program_id(0)
    past = sbuf.shape[1]
    npages = past // page
    nq = q_ref.shape[1]
    dot = functools.partial(jnp.dot, preferred_element_type=F32)

    slot = b % 2

    def cmp_copy(seq, p, sl):
        dst = pl.ds(pl.multiple_of(p * page, page), page)
        return pltpu.make_async_copy(pool_c.at[pt_ref[seq * npages + p]], cbuf.at[sl, :, dst], sems.at[sl])

    def slc_copy(seq, p):
        dst = pl.ds(pl.multiple_of(p * page, page), page)
        return pltpu.make_async_copy(pool_s.at[pt_ref[seq * npages + p]], sbuf.at[:, dst], sems.at[2])

    def for_pages(fn):
        def body(p, c):
            fn(p)
            return c
        lax.fori_loop(0, npages, body, 0)

    @pl.when(b == 0)
    def _():
        for_pages(lambda p: cmp_copy(0, p, 0).start())

    for_pages(lambda p: slc_copy(b, p).start())

    @pl.when(b + 1 < pl.num_programs(0))
    def _():
        for_pages(lambda p: cmp_copy(b + 1, p, 1 - slot).start())

    for_pages(lambda p: cmp_copy(b, p, slot).wait())

    tchunk = 4 * LANES
    for half in range(2):
        for c in range(past // tchunk):
            xrow[half, c * tchunk:(c + 1) * tchunk, :] = cbuf[slot, half * LANES:(half + 1) * LANES,
                                                              c * tchunk:(c + 1) * tchunk].T
    nchunks = past // CMP_STRIDE
    kc = _seg_rms(_compress_half(xrow.at[0], pe_ref.at[0], wa_ref.at[0], wb_ref.at[0], w2_ref.at[0], nchunks),
                  kg_ref[...]).astype(BF16)
    vc = _compress_half(xrow.at[1], pe_ref.at[1], wa_ref.at[1], wb_ref.at[1], w2_ref.at[1], nchunks).astype(BF16)

    rows = NSA_HPG * nq
    srow = lax.broadcasted_iota(jnp.int32, (rows, 1), 0) % nq
    qpos = past + srow
    gates = gate_ref[0]
    pad_keys = lambda x: jnp.concatenate([x, jnp.zeros((LANES - nq, LANES), F32)], axis=0).astype(BF16)
    new_ok = _lane((1, LANES)) <= jnp.minimum(srow, n_new - 1)
    ks_new, vs_new = pad_keys(ksn_ref[0, :, 0:LANES]), pad_keys(ksn_ref[0, :, LANES:256])
    kw_new, vw_new = pad_keys(kwn_ref[0, :, 0:LANES]), pad_keys(kwn_ref[0, :, LANES:256])

    for_pages(lambda p: slc_copy(b, p).wait())
    k_aug_t = jnp.concatenate([sbuf[0:LANES, :].astype(BF16), hot_ref[...]], axis=0)
    vs_t = sbuf[LANES:256, :].astype(BF16)
    nwin = win_ref.shape[2]
    kw_t, vw_t = win_ref[0, 0:LANES, :].astype(BF16), win_ref[0, LANES:256, :].astype(BF16)

    qgs, ocs, p4s = [], [], []
    for g in range(NSA_GROUPS):
        qg = jnp.concatenate(
            [q_ref[0, :, (NSA_HPG * g + h) * LANES:(NSA_HPG * g + h + 1) * LANES] for h in range(NSA_HPG)],
            axis=0).astype(BF16)
        p, oc = _cmp_branch(qg, qpos, kc, vc)
        qgs.append(qg)
        ocs.append(oc)
        p4s.append(p[0:nq] + p[nq:2 * nq] + p[2 * nq:3 * nq] + p[3 * nq:4 * nq])
    p4 = jnp.concatenate(p4s, axis=0)
    ps_t = sum(_dot_nt(mmap_ref[...], part) for part in _split3(p4))
    cur = (past + _lane((1, NSA_GROUPS * nq)) % nq) // SLC_BLK
    bias_t = _select_bias(ps_t, cur, N_SEL - 1, 0, cur_in_range=False)
    bias_all = jnp.concatenate([bias_t, jnp.zeros((LANES, LANES - NSA_GROUPS * nq), F32)], axis=1).T

    heads = []
    for g in range(NSA_GROUPS):
        qg, oc = qgs[g], ocs[g]
        bias = bias_all[g * nq:(g + 1) * nq]
        qaug = jnp.concatenate([qg, jnp.concatenate([bias] * NSA_HPG, axis=0).astype(BF16)], axis=1)
        s_old = dot(qaug, k_aug_t)
        s_new = jnp.where(new_ok, _dot_nt(qg, ks_new), NEG)
        m = jnp.maximum(jnp.max(s_old, axis=-1, keepdims=True), jnp.max(s_new, axis=-1, keepdims=True))
        p_old, p_new = jnp.exp2(s_old - m), jnp.exp2(s_new - m)
        inv = 1.0 / (jnp.sum(p_old, axis=-1, keepdims=True) + jnp.sum(p_new, axis=-1, keepdims=True))
        osel = (_dot_nt(p_old.astype(BF16), vs_t) + dot(p_new.astype(BF16), vs_new)) * inv
        sw_old = jnp.where(_lane((1, nwin)) + (WINDOW - nwin) > srow, dot(qg, kw_t), NEG)
        sw_new = jnp.where(new_ok, _dot_nt(qg, kw_new), NEG)
        mw = jnp.maximum(jnp.max(sw_old, axis=-1, keepdims=True), jnp.max(sw_new, axis=-1, keepdims=True))
        pw_old, pw_new = jnp.exp2(sw_old - mw), jnp.exp2(sw_new - mw)
        invw = 1.0 / (jnp.sum(pw_old, axis=-1, keepdims=True) + jnp.sum(pw_new, axis=-1, keepdims=True))
        ow = (_dot_nt(pw_old.astype(BF16), vw_t) + dot(pw_new.astype(BF16), vw_new)) * invw
        heads += _gate_heads(gates, g, oc, osel, ow, nq)
    _store_heads(heads, o_ref)


def _nsa_decode(q, gates, ksn, kwn, win_t, pool_c, pool_s, page_table, cw, n_new):
    DB, nq, _ = q.shape
    n_pages = page_table.shape[1]
    page = pool_c.shape[2]
    past = n_pages * page
    nchunks = past // CMP_STRIDE
    mmap = _importance_map(nchunks)
    hot = (jnp.arange(LANES, dtype=jnp.int32)[:, None] == jnp.arange(past, dtype=jnp.int32)[None, :] // SLC_BLK)
    hot = hot.astype(BF16)
    per_seq = lambda a: pl.BlockSpec((1,) + a.shape[1:], lambda b, pt: (b,) + (0,) * (a.ndim - 1))
    const = lambda a: pl.BlockSpec(a.shape, lambda b, pt: (0,) * a.ndim, pipeline_mode=pl.Buffered(1))
    hbm = pl.BlockSpec(memory_space=pl.ANY)
    grid_spec = pltpu.PrefetchScalarGridSpec(
        num_scalar_prefetch=1,
        grid=(DB,),
        in_specs=[per_seq(q), per_seq(gates), per_seq(ksn), per_seq(kwn), per_seq(win_t), hbm, hbm,
                  const(cw['pe']), const(cw['wa']), const(cw['wb']), const(cw['w2']), const(cw['kg']), const(mmap),
                  const(hot)],
        out_specs=pl.BlockSpec((1, nq, NSA_HEADS * HEAD_DIM), lambda b, pt: (b, 0, 0)),
        scratch_shapes=[pltpu.VMEM((2, 256, past), F32), pltpu.VMEM((256, past), F32),
                        pltpu.VMEM((2, past, LANES), F32), pltpu.SemaphoreType.DMA((3,))],
    )
    return pl.pallas_call(
        functools.partial(_nsa_decode_kernel, page=page, n_new=n_new),
        grid_spec=grid_spec,
        out_shape=jax.ShapeDtypeStruct((DB, nq, NSA_HEADS * HEAD_DIM), F32),
        compiler_params=_cparams(("arbitrary",)),
        name="nsa_sample",
    )(page_table.reshape(-1), q, gates, ksn, kwn, win_t, pool_c, pool_s,
      cw['pe'], cw['wa'], cw['wb'], cw['w2'], cw['kg'], mmap, hot)


def _nsa_prompt_kernel(q_ref, gate_ref, kc_ref, vct_ref, ks_ref, vst_ref, kw_ref, vwt_ref, mmap_ref, o_ref,
                       m_sc, l_sc, acc_sc, sa_sc, sb_sc):
    i = pl.program_id(1)
    s0 = i * Q_BLK
    cols = NSA_HPG * Q_BLK
    qpos = s0 + _lane((1, cols)) % Q_BLK
    cur = (s0 + _lane((1, Q_BLK))) // SLC_BLK
    kc, vct = kc_ref[0], vct_ref[0]
    ncmp = kc.shape[0]
    n_full = s0 // KEY_TILE
    w0 = pl.multiple_of(jnp.maximum(s0 - WINDOW, 0), Q_BLK)
    nw = WINDOW + Q_BLK
    dot = functools.partial(jnp.dot, preferred_element_type=F32)

    def softmax_keys(s):
        p = jnp.exp2(s - jnp.max(s, axis=0, keepdims=True))
        return p, 1.0 / jnp.sum(p, axis=0, keepdims=True)

    qaugs, qts, ps_ts = [], [], []
    branch = []
    n_idx = lax.broadcasted_iota(jnp.int32, (ncmp, cols), 0)
    n_last = lax.shift_right_arithmetic(qpos - (CMP_BLK - 1), CMP_STRIDE.bit_length() - 1)
    r_idx = lax.broadcasted_iota(jnp.int32, (nw, cols), 0)
    r_last = qpos - w0
    for g in range(NSA_GROUPS):
        vg = slice(g * HEAD_DIM, (g + 1) * HEAD_DIM)
        qt = jnp.concatenate([q_ref[(NSA_HPG * g + h) * LANES:(NSA_HPG * g + h + 1) * LANES, :]
                              for h in range(NSA_HPG)], axis=1)
        s = jnp.where(n_idx <= n_last, dot(kc, qt), NEG)
        p, inv = softmax_keys(s)
        p = p * jnp.where(n_last >= 0, inv, 0.0)
        oc = dot(vct[vg], p.astype(BF16))
        p4 = p[:, 0:Q_BLK] + p[:, Q_BLK:2 * Q_BLK] + p[:, 2 * Q_BLK:3 * Q_BLK] + p[:, 3 * Q_BLK:4 * Q_BLK]
        ps_ts.append(sum(dot(mmap_ref[...], part) for part in _split3(p4)))
        qts.append(qt)
        sw = dot(kw_ref[0, pl.ds(w0, nw), :], qt)
        sw = jnp.where(r_idx <= r_last, jnp.where(r_idx > r_last - WINDOW, sw, NEG), NEG)
        pw, invw = softmax_keys(sw)
        ow = dot(vwt_ref[pl.ds(g * HEAD_DIM, HEAD_DIM), pl.ds(w0, nw)], pw.astype(BF16)) * invw
        branch.append((oc, ow))
        m_sc[g] = jnp.full(m_sc.shape[1:], NEG, F32)
        l_sc[g] = jnp.zeros(l_sc.shape[1:], F32)
        acc_sc[g] = jnp.zeros(acc_sc.shape[1:], F32)
    bias_all = _select_bias(jnp.concatenate(ps_ts, axis=1), jnp.concatenate([cur] * NSA_GROUPS, axis=1),
                            N_SEL, 0).astype(BF16)
    for g in range(NSA_GROUPS):
        bias = bias_all[:, g * Q_BLK:(g + 1) * Q_BLK]
        qaugs.append(jnp.concatenate([qts[g], jnp.concatenate([bias] * NSA_HPG, axis=1)], axis=0))

    def scores(kt, s_ref):
        k_aug = ks_ref[0, pl.ds(pl.multiple_of(kt * KEY_TILE, KEY_TILE), KEY_TILE), :]
        for g in range(NSA_GROUPS):
            s_ref[g] = dot(k_aug, qaugs[g])

    def consume(kt, s_ref, causal):
        k0 = pl.multiple_of(kt * KEY_TILE, KEY_TILE)
        v_t = vst_ref[:, pl.ds(k0, KEY_TILE)]
        for g in range(NSA_GROUPS):
            s = s_ref[g]
            if causal:
                s = jnp.where(k0 + lax.broadcasted_iota(jnp.int32, s.shape, 0) <= qpos, s, NEG)
            m_old = m_sc[g]
            m_new = jnp.maximum(m_old, jnp.max(s, axis=0, keepdims=True))
            alpha = jnp.exp2(m_old - m_new)
            p = jnp.exp2(s - m_new)
            l_sc[g] = alpha * l_sc[g] + jnp.sum(p, axis=0, keepdims=True)
            acc_sc[g] = alpha * acc_sc[g] + dot(v_t[g * HEAD_DIM:(g + 1) * HEAD_DIM], p.astype(BF16))
            m_sc[g] = m_new

    def two_tiles(u, c):
        scores(2 * u + 1, sb_sc)
        consume(2 * u, sa_sc, False)
        scores(2 * u + 2, sa_sc)
        consume(2 * u + 1, sb_sc, False)
        return c

    scores(0, sa_sc)
    lax.fori_loop(0, n_full // 2, two_tiles, 0)

    @pl.when(n_full % 2 == 1)
    def _():
        scores(n_full, sb_sc)
        consume(n_full - 1, sa_sc, False)
        consume(n_full, sb_sc, True)

    @pl.when(n_full % 2 == 0)
    def _():
        consume(n_full, sa_sc, True)

    pieces = []
    for g in range(NSA_GROUPS):
        oc, ow = branch[g]
        osel = acc_sc[g] * (1.0 / l_sc[g])
        for h in range(NSA_HPG):
            cs = slice(h * Q_BLK, (h + 1) * Q_BLK)
            c0 = (NSA_HPG * g + h) * 3
            pieces.append(gate_ref[c0:c0 + 1, :] * oc[:, cs] + gate_ref[c0 + 1:c0 + 2, :] * osel[:, cs]
                          + gate_ref[c0 + 2:c0 + 3, :] * ow[:, cs])
    o_ref[0] = jnp.concatenate(pieces, axis=0).T.astype(o_ref.dtype)


def _importance_map(ncmp):
    j = jnp.arange(LANES)[:, None]
    n = jnp.arange(ncmp)[None, :]
    return ((n >= SLC_RATIO * j - CMP_OVL) & (n < SLC_RATIO * j + SLC_RATIO)).astype(BF16)


def _nsa_prompt(B, qt, gates_t, kc, vct, ks_aug, vs_t, kw, vw_t):
    S = qt.shape[1] // B
    nq = S // Q_BLK
    ncmp = kc.shape[1]
    cols = NSA_HPG * Q_BLK
    mmap = _importance_map(ncmp)
    per_block = lambda n: pl.BlockSpec((n, Q_BLK), lambda b, i: (0, b * nq + i))
    seq_rows = lambda n: pl.BlockSpec((1, S, n), lambda b, i: (b, 0, 0))
    seq_cols = pl.BlockSpec((LANES, S), lambda b, i: (0, b))
    return pl.pallas_call(
        _nsa_prompt_kernel,
        grid=(B, nq),
        in_specs=[per_block(qt.shape[0]), per_block(LANES),
                  pl.BlockSpec((1, ncmp, LANES), lambda b, i: (b, 0, 0)),
                  pl.BlockSpec((1, LANES, ncmp), lambda b, i: (b, 0, 0)),
                  seq_rows(256), seq_cols, seq_rows(LANES), seq_cols,
                  pl.BlockSpec(mmap.shape, lambda b, i: (0, 0))],
        out_specs=pl.BlockSpec((1, Q_BLK, NSA_HEADS * HEAD_DIM), lambda b, i: (b, i, 0)),
        out_shape=jax.ShapeDtypeStruct((B, S, NSA_HEADS * HEAD_DIM), BF16),
        scratch_shapes=[pltpu.VMEM((NSA_GROUPS, 1, cols), F32), pltpu.VMEM((NSA_GROUPS, 1, cols), F32),
                        pltpu.VMEM((NSA_GROUPS, HEAD_DIM, cols), F32),
                        pltpu.VMEM((NSA_GROUPS, KEY_TILE, cols), F32), pltpu.VMEM((NSA_GROUPS, KEY_TILE, cols), F32)],
        compiler_params=_cparams(("arbitrary", "arbitrary")),
        name="nsa_prompt",
    )(qt, gates_t, kc, vct, ks_aug.reshape(B, S, 256), vs_t, kw.reshape(B, S, LANES), vw_t, mmap)


def _ret_kernel(rq_ref, rk_ref, rv_ref, r0_ref, dmask_ref, xi_ref, zeta_ref, dec_ref, o_ref, rout_ref, r_sc):
    @pl.when(pl.program_id(1) == 0)
    def _():
        r_sc[...] = r0_ref[0]

    C = rq_ref.shape[1]
    low = _lane((C, LANES)) < HEAD_DIM
    diag = lax.broadcasted_iota(jnp.int32, (LANES, LANES), 0) // HEAD_DIM == _lane((LANES, LANES)) // HEAD_DIM
    for pr in range(RET_HEADS // 2):
        sl = slice(pr * LANES, (pr + 1) * LANES)
        k = rk_ref[0, :, sl]
        qb, kb, vb = rq_ref[0, :, sl].astype(BF16), k.astype(BF16), rv_ref[0, :, sl].astype(BF16)
        zero = jnp.zeros_like(qb)
        s0 = _dot_nt(jnp.where(low, qb, zero), kb) * dmask_ref[2 * pr]
        s1 = _dot_nt(jnp.where(low, zero, qb), kb) * dmask_ref[2 * pr + 1]
        o = jnp.where(low, jnp.dot(s0.astype(BF16), vb, preferred_element_type=F32),
                      jnp.dot(s1.astype(BF16), vb, preferred_element_type=F32))
        r = r_sc[pr]
        o_ref[0, :, sl] = o + jnp.dot(qb, r.astype(BF16), preferred_element_type=F32) * xi_ref[:, sl]
        kz = (k * zeta_ref[:, sl]).astype(BF16)
        upd = lax.dot_general(kz, vb, (((0,), (0,)), ((), ())), preferred_element_type=F32)
        r_sc[pr] = dec_ref[:, sl] * r + jnp.where(diag, upd, 0.0)
    rout_ref[0] = r_sc[...]


def _ret_tables(c_true, c_pad):
    lg = jnp.log(1.0 - 2.0 ** (-5.0 - jnp.arange(RET_HEADS, dtype=F32)))
    idx = jnp.arange(c_pad, dtype=F32)
    diff = idx[:, None] - idx[None, :]
    dmask = jnp.where(diff >= 0, jnp.exp(jnp.maximum(diff, 0.0)[None] * lg[:, None, None]), 0.0)
    lanes = lambda a: jnp.repeat(a, HEAD_DIM, axis=-1)
    xi = lanes(jnp.exp((idx + 1.0)[:, None] * lg[None, :]))
    zeta = lanes(jnp.exp((c_true - 1.0 - idx)[:, None] * lg[None, :]))
    dec = lanes(jnp.exp(c_true * lg)[None, :])
    return dmask, xi, zeta, dec


def _retention(rq, rk, rv, r0, c_true, c_pad):
    B, T, _ = rq.shape
    dmask, xi, zeta, dec = _ret_tables(c_true, c_pad)
    row = pl.BlockSpec((1, c_pad, 256), lambda b, c: (b, c, 0))
    st = pl.BlockSpec((1, 2, LANES, LANES), lambda b, c: (b, 0, 0, 0))
    const = lambda a: pl.BlockSpec(a.shape, lambda b, c: (0,) * a.ndim)
    return pl.pallas_call(
        _ret_kernel,
        grid=(B, T // c_pad),
        in_specs=[row, row, row, st, const(dmask), const(xi), const(zeta), const(dec)],
        out_specs=[row, st],
        out_shape=[jax.ShapeDtypeStruct((B, T, 256), F32), jax.ShapeDtypeStruct((B, 2, LANES, LANES), F32)],
        scratch_shapes=[pltpu.VMEM((2, LANES, LANES), F32)],
        compiler_params=_cparams(("arbitrary", "arbitrary")),
        name="retention",
    )(rq, rk, rv, r0, dmask, xi, zeta, dec)


def _state_to_pairs(r):
    B = r.shape[0]
    r = r.reshape(B, 2, 2, HEAD_DIM, HEAD_DIM)
    eye = jnp.eye(2, dtype=r.dtype)
    return jnp.einsum('bphde,hk->bphdke', r, eye).reshape(B, 2, LANES, LANES)


def _pairs_to_state(rp):
    B = rp.shape[0]
    rp = rp.reshape(B, 2, 2, HEAD_DIM, 2, HEAD_DIM)
    return jnp.stack([rp[:, :, 0, :, 0, :], rp[:, :, 1, :, 1, :]], axis=2).reshape(B, RET_HEADS, HEAD_DIM, HEAD_DIM)


def _mem_kv_kernel(m_ref, g_ref, w_ref, kg_ref, kv_out):
    x = m_ref[...]
    ms = jnp.mean(x * x, axis=-1, keepdims=True)
    xn = (x * lax.rsqrt(ms + EPS) * g_ref[...]).astype(BF16)
    z = jnp.dot(xn, w_ref[...], preferred_element_type=F32)
    for c in range(2):
        kv_out[:, c * LANES:(c + 1) * LANES] = _seg_rms(z[:, c * LANES:(c + 1) * LANES], kg_ref[...])
    kv_out[:, 256:512] = z[:, 256:512]


def _mem_kv(mem, mem_norm_g, w_mem_kv, mem_k_norm_g):
    T, D = mem.shape
    w = w_mem_kv.astype(BF16)
    full = lambda a: pl.BlockSpec(a.shape, lambda i: (0, 0))
    g = mem_norm_g[None, :]
    kg = jnp.tile(mem_k_norm_g, 2)[None, :]
    return pl.pallas_call(
        _mem_kv_kernel,
        grid=(1,),
        in_specs=[full(mem), full(g), full(w), full(kg)],
        out_specs=pl.BlockSpec((T, 512), lambda i: (0, 0)),
        out_shape=jax.ShapeDtypeStruct((T, 512), F32),
        compiler_params=_cparams(("arbitrary",)),
        name="mem_kv",
    )(mem, g, w, kg)


def _mem_attn_kernel(mq_ref, mkv_ref, o_ref):
    q = mq_ref[0]
    rows = q.shape[0]
    low = _lane((rows, LANES)) < HEAD_DIM
    zero = jnp.zeros((rows, LANES), F32)
    for pr in range(MEM_HEADS // 2):
        sl = slice(pr * LANES, (pr + 1) * LANES)
        qp = q[:, sl].astype(F32)
        k = mkv_ref[0, :, sl].astype(BF16)
        v = mkv_ref[0, :, 256 + pr * LANES:256 + (pr + 1) * LANES].astype(BF16)
        outs = []
        for hh in range(2):
            qm = jnp.where(low, qp, zero) if hh == 0 else jnp.where(low, zero, qp)
            s = _dot_nt(qm.astype(BF16), k) * (HEAD_DIM ** -0.5)
            m = jnp.max(s, axis=-1, keepdims=True)
            p = jnp.exp(s - m)
            o = jnp.dot(p.astype(BF16), v, preferred_element_type=F32)
            outs.append(o * (1.0 / jnp.sum(p, axis=-1, keepdims=True)))
        o_ref[0, :, sl] = jnp.where(low, outs[0], outs[1]).astype(o_ref.dtype)


def _mem_attn(mq, mkv, tm):
    B, R, _ = mq.shape
    return pl.pallas_call(
        _mem_attn_kernel,
        grid=(B, R // tm),
        in_specs=[pl.BlockSpec((1, tm, 256), lambda b, i: (b, i, 0)),
                  pl.BlockSpec((1,) + mkv.shape[1:], lambda b, i: (b, 0, 0))],
        out_specs=pl.BlockSpec((1, tm, 256), lambda b, i: (b, i, 0)),
        out_shape=jax.ShapeDtypeStruct((B, R, 256), BF16),
        compiler_params=_cparams(("arbitrary", "arbitrary")),
        name="mem_attn",
    )(mq, mkv)


TOK_ROWS = 8


def _load_token_tiles(ref, n):
    return jnp.concatenate([ref[pl.ds(c, n, stride=TOK_ROWS), :] for c in range(TOK_ROWS)], axis=1)


def _store_token_tiles(ref, x):
    n = x.shape[0]
    for c in range(TOK_ROWS):
        ref[pl.ds(c, n, stride=TOK_ROWS), :] = x[:, c * LANES:(c + 1) * LANES]


def _mix_kernel(x_ref, onsa_ref, oret_ref, rg_ref, omem_ref, wout_ref, rgain_ref, g2_ref,
                wr_hi_ref, wr_lo_ref, br_ref, x1_out, h_out, topi_out, topg_out):
    parts = [onsa_ref[...]]
    for c in range(2):
        sl = slice(c * LANES, (c + 1) * LANES)
        parts.append((_seg_rms(oret_ref[:, sl], rgain_ref[...]) * jax.nn.silu(rg_ref[:, sl])).astype(BF16))
    parts.append(omem_ref[...])
    mix = jnp.concatenate(parts, axis=1)
    x1 = x_ref[...] + jnp.dot(mix, wout_ref[...], preferred_element_type=F32)
    x1_out[...] = x1
    ms = jnp.mean(x1 * x1, axis=-1, keepdims=True)
    h = x1 * lax.rsqrt(ms + EPS) * g2_ref[...]
    _store_token_tiles(h_out, h)
    h_hi, h_lo, _ = _split3(h)
    logits = (jnp.dot(h_hi, wr_hi_ref[...], preferred_element_type=F32)
              + jnp.dot(h_hi, wr_lo_ref[...], preferred_element_type=F32)
              + jnp.dot(h_lo, wr_hi_ref[...], preferred_element_type=F32)) + br_ref[...]
    lane = _lane(logits.shape)
    key = logits
    topi = jnp.zeros(logits.shape, jnp.int32)
    topv = jnp.zeros(logits.shape, F32)
    for r in range(TOP_K):
        m = jnp.max(key, axis=-1, keepdims=True)
        idx = jnp.min(jnp.where(key == m, lane, LANES), axis=-1, keepdims=True)
        if r == 0:
            m0 = m
        topi = jnp.where(lane == r, idx, topi)
        topv = jnp.where(lane == r, jnp.exp(m - m0), topv)
        key = jnp.where(lane == idx, -jnp.inf, key)
    topi_out[...] = topi
    topg_out[...] = topv * (1.0 / jnp.sum(topv, axis=-1, keepdims=True))


def _mix(x, onsa, oret, rg, omem, mw, tm):
    T, D = x.shape
    row = lambda n: pl.BlockSpec((tm, n), lambda i: (i, 0))
    full = lambda a: pl.BlockSpec(a.shape, lambda i: (0, 0))
    names = ('wout', 'rgain', 'g2', 'wr_hi', 'wr_lo', 'br')
    return pl.pallas_call(
        _mix_kernel,
        grid=(T // tm,),
        in_specs=[row(D), row(512), row(256), row(256), row(256)] + [full(mw[n]) for n in names],
        out_specs=[row(D), pl.BlockSpec((tm * TOK_ROWS, LANES), lambda i: (i, 0)), row(LANES), row(LANES)],
        out_shape=[jax.ShapeDtypeStruct((T, D), F32), jax.ShapeDtypeStruct((T * TOK_ROWS, LANES), F32),
                   jax.ShapeDtypeStruct((T, LANES), jnp.int32), jax.ShapeDtypeStruct((T, LANES), F32)],
        compiler_params=_cparams(("arbitrary",)),
        name="mix",
    )(x, onsa, oret, rg, omem, *[mw[n] for n in names])


def _prep_mix_weights(w_out, ret_norm_g, norm2_g, w_router, b_router):
    wr = jnp.pad(w_router, ((0, 0), (0, LANES - N_EXPERTS)))
    wr_hi = wr.astype(BF16)
    wr_lo = (wr - wr_hi.astype(F32)).astype(BF16)
    br = jnp.concatenate([b_router.astype(F32), jnp.full((LANES - N_EXPERTS,), NEG, F32)])[None, :]
    return dict(wout=w_out.astype(BF16), rgain=jnp.tile(ret_norm_g, 2)[None, :], g2=norm2_g[None, :],
                wr_hi=wr_hi, wr_lo=wr_lo, br=br)


MOE_TOK_TILE = 128


def _row_copy(src, i, dst, j, sem):
    return pltpu.make_async_copy(src.at[pl.ds(pl.multiple_of(i, TOK_ROWS), TOK_ROWS)],
                                 dst.at[pl.ds(pl.multiple_of(j, TOK_ROWS), TOK_ROWS)], sem)


def _dispatch_kernel(pad_ref, dest_ref, h_ref, dest2_ref, h2_ref, xb_out, sem, zeros, zsem):
    blk_rows = zeros.shape[0]
    n_blk = xb_out.shape[0] // blk_rows

    @pl.when(pl.program_id(0) == 0)
    def _():
        zeros[...] = jnp.zeros(zeros.shape, zeros.dtype)

        def zero_block(row):
            return pltpu.make_async_copy(zeros, xb_out.at[pl.ds(pl.multiple_of(row, blk_rows), blk_rows)], zsem)

        def each_partial_block(fn):
            for e in range(N_EXPERTS):
                @pl.when(pad_ref[N_EXPERTS + e] > 0)
                def _(e=e):
                    fn(pad_ref[e])

            def body(j, c):
                fn(j * blk_rows)
                return c
            lax.fori_loop(pad_ref[2 * N_EXPERTS], n_blk, body, 0)

        each_partial_block(lambda row: zero_block(row).start())
        each_partial_block(lambda row: zero_block(row).wait())

    def scatter(d_ref, src_ref):
        tile = d_ref.shape[0] // TOP_K

        def start(t, c):
            for k in range(TOP_K):
                _row_copy(src_ref, t * TOK_ROWS, xb_out, d_ref[t * TOP_K + k], sem).start()
            return c

        def wait(t, c):
            for k in range(TOP_K):
                _row_copy(src_ref, 0, xb_out, 0, sem).wait()
            return c

        lax.fori_loop(0, tile, start, 0, unroll=8)
        lax.fori_loop(0, tile, wait, 0, unroll=8)

    scatter(dest_ref, h_ref)

    @pl.when(pl.program_id(0) == pl.num_programs(0) - 1)
    def _():
        scatter(dest2_ref, h2_ref)


def _dispatch(h, dest, h2, dest2, pad_info, xb_rows):
    T = h.shape[0] // TOK_ROWS
    n = MOE_TOK_TILE * TOP_K
    blk_rows = MOE_ROWS * TOK_ROWS
    grid_spec = pltpu.PrefetchScalarGridSpec(
        num_scalar_prefetch=1,
        grid=(T // MOE_TOK_TILE,),
        in_specs=[pl.BlockSpec((n,), lambda i, pad: (i,), memory_space=pltpu.SMEM),
                  pl.BlockSpec((MOE_TOK_TILE * TOK_ROWS, LANES), lambda i, pad: (i, 0)),
                  pl.BlockSpec(memory_space=pltpu.SMEM),
                  pl.BlockSpec(h2.shape, lambda i, pad: (0, 0))],
        out_specs=pl.BlockSpec(memory_space=pl.ANY),
        scratch_shapes=[pltpu.SemaphoreType.DMA(()), pltpu.VMEM((blk_rows, LANES), F32),
                        pltpu.SemaphoreType.DMA(())],
    )
    return pl.pallas_call(
        _dispatch_kernel,
        grid_spec=grid_spec,
        out_shape=jax.ShapeDtypeStruct((xb_rows, LANES), F32),
        compiler_params=_cparams(("arbitrary",)),
        name="moe_dispatch",
    )(pad_info, dest.reshape(-1), h, dest2.reshape(-1), h2)


def _moe_kernel(be_ref, nb_ref, x_ref, wup_ref, bup_ref, wdn_ref, bdn_ref, y_ref, wup_bf, wdn_bf):
    j = pl.program_id(0)

    @pl.when(j < nb_ref[0])
    def _():
        @pl.when(jnp.logical_or(j == 0, be_ref[j] != be_ref[jnp.maximum(j - 1, 0)]))
        def _():
            wup_bf[...] = wup_ref[0].astype(BF16)
            wdn_bf[...] = wdn_ref[0].astype(BF16)

        x = _load_token_tiles(x_ref, MOE_ROWS).astype(BF16)
        up = jnp.dot(x, wup_bf[...], preferred_element_type=F32) + bup_ref[0]
        x_glu = jnp.minimum(up[:, :D_FF], SWIGLU_LIMIT)
        x_lin = jnp.clip(up[:, D_FF:], -SWIGLU_LIMIT, SWIGLU_LIMIT)
        act = x_glu * jax.nn.sigmoid(SWIGLU_ALPHA * x_glu) * (x_lin + 1.0)
        _store_token_tiles(y_ref, jnp.dot(act.astype(BF16), wdn_bf[...], preferred_element_type=F32) + bdn_ref[0])

    @pl.when(j >= nb_ref[0])
    def _():
        y_ref[...] = jnp.zeros(y_ref.shape, y_ref.dtype)


def _moe_experts(xb, blk_expert, n_used, w_up, b_up, w_down, b_down):
    D = w_up.shape[1]
    blk = pl.BlockSpec((MOE_ROWS * TOK_ROWS, LANES), lambda j, be, nb: (j, 0))
    grid_spec = pltpu.PrefetchScalarGridSpec(
        num_scalar_prefetch=2,
        grid=(xb.shape[0] // (MOE_ROWS * TOK_ROWS),),
        in_specs=[blk,
                  pl.BlockSpec((1, D, 2 * D_FF), lambda j, be, nb: (be[j], 0, 0)),
                  pl.BlockSpec((1, 1, 2 * D_FF), lambda j, be, nb: (be[j], 0, 0)),
                  pl.BlockSpec((1, D_FF, D), lambda j, be, nb: (be[j], 0, 0)),
                  pl.BlockSpec((1, 1, D), lambda j, be, nb: (be[j], 0, 0))],
        out_specs=blk,
        scratch_shapes=[pltpu.VMEM((D, 2 * D_FF), BF16), pltpu.VMEM((D_FF, D), BF16)],
    )
    return pl.pallas_call(
        _moe_kernel,
        grid_spec=grid_spec,
        out_shape=jax.ShapeDtypeStruct(xb.shape, F32),
        compiler_params=_cparams(("arbitrary",)),
        name="moe_experts",
    )(blk_expert, n_used, xb, w_up, b_up[:, None, :], w_down, b_down[:, None, :])


def _combine_kernel(dest_ref, next_ref, g_ref, x1_ref, yb_hbm, out_ref, buf, sems, *, n_steps):
    tile = x1_ref.shape[0]
    i = pl.program_id(0) if n_steps > 1 else 0
    slot = i % 2

    def gather(d_ref, sl):
        def start(t, c):
            for k in range(TOP_K):
                _row_copy(yb_hbm, d_ref[t * TOP_K + k], buf.at[sl, k], t * TOK_ROWS, sems.at[sl]).start()
            return c
        lax.fori_loop(0, tile, start, 0, unroll=8)

    if n_steps == 1:
        gather(dest_ref, 0)
    else:
        @pl.when(i == 0)
        def _():
            gather(dest_ref, 0)

        @pl.when(i + 1 < n_steps)
        def _():
            gather(next_ref, 1 - slot)

    def wait(t, c):
        for k in range(TOP_K):
            _row_copy(yb_hbm, 0, buf.at[slot, k], 0, sems.at[slot]).wait()
        return c

    lax.fori_loop(0, tile, wait, 0, unroll=8)
    acc = x1_ref[...]
    for k in range(TOP_K):
        acc = acc + g_ref[:, k:k + 1] * _load_token_tiles(buf.at[slot, k], tile)
    out_ref[...] = acc


def _combine(x1, topg, dest, yb):
    T, D = x1.shape
    tile = MOE_TOK_TILE
    row = lambda n: pl.BlockSpec((tile, n), lambda i: (i, 0))
    last = T // tile - 1
    slots = lambda step: pl.BlockSpec((tile * TOP_K,), step, memory_space=pltpu.SMEM)
    return pl.pallas_call(
        functools.partial(_combine_kernel, n_steps=T // tile),
        grid=(T // tile,),
        in_specs=[slots(lambda i: (i,)), slots(lambda i: (jnp.minimum(i + 1, last),)),
                  row(LANES), row(D), pl.BlockSpec(memory_space=pl.ANY)],
        out_specs=row(D),
        out_shape=jax.ShapeDtypeStruct((T, D), F32),
        scratch_shapes=[pltpu.VMEM((2, TOP_K, tile * TOK_ROWS, LANES), F32), pltpu.SemaphoreType.DMA((2,))],
        compiler_params=_cparams(("arbitrary",)),
        name="moe_combine",
    )(dest.reshape(-1), dest.reshape(-1), topg, x1, yb)


def _route(topi):
    T = topi.shape[0]
    onehot = (topi[:, :, None] == jnp.arange(N_EXPERTS, dtype=jnp.int32)).astype(jnp.int32).sum(axis=1)
    rank = jnp.cumsum(onehot, axis=0) - onehot
    counts = onehot.sum(axis=0)
    padded = (counts + MOE_ROWS - 1) // MOE_ROWS * MOE_ROWS
    pad_end = jnp.cumsum(padded)
    pad_start = pad_end - padded
    chosen = topi[:, :, None] == jnp.arange(N_EXPERTS, dtype=jnp.int32)
    dest = jnp.where(chosen, (pad_start[None, :] + rank)[:, None, :], 0).sum(axis=-1)
    n_blk = (T * TOP_K + N_EXPERTS * (MOE_ROWS - 1) + MOE_ROWS - 1) // MOE_ROWS
    blk_start = jnp.arange(n_blk, dtype=jnp.int32) * MOE_ROWS
    blk_expert = jnp.minimum((pad_end[None, :] <= blk_start[:, None]).sum(axis=1), N_EXPERTS - 1)
    n_used = pad_end[-1:] // MOE_ROWS
    pad_info = jnp.concatenate([(pad_end - MOE_ROWS) * TOK_ROWS, padded, n_used]).astype(jnp.int32)
    return (dest * TOK_ROWS).astype(jnp.int32), blk_expert.astype(jnp.int32), n_used.astype(jnp.int32), pad_info, n_blk


SAMPLE_ROWS = 8


def _token_mixers(x, pw, cos, sin, tm, key_major):
    names = ('q', 'kvc', 'kvs', 'kvw', 'ks_bf', 'kw_bf', 'gates', 'rq', 'rk', 'rv', 'rg', 'mq', 'vs_t', 'vw_t')
    return dict(zip(names, _project(x, pw, cos, sin, tm, key_major)))


def kernel(x_prompt, x_sample, mem_prompt, cache_cmp_kv, cache_slc_kv, cache_win_kv, state_ret, cache_mem_kv, page_table, norm1_g, w_in, q_norm_g, k_norm_cmp_g, k_norm_slc_g, k_norm_win_g, cmp_pe_k, cmp_w1_k, cmp_w2_k, cmp_pe_v, cmp_w1_v, cmp_w2_v, ret_norm_g, mem_norm_g, w_mem_kv, mem_q_norm_g, mem_k_norm_g, w_out, norm2_g, w_router, b_router, w_up, b_up, w_down, b_down):
    B, S, D = x_prompt.shape
    DB, QS, _ = x_sample.shape
    n_mem = mem_prompt.shape[1]
    n_pages, page = page_table.shape[1], cache_cmp_kv.shape[2]
    past = n_pages * page
    NQ = SAMPLE_ROWS
    G, HD = NSA_GROUPS, HEAD_DIM
    win_rows = min(WINDOW, S)
    TP, TS = B * S, DB * NQ

    cos_p, sin_p = _rope_tables(jnp.arange(S, dtype=jnp.int32))
    cos_s, sin_s = _rope_tables(past + jnp.arange(NQ, dtype=jnp.int32))
    cos_s, sin_s = jnp.tile(cos_s, (DB, 1)), jnp.tile(sin_s, (DB, 1))

    xp = x_prompt.reshape(TP, D)
    xs = jnp.pad(x_sample, ((0, 0), (0, NQ - QS), (0, 0))).reshape(TS, D)
    unpad = lambda a: a.reshape(DB, NQ, -1)[:, :QS]
    outs = [[] for _ in range(9)]
    for l in range(w_in.shape[0]):
        pw = _prep_proj_weights(norm1_g[l], w_in[l], q_norm_g[l], k_norm_slc_g[l], k_norm_win_g[l], mem_q_norm_g[l])
        cw = _prep_compress_weights(cmp_pe_k[l], cmp_w1_k[l], cmp_w2_k[l], cmp_pe_v[l], cmp_w1_v[l], cmp_w2_v[l],
                                    k_norm_cmp_g[l])
        mw = _prep_mix_weights(w_out[l], ret_norm_g[l], norm2_g[l], w_router[l], b_router[l])

        t = _token_mixers(xp, pw, cos_p, sin_p, 512, True)
        kc, vct = _compress(t['kvc'].reshape(B, S, 256), cw)
        o_nsa = _nsa_prompt(B, t['q'], t['gates'], kc, vct, t['ks_bf'], t['vs_t'], t['kw_bf'], t['vw_t'])
        o_ret, r_p = _retention(t['rq'].reshape(B, S, 256), t['rk'].reshape(B, S, 256), t['rv'].reshape(B, S, 256),
                                jnp.zeros((B, 2, LANES, LANES), F32), RET_CHUNK, RET_CHUNK)
        mkv = _mem_kv(mem_prompt.reshape(B * n_mem, D), mem_norm_g[l], w_mem_kv[l], mem_k_norm_g[l])
        o_mem = _mem_attn(t['mq'].reshape(B, S, 256), mkv.reshape(B, n_mem, 512), 512)
        x1_p, h_p, topi_p, topg_p = _mix(xp, o_nsa.reshape(TP, -1), o_ret.reshape(TP, 256), t['rg'],
                                         o_mem.reshape(TP, 256), mw, 512)
        outs[0].append(t['kvc'].reshape(B, S, 2, G, HD))
        outs[1].append(t['kvs'].reshape(B, S, 2, G, HD))
        outs[2].append(t['kvw'].reshape(B, S, 2, G, HD)[:, S - win_rows:])
        outs[3].append(_pairs_to_state(r_p))
        outs[4].append(mkv.reshape(B, n_mem, 2, MEM_HEADS, HD))

        t = _token_mixers(xs, pw, cos_s, sin_s, TS, False)
        win = cache_win_kv[l]
        feat_major = lambda c: jnp.moveaxis(c, 1, -1).reshape(c.shape[0], 256, c.shape[1])
        o_nsa = _nsa_decode(t['q'].astype(F32).reshape(DB, NQ, -1), t['gates'].reshape(DB, NQ, LANES),
                            t['kvs'].reshape(DB, NQ, 256), t['kvw'].reshape(DB, NQ, 256),
                            feat_major(win), feat_major(cache_cmp_kv[l]), feat_major(cache_slc_kv[l]),
                            page_table, cw, QS)
        o_ret, r_s = _retention(t['rq'].reshape(DB, NQ, 256), t['rk'].reshape(DB, NQ, 256),
                                t['rv'].reshape(DB, NQ, 256), _state_to_pairs(state_ret[l].astype(F32)), QS, NQ)
        o_mem = _mem_attn(t['mq'].reshape(DB, NQ, 256), cache_mem_kv[l].reshape(DB, n_mem, 512), NQ)
        x1_s, h_s, topi_s, topg_s = _mix(xs, o_nsa.reshape(TS, -1).astype(BF16), o_ret.reshape(TS, 256), t['rg'],
                                         o_mem.reshape(TS, 256), mw, TS)
        kv5 = lambda a: unpad(a).reshape(DB, QS, 2, G, HD)
        outs[5].append(kv5(t['kvc']))
        outs[6].append(kv5(t['kvs']))
        outs[7].append(jnp.concatenate([win, kv5(t['kvw'])], axis=1)[:, QS:])
        outs[8].append(_pairs_to_state(r_s))

        valid = lambda a: unpad(a).reshape(DB * QS, -1)
        dest, blk_expert, n_used, pad_info, n_blk = _route(
            jnp.concatenate([topi_p, valid(topi_s)], axis=0)[:, :TOP_K])
        h_s_valid = h_s.reshape(DB, NQ, TOK_ROWS, LANES)[:, :QS].reshape(DB * QS * TOK_ROWS, LANES)
        xb = _dispatch(h_p, dest[:TP], h_s_valid, dest[TP:], pad_info, n_blk * MOE_ROWS * TOK_ROWS)
        yb = _moe_experts(xb, blk_expert, n_used, w_up[l], b_up[l], w_down[l], b_down[l])
        xp = _combine(x1_p, topg_p, dest[:TP], yb)
        xs_new = _combine(valid(x1_s), valid(topg_s), dest[TP:], yb)
        xs = jnp.pad(xs_new.reshape(DB, QS, D), ((0, 0), (0, NQ - QS), (0, 0))).reshape(TS, D)
    y_sample = xs.reshape(DB, NQ, D)[:, :QS]
    return (xp.reshape(B, S, D), y_sample) + tuple(jnp.stack(o) for o in outs)
```
